```python
import jax, jax.numpy as jnp
from jax import lax
import numpy as np

D_MODEL = 2048
BATCH = 32
SEQ = 256
DEPTH = 1
DEC_BATCH = 4
DEC_SEQ = 2048
PAST_LEN = 256

GRID_W = 64
GLA_HEADS = 8
GLA_DK = 64
GLA_DV = 128
GATE_RANK = 16
GATE_TAU = 16.0
CHUNK = 64
FNET_GROUPS = 8
FNET_CH = 128
QK_WIDTH = GLA_HEADS * GLA_DK
GLA_WIDTH = GLA_HEADS * GLA_DV
FNET_WIDTH = FNET_GROUPS * FNET_CH
MIX_WIDTH = GLA_WIDTH + FNET_WIDTH
IN_COLS = 2 * QK_WIDTH + 2 * GLA_WIDTH + 2 * GATE_RANK + FNET_WIDTH
N_GROUPS = 4
EXPERTS_PER_GROUP = 8
N_EXPERTS = N_GROUPS * EXPERTS_PER_GROUP
TOP_K_FINE = 2
D_EXPERT = 512
EPS = 1e-6

kernel_name = "hymba_gla_fnet_hmoe_prefix_step"


def _rmsnorm(x, g):
    xf = x.astype(jnp.float32)
    y = xf * lax.rsqrt(jnp.mean(xf * xf, axis=-1, keepdims=True) + EPS)
    return (y * g.astype(jnp.float32)).astype(x.dtype)


def _gla_scan(q, k, v, logg, s0):
    b_, h_, l_, _ = q.shape
    dv = v.shape[-1]
    n = l_ // CHUNK

    def to_chunks(t):
        return t.reshape(b_, h_, n, CHUNK, t.shape[-1]).transpose(2, 0, 1, 3, 4)

    mask = jnp.tril(jnp.ones((CHUNK, CHUNK), dtype=bool))

    def step(S, inp):
        qc, kc, vc, gc = inp
        cum = jnp.cumsum(gc, axis=-2)
        last = cum[..., -1:, :]
        q_e = qc * jnp.exp(cum)
        k_e = kc * jnp.exp(-cum)
        scores = jnp.where(mask, jnp.einsum('bhid,bhjd->bhij', q_e, k_e), 0.0)
        o = jnp.einsum('bhij,bhjv->bhiv', scores, vc) + jnp.einsum('bhid,bhdv->bhiv', q_e, S)
        S = jnp.exp(last)[..., 0, :, None] * S + jnp.einsum('bhjd,bhjv->bhdv', kc * jnp.exp(last - cum), vc)
        return S, o

    s_fin, o = lax.scan(step, s0, (to_chunks(q), to_chunks(k), to_chunks(v), to_chunks(logg)))
    o = o.transpose(1, 2, 0, 3, 4).reshape(b_, h_, l_, dv)
    return o, s_fin


def _heads(t, d):
    bsz, l_, _ = t.shape
    return t.reshape(bsz, l_, GLA_HEADS, d).transpose(0, 2, 1, 3).astype(jnp.float32)


def _mixer(h, s0_f, s0_b, grid, w_in, w_gate_f, b_gate_f, w_gate_b, b_gate_b, norm_gla, w_out):
    bsz, l_, _ = h.shape
    proj = h @ w_in
    i1 = QK_WIDTH
    i2 = 2 * QK_WIDTH
    i3 = i2 + GLA_WIDTH
    i4 = i3 + GLA_WIDTH
    i5 = i4 + GATE_RANK
    i6 = i5 + GATE_RANK
    q, k, v, og, rf, rb, u = jnp.split(proj, [i1, i2, i3, i4, i5, i6], axis=-1)
    q = _heads(q, GLA_DK) * (GLA_DK ** -0.5)
    k = _heads(k, GLA_DK)
    v = _heads(v, GLA_DV)
    gf = _heads(jax.nn.log_sigmoid((rf @ w_gate_f + b_gate_f).astype(jnp.float32)) / GATE_TAU, GLA_DK)
    gb = _heads(jax.nn.log_sigmoid((rb @ w_gate_b + b_gate_b).astype(jnp.float32)) / GATE_TAU, GLA_DK)
    o_f, s_f = _gla_scan(q, k, v, gf, s0_f.astype(jnp.float32))
    o_b_rev, s_b = _gla_scan(jnp.flip(q, 2), jnp.flip(k, 2), jnp.flip(v, 2), jnp.flip(gb, 2),
                             s0_b.astype(jnp.float32))
    o = o_f + jnp.flip(o_b_rev, 2)
    o = o * lax.rsqrt(jnp.mean(o * o, axis=-1, keepdims=True) + EPS) \
        * norm_gla.astype(jnp.float32)[None, :, None, :]
    o = o.transpose(0, 2, 1, 3).reshape(bsz, l_, GLA_WIDTH)
    o = (o * jax.nn.silu(og.astype(jnp.float32))).astype(h.dtype)
    uf = u.astype(jnp.float32)
    if grid:
        rows = l_ // GRID_W
        uf = uf.reshape(bsz, rows, GRID_W, FNET_GROUPS, FNET_CH)
        f = jnp.fft.fftn(uf, axes=(1, 2, 4), norm="ortho").real
    else:
        uf = uf.reshape(bsz, l_, FNET_GROUPS, FNET_CH)
        f = jnp.fft.fftn(uf, axes=(1, 3), norm="ortho").real
    f = f.reshape(bsz, l_, FNET_WIDTH).astype(h.dtype)
    y = jnp.concatenate([o, f], axis=-1) @ w_out
    return y, s_f, s_b


def _moe(h, w_rg, b_rg, w_re, b_re, w_eg, w_eu, w_ed):
    bsz, l_, d = h.shape
    x = h.reshape(-1, d)
    lg = (x @ w_rg + b_rg).astype(jnp.float32)
    pg = jax.nn.softmax(lg, axis=-1)
    gsel = jnp.argmax(lg, axis=-1)
    pg_sel = jnp.max(pg, axis=-1, keepdims=True)
    le = (x @ w_re + b_re).astype(jnp.float32).reshape(-1, N_GROUPS, EXPERTS_PER_GROUP)
    le_sel = jnp.einsum('tge,tg->te', le, jax.nn.one_hot(gsel, N_GROUPS, dtype=jnp.float32))
    tv, ti = lax.top_k(le_sel, TOP_K_FINE)
    wts = jax.nn.softmax(tv, axis=-1) * pg_sel
    eid = gsel[:, None] * EXPERTS_PER_GROUP + ti
    combine = jnp.sum(jax.nn.one_hot(eid, N_EXPERTS, dtype=jnp.float32) * wts[..., None], axis=1)
    combine = combine.astype(h.dtype)
    y = jnp.zeros_like(x)
    for g in range(N_GROUPS):
        sl = slice(g * EXPERTS_PER_GROUP, (g + 1) * EXPERTS_PER_GROUP)
        hid = jax.nn.silu(jnp.einsum('td,edf->tef', x, w_eg[sl])) * jnp.einsum('td,edf->tef', x, w_eu[sl])
        y = y + jnp.einsum('tef,efd->td', hid * combine[:, sl, None], w_ed[sl])
    return y.reshape(bsz, l_, d)


def _layer(x, cond, s0_f, s0_b, grid, w_ada, b_ada, norm_attn, norm_ffn, w_in, w_gate_f, b_gate_f,
           w_gate_b, b_gate_b, norm_gla, w_out, w_rg, b_rg, w_re, b_re, w_eg, w_eu, w_ed):
    mod = jax.nn.silu(cond) @ w_ada + b_ada
    sh1, sc1, ga1, sh2, sc2, ga2 = jnp.split(mod[:, None, :], 6, axis=-1)
    h = _rmsnorm(x, norm_attn) * (1 + sc1) + sh1
    y, s_f, s_b = _mixer(h, s0_f, s0_b, grid, w_in, w_gate_f, b_gate_f, w_gate_b, b_gate_b, norm_gla, w_out)
    x = x + ga1 * y
    h = _rmsnorm(x, norm_ffn) * (1 + sc2) + sh2
    x = x + ga2 * _moe(h, w_rg, b_rg, w_re, b_re, w_eg, w_eu, w_ed)
    return x, s_f, s_b


def setup_inputs(seed: int = 0) -> dict:
    key = jax.random.key(seed)
    ks = jax.random.split(key, 32)
    f32 = jnp.float32

    def nrm(i, shape, scale):
        return jax.random.normal(ks[i], shape, f32) * scale

    st_shape = (DEC_BATCH, DEPTH, GLA_HEADS, GLA_DK, GLA_DV)
    return {
        "x_prompt": nrm(0, (BATCH, SEQ, D_MODEL), 1.0),
        "x_sample": nrm(1, (DEC_BATCH, DEC_SEQ, D_MODEL), 1.0),
        "state_gla_fwd": nrm(2, st_shape, 1.0),
        "state_gla_bwd": nrm(3, st_shape, 1.0),
        "c": nrm(4, (DEC_BATCH, D_MODEL), 1.0),
        "c_ctx": nrm(5, (D_MODEL,), 1.0),
        "w_ada": nrm(6, (DEPTH, D_MODEL, 6 * D_MODEL), 0.5 * D_MODEL ** -0.5),
        "b_ada": nrm(7, (DEPTH, 6 * D_MODEL), 0.01),
        "norm_attn": 1.0 + nrm(8, (DEPTH, D_MODEL), 0.02),
        "norm_ffn": 1.0 + nrm(9, (DEPTH, D_MODEL), 0.02),
        "w_in": nrm(10, (DEPTH, D_MODEL, IN_COLS), D_MODEL ** -0.5),
        "w_gate_fwd": nrm(11, (DEPTH, GATE_RANK, QK_WIDTH), GATE_RANK ** -0.5),
        "b_gate_fwd": nrm(12, (DEPTH, QK_WIDTH), 0.1),
        "w_gate_bwd": nrm(13, (DEPTH, GATE_RANK, QK_WIDTH), GATE_RANK ** -0.5),
        "b_gate_bwd": nrm(14, (DEPTH, QK_WIDTH), 0.1),
        "norm_gla": 1.0 + nrm(15, (DEPTH, GLA_HEADS, GLA_DV), 0.02),
        "w_out": nrm(16, (DEPTH, MIX_WIDTH, D_MODEL), MIX_WIDTH ** -0.5),
        "w_router_group": nrm(17, (DEPTH, D_MODEL, N_GROUPS), D_MODEL ** -0.5),
        "b_router_group": nrm(18, (DEPTH, N_GROUPS), 0.01),
        "w_router_expert": nrm(19, (DEPTH, D_MODEL, N_EXPERTS), D_MODEL ** -0.5),
        "b_router_expert": nrm(20, (DEPTH, N_EXPERTS), 0.01),
        "w_expert_gate": nrm(21, (DEPTH, N_EXPERTS, D_MODEL, D_EXPERT), D_MODEL ** -0.5),
        "w_expert_up": nrm(22, (DEPTH, N_EXPERTS, D_MODEL, D_EXPERT), D_MODEL ** -0.5),
        "w_expert_down": nrm(23, (DEPTH, N_EXPERTS, D_EXPERT, D_MODEL), D_EXPERT ** -0.5),
        "norm_final": 1.0 + nrm(24, (D_MODEL,), 0.02),
    }


def reference(x_prompt, x_sample, state_gla_fwd, state_gla_bwd, c, c_ctx, w_ada, b_ada, norm_attn,
              norm_ffn, w_in, w_gate_fwd, b_gate_fwd, w_gate_bwd, b_gate_bwd, norm_gla, w_out,
              w_router_group, b_router_group, w_router_expert, b_router_expert, w_expert_gate,
              w_expert_up, w_expert_down, norm_final):
    def lp(l):
        return (w_ada[l], b_ada[l], norm_attn[l], norm_ffn[l], w_in[l], w_gate_fwd[l], b_gate_fwd[l],
                w_gate_bwd[l], b_gate_bwd[l], norm_gla[l], w_out[l], w_router_group[l], b_router_group[l],
                w_router_expert[l], b_router_expert[l], w_expert_gate[l], w_expert_up[l], w_expert_down[l])

    zero_state = jnp.zeros((x_prompt.shape[0], GLA_HEADS, GLA_DK, GLA_DV), jnp.float32)
    xp = x_prompt
    fwd_states, bwd_states = [], []
    for l in range(DEPTH):
        xp, s_f, s_b = _layer(xp, c_ctx[None, :], zero_state, zero_state, False, *lp(l))
        fwd_states.append(s_f)
        bwd_states.append(s_b)
    y_prompt = _rmsnorm(xp, norm_final)
    new_state_gla_fwd = jnp.stack(fwd_states, axis=1).astype(x_prompt.dtype)
    new_state_gla_bwd = jnp.stack(bwd_states, axis=1).astype(x_prompt.dtype)

    xs = x_sample
    for l in range(DEPTH):
        xs, _, _ = _layer(xs, c, state_gla_fwd[:, l], state_gla_bwd[:, l], True, *lp(l))
    y_sample = _rmsnorm(xs, norm_final)
    return (y_prompt, y_sample, new_state_gla_fwd, new_state_gla_bwd)
```

```python
import functools

import numpy as np
import jax
import jax.numpy as jnp
from jax import lax
from jax.experimental import pallas as pl
from jax.experimental.pallas import tpu as pltpu

F32 = jnp.float32
BF16 = jnp.bfloat16
I32 = jnp.int32

D_MODEL = 2048
N_CTX = 32
L_CTX = 256
N_LAT = 4
L_LAT = 2048
GRID_H = 32
GRID_W = 64
T_CTX = N_CTX * L_CTX
T_LAT = N_LAT * L_LAT
T_ALL = T_CTX + T_LAT
HEADS = 8
DK = 64
DV = 128
RANK = 16
TAU = 16.0
CHUNK = 64
FN_G = 8
FN_C = 128
QK_W = HEADS * DK
V_W = HEADS * DV
N_GROUPS = 4
EPG = 8
N_EXP = N_GROUPS * EPG
D_EXP = 512
EPS = 1e-6

LANES = 128
VMEM_LIMIT = 56 * 1024 * 1024

TM_MOE = 256
NT_MOE = (2 * T_ALL) // TM_MOE + N_EXP
P_MOE = NT_MOE * TM_MOE


def _dot(a, b):
    return jnp.dot(a, b, preferred_element_type=F32)


def _dot_nt(a, b):
    return lax.dot_general(a, b, (((1,), (1,)), ((), ())), preferred_element_type=F32)


def _dot_tn(a, b):
    return lax.dot_general(a, b, (((0,), (0,)), ((), ())), preferred_element_type=F32)


def _split2(x):
    hi = x.astype(BF16)
    lo = (x - hi.astype(F32)).astype(BF16)
    return hi, lo


def _split3(x):
    a = x.astype(BF16)
    r = x - a.astype(F32)
    b = r.astype(BF16)
    c = (r - b.astype(F32)).astype(BF16)
    return a, b, c


def _silu(x):
    return x * (1.0 / (1.0 + jnp.exp(-x)))


def _rms(x, g):
    return x * lax.rsqrt(jnp.mean(x * x, axis=-1, keepdims=True) + EPS) * g


def _cparams(sem):
    return pltpu.CompilerParams(dimension_semantics=sem, vmem_limit_bytes=VMEM_LIMIT)


def _ada_kernel(c_ref, w_ref, b_ref, o_ref):
    s_hi, s_lo = _split2(_silu(c_ref[...]))
    w = w_ref[...]
    w_hi = w.astype(BF16)
    w_lo = (w - w_hi.astype(F32)).astype(BF16)
    o_ref[...] = _dot(s_hi, w_hi) + _dot(s_lo, w_hi) + _dot(s_hi, w_lo) + b_ref[...]


def _ada(cond8, w_ada, b_ada):
    tn = 768
    n6 = 6 * D_MODEL
    return pl.pallas_call(
        _ada_kernel,
        grid=(n6 // tn,),
        in_specs=[pl.BlockSpec((8, D_MODEL), lambda j: (0, 0)),
                  pl.BlockSpec((D_MODEL, tn), lambda j: (0, j)),
                  pl.BlockSpec((1, tn), lambda j: (0, j))],
        out_specs=pl.BlockSpec((8, tn), lambda j: (0, j)),
        out_shape=jax.ShapeDtypeStruct((8, n6), F32),
        compiler_params=_cparams(("arbitrary",)),
        name="ada",
    )(cond8, w_ada, b_ada)


TM_IN = 512
TN_IN = 1024
N_MAIN = 4096


def _cond_row(tile, tm):
    ctx_tiles = T_CTX // tm
    per_seq = L_LAT // tm
    return jnp.where(tile < ctx_tiles, 0, 1 + (jnp.maximum(tile - ctx_tiles, 0)) // per_seq)


def _inproj_kernel(xp_ref, xs_ref, mod_ref, g_ref, w_ref, wr_ref, proj_ref, r_ref, hn_ref):
    i = pl.program_id(0)
    n = pl.program_id(1)

    @pl.when(n == 0)
    def _():
        x = jnp.where(i < T_CTX // TM_IN, xp_ref[...], xs_ref[...])
        sh1 = mod_ref[:, 0:D_MODEL]
        sc1 = mod_ref[:, D_MODEL:2 * D_MODEL]
        h = _rms(x, g_ref[...]) * (1.0 + sc1) + sh1
        hb = h.astype(BF16)
        hn_ref[...] = hb
        r_ref[...] = _dot(hb, wr_ref[...])

    proj_ref[...] = _dot(hn_ref[...], w_ref[...])


def _inproj(xp, xs, mod3, g_attn, w_main, w_r):
    nt = T_ALL // TM_IN
    nctx = T_CTX // TM_IN
    return pl.pallas_call(
        _inproj_kernel,
        grid=(nt, N_MAIN // TN_IN),
        in_specs=[pl.BlockSpec((TM_IN, D_MODEL), lambda i, n: (jnp.minimum(i, nctx - 1), 0)),
                  pl.BlockSpec((TM_IN, D_MODEL), lambda i, n: (jnp.maximum(i - nctx, 0), 0)),
                  pl.BlockSpec((None, 1, 6 * D_MODEL), lambda i, n: (_cond_row(i, TM_IN), 0, 0)),
                  pl.BlockSpec((1, D_MODEL), lambda i, n: (0, 0)),
                  pl.BlockSpec((D_MODEL, TN_IN), lambda i, n: (0, n)),
                  pl.BlockSpec((D_MODEL, LANES), lambda i, n: (0, 0))],
        out_specs=[pl.BlockSpec((TM_IN, TN_IN), lambda i, n: (i, n)),
                   pl.BlockSpec((TM_IN, LANES), lambda i, n: (i, 0))],
        out_shape=[jax.ShapeDtypeStruct((T_ALL, N_MAIN), F32),
                   jax.ShapeDtypeStruct((T_ALL, LANES), F32)],
        scratch_shapes=[pltpu.VMEM((TM_IN, D_MODEL), BF16)],
        compiler_params=_cparams(("arbitrary", "arbitrary")),
        name="inproj",
    )(xp, xs, mod3, g_attn, w_main, w_r)


SB = 256


def _gla_kernel(q_ref, k_ref, v_ref, og_ref, r_ref, wgh_ref, wgl_ref, bg_ref, ng_ref,
                tlu_ref, bd_ref, s0f_ref, s0b_ref,
                o_ref, sf_ref, sb_ref,
                cum_s, last_s, qe_s, ke_s, kd_s, dec_s, kv_s, sbs_s, *, seq_len):
    n_chunks = seq_len // CHUNK
    n_sb = seq_len // SB
    lane = lax.broadcasted_iota(I32, (1, LANES), 1)
    m_f = lane < DK

    r_hi, r_lo = _split2(r_ref[...])
    z = _dot(r_hi, wgh_ref[...]) + _dot(r_lo, wgh_ref[...]) + _dot(r_hi, wgl_ref[...]) + bg_ref[...]
    g_all = (jnp.minimum(z, 0.0) - jnp.log(1.0 + jnp.exp(-jnp.abs(z)))) * (1.0 / TAU)

    q_pair = q_ref[...]
    k_pair = k_ref[...]
    q_roll = pltpu.roll(q_pair, DK, axis=1)
    k_roll = pltpu.roll(k_pair, DK, axis=1)

    row64 = lax.broadcasted_iota(I32, (CHUNK, CHUNK), 0)
    col64 = lax.broadcasted_iota(I32, (CHUNK, CHUNK), 1)
    tril = row64 >= col64
    triu = row64 <= col64

    for j in range(2):
        g = g_all[:, j * LANES:(j + 1) * LANES]
        g_f = jnp.where(m_f, g, 0.0)
        g_b = jnp.where(m_f, 0.0, g)

        dec_s[...] = g_f
        last_s[...] = g_b

        def cum_body(s, carry):
            rows = pl.ds(pl.multiple_of(s * SB, SB), SB)
            cum = jnp.zeros((SB, LANES), F32)
            tot = jnp.zeros((SB, LANES), F32)
            pf = _split3(dec_s[rows, :])
            pb = _split3(last_s[rows, :])
            for a, b in zip(pf, pb):
                cum = cum + _dot(tlu_ref[...], jnp.concatenate([a, b], axis=0))
                tot = tot + _dot(bd_ref[...], a + b)
            cum_s[rows, :] = cum
            last_s[rows, :] = tot
            return carry

        lax.fori_loop(0, n_sb, cum_body, 0)

        cum = cum_s[...]
        last = last_s[...]
        if j == 0:
            q2 = jnp.where(m_f, q_pair, q_roll)
            k2 = jnp.where(m_f, k_pair, k_roll)
        else:
            q2 = jnp.where(m_f, q_roll, q_pair)
            k2 = jnp.where(m_f, k_roll, k_pair)
        qe_s[...] = ((q2 * (DK ** -0.5)) * jnp.exp(cum)).astype(BF16)
        ke_s[...] = (k2 * jnp.exp(-cum)).astype(BF16)
        kd_s[...] = (k2 * jnp.exp(last - cum)).astype(BF16)
        dec_s[...] = jnp.exp(last)

        lo = j * DV

        def kv_body(c, carry):
            rows = pl.ds(pl.multiple_of(c * CHUNK, CHUNK), CHUNK)
            vc = v_ref[rows, lo:lo + DV].astype(BF16)
            kv_s[c] = _dot_tn(kd_s[rows, :], vc)
            return carry

        lax.fori_loop(0, n_chunks, kv_body, 0)

        def dcol(c):
            d_row = dec_s[pl.ds(pl.multiple_of(c * CHUNK, CHUNK), 1), :]
            return jnp.transpose(jnp.broadcast_to(d_row, (LANES, LANES)))

        def bwd_body(t, s_b):
            c = n_chunks - 1 - t
            sbs_s[c] = s_b
            return dcol(c)[DK:2 * DK, :] * s_b + kv_s[c, DK:2 * DK, :]

        s_b_fin = lax.fori_loop(0, n_chunks, bwd_body, s0b_ref[j])
        sb_ref[j] = s_b_fin

        ng = ng_ref[:, lo:lo + DV]

        def fwd_body(c, s_f):
            rows = pl.ds(pl.multiple_of(c * CHUNK, CHUNK), CHUNK)
            qe = qe_s[rows, :]
            ke = ke_s[rows, :]
            vc = v_ref[rows, lo:lo + DV].astype(BF16)
            zero = jnp.zeros_like(qe)
            q_st = jnp.concatenate([jnp.where(m_f, qe, zero), jnp.where(m_f, zero, qe)], axis=0)
            sc = _dot_nt(q_st, ke)
            p = jnp.where(tril, sc[0:CHUNK, :], 0.0) + jnp.where(triu, sc[CHUNK:2 * CHUNK, :], 0.0)
            s2 = jnp.concatenate([s_f, sbs_s[c]], axis=0).astype(BF16)
            o = _dot(p.astype(BF16), vc) + _dot(qe, s2)
            o = o * lax.rsqrt(jnp.mean(o * o, axis=-1, keepdims=True) + EPS) * ng
            o_ref[rows, lo:lo + DV] = o * _silu(og_ref[rows, lo:lo + DV])
            return dcol(c)[0:DK, :] * s_f + kv_s[c, 0:DK, :]

        s_f_fin = lax.fori_loop(0, n_chunks, fwd_body, s0f_ref[j])
        sf_ref[j] = s_f_fin


def _gla(proj, r_all, wg_hi, wg_lo, bg, ng, tlu, bd, s0f, s0b, *, seq_len, n_seq, row_block0):
    kern = functools.partial(_gla_kernel, seq_len=seq_len)
    n_chunks = seq_len // CHUNK
    rb = lambda b: b + row_block0
    return pl.pallas_call(
        kern,
        grid=(n_seq, HEADS // 2),
        in_specs=[pl.BlockSpec((seq_len, LANES), lambda b, h: (rb(b), h)),
                  pl.BlockSpec((seq_len, LANES), lambda b, h: (rb(b), QK_W // LANES + h)),
                  pl.BlockSpec((seq_len, 2 * DV), lambda b, h: (rb(b), 2 * QK_W // (2 * DV) + h)),
                  pl.BlockSpec((seq_len, 2 * DV), lambda b, h: (rb(b), (2 * QK_W + V_W) // (2 * DV) + h)),
                  pl.BlockSpec((seq_len, LANES), lambda b, h: (rb(b), 0)),
                  pl.BlockSpec((None, LANES, 2 * LANES), lambda b, h: (h, 0, 0)),
                  pl.BlockSpec((None, LANES, 2 * LANES), lambda b, h: (h, 0, 0)),
                  pl.BlockSpec((None, 1, 2 * LANES), lambda b, h: (h, 0, 0)),
                  pl.BlockSpec((None, 1, 2 * DV), lambda b, h: (h, 0, 0)),
                  pl.BlockSpec((SB, 2 * SB), lambda b, h: (0, 0)),
                  pl.BlockSpec((SB, SB), lambda b, h: (0, 0)),
                  pl.BlockSpec((None, 2, DK, DV), lambda b, h: (b, h, 0, 0)),
                  pl.BlockSpec((None, 2, DK, DV), lambda b, h: (b, h, 0, 0))],
        out_specs=[pl.BlockSpec((seq_len, 2 * DV), lambda b, h: (b, h)),
                   pl.BlockSpec((None, 2, DK, DV), lambda b, h: (b, h, 0, 0)),
                   pl.BlockSpec((None, 2, DK, DV), lambda b, h: (b, h, 0, 0))],
        out_shape=[jax.ShapeDtypeStruct((n_seq * seq_len, V_W), F32),
                   jax.ShapeDtypeStruct((n_seq, HEADS, DK, DV), F32),
                   jax.ShapeDtypeStruct((n_seq, HEADS, DK, DV), F32)],
        scratch_shapes=[pltpu.VMEM((seq_len, LANES), F32),
                        pltpu.VMEM((seq_len, LANES), F32),
                        pltpu.VMEM((seq_len, LANES), BF16),
                        pltpu.VMEM((seq_len, LANES), BF16),
                        pltpu.VMEM((seq_len, LANES), BF16),
                        pltpu.VMEM((seq_len, LANES), F32),
                        pltpu.VMEM((n_chunks, LANES, DV), F32),
                        pltpu.VMEM((n_chunks, DK, DV), F32)],
        compiler_params=_cparams(("arbitrary", "arbitrary")),
        name="gla_%d" % seq_len,
    )(proj, proj, proj, proj, r_all, wg_hi, wg_lo, bg, ng, tlu, bd, s0f, s0b)


def _fnet_stage_a(u_bf, cs):
    cparts, sparts = [], []
    for g in range(FN_G):
        ab = _dot(u_bf[:, g * FN_C:(g + 1) * FN_C], cs)
        cparts.append(ab[:, 0:FN_C])
        sparts.append(ab[:, FN_C:2 * FN_C])
    return jnp.concatenate(cparts, axis=1), jnp.concatenate(sparts, axis=1)


def _fnet_ctx_kernel(u_ref, cs_ref, p2_ref, f_ref):
    uc, us = _fnet_stage_a(u_ref[...].astype(BF16), cs_ref[...].astype(BF16))
    ab = jnp.concatenate([uc, us], axis=0).astype(BF16)
    f_ref[...] = _dot(p2_ref[...].astype(BF16), ab)


def _fnet_ctx(proj, cs, p2):
    return pl.pallas_call(
        _fnet_ctx_kernel,
        grid=(N_CTX,),
        in_specs=[pl.BlockSpec((L_CTX, FN_G * FN_C), lambda b: (b, 3)),
                  pl.BlockSpec((FN_C, 2 * FN_C), lambda b: (0, 0)),
                  pl.BlockSpec((L_CTX, 2 * L_CTX), lambda b: (0, 0))],
        out_specs=pl.BlockSpec((L_CTX, FN_G * FN_C), lambda b: (b, 0)),
        out_shape=jax.ShapeDtypeStruct((T_CTX, FN_G * FN_C), F32),
        compiler_params=_cparams(("arbitrary",)),
        name="fnet_ctx",
    )(proj, cs, p2)


TM_FL = 256
RT_FL = 256


def _fnet_lat_kernel(u_ref, cs_ref, kr_ref, f_ref, ab_s):
    m = pl.program_id(1)

    @pl.when(m == 0)
    def _():
        def body(t, carry):
            rows = pl.ds(pl.multiple_of(t * RT_FL, RT_FL), RT_FL)
            uc, us = _fnet_stage_a(u_ref[rows, :].astype(BF16), cs_ref[...].astype(BF16))
            ab_s[rows, :] = uc.astype(BF16)
            ab_s[pl.ds(pl.multiple_of(L_LAT + t * RT_FL, RT_FL), RT_FL), :] = us.astype(BF16)
            return carry
        lax.fori_loop(0, L_LAT // RT_FL, body, 0)

    f_ref[...] = _dot(kr_ref[...].astype(BF16), ab_s[...])


def _fnet_lat(proj, cs, kr):
    nm = L_LAT // TM_FL
    return pl.pallas_call(
        _fnet_lat_kernel,
        grid=(N_LAT, nm),
        in_specs=[pl.BlockSpec((L_LAT, FN_G * FN_C), lambda b, m: (T_CTX // L_LAT + b, 3)),
                  pl.BlockSpec((FN_C, 2 * FN_C), lambda b, m: (0, 0)),
                  pl.BlockSpec((TM_FL, 2 * L_LAT), lambda b, m: (m, 0))],
        out_specs=pl.BlockSpec((TM_FL, FN_G * FN_C), lambda b, m: (b * nm + m, 0)),
        out_shape=jax.ShapeDtypeStruct((T_LAT, FN_G * FN_C), F32),
        scratch_shapes=[pltpu.VMEM((2 * L_LAT, FN_G * FN_C), BF16)],
        compiler_params=_cparams(("arbitrary", "arbitrary")),
        name="fnet_lat",
    )(proj, cs, kr)


TM_OUT = 256
LANE_E0 = N_GROUPS


def _outproj_kernel(oc_ref, ol_ref, fc_ref, fl_ref, xp_ref, xs_ref, mod_ref, g_ref, wo_ref, wf_ref,
                    wrh_ref, wrl_ref, br_ref, tri_ref,
                    x1_ref, h2_ref, ri_ref, rw_ref, cnt_ref, carry_s):
    i = pl.program_id(0)
    is_ctx = i < T_CTX // TM_OUT

    @pl.when(i == 0)
    def _():
        carry_s[...] = jnp.zeros_like(carry_s)

    o = jnp.where(is_ctx, oc_ref[...], ol_ref[...]).astype(BF16)
    f = jnp.where(is_ctx, fc_ref[...], fl_ref[...]).astype(BF16)
    x = jnp.where(is_ctx, xp_ref[...], xs_ref[...])
    y = _dot(o, wo_ref[...]) + _dot(f, wf_ref[...])
    ga1 = mod_ref[:, 2 * D_MODEL:3 * D_MODEL]
    sh2 = mod_ref[:, 3 * D_MODEL:4 * D_MODEL]
    sc2 = mod_ref[:, 4 * D_MODEL:5 * D_MODEL]
    x1 = x + ga1 * y
    x1_ref[...] = x1
    h2 = _rms(x1, g_ref[...]) * (1.0 + sc2) + sh2
    h2_ref[...] = h2

    h_hi, h_lo = _split2(h2)
    lg_all = _dot(h_hi, wrh_ref[...]) + _dot(h_lo, wrh_ref[...]) + _dot(h_hi, wrl_ref[...]) + br_ref[...]

    lane_i = lax.broadcasted_iota(I32, (TM_OUT, LANES), 1)
    lane = lane_i.astype(F32)
    neg = jnp.float32(-jnp.inf)
    big = jnp.float32(LANES)
    lg = jnp.where(lane_i < N_GROUPS, lg_all, neg)
    gmax = jnp.max(lg, axis=1, keepdims=True)
    gsel = jnp.min(jnp.where(lg == gmax, lane, big), axis=1, keepdims=True)
    den = jnp.sum(jnp.exp(lg - gmax), axis=1, keepdims=True)
    pg_sel = 1.0 / den

    e_idx = lane_i - LANE_E0
    egrp = (e_idx >> 3).astype(F32)
    emask = (e_idx >= 0) & (e_idx < N_EXP) & (egrp == gsel)
    m1 = jnp.where(emask, lg_all, neg)
    v1 = jnp.max(m1, axis=1, keepdims=True)
    i1 = jnp.min(jnp.where(m1 == v1, lane, big), axis=1, keepdims=True)
    m2 = jnp.where(lane == i1, neg, m1)
    v2 = jnp.max(m2, axis=1, keepdims=True)
    i2 = jnp.min(jnp.where(m2 == v2, lane, big), axis=1, keepdims=True)
    e2 = jnp.exp(v2 - v1)
    inv = 1.0 / (1.0 + e2)
    w1 = inv * pg_sel
    w2 = (e2 * inv) * pg_sel

    oh1 = lane == i1
    oh2 = lane == i2
    oh = jnp.where(oh1 | oh2, 1.0, 0.0)
    before = _dot(tri_ref[...], oh.astype(BF16)) + carry_s[...]
    r1 = jnp.sum(jnp.where(oh1, before, 0.0), axis=1, keepdims=True)
    r2 = jnp.sum(jnp.where(oh2, before, 0.0), axis=1, keepdims=True)
    carry = carry_s[...] + jnp.sum(oh, axis=0, keepdims=True)
    carry_s[...] = carry
    cnt_ref[...] = carry.astype(I32)

    ri = jnp.where(lane_i == 0, i1 - LANE_E0,
                   jnp.where(lane_i == 1, i2 - LANE_E0,
                             jnp.where(lane_i == 2, r1, jnp.where(lane_i == 3, r2, 0.0))))
    rw = jnp.where(lane_i == 0, w1, jnp.where(lane_i == 1, w2, 0.0))
    ri_ref[...] = ri[:, 0:8].astype(I32)
    rw_ref[...] = rw[:, 0:8]


def _outproj(o_ctx, o_lat, f_ctx, f_lat, xp, xs, mod3, g_ffn, wo, wf, wr_hi, wr_lo, br, tri):
    nt = T_ALL // TM_OUT
    nctx = T_CTX // TM_OUT
    ctx_map = lambda i: (jnp.minimum(i, nctx - 1), 0)
    lat_map = lambda i: (jnp.maximum(i - nctx, 0), 0)
    const = lambda i: (0, 0)
    return pl.pallas_call(
        _outproj_kernel,
        grid=(nt,),
        in_specs=[pl.BlockSpec((TM_OUT, V_W), ctx_map),
                  pl.BlockSpec((TM_OUT, V_W), lat_map),
                  pl.BlockSpec((TM_OUT, FN_G * FN_C), ctx_map),
                  pl.BlockSpec((TM_OUT, FN_G * FN_C), lat_map),
                  pl.BlockSpec((TM_OUT, D_MODEL), ctx_map),
                  pl.BlockSpec((TM_OUT, D_MODEL), lat_map),
                  pl.BlockSpec((None, 1, 6 * D_MODEL), lambda i: (_cond_row(i, TM_OUT), 0, 0)),
                  pl.BlockSpec((1, D_MODEL), const),
                  pl.BlockSpec((V_W, D_MODEL), const),
                  pl.BlockSpec((FN_G * FN_C, D_MODEL), const),
                  pl.BlockSpec((D_MODEL, LANES), const),
                  pl.BlockSpec((D_MODEL, LANES), const),
                  pl.BlockSpec((1, LANES), const),
                  pl.BlockSpec((TM_OUT, TM_OUT), const)],
        out_specs=[pl.BlockSpec((TM_OUT, D_MODEL), lambda i: (i, 0)),
                   pl.BlockSpec((TM_OUT, D_MODEL), lambda i: (i, 0)),
                   pl.BlockSpec((TM_OUT, 8), lambda i: (i, 0)),
                   pl.BlockSpec((TM_OUT, 8), lambda i: (i, 0)),
                   pl.BlockSpec((1, LANES), const)],
        out_shape=[jax.ShapeDtypeStruct((T_ALL, D_MODEL), F32),
                   jax.ShapeDtypeStruct((T_ALL, D_MODEL), F32),
                   jax.ShapeDtypeStruct((T_ALL, 8), I32),
                   jax.ShapeDtypeStruct((T_ALL, 8), F32),
                   jax.ShapeDtypeStruct((1, LANES), I32)],
        scratch_shapes=[pltpu.VMEM((1, LANES), F32)],
        compiler_params=_cparams(("arbitrary",)),
        name="outproj",
    )(o_ctx, o_lat, f_ctx, f_lat, xp, xs, mod3, g_ffn, wo, wf, wr_hi, wr_lo, br, tri)


def _moe_kernel(texp_ref, tnv_ref, meta_ref, asg_ref,
                h2_hbm, wg_ref, wu_ref, wd_ref,
                out_hbm,
                xbuf, ybuf, wg_s, wu_s, wd_s, gsem, ssem):
    i = pl.program_id(0)
    nt = meta_ref[0]
    slot = i % 2

    def row_gather(tile, sl, j):
        a = asg_ref[tile * TM_MOE + j]
        tok = jnp.maximum(a, 0) >> 1
        return pltpu.make_async_copy(h2_hbm.at[pl.ds(tok, 1)], xbuf.at[sl, pl.ds(j, 1)], gsem.at[sl])

    def row_scatter(tile, sl, j):
        a = jnp.maximum(asg_ref[tile * TM_MOE + j], 0)
        return pltpu.make_async_copy(ybuf.at[sl, pl.ds(j, 1)], out_hbm.at[pl.ds(a, 1)], ssem.at[sl])

    def start_gather(tile, sl):
        def body(j, c):
            row_gather(tile, sl, j).start()
            return c
        lax.fori_loop(0, TM_MOE, body, 0)

    def wait_scatter(tile, sl):
        def body(j, c):
            row_scatter(tile, sl, j).wait()
            return c
        lax.fori_loop(0, tnv_ref[tile], body, 0)

    @pl.when(i == 0)
    def _():
        start_gather(0, 0)

    @pl.when(i + 1 < nt)
    def _():
        start_gather(i + 1, 1 - slot)

    @pl.when(i < nt)
    def _():
        prev = texp_ref[jnp.maximum(i - 1, 0)]

        @pl.when((i == 0) | (texp_ref[i] != prev))
        def _():
            wg_s[...] = wg_ref[...].astype(BF16)
            wu_s[...] = wu_ref[...].astype(BF16)
            wd_s[...] = wd_ref[...].astype(BF16)

        pltpu.make_async_copy(h2_hbm.at[pl.ds(0, TM_MOE)], xbuf.at[slot], gsem.at[slot]).wait()
        x = xbuf[slot].astype(BF16)
        g = _dot(x, wg_s[...])
        u = _dot(x, wu_s[...])
        hid = (_silu(g) * u).astype(BF16)
        y = _dot(hid, wd_s[...])

        @pl.when(i >= 2)
        def _():
            wait_scatter(i - 2, slot)

        ybuf[slot] = y

        def sbody(j, c):
            row_scatter(i, slot, j).start()
            return c
        lax.fori_loop(0, tnv_ref[i], sbody, 0)

    @pl.when(i == nt - 1)
    def _():
        @pl.when(i >= 1)
        def _():
            wait_scatter(i - 1, 1 - slot)
        wait_scatter(i, slot)


def _moe(texp, tnv, meta, asg, h2, w_eg, w_eu, w_ed):
    grid_spec = pltpu.PrefetchScalarGridSpec(
        num_scalar_prefetch=4,
        grid=(NT_MOE,),
        in_specs=[pl.BlockSpec(memory_space=pl.ANY),
                  pl.BlockSpec((None, D_MODEL, D_EXP), lambda i, te, tn, me, a: (te[i], 0, 0)),
                  pl.BlockSpec((None, D_MODEL, D_EXP), lambda i, te, tn, me, a: (te[i], 0, 0)),
                  pl.BlockSpec((None, D_EXP, D_MODEL), lambda i, te, tn, me, a: (te[i], 0, 0))],
        out_specs=pl.BlockSpec(memory_space=pl.ANY),
        scratch_shapes=[pltpu.VMEM((2, TM_MOE, D_MODEL), F32),
                        pltpu.VMEM((2, TM_MOE, D_MODEL), F32),
                        pltpu.VMEM((D_MODEL, D_EXP), BF16),
                        pltpu.VMEM((D_MODEL, D_EXP), BF16),
                        pltpu.VMEM((D_EXP, D_MODEL), BF16),
                        pltpu.SemaphoreType.DMA((2,)),
                        pltpu.SemaphoreType.DMA((2,))])
    return pl.pallas_call(
        _moe_kernel,
        grid_spec=grid_spec,
        out_shape=jax.ShapeDtypeStruct((2 * T_ALL, D_MODEL), F32),
        compiler_params=_cparams(("arbitrary",)),
        name="moe",
    )(texp, tnv, meta, asg, h2, w_eg, w_eu, w_ed)


TM_FIN = 256


def _final_kernel(x1_ref, y2_ref, rw_ref, mod_ref, g_ref, yp_ref, ys_ref):
    i = pl.program_id(0)
    ga2 = mod_ref[:, 5 * D_MODEL:6 * D_MODEL]
    w0 = rw_ref[:, 0:1]
    w1 = rw_ref[:, 1:2]
    y = x1_ref[...] + ga2 * (w0 * y2_ref[:, 0:D_MODEL] + w1 * y2_ref[:, D_MODEL:2 * D_MODEL])
    out = _rms(y, g_ref[...])

    @pl.when(i < T_CTX // TM_FIN)
    def _():
        yp_ref[...] = out

    @pl.when(i >= T_CTX // TM_FIN)
    def _():
        ys_ref[...] = out


def _final(x1, y2, rw, mod3, g_fin):
    nt = T_ALL // TM_FIN
    nctx = T_CTX // TM_FIN
    return pl.pallas_call(
        _final_kernel,
        grid=(nt,),
        in_specs=[pl.BlockSpec((TM_FIN, D_MODEL), lambda i: (i, 0)),
                  pl.BlockSpec((TM_FIN, 2 * D_MODEL), lambda i: (i, 0)),
                  pl.BlockSpec((TM_FIN, 8), lambda i: (i, 0)),
                  pl.BlockSpec((None, 1, 6 * D_MODEL), lambda i: (_cond_row(i, TM_FIN), 0, 0)),
                  pl.BlockSpec((1, D_MODEL), lambda i: (0, 0))],
        out_specs=[pl.BlockSpec((TM_FIN, D_MODEL), lambda i: (jnp.minimum(i, nctx - 1), 0)),
                   pl.BlockSpec((TM_FIN, D_MODEL), lambda i: (jnp.maximum(i - nctx, 0), 0))],
        out_shape=[jax.ShapeDtypeStruct((T_CTX, D_MODEL), F32),
                   jax.ShapeDtypeStruct((T_LAT, D_MODEL), F32)],
        compiler_params=_cparams(("arbitrary",)),
        name="final",
    )(x1, y2, rw, mod3, g_fin)


def _np_bf16(a):
    return jnp.asarray(np.asarray(a, np.float32), dtype=BF16)


def _np_f32(a):
    return jnp.asarray(np.asarray(a, np.float32))


@functools.lru_cache(maxsize=None)
def _constants():
    c = {}
    k = np.arange(FN_C)
    ang = 2.0 * np.pi * np.outer(k, k) / FN_C
    c["cs"] = np.concatenate([np.cos(ang), np.sin(ang)], axis=1) / np.sqrt(FN_C)
    p = np.arange(L_CTX)
    ang = 2.0 * np.pi * np.outer(p, p) / L_CTX
    c["p2"] = np.concatenate([np.cos(ang), -np.sin(ang)], axis=1) / np.sqrt(L_CTX)
    pos = np.arange(L_LAT)
    rr, cc = pos // GRID_W, pos % GRID_W
    num = (np.outer(rr, rr) * (GRID_W // GRID_H) + np.outer(cc, cc)) % GRID_W
    ang = 2.0 * np.pi * num / GRID_W
    c["kr"] = np.concatenate([np.cos(ang), -np.sin(ang)], axis=1) / np.sqrt(L_LAT)
    i = np.arange(SB)
    same = (i[:, None] // CHUNK) == (i[None, :] // CHUNK)
    tl = same & (i[:, None] >= i[None, :])
    tu = same & (i[:, None] <= i[None, :])
    c["tlu"] = np.concatenate([tl, tu], axis=1).astype(np.float32)
    c["bd"] = same.astype(np.float32)
    c["tri"] = (i[:, None] > i[None, :]).astype(np.float32)
    return c


def kernel(x_prompt, x_sample, state_gla_fwd, state_gla_bwd, c, c_ctx, w_ada, b_ada, norm_attn, norm_ffn, w_in, w_gate_fwd, b_gate_fwd, w_gate_bwd, b_gate_bwd, norm_gla, w_out, w_router_group, b_router_group, w_router_expert, b_router_expert, w_expert_gate, w_expert_up, w_expert_down, norm_final):
    assert w_ada.shape[0] == 1, "single layer"
    cst = _constants()
    cs, p2, kr = _np_f32(cst["cs"]), _np_f32(cst["p2"]), _np_f32(cst["kr"])
    tlu, bd, tri = _np_bf16(cst["tlu"]), _np_bf16(cst["bd"]), _np_bf16(cst["tri"])

    xp = x_prompt.reshape(T_CTX, D_MODEL)
    xs = x_sample.reshape(T_LAT, D_MODEL)

    cond8 = jnp.concatenate([c_ctx[None, :], c, jnp.zeros((3, D_MODEL), F32)], axis=0)
    mod = _ada(cond8, w_ada[0], b_ada[0][None, :])
    mod3 = mod.reshape(8, 1, 6 * D_MODEL)

    wi = w_in[0]
    i_og = 2 * QK_W + 2 * V_W
    i_u = i_og + 2 * RANK
    w_main = jnp.concatenate([wi[:, :i_og], wi[:, i_u:]], axis=1).astype(BF16)
    w_r = jnp.pad(wi[:, i_og:i_u], ((0, 0), (0, LANES - 2 * RANK))).astype(BF16)

    wgf = w_gate_fwd[0].reshape(RANK, HEADS, DK)
    wgb = w_gate_bwd[0].reshape(RANK, HEADS, DK)
    zf = jnp.zeros_like(wgf)
    top = jnp.stack([wgf, zf], axis=2)
    bot = jnp.stack([zf, wgb], axis=2)
    wg = jnp.concatenate([top, bot], axis=0)
    wg = wg.reshape(2 * RANK, HEADS // 2, 4 * DK).transpose(1, 0, 2)
    wg = jnp.pad(wg, ((0, 0), (0, LANES - 2 * RANK), (0, 0)))
    wg_hi = wg.astype(BF16)
    wg_lo = (wg - wg_hi.astype(F32)).astype(BF16)
    bg = jnp.stack([b_gate_fwd[0].reshape(HEADS, DK), b_gate_bwd[0].reshape(HEADS, DK)], axis=1)
    bg = bg.reshape(HEADS // 2, 1, 4 * DK)
    ng = norm_gla[0].reshape(HEADS // 2, 1, 2 * DV)

    proj, r_all = _inproj(xp, xs, mod3, norm_attn, w_main, w_r)

    zero_state = jnp.zeros((N_CTX, HEADS, DK, DV), F32)
    o_ctx, sf_ctx, sb_ctx = _gla(proj, r_all, wg_hi, wg_lo, bg, ng, tlu, bd, zero_state, zero_state,
                                 seq_len=L_CTX, n_seq=N_CTX, row_block0=0)
    o_lat, _, _ = _gla(proj, r_all, wg_hi, wg_lo, bg, ng, tlu, bd,
                       state_gla_fwd[:, 0], state_gla_bwd[:, 0],
                       seq_len=L_LAT, n_seq=N_LAT, row_block0=T_CTX // L_LAT)

    f_ctx = _fnet_ctx(proj, cs, p2)
    f_lat = _fnet_lat(proj, cs, kr)

    wo = w_out[0][:V_W].astype(BF16)
    wf = w_out[0][V_W:].astype(BF16)
    wr = jnp.concatenate([w_router_group[0], w_router_expert[0]], axis=1)
    wr = jnp.pad(wr, ((0, 0), (0, LANES - N_GROUPS - N_EXP)))
    wr_hi = wr.astype(BF16)
    wr_lo = (wr - wr_hi.astype(F32)).astype(BF16)
    br = jnp.pad(jnp.concatenate([b_router_group[0], b_router_expert[0]]), (0, LANES - N_GROUPS - N_EXP))[None, :]

    x1, h2, ri, rw, cnt = _outproj(o_ctx, o_lat, f_ctx, f_lat, xp, xs, mod3, norm_ffn, wo, wf,
                                   wr_hi, wr_lo, br, tri)

    counts = cnt[0, LANE_E0:LANE_E0 + N_EXP]
    tiles_e = (counts + TM_MOE - 1) // TM_MOE
    tile_end = jnp.cumsum(tiles_e)
    tile_start = tile_end - tiles_e
    n_tiles = tile_end[-1]
    eid = ri[:, 0:2]
    pos = tile_start[eid] * TM_MOE + ri[:, 2:4]
    asg = jnp.full((P_MOE,), -1, I32).at[pos.reshape(-1)].set(jnp.arange(2 * T_ALL, dtype=I32))
    tidx = jnp.minimum(jnp.arange(NT_MOE, dtype=I32), n_tiles - 1)
    texp = jnp.sum(tidx[:, None] >= tile_end[None, :], axis=1).astype(I32)
    tnv = jnp.clip(counts[texp] - (tidx - tile_start[texp]) * TM_MOE, 0, TM_MOE)
    tnv = jnp.where(jnp.arange(NT_MOE) < n_tiles, tnv, 0).astype(I32)
    meta = n_tiles.reshape(1).astype(I32)

    y2 = _moe(texp, tnv, meta, asg, h2, w_expert_gate[0], w_expert_up[0], w_expert_down[0])
    y_prompt, y_sample = _final(x1, y2.reshape(T_ALL, 2 * D_MODEL), rw, mod3, norm_final[None, :])

    st_shape = (N_CTX, 1, HEADS, DK, DV)
    return (y_prompt.reshape(N_CTX, L_CTX, D_MODEL), y_sample.reshape(N_LAT, L_LAT, D_MODEL),
            sf_ctx.reshape(st_shape), sb_ctx.reshape(st_shape))
```

```python
import functools

import numpy as np
import jax
import jax.numpy as jnp
from jax import lax
from jax.experimental import pallas as pl
from jax.experimental.pallas import tpu as pltpu

F32 = jnp.float32
BF16 = jnp.bfloat16
I32 = jnp.int32

D_MODEL = 2048
N_CTX = 32
L_CTX = 256
N_LAT = 4
L_LAT = 2048
GRID_H = 32
GRID_W = 64
T_CTX = N_CTX * L_CTX
T_LAT = N_LAT * L_LAT
T_ALL = T_CTX + T_LAT
HEADS = 8
DK = 64
DV = 128
RANK = 16
TAU = 16.0
CHUNK = 64
FN_G = 8
FN_C = 128
QK_W = HEADS * DK
V_W = HEADS * DV
N_GROUPS = 4
EPG = 8
N_EXP = N_GROUPS * EPG
D_EXP = 512
EPS = 1e-6

LANES = 128
VMEM_LIMIT = 56 * 1024 * 1024

TM_MOE = 256
NT_MOE = (2 * T_ALL) // TM_MOE + N_EXP
P_MOE = NT_MOE * TM_MOE


def _dot(a, b):
    return jnp.dot(a, b, preferred_element_type=F32)


def _dot_nt(a, b):
    return lax.dot_general(a, b, (((1,), (1,)), ((), ())), preferred_element_type=F32)


def _split2(x):
    hi = x.astype(BF16)
    lo = (x - hi.astype(F32)).astype(BF16)
    return hi, lo


def _silu(x):
    return x * (1.0 / (1.0 + jnp.exp(-x)))


def _rms(x, g):
    return x * lax.rsqrt(jnp.mean(x * x, axis=-1, keepdims=True) + EPS) * g


def _cparams(sem):
    return pltpu.CompilerParams(dimension_semantics=sem, vmem_limit_bytes=VMEM_LIMIT)


def _ada_kernel(c_ref, w_ref, b_ref, o_ref):
    s_hi, s_lo = _split2(_silu(c_ref[...]))
    w = w_ref[...]
    w_hi = w.astype(BF16)
    w_lo = (w - w_hi.astype(F32)).astype(BF16)
    o_ref[...] = _dot(s_hi, w_hi) + _dot(s_lo, w_hi) + _dot(s_hi, w_lo) + b_ref[...]


def _ada(cond8, w_ada, b_ada):
    tn = 768
    n6 = 6 * D_MODEL
    return pl.pallas_call(
        _ada_kernel,
        grid=(n6 // tn,),
        in_specs=[pl.BlockSpec((8, D_MODEL), lambda j: (0, 0)),
                  pl.BlockSpec((D_MODEL, tn), lambda j: (0, j)),
                  pl.BlockSpec((1, tn), lambda j: (0, j))],
        out_specs=pl.BlockSpec((8, tn), lambda j: (0, j)),
        out_shape=jax.ShapeDtypeStruct((8, n6), F32),
        compiler_params=_cparams(("arbitrary",)),
        name="ada",
    )(cond8, w_ada, b_ada)


TM_IN = 512
TN_IN = 1024
N_MAIN = 4096


def _cond_row(tile, tm):
    ctx_tiles = T_CTX // tm
    per_seq = L_LAT // tm
    return jnp.where(tile < ctx_tiles, 0, 1 + (jnp.maximum(tile - ctx_tiles, 0)) // per_seq)


def _inproj_kernel(xp_ref, xs_ref, mod_ref, g_ref, w_ref, wr_ref, proj_ref, r_ref, hn_ref):
    i = pl.program_id(0)
    n = pl.program_id(1)

    @pl.when(n == 0)
    def _():
        x = jnp.where(i < T_CTX // TM_IN, xp_ref[...], xs_ref[...])
        sh1 = mod_ref[:, 0:D_MODEL]
        sc1 = mod_ref[:, D_MODEL:2 * D_MODEL]
        h = _rms(x, g_ref[...]) * (1.0 + sc1) + sh1
        hb = h.astype(BF16)
        hn_ref[...] = hb
        r_ref[...] = _dot(hb, wr_ref[...])

    proj_ref[...] = _dot(hn_ref[...], w_ref[...])


def _inproj(xp, xs, mod3, g_attn, w_main, w_r):
    nt = T_ALL // TM_IN
    nctx = T_CTX // TM_IN
    return pl.pallas_call(
        _inproj_kernel,
        grid=(nt, N_MAIN // TN_IN),
        in_specs=[pl.BlockSpec((TM_IN, D_MODEL), lambda i, n: (jnp.minimum(i, nctx - 1), 0)),
                  pl.BlockSpec((TM_IN, D_MODEL), lambda i, n: (jnp.maximum(i - nctx, 0), 0)),
                  pl.BlockSpec((None, 1, 6 * D_MODEL), lambda i, n: (_cond_row(i, TM_IN), 0, 0)),
                  pl.BlockSpec((1, D_MODEL), lambda i, n: (0, 0)),
                  pl.BlockSpec((D_MODEL, TN_IN), lambda i, n: (0, n)),
                  pl.BlockSpec((D_MODEL, LANES), lambda i, n: (0, 0))],
        out_specs=[pl.BlockSpec((TM_IN, TN_IN), lambda i, n: (i, n)),
                   pl.BlockSpec((TM_IN, LANES), lambda i, n: (i, 0))],
        out_shape=[jax.ShapeDtypeStruct((T_ALL, N_MAIN), F32),
                   jax.ShapeDtypeStruct((T_ALL, LANES), F32)],
        scratch_shapes=[pltpu.VMEM((TM_IN, D_MODEL), BF16)],
        compiler_params=_cparams(("arbitrary", "arbitrary")),
        name="inproj",
    )(xp, xs, mod3, g_attn, w_main, w_r)


SB = 256
CPB = SB // CHUNK


def _gla_kernel(q_ref, k_ref, v_ref, og_ref, r_ref, wgh_ref, wgl_ref, bg_ref, ng_ref,
                tlu_ref, s0f_ref, s0b_ref,
                o_ref, sf_ref, sb_ref,
                cum_s, last_s, qe_s, ke_s, kd_s, dec_s, kv_s, sbs_s, *, seq_len):
    n_chunks = seq_len // CHUNK
    n_sb = seq_len // SB
    lane = lax.broadcasted_iota(I32, (1, LANES), 1)
    m_f = lane < DK

    r_hi, r_lo = _split2(r_ref[...])
    z = _dot(r_hi, wgh_ref[...]) + _dot(r_lo, wgh_ref[...]) + _dot(r_hi, wgl_ref[...]) + bg_ref[...]
    g_all = (jnp.minimum(z, 0.0) - jnp.log(1.0 + jnp.exp(-jnp.abs(z)))) * (1.0 / TAU)

    q_pair = q_ref[...]
    k_pair = k_ref[...]
    q_roll = pltpu.roll(q_pair, DK, axis=1)
    k_roll = pltpu.roll(k_pair, DK, axis=1)

    row_b = lax.broadcasted_iota(I32, (SB, SB), 0)
    col_b = lax.broadcasted_iota(I32, (SB, SB), 1)
    same_chunk = (row_b // CHUNK) == (col_b // CHUNK)
    tril = same_chunk & (row_b >= col_b)
    triu = same_chunk & (row_b <= col_b)
    row_chunk = lax.broadcasted_iota(I32, (SB, 1), 0) // CHUNK
    col_chunk = lax.broadcasted_iota(I32, (1, SB), 1) // CHUNK

    m_f2 = (lax.broadcasted_iota(I32, (1, 2 * LANES), 1) % LANES) < DK
    cum_s[...] = g_all

    def cum_body(s, carry):
        rows = pl.ds(pl.multiple_of(s * SB, SB), SB)
        g = cum_s[rows, :]
        f_hi, f_lo = _split2(jnp.where(m_f2, g, 0.0))
        b_hi, b_lo = _split2(jnp.where(m_f2, 0.0, g))
        cum = (_dot(tlu_ref[...], jnp.concatenate([f_hi, b_hi], axis=0))
               + _dot(tlu_ref[...], jnp.concatenate([f_lo, b_lo], axis=0)))
        cum_s[rows, :] = cum
        tots = []
        for c in range(CPB):
            tot = jnp.where(m_f2, cum[(c + 1) * CHUNK - 1:(c + 1) * CHUNK, :], cum[c * CHUNK:c * CHUNK + 1, :])
            dec_s[s * CPB + c] = jnp.exp(tot)
            tots.append(jnp.broadcast_to(tot, (CHUNK, 2 * LANES)))
        last_s[rows, :] = jnp.concatenate(tots, axis=0)
        return carry

    lax.fori_loop(0, n_sb, cum_body, 0)

    for j in range(2):
        cum = cum_s[:, j * LANES:(j + 1) * LANES]
        last = last_s[:, j * LANES:(j + 1) * LANES]
        if j == 0:
            q2 = jnp.where(m_f, q_pair, q_roll)
            k2 = jnp.where(m_f, k_pair, k_roll)
        else:
            q2 = jnp.where(m_f, q_roll, q_pair)
            k2 = jnp.where(m_f, k_roll, k_pair)
        qe_s[...] = ((q2 * (DK ** -0.5)) * jnp.exp(cum)).astype(BF16)
        ke_s[...] = (k2 * jnp.exp(-cum)).astype(BF16)
        kd_s[...] = (k2 * jnp.exp(last - cum)).astype(BF16)

        lo = j * DV
        ng = ng_ref[:, lo:lo + DV]

        def kv_body(s, carry):
            rows = pl.ds(pl.multiple_of(s * SB, SB), SB)
            v_t = jnp.transpose(v_ref[rows, lo:lo + DV]).astype(BF16)
            zero = jnp.zeros_like(v_t)
            v_st = jnp.concatenate([jnp.where(col_chunk == c, v_t, zero) for c in range(CPB)], axis=0)
            kv = _dot(v_st, kd_s[rows, :])
            for c in range(CPB):
                kv_s[s * CPB + c] = kv[c * LANES:(c + 1) * LANES, :]
            return carry

        lax.fori_loop(0, n_sb, kv_body, 0)

        def dec_row(c):
            return dec_s[c][:, lo:lo + LANES]

        st0 = jnp.transpose(jnp.concatenate([s0f_ref[j], s0b_ref[j]], axis=0))

        def bwd_body(t, st_b):
            c = n_chunks - 1 - t
            sbs_s[c] = st_b
            return st_b * dec_row(c) + kv_s[c]

        st_b_fin = lax.fori_loop(0, n_chunks, bwd_body, st0, unroll=CPB)

        def fwd_body(s, st_f):
            rows = pl.ds(pl.multiple_of(s * SB, SB), SB)
            qe = qe_s[rows, :]
            ke = ke_s[rows, :]
            v_b = v_ref[rows, lo:lo + DV].astype(BF16)
            zero = jnp.zeros_like(qe)
            st = st_f
            q_parts, s_parts = [], []
            for c in range(CPB):
                ci = s * CPB + c
                s_parts.append(jnp.where(m_f, st, sbs_s[ci]).astype(BF16))
                q_parts.append(jnp.where(row_chunk == c, qe, zero))
                st = st * dec_row(ci) + kv_s[ci]
            o = _dot_nt(jnp.concatenate(q_parts, axis=1), jnp.concatenate(s_parts, axis=1))
            q_st = jnp.concatenate([jnp.where(m_f, qe, zero), jnp.where(m_f, zero, qe)], axis=0)
            sc = _dot_nt(q_st, ke)
            p = jnp.where(tril, sc[0:SB, :], 0.0) + jnp.where(triu, sc[SB:2 * SB, :], 0.0)
            o = o + _dot(p.astype(BF16), v_b)
            o = o * lax.rsqrt(jnp.mean(o * o, axis=-1, keepdims=True) + EPS) * ng
            o_ref[rows, lo:lo + DV] = o * _silu(og_ref[rows, lo:lo + DV])
            return st

        st_f_fin = lax.fori_loop(0, n_sb, fwd_body, st0)
        sf_ref[j] = jnp.transpose(st_f_fin)[0:DK, :]
        sb_ref[j] = jnp.transpose(st_b_fin)[DK:2 * DK, :]


def _gla(proj, r_all, wg_hi, wg_lo, bg, ng, tlu, s0f, s0b, *, seq_len, n_seq, row_block0):
    kern = functools.partial(_gla_kernel, seq_len=seq_len)
    n_chunks = seq_len // CHUNK
    rb = lambda b: b + row_block0
    return pl.pallas_call(
        kern,
        grid=(n_seq, HEADS // 2),
        in_specs=[pl.BlockSpec((seq_len, LANES), lambda b, h: (rb(b), h)),
                  pl.BlockSpec((seq_len, LANES), lambda b, h: (rb(b), QK_W // LANES + h)),
                  pl.BlockSpec((seq_len, 2 * DV), lambda b, h: (rb(b), 2 * QK_W // (2 * DV) + h)),
                  pl.BlockSpec((seq_len, 2 * DV), lambda b, h: (rb(b), (2 * QK_W + V_W) // (2 * DV) + h)),
                  pl.BlockSpec((seq_len, LANES), lambda b, h: (rb(b), 0)),
                  pl.BlockSpec((None, LANES, 2 * LANES), lambda b, h: (h, 0, 0)),
                  pl.BlockSpec((None, LANES, 2 * LANES), lambda b, h: (h, 0, 0)),
                  pl.BlockSpec((None, 1, 2 * LANES), lambda b, h: (h, 0, 0)),
                  pl.BlockSpec((None, 1, 2 * DV), lambda b, h: (h, 0, 0)),
                  pl.BlockSpec((SB, 2 * SB), lambda b, h: (0, 0)),
                  pl.BlockSpec((None, 2, DK, DV), lambda b, h: (b, h, 0, 0)),
                  pl.BlockSpec((None, 2, DK, DV), lambda b, h: (b, h, 0, 0))],
        out_specs=[pl.BlockSpec((seq_len, 2 * DV), lambda b, h: (b, h)),
                   pl.BlockSpec((None, 2, DK, DV), lambda b, h: (b, h, 0, 0)),
                   pl.BlockSpec((None, 2, DK, DV), lambda b, h: (b, h, 0, 0))],
        out_shape=[jax.ShapeDtypeStruct((n_seq * seq_len, V_W), F32),
                   jax.ShapeDtypeStruct((n_seq, HEADS, DK, DV), F32),
                   jax.ShapeDtypeStruct((n_seq, HEADS, DK, DV), F32)],
        scratch_shapes=[pltpu.VMEM((seq_len, 2 * LANES), F32),
                        pltpu.VMEM((seq_len, 2 * LANES), F32),
                        pltpu.VMEM((seq_len, LANES), BF16),
                        pltpu.VMEM((seq_len, LANES), BF16),
                        pltpu.VMEM((seq_len, LANES), BF16),
                        pltpu.VMEM((n_chunks, 1, 2 * LANES), F32),
                        pltpu.VMEM((n_chunks, DV, LANES), F32),
                        pltpu.VMEM((n_chunks, DV, LANES), F32)],
        compiler_params=_cparams(("arbitrary", "arbitrary")),
        name="gla_%d" % seq_len,
    )(proj, proj, proj, proj, r_all, wg_hi, wg_lo, bg, ng, tlu, s0f, s0b)


def _fnet_stage_a(u_bf, cs):
    cparts, sparts = [], []
    for g in range(FN_G):
        ab = _dot(u_bf[:, g * FN_C:(g + 1) * FN_C], cs)
        cparts.append(ab[:, 0:FN_C])
        sparts.append(ab[:, FN_C:2 * FN_C])
    return jnp.concatenate(cparts, axis=1), jnp.concatenate(sparts, axis=1)


def _fnet_ctx_kernel(u_ref, cs_ref, p2_ref, f_ref):
    uc, us = _fnet_stage_a(u_ref[...].astype(BF16), cs_ref[...].astype(BF16))
    ab = jnp.concatenate([uc, us], axis=0).astype(BF16)
    f_ref[...] = _dot(p2_ref[...].astype(BF16), ab)


def _fnet_ctx(proj, cs, p2):
    return pl.pallas_call(
        _fnet_ctx_kernel,
        grid=(N_CTX,),
        in_specs=[pl.BlockSpec((L_CTX, FN_G * FN_C), lambda b: (b, 3)),
                  pl.BlockSpec((FN_C, 2 * FN_C), lambda b: (0, 0)),
                  pl.BlockSpec((L_CTX, 2 * L_CTX), lambda b: (0, 0))],
        out_specs=pl.BlockSpec((L_CTX, FN_G * FN_C), lambda b: (b, 0)),
        out_shape=jax.ShapeDtypeStruct((T_CTX, FN_G * FN_C), F32),
        compiler_params=_cparams(("arbitrary",)),
        name="fnet_ctx",
    )(proj, cs, p2)


TM_FL = 256
RT_FL = 256


def _fnet_lat_kernel(u_ref, cs_ref, kr_ref, f_ref, ab_s):
    m = pl.program_id(1)

    @pl.when(m == 0)
    def _():
        def body(t, carry):
            rows = pl.ds(pl.multiple_of(t * RT_FL, RT_FL), RT_FL)
            uc, us = _fnet_stage_a(u_ref[rows, :].astype(BF16), cs_ref[...].astype(BF16))
            ab_s[rows, :] = uc.astype(BF16)
            ab_s[pl.ds(pl.multiple_of(L_LAT + t * RT_FL, RT_FL), RT_FL), :] = us.astype(BF16)
            return carry
        lax.fori_loop(0, L_LAT // RT_FL, body, 0)

    f_ref[...] = _dot(kr_ref[...].astype(BF16), ab_s[...])


def _fnet_lat(proj, cs, kr):
    nm = L_LAT // TM_FL
    return pl.pallas_call(
        _fnet_lat_kernel,
        grid=(N_LAT, nm),
        in_specs=[pl.BlockSpec((L_LAT, FN_G * FN_C), lambda b, m: (T_CTX // L_LAT + b, 3)),
                  pl.BlockSpec((FN_C, 2 * FN_C), lambda b, m: (0, 0)),
                  pl.BlockSpec((TM_FL, 2 * L_LAT), lambda b, m: (m, 0))],
        out_specs=pl.BlockSpec((TM_FL, FN_G * FN_C), lambda b, m: (b * nm + m, 0)),
        out_shape=jax.ShapeDtypeStruct((T_LAT, FN_G * FN_C), F32),
        scratch_shapes=[pltpu.VMEM((2 * L_LAT, FN_G * FN_C), BF16)],
        compiler_params=_cparams(("arbitrary", "arbitrary")),
        name="fnet_lat",
    )(proj, cs, kr)


TM_OUT = 256
LANE_E0 = N_GROUPS


def _outproj_kernel(oc_ref, ol_ref, fc_ref, fl_ref, xp_ref, xs_ref, mod_ref, g_ref, wo_ref, wf_ref,
                    wrh_ref, wrl_ref, br_ref, tri_ref,
                    x1_ref, h2_ref, ri_ref, rw_ref, cnt_ref, carry_s):
    i = pl.program_id(0)
    is_ctx = i < T_CTX // TM_OUT

    @pl.when(i == 0)
    def _():
        carry_s[...] = jnp.zeros_like(carry_s)

    o = jnp.where(is_ctx, oc_ref[...], ol_ref[...]).astype(BF16)
    f = jnp.where(is_ctx, fc_ref[...], fl_ref[...]).astype(BF16)
    x = jnp.where(is_ctx, xp_ref[...], xs_ref[...])
    y = _dot(o, wo_ref[...]) + _dot(f, wf_ref[...])
    ga1 = mod_ref[:, 2 * D_MODEL:3 * D_MODEL]
    sh2 = mod_ref[:, 3 * D_MODEL:4 * D_MODEL]
    sc2 = mod_ref[:, 4 * D_MODEL:5 * D_MODEL]
    x1 = x + ga1 * y
    x1_ref[...] = x1
    h2 = _rms(x1, g_ref[...]) * (1.0 + sc2) + sh2
    h2_ref[...] = h2

    h_hi, h_lo = _split2(h2)
    lg_all = _dot(h_hi, wrh_ref[...]) + _dot(h_lo, wrh_ref[...]) + _dot(h_hi, wrl_ref[...]) + br_ref[...]

    lane_i = lax.broadcasted_iota(I32, (TM_OUT, LANES), 1)
    lane = lane_i.astype(F32)
    neg = jnp.float32(-jnp.inf)
    big = jnp.float32(LANES)
    lg = jnp.where(lane_i < N_GROUPS, lg_all, neg)
    gmax = jnp.max(lg, axis=1, keepdims=True)
    gsel = jnp.min(jnp.where(lg == gmax, lane, big), axis=1, keepdims=True)
    den = jnp.sum(jnp.exp(lg - gmax), axis=1, keepdims=True)
    pg_sel = 1.0 / den

    e_idx = lane_i - LANE_E0
    egrp = (e_idx >> 3).astype(F32)
    emask = (e_idx >= 0) & (e_idx < N_EXP) & (egrp == gsel)
    m1 = jnp.where(emask, lg_all, neg)
    v1 = jnp.max(m1, axis=1, keepdims=True)
    i1 = jnp.min(jnp.where(m1 == v1, lane, big), axis=1, keepdims=True)
    m2 = jnp.where(lane == i1, neg, m1)
    v2 = jnp.max(m2, axis=1, keepdims=True)
    i2 = jnp.min(jnp.where(m2 == v2, lane, big), axis=1, keepdims=True)
    e2 = jnp.exp(v2 - v1)
    inv = 1.0 / (1.0 + e2)
    w1 = inv * pg_sel
    w2 = (e2 * inv) * pg_sel

    oh1 = lane == i1
    oh2 = lane == i2
    oh = jnp.where(oh1 | oh2, 1.0, 0.0)
    before = _dot(tri_ref[...], oh.astype(BF16)) + carry_s[...]
    r1 = jnp.sum(jnp.where(oh1, before, 0.0), axis=1, keepdims=True)
    r2 = jnp.sum(jnp.where(oh2, before, 0.0), axis=1, keepdims=True)
    carry = carry_s[...] + jnp.sum(oh, axis=0, keepdims=True)
    carry_s[...] = carry
    cnt_ref[...] = carry.astype(I32)

    ri = jnp.where(lane_i == 0, i1 - LANE_E0,
                   jnp.where(lane_i == 1, i2 - LANE_E0,
                             jnp.where(lane_i == 2, r1, jnp.where(lane_i == 3, r2, 0.0))))
    rw = jnp.where(lane_i == 0, w1, jnp.where(lane_i == 1, w2, 0.0))
    ri_ref[...] = ri[:, 0:8].astype(I32)
    rw_ref[...] = rw[:, 0:8]


def _outproj(o_ctx, o_lat, f_ctx, f_lat, xp, xs, mod3, g_ffn, wo, wf, wr_hi, wr_lo, br, tri):
    nt = T_ALL // TM_OUT
    nctx = T_CTX // TM_OUT
    ctx_map = lambda i: (jnp.minimum(i, nctx - 1), 0)
    lat_map = lambda i: (jnp.maximum(i - nctx, 0), 0)
    const = lambda i: (0, 0)
    return pl.pallas_call(
        _outproj_kernel,
        grid=(nt,),
        in_specs=[pl.BlockSpec((TM_OUT, V_W), ctx_map),
                  pl.BlockSpec((TM_OUT, V_W), lat_map),
                  pl.BlockSpec((TM_OUT, FN_G * FN_C), ctx_map),
                  pl.BlockSpec((TM_OUT, FN_G * FN_C), lat_map),
                  pl.BlockSpec((TM_OUT, D_MODEL), ctx_map),
                  pl.BlockSpec((TM_OUT, D_MODEL), lat_map),
                  pl.BlockSpec((None, 1, 6 * D_MODEL), lambda i: (_cond_row(i, TM_OUT), 0, 0)),
                  pl.BlockSpec((1, D_MODEL), const),
                  pl.BlockSpec((V_W, D_MODEL), const),
                  pl.BlockSpec((FN_G * FN_C, D_MODEL), const),
                  pl.BlockSpec((D_MODEL, LANES), const),
                  pl.BlockSpec((D_MODEL, LANES), const),
                  pl.BlockSpec((1, LANES), const),
                  pl.BlockSpec((TM_OUT, TM_OUT), const)],
        out_specs=[pl.BlockSpec((TM_OUT, D_MODEL), lambda i: (i, 0)),
                   pl.BlockSpec((TM_OUT, D_MODEL), lambda i: (i, 0)),
                   pl.BlockSpec((TM_OUT, 8), lambda i: (i, 0)),
                   pl.BlockSpec((TM_OUT, 8), lambda i: (i, 0)),
                   pl.BlockSpec((1, LANES), const)],
        out_shape=[jax.ShapeDtypeStruct((T_ALL, D_MODEL), F32),
                   jax.ShapeDtypeStruct((T_ALL, D_MODEL), F32),
                   jax.ShapeDtypeStruct((T_ALL, 8), I32),
                   jax.ShapeDtypeStruct((T_ALL, 8), F32),
                   jax.ShapeDtypeStruct((1, LANES), I32)],
        scratch_shapes=[pltpu.VMEM((1, LANES), F32)],
        compiler_params=_cparams(("arbitrary",)),
        name="outproj",
    )(o_ctx, o_lat, f_ctx, f_lat, xp, xs, mod3, g_ffn, wo, wf, wr_hi, wr_lo, br, tri)


TOK_BITS = 14
TOK_MASK = (1 << TOK_BITS) - 1
Y_SLOTS = 3
N_UP_CHUNKS = 4
N_DN_CHUNKS = 8
Y_ROWS = 2 * T_ALL + 2 * TM_MOE


def _moe_kernel(texp_ref, meta_ref, code_ref,
                h2_hbm, wg_ref, wu_ref, wd_ref,
                out_hbm,
                xbuf, ybuf, wg_s, wu_s, wd_s, gsem, ssem):
    i = pl.program_id(0)
    nt = meta_ref[0]
    xs = i % 2

    def gather_row(tile, sl, j):
        tok = code_ref[(tile + 2) * TM_MOE + j] & TOK_MASK
        pltpu.make_async_copy(h2_hbm.at[pl.ds(tok, 1)], xbuf.at[sl, pl.ds(j, 1)], gsem.at[sl]).start()

    def scatter_row(tile, sl, j):
        dst = code_ref[(tile + 2) * TM_MOE + j] >> TOK_BITS
        pltpu.make_async_copy(ybuf.at[sl, pl.ds(j, 1)], out_hbm.at[pl.ds(dst, 1)], ssem.at[sl]).start()

    def gather_wait(sl):
        pltpu.make_async_copy(h2_hbm.at[pl.ds(0, TM_MOE)], xbuf.at[sl], gsem.at[sl]).wait()

    def scatter_wait(sl):
        pltpu.make_async_copy(ybuf.at[sl], out_hbm.at[pl.ds(0, TM_MOE)], ssem.at[sl]).wait()

    def y_slot(tile):
        return (tile + 1) % Y_SLOTS

    @pl.when(i == 0)
    def _():
        ybuf[y_slot(-2)] = jnp.zeros((TM_MOE, D_MODEL), F32)
        ybuf[y_slot(-1)] = jnp.zeros((TM_MOE, D_MODEL), F32)

        def body(j, c):
            gather_row(0, 0, j)
            scatter_row(-2, y_slot(-2), j)
            return c
        lax.fori_loop(0, TM_MOE, body, 0)

    @pl.when((i >= 1) & (i <= nt))
    def _():
        scatter_wait(y_slot(i - 3))

    @pl.when(i < nt)
    def _():
        prev = texp_ref[jnp.maximum(i - 1, 0)]

        @pl.when((i == 0) | (texp_ref[i] != prev))
        def _():
            wg_s[...] = wg_ref[...].astype(BF16)
            wu_s[...] = wu_ref[...].astype(BF16)
            wd_s[...] = wd_ref[...].astype(BF16)

        gather_wait(xs)
        x = xbuf[xs].astype(BF16)

        issues = []
        for j in range(TM_MOE):
            issues.append(functools.partial(gather_row, i + 1, 1 - xs, j))
            issues.append(functools.partial(scatter_row, i - 1, y_slot(i - 1), j))
        n_groups = N_UP_CHUNKS + N_DN_CHUNKS
        per_group = -(-len(issues) // n_groups)

        def issue_group(k):
            for fn in issues[k * per_group:(k + 1) * per_group]:
                fn()

        wu_c = D_EXP // N_UP_CHUNKS
        hid = []
        for n in range(N_UP_CHUNKS):
            issue_group(n)
            g = _dot(x, wg_s[:, n * wu_c:(n + 1) * wu_c])
            u = _dot(x, wu_s[:, n * wu_c:(n + 1) * wu_c])
            hid.append((_silu(g) * u).astype(BF16))
        hid = jnp.concatenate(hid, axis=1)
        wd_c = D_MODEL // N_DN_CHUNKS
        ys = y_slot(i)
        for n in range(N_DN_CHUNKS):
            issue_group(N_UP_CHUNKS + n)
            ybuf[ys, :, n * wd_c:(n + 1) * wd_c] = _dot(hid, wd_s[:, n * wd_c:(n + 1) * wd_c])

    @pl.when(i == nt)
    def _():
        gather_wait(xs)

        def body(j, c):
            scatter_row(nt - 1, y_slot(nt - 1), j)
            return c
        lax.fori_loop(0, TM_MOE, body, 0)
        scatter_wait(y_slot(nt - 2))
        scatter_wait(y_slot(nt - 1))


def _moe(texp, meta, code, h2, w_eg, w_eu, w_ed):
    wmap = lambda i, te, me, co: (te[i], 0, 0)
    grid_spec = pltpu.PrefetchScalarGridSpec(
        num_scalar_prefetch=3,
        grid=(NT_MOE + 1,),
        in_specs=[pl.BlockSpec(memory_space=pl.ANY),
                  pl.BlockSpec((None, D_MODEL, D_EXP), wmap),
                  pl.BlockSpec((None, D_MODEL, D_EXP), wmap),
                  pl.BlockSpec((None, D_EXP, D_MODEL), wmap)],
        out_specs=pl.BlockSpec(memory_space=pl.ANY),
        scratch_shapes=[pltpu.VMEM((2, TM_MOE, D_MODEL), F32),
                        pltpu.VMEM((Y_SLOTS, TM_MOE, D_MODEL), F32),
                        pltpu.VMEM((D_MODEL, D_EXP), BF16),
                        pltpu.VMEM((D_MODEL, D_EXP), BF16),
                        pltpu.VMEM((D_EXP, D_MODEL), BF16),
                        pltpu.SemaphoreType.DMA((2,)),
                        pltpu.SemaphoreType.DMA((Y_SLOTS,))])
    return pl.pallas_call(
        _moe_kernel,
        grid_spec=grid_spec,
        out_shape=jax.ShapeDtypeStruct((Y_ROWS, D_MODEL), F32),
        compiler_params=_cparams(("arbitrary",)),
        name="moe",
    )(texp, meta, code, h2, w_eg, w_eu, w_ed)


TM_FIN = 256


def _final_kernel(x1_ref, y0_ref, y1_ref, rw_ref, mod_ref, g_ref, yp_ref, ys_ref):
    i = pl.program_id(0)
    ga2 = mod_ref[:, 5 * D_MODEL:6 * D_MODEL]
    w0 = rw_ref[:, 0:1]
    w1 = rw_ref[:, 1:2]
    y = x1_ref[...] + ga2 * (w0 * y0_ref[...] + w1 * y1_ref[...])
    out = _rms(y, g_ref[...])

    @pl.when(i < T_CTX // TM_FIN)
    def _():
        yp_ref[...] = out

    @pl.when(i >= T_CTX // TM_FIN)
    def _():
        ys_ref[...] = out


def _final(x1, y2, rw, mod3, g_fin):
    nt = T_ALL // TM_FIN
    nctx = T_CTX // TM_FIN
    return pl.pallas_call(
        _final_kernel,
        grid=(nt,),
        in_specs=[pl.BlockSpec((TM_FIN, D_MODEL), lambda i: (i, 0)),
                  pl.BlockSpec((TM_FIN, D_MODEL), lambda i: (i, 0)),
                  pl.BlockSpec((TM_FIN, D_MODEL), lambda i: (nt + i, 0)),
                  pl.BlockSpec((TM_FIN, 8), lambda i: (i, 0)),
                  pl.BlockSpec((None, 1, 6 * D_MODEL), lambda i: (_cond_row(i, TM_FIN), 0, 0)),
                  pl.BlockSpec((1, D_MODEL), lambda i: (0, 0))],
        out_specs=[pl.BlockSpec((TM_FIN, D_MODEL), lambda i: (jnp.minimum(i, nctx - 1), 0)),
                   pl.BlockSpec((TM_FIN, D_MODEL), lambda i: (jnp.maximum(i - nctx, 0), 0))],
        out_shape=[jax.ShapeDtypeStruct((T_CTX, D_MODEL), F32),
                   jax.ShapeDtypeStruct((T_LAT, D_MODEL), F32)],
        compiler_params=_cparams(("arbitrary",)),
        name="final",
    )(x1, y2, y2, rw, mod3, g_fin)


def _np_bf16(a):
    return jnp.asarray(np.asarray(a, np.float32), dtype=BF16)


def _np_f32(a):
    return jnp.asarray(np.asarray(a, np.float32))


@functools.lru_cache(maxsize=None)
def _constants():
    c = {}
    k = np.arange(FN_C)
    ang = 2.0 * np.pi * np.outer(k, k) / FN_C
    c["cs"] = np.concatenate([np.cos(ang), np.sin(ang)], axis=1) / np.sqrt(FN_C)
    p = np.arange(L_CTX)
    ang = 2.0 * np.pi * np.outer(p, p) / L_CTX
    c["p2"] = np.concatenate([np.cos(ang), -np.sin(ang)], axis=1) / np.sqrt(L_CTX)
    pos = np.arange(L_LAT)
    rr, cc = pos // GRID_W, pos % GRID_W
    num = (np.outer(rr, rr) * (GRID_W // GRID_H) + np.outer(cc, cc)) % GRID_W
    ang = 2.0 * np.pi * num / GRID_W
    c["kr"] = np.concatenate([np.cos(ang), -np.sin(ang)], axis=1) / np.sqrt(L_LAT)
    i = np.arange(SB)
    same = (i[:, None] // CHUNK) == (i[None, :] // CHUNK)
    tl = same & (i[:, None] >= i[None, :])
    tu = same & (i[:, None] <= i[None, :])
    c["tlu"] = np.concatenate([tl, tu], axis=1).astype(np.float32)
    c["tri"] = (i[:, None] > i[None, :]).astype(np.float32)
    return c


def kernel(x_prompt, x_sample, state_gla_fwd, state_gla_bwd, c, c_ctx, w_ada, b_ada, norm_attn, norm_ffn, w_in, w_gate_fwd, b_gate_fwd, w_gate_bwd, b_gate_bwd, norm_gla, w_out, w_router_group, b_router_group, w_router_expert, b_router_expert, w_expert_gate, w_expert_up, w_expert_down, norm_final):
    assert w_ada.shape[0] == 1, "single layer"
    cst = _constants()
    cs, p2, kr = _np_f32(cst["cs"]), _np_f32(cst["p2"]), _np_f32(cst["kr"])
    tlu, tri = _np_bf16(cst["tlu"]), _np_bf16(cst["tri"])

    xp = x_prompt.reshape(T_CTX, D_MODEL)
    xs = x_sample.reshape(T_LAT, D_MODEL)

    cond8 = jnp.concatenate([c_ctx[None, :], c, jnp.zeros((3, D_MODEL), F32)], axis=0)
    mod = _ada(cond8, w_ada[0], b_ada[0][None, :])
    mod3 = mod.reshape(8, 1, 6 * D_MODEL)

    wi = w_in[0]
    i_og = 2 * QK_W + 2 * V_W
    i_u = i_og + 2 * RANK
    w_main = jnp.concatenate([wi[:, :i_og], wi[:, i_u:]], axis=1).astype(BF16)
    w_r = jnp.pad(wi[:, i_og:i_u], ((0, 0), (0, LANES - 2 * RANK))).astype(BF16)

    wgf = w_gate_fwd[0].reshape(RANK, HEADS, DK)
    wgb = w_gate_bwd[0].reshape(RANK, HEADS, DK)
    zf = jnp.zeros_like(wgf)
    top = jnp.stack([wgf, zf], axis=2)
    bot = jnp.stack([zf, wgb], axis=2)
    wg = jnp.concatenate([top, bot], axis=0)
    wg = wg.reshape(2 * RANK, HEADS // 2, 4 * DK).transpose(1, 0, 2)
    wg = jnp.pad(wg, ((0, 0), (0, LANES - 2 * RANK), (0, 0)))
    wg_hi = wg.astype(BF16)
    wg_lo = (wg - wg_hi.astype(F32)).astype(BF16)
    bg = jnp.stack([b_gate_fwd[0].reshape(HEADS, DK), b_gate_bwd[0].reshape(HEADS, DK)], axis=1)
    bg = bg.reshape(HEADS // 2, 1, 4 * DK)
    ng = norm_gla[0].reshape(HEADS // 2, 1, 2 * DV)

    proj, r_all = _inproj(xp, xs, mod3, norm_attn, w_main, w_r)

    zero_state = jnp.zeros((N_CTX, HEADS, DK, DV), F32)
    o_ctx, sf_ctx, sb_ctx = _gla(proj, r_all, wg_hi, wg_lo, bg, ng, tlu, zero_state, zero_state,
                                 seq_len=L_CTX, n_seq=N_CTX, row_block0=0)
    o_lat, _, _ = _gla(proj, r_all, wg_hi, wg_lo, bg, ng, tlu,
                       state_gla_fwd[:, 0], state_gla_bwd[:, 0],
                       seq_len=L_LAT, n_seq=N_LAT, row_block0=T_CTX // L_LAT)

    f_ctx = _fnet_ctx(proj, cs, p2)
    f_lat = _fnet_lat(proj, cs, kr)

    wo = w_out[0][:V_W].astype(BF16)
    wf = w_out[0][V_W:].astype(BF16)
    wr = jnp.concatenate([w_router_group[0], w_router_expert[0]], axis=1)
    wr = jnp.pad(wr, ((0, 0), (0, LANES - N_GROUPS - N_EXP)))
    wr_hi = wr.astype(BF16)
    wr_lo = (wr - wr_hi.astype(F32)).astype(BF16)
    br = jnp.pad(jnp.concatenate([b_router_group[0], b_router_expert[0]]), (0, LANES - N_GROUPS - N_EXP))[None, :]

    x1, h2, ri, rw, cnt = _outproj(o_ctx, o_lat, f_ctx, f_lat, xp, xs, mod3, norm_ffn, wo, wf,
                                   wr_hi, wr_lo, br, tri)

    counts = cnt[0, LANE_E0:LANE_E0 + N_EXP]
    tiles_e = (counts + TM_MOE - 1) // TM_MOE
    tile_end = jnp.cumsum(tiles_e)
    tile_start = tile_end - tiles_e
    n_tiles = tile_end[-1]
    eid = ri[:, 0:2]
    pos = tile_start[eid] * TM_MOE + ri[:, 2:4]
    asg = jnp.full((P_MOE,), -1, I32).at[pos.reshape(-1)].set(jnp.arange(2 * T_ALL, dtype=I32))
    pad_tile = jnp.full((TM_MOE,), -1, I32)
    asg_ext = jnp.concatenate([pad_tile, pad_tile, asg, pad_tile])
    p_idx = jnp.arange((NT_MOE + 3) * TM_MOE, dtype=I32)
    spare = 2 * T_ALL + ((p_idx // TM_MOE) % 2) * TM_MOE + p_idx % TM_MOE
    tok = jnp.where(asg_ext >= 0, asg_ext >> 1, 0)
    dst = jnp.where(asg_ext >= 0, (asg_ext & 1) * T_ALL + (asg_ext >> 1), spare)
    code = (dst << TOK_BITS) | tok
    tidx = jnp.minimum(jnp.arange(NT_MOE + 1, dtype=I32), n_tiles - 1)
    texp = jnp.sum(tidx[:, None] >= tile_end[None, :], axis=1).astype(I32)
    meta = n_tiles.reshape(1).astype(I32)

    y2 = _moe(texp, meta, code, h2, w_expert_gate[0], w_expert_up[0], w_expert_down[0])
    y_prompt, y_sample = _final(x1, y2, rw, mod3, norm_final[None, :])

    st_shape = (N_CTX, 1, HEADS, DK, DV)
    return (y_prompt.reshape(N_CTX, L_CTX, D_MODEL), y_sample.reshape(N_LAT, L_LAT, D_MODEL),
            sf_ctx.reshape(st_shape), sb_ctx.reshape(st_shape))
```

```python
import functools

import numpy as np
import jax
import jax.numpy as jnp
from jax import lax
from jax.experimental import pallas as pl
from jax.experimental.pallas import tpu as pltpu

F32 = jnp.float32
BF16 = jnp.bfloat16
I32 = jnp.int32

D_MODEL = 2048
N_CTX = 32
L_CTX = 256
N_LAT = 4
L_LAT = 2048
GRID_H = 32
GRID_W = 64
T_CTX = N_CTX * L_CTX
T_LAT = N_LAT * L_LAT
T_ALL = T_CTX + T_LAT
HEADS = 8
DK = 64
DV = 128
RANK = 16
TAU = 16.0
CHUNK = 64
FN_G = 8
FN_C = 128
QK_W = HEADS * DK
V_W = HEADS * DV
N_GROUPS = 4
EPG = 8
N_EXP = N_GROUPS * EPG
D_EXP = 512
EPS = 1e-6

LANES = 128
VMEM_LIMIT = 56 * 1024 * 1024

TM_MOE = 256


def _dot(a, b):
    return jnp.dot(a, b, preferred_element_type=F32)


def _dot_nt(a, b):
    return lax.dot_general(a, b, (((1,), (1,)), ((), ())), preferred_element_type=F32)


def _split2(x):
    hi = x.astype(BF16)
    lo = (x - hi.astype(F32)).astype(BF16)
    return hi, lo


def _silu(x):
    return x * (1.0 / (1.0 + jnp.exp(-x)))


def _rms(x, g):
    return x * lax.rsqrt(jnp.mean(x * x, axis=-1, keepdims=True) + EPS) * g


def _cparams(sem):
    return pltpu.CompilerParams(dimension_semantics=sem, vmem_limit_bytes=VMEM_LIMIT)


def _ada_kernel(c_ref, w_ref, b_ref, o_ref):
    s_hi, s_lo = _split2(_silu(c_ref[...]))
    w = w_ref[...]
    w_hi = w.astype(BF16)
    w_lo = (w - w_hi.astype(F32)).astype(BF16)
    o_ref[...] = _dot(s_hi, w_hi) + _dot(s_lo, w_hi) + _dot(s_hi, w_lo) + b_ref[...]


def _ada(cond8, w_ada, b_ada):
    tn = 768
    n6 = 6 * D_MODEL
    return pl.pallas_call(
        _ada_kernel,
        grid=(n6 // tn,),
        in_specs=[pl.BlockSpec((8, D_MODEL), lambda j: (0, 0)),
                  pl.BlockSpec((D_MODEL, tn), lambda j: (0, j)),
                  pl.BlockSpec((1, tn), lambda j: (0, j))],
        out_specs=pl.BlockSpec((8, tn), lambda j: (0, j)),
        out_shape=jax.ShapeDtypeStruct((8, n6), F32),
        compiler_params=_cparams(("arbitrary",)),
        name="ada",
    )(cond8, w_ada, b_ada)


TM_IN = 512
TN_IN = 1024
N_MAIN = 4096


def _cond_row(tile, tm):
    ctx_tiles = T_CTX // tm
    per_seq = L_LAT // tm
    return jnp.where(tile < ctx_tiles, 0, 1 + (jnp.maximum(tile - ctx_tiles, 0)) // per_seq)


def _inproj_kernel(xp_ref, xs_ref, mod_ref, g_ref, w_ref, wr_ref, proj_ref, r_ref, hn_ref):
    i = pl.program_id(0)
    n = pl.program_id(1)

    @pl.when(n == 0)
    def _():
        x = jnp.where(i < T_CTX // TM_IN, xp_ref[...], xs_ref[...])
        sh1 = mod_ref[:, 0:D_MODEL]
        sc1 = mod_ref[:, D_MODEL:2 * D_MODEL]
        h = _rms(x, g_ref[...]) * (1.0 + sc1) + sh1
        hb = h.astype(BF16)
        hn_ref[...] = hb
        r_ref[...] = _dot(hb, wr_ref[...])

    proj_ref[...] = _dot(hn_ref[...], w_ref[...])


def _inproj(xp, xs, mod3, g_attn, w_main, w_r):
    nt = T_ALL // TM_IN
    nctx = T_CTX // TM_IN
    return pl.pallas_call(
        _inproj_kernel,
        grid=(nt, N_MAIN // TN_IN),
        in_specs=[pl.BlockSpec((TM_IN, D_MODEL), lambda i, n: (jnp.minimum(i, nctx - 1), 0)),
                  pl.BlockSpec((TM_IN, D_MODEL), lambda i, n: (jnp.maximum(i - nctx, 0), 0)),
                  pl.BlockSpec((None, 1, 6 * D_MODEL), lambda i, n: (_cond_row(i, TM_IN), 0, 0)),
                  pl.BlockSpec((1, D_MODEL), lambda i, n: (0, 0)),
                  pl.BlockSpec((D_MODEL, TN_IN), lambda i, n: (0, n)),
                  pl.BlockSpec((D_MODEL, LANES), lambda i, n: (0, 0))],
        out_specs=[pl.BlockSpec((TM_IN, TN_IN), lambda i, n: (i, n)),
                   pl.BlockSpec((TM_IN, LANES), lambda i, n: (i, 0))],
        out_shape=[jax.ShapeDtypeStruct((T_ALL, N_MAIN), F32),
                   jax.ShapeDtypeStruct((T_ALL, LANES), F32)],
        scratch_shapes=[pltpu.VMEM((TM_IN, D_MODEL), BF16)],
        compiler_params=_cparams(("arbitrary", "arbitrary")),
        name="inproj",
    )(xp, xs, mod3, g_attn, w_main, w_r)


SB = 256
CPB = SB // CHUNK


def _gla_kernel(q_ref, k_ref, v_ref, og_ref, r_ref, wgh_ref, wgl_ref, bg_ref, ng_ref,
                tlu_ref, s0f_ref, s0b_ref,
                o_ref, sf_ref, sb_ref,
                cum_s, last_s, qe_s, ke_s, kd_s, dec_s, kv_s, sbs_s, *, seq_len):
    n_chunks = seq_len // CHUNK
    n_sb = seq_len // SB
    lane = lax.broadcasted_iota(I32, (1, LANES), 1)
    m_f = lane < DK

    r_hi, r_lo = _split2(r_ref[...])
    z = _dot(r_hi, wgh_ref[...]) + _dot(r_lo, wgh_ref[...]) + _dot(r_hi, wgl_ref[...]) + bg_ref[...]
    g_all = (jnp.minimum(z, 0.0) - jnp.log(1.0 + jnp.exp(-jnp.abs(z)))) * (1.0 / TAU)

    q_pair = q_ref[...]
    k_pair = k_ref[...]
    q_roll = pltpu.roll(q_pair, DK, axis=1)
    k_roll = pltpu.roll(k_pair, DK, axis=1)

    row_b = lax.broadcasted_iota(I32, (SB, SB), 0)
    col_b = lax.broadcasted_iota(I32, (SB, SB), 1)
    same_chunk = (row_b // CHUNK) == (col_b // CHUNK)
    tril = same_chunk & (row_b >= col_b)
    triu = same_chunk & (row_b <= col_b)
    row_chunk = lax.broadcasted_iota(I32, (SB, 1), 0) // CHUNK
    col_chunk = lax.broadcasted_iota(I32, (1, SB), 1) // CHUNK

    m_f2 = (lax.broadcasted_iota(I32, (1, 2 * LANES), 1) % LANES) < DK
    cum_s[...] = g_all

    def cum_body(s, carry):
        rows = pl.ds(pl.multiple_of(s * SB, SB), SB)
        g = cum_s[rows, :]
        f_hi, f_lo = _split2(jnp.where(m_f2, g, 0.0))
        b_hi, b_lo = _split2(jnp.where(m_f2, 0.0, g))
        cum = (_dot(tlu_ref[...], jnp.concatenate([f_hi, b_hi], axis=0))
               + _dot(tlu_ref[...], jnp.concatenate([f_lo, b_lo], axis=0)))
        cum_s[rows, :] = cum
        tots = []
        for c in range(CPB):
            tot = jnp.where(m_f2, cum[(c + 1) * CHUNK - 1:(c + 1) * CHUNK, :], cum[c * CHUNK:c * CHUNK + 1, :])
            dec_s[s * CPB + c] = jnp.exp(tot)
            tots.append(jnp.broadcast_to(tot, (CHUNK, 2 * LANES)))
        last_s[rows, :] = jnp.concatenate(tots, axis=0)
        return carry

    lax.fori_loop(0, n_sb, cum_body, 0)

    for j in range(2):
        cum = cum_s[:, j * LANES:(j + 1) * LANES]
        last = last_s[:, j * LANES:(j + 1) * LANES]
        if j == 0:
            q2 = jnp.where(m_f, q_pair, q_roll)
            k2 = jnp.where(m_f, k_pair, k_roll)
        else:
            q2 = jnp.where(m_f, q_roll, q_pair)
            k2 = jnp.where(m_f, k_roll, k_pair)
        qe_s[...] = ((q2 * (DK ** -0.5)) * jnp.exp(cum)).astype(BF16)
        ke_s[...] = (k2 * jnp.exp(-cum)).astype(BF16)
        kd_s[...] = (k2 * jnp.exp(last - cum)).astype(BF16)

        lo = j * DV
        ng = ng_ref[:, lo:lo + DV]

        def kv_body(s, carry):
            rows = pl.ds(pl.multiple_of(s * SB, SB), SB)
            v_t = jnp.transpose(v_ref[rows, lo:lo + DV]).astype(BF16)
            zero = jnp.zeros_like(v_t)
            v_st = jnp.concatenate([jnp.where(col_chunk == c, v_t, zero) for c in range(CPB)], axis=0)
            kv = _dot(v_st, kd_s[rows, :])
            for c in range(CPB):
                kv_s[s * CPB + c] = kv[c * LANES:(c + 1) * LANES, :]
            return carry

        lax.fori_loop(0, n_sb, kv_body, 0)

        def dec_row(c):
            return dec_s[c][:, lo:lo + LANES]

        st0 = jnp.transpose(jnp.concatenate([s0f_ref[j], s0b_ref[j]], axis=0))

        def bwd_body(t, st_b):
            c = n_chunks - 1 - t
            sbs_s[c] = st_b
            return st_b * dec_row(c) + kv_s[c]

        st_b_fin = lax.fori_loop(0, n_chunks, bwd_body, st0, unroll=CPB)

        def fwd_body(s, st_f):
            rows = pl.ds(pl.multiple_of(s * SB, SB), SB)
            qe = qe_s[rows, :]
            ke = ke_s[rows, :]
            v_b = v_ref[rows, lo:lo + DV].astype(BF16)
            zero = jnp.zeros_like(qe)
            st = st_f
            q_parts, s_parts = [], []
            for c in range(CPB):
                ci = s * CPB + c
                s_parts.append(jnp.where(m_f, st, sbs_s[ci]).astype(BF16))
                q_parts.append(jnp.where(row_chunk == c, qe, zero))
                st = st * dec_row(ci) + kv_s[ci]
            o = _dot_nt(jnp.concatenate(q_parts, axis=1), jnp.concatenate(s_parts, axis=1))
            q_st = jnp.concatenate([jnp.where(m_f, qe, zero), jnp.where(m_f, zero, qe)], axis=0)
            sc = _dot_nt(q_st, ke)
            p = jnp.where(tril, sc[0:SB, :], 0.0) + jnp.where(triu, sc[SB:2 * SB, :], 0.0)
            o = o + _dot(p.astype(BF16), v_b)
            o = o * lax.rsqrt(jnp.mean(o * o, axis=-1, keepdims=True) + EPS) * ng
            o_ref[rows, lo:lo + DV] = o * _silu(og_ref[rows, lo:lo + DV])
            return st

        st_f_fin = lax.fori_loop(0, n_sb, fwd_body, st0)
        sf_ref[j] = jnp.transpose(st_f_fin)[0:DK, :]
        sb_ref[j] = jnp.transpose(st_b_fin)[DK:2 * DK, :]


def _gla(proj, r_all, wg_hi, wg_lo, bg, ng, tlu, s0f, s0b, *, seq_len, n_seq, row_block0):
    kern = functools.partial(_gla_kernel, seq_len=seq_len)
    n_chunks = seq_len // CHUNK
    rb = lambda b: b + row_block0
    return pl.pallas_call(
        kern,
        grid=(n_seq, HEADS // 2),
        in_specs=[pl.BlockSpec((seq_len, LANES), lambda b, h: (rb(b), h)),
                  pl.BlockSpec((seq_len, LANES), lambda b, h: (rb(b), QK_W // LANES + h)),
                  pl.BlockSpec((seq_len, 2 * DV), lambda b, h: (rb(b), 2 * QK_W // (2 * DV) + h)),
                  pl.BlockSpec((seq_len, 2 * DV), lambda b, h: (rb(b), (2 * QK_W + V_W) // (2 * DV) + h)),
                  pl.BlockSpec((seq_len, LANES), lambda b, h: (rb(b), 0)),
                  pl.BlockSpec((None, LANES, 2 * LANES), lambda b, h: (h, 0, 0)),
                  pl.BlockSpec((None, LANES, 2 * LANES), lambda b, h: (h, 0, 0)),
                  pl.BlockSpec((None, 1, 2 * LANES), lambda b, h: (h, 0, 0)),
                  pl.BlockSpec((None, 1, 2 * DV), lambda b, h: (h, 0, 0)),
                  pl.BlockSpec((SB, 2 * SB), lambda b, h: (0, 0)),
                  pl.BlockSpec((None, 2, DK, DV), lambda b, h: (b, h, 0, 0)),
                  pl.BlockSpec((None, 2, DK, DV), lambda b, h: (b, h, 0, 0))],
        out_specs=[pl.BlockSpec((seq_len, 2 * DV), lambda b, h: (b, h)),
                   pl.BlockSpec((None, 2, DK, DV), lambda b, h: (b, h, 0, 0)),
                   pl.BlockSpec((None, 2, DK, DV), lambda b, h: (b, h, 0, 0))],
        out_shape=[jax.ShapeDtypeStruct((n_seq * seq_len, V_W), F32),
                   jax.ShapeDtypeStruct((n_seq, HEADS, DK, DV), F32),
                   jax.ShapeDtypeStruct((n_seq, HEADS, DK, DV), F32)],
        scratch_shapes=[pltpu.VMEM((seq_len, 2 * LANES), F32),
                        pltpu.VMEM((seq_len, 2 * LANES), F32),
                        pltpu.VMEM((seq_len, LANES), BF16),
                        pltpu.VMEM((seq_len, LANES), BF16),
                        pltpu.VMEM((seq_len, LANES), BF16),
                        pltpu.VMEM((n_chunks, 1, 2 * LANES), F32),
                        pltpu.VMEM((n_chunks, DV, LANES), F32),
                        pltpu.VMEM((n_chunks, DV, LANES), F32)],
        compiler_params=_cparams(("arbitrary", "arbitrary")),
        name="gla_%d" % seq_len,
    )(proj, proj, proj, proj, r_all, wg_hi, wg_lo, bg, ng, tlu, s0f, s0b)


def _fnet_stage_a(u_bf, cs):
    cparts, sparts = [], []
    for g in range(FN_G):
        ab = _dot(u_bf[:, g * FN_C:(g + 1) * FN_C], cs)
        cparts.append(ab[:, 0:FN_C])
        sparts.append(ab[:, FN_C:2 * FN_C])
    return jnp.concatenate(cparts, axis=1), jnp.concatenate(sparts, axis=1)


def _fnet_ctx_kernel(u_ref, cs_ref, p2_ref, f_ref):
    uc, us = _fnet_stage_a(u_ref[...].astype(BF16), cs_ref[...].astype(BF16))
    ab = jnp.concatenate([uc, us], axis=0).astype(BF16)
    f_ref[...] = _dot(p2_ref[...].astype(BF16), ab)


def _fnet_ctx(proj, cs, p2):
    return pl.pallas_call(
        _fnet_ctx_kernel,
        grid=(N_CTX,),
        in_specs=[pl.BlockSpec((L_CTX, FN_G * FN_C), lambda b: (b, 3)),
                  pl.BlockSpec((FN_C, 2 * FN_C), lambda b: (0, 0)),
                  pl.BlockSpec((L_CTX, 2 * L_CTX), lambda b: (0, 0))],
        out_specs=pl.BlockSpec((L_CTX, FN_G * FN_C), lambda b: (b, 0)),
        out_shape=jax.ShapeDtypeStruct((T_CTX, FN_G * FN_C), F32),
        compiler_params=_cparams(("arbitrary",)),
        name="fnet_ctx",
    )(proj, cs, p2)


TM_FL = 256
RT_FL = 256


def _fnet_lat_kernel(u_ref, cs_ref, kr_ref, f_ref, ab_s):
    m = pl.program_id(1)

    @pl.when(m == 0)
    def _():
        def body(t, carry):
            rows = pl.ds(pl.multiple_of(t * RT_FL, RT_FL), RT_FL)
            uc, us = _fnet_stage_a(u_ref[rows, :].astype(BF16), cs_ref[...].astype(BF16))
            ab_s[rows, :] = uc.astype(BF16)
            ab_s[pl.ds(pl.multiple_of(L_LAT + t * RT_FL, RT_FL), RT_FL), :] = us.astype(BF16)
            return carry
        lax.fori_loop(0, L_LAT // RT_FL, body, 0)

    f_ref[...] = _dot(kr_ref[...].astype(BF16), ab_s[...])


def _fnet_lat(proj, cs, kr):
    nm = L_LAT // TM_FL
    return pl.pallas_call(
        _fnet_lat_kernel,
        grid=(N_LAT, nm),
        in_specs=[pl.BlockSpec((L_LAT, FN_G * FN_C), lambda b, m: (T_CTX // L_LAT + b, 3)),
                  pl.BlockSpec((FN_C, 2 * FN_C), lambda b, m: (0, 0)),
                  pl.BlockSpec((TM_FL, 2 * L_LAT), lambda b, m: (m, 0))],
        out_specs=pl.BlockSpec((TM_FL, FN_G * FN_C), lambda b, m: (b * nm + m, 0)),
        out_shape=jax.ShapeDtypeStruct((T_LAT, FN_G * FN_C), F32),
        scratch_shapes=[pltpu.VMEM((2 * L_LAT, FN_G * FN_C), BF16)],
        compiler_params=_cparams(("arbitrary", "arbitrary")),
        name="fnet_lat",
    )(proj, cs, kr)


TM_OUT = 256
LANE_E0 = N_GROUPS
ROWS_PER_BLK = 8
PACK_ROWS = -(-(2 * TM_OUT + N_EXP * (ROWS_PER_BLK - 1)) // 256) * 256
PACK_BLKS = PACK_ROWS // ROWS_PER_BLK
N_TOK_TILES = T_ALL // TM_OUT
BLK_PER_TILE = TM_MOE // ROWS_PER_BLK
USED_BLKS = (2 * TM_OUT + N_EXP * (ROWS_PER_BLK - 1)) // ROWS_PER_BLK
HS_ROWS = N_TOK_TILES * PACK_ROWS
assert USED_BLKS < PACK_BLKS and 2 * BLK_PER_TILE <= N_TOK_TILES


def _outproj_kernel(oc_ref, ol_ref, fc_ref, fl_ref, xp_ref, xs_ref, mod_ref, g_ref, wo_ref, wf_ref,
                    wrh_ref, wrl_ref, br_ref, tri_ref, su_ref,
                    x1_ref, hs_ref, rw_ref, nb_ref, lb_ref):
    i = pl.program_id(0)
    is_ctx = i < T_CTX // TM_OUT

    o = jnp.where(is_ctx, oc_ref[...], ol_ref[...]).astype(BF16)
    f = jnp.where(is_ctx, fc_ref[...], fl_ref[...]).astype(BF16)
    x = jnp.where(is_ctx, xp_ref[...], xs_ref[...])
    y = _dot(o, wo_ref[...]) + _dot(f, wf_ref[...])
    ga1 = mod_ref[:, 2 * D_MODEL:3 * D_MODEL]
    sh2 = mod_ref[:, 3 * D_MODEL:4 * D_MODEL]
    sc2 = mod_ref[:, 4 * D_MODEL:5 * D_MODEL]
    x1 = x + ga1 * y
    x1_ref[...] = x1
    h2 = _rms(x1, g_ref[...]) * (1.0 + sc2) + sh2

    h_hi, h_lo = _split2(h2)
    lg_all = _dot(h_hi, wrh_ref[...]) + _dot(h_lo, wrh_ref[...]) + _dot(h_hi, wrl_ref[...]) + br_ref[...]

    lane_i = lax.broadcasted_iota(I32, (TM_OUT, LANES), 1)
    lane = lane_i.astype(F32)
    neg = jnp.float32(-jnp.inf)
    big = jnp.float32(LANES)
    lg = jnp.where(lane_i < N_GROUPS, lg_all, neg)
    gmax = jnp.max(lg, axis=1, keepdims=True)
    gsel = jnp.min(jnp.where(lg == gmax, lane, big), axis=1, keepdims=True)
    den = jnp.sum(jnp.exp(lg - gmax), axis=1, keepdims=True)
    pg_sel = 1.0 / den

    e_idx = lane_i - LANE_E0
    egrp = (e_idx >> 3).astype(F32)
    emask = (e_idx >= 0) & (e_idx < N_EXP) & (egrp == gsel)
    m1 = jnp.where(emask, lg_all, neg)
    v1 = jnp.max(m1, axis=1, keepdims=True)
    i1 = jnp.min(jnp.where(m1 == v1, lane, big), axis=1, keepdims=True)
    m2 = jnp.where(lane == i1, neg, m1)
    v2 = jnp.max(m2, axis=1, keepdims=True)
    i2 = jnp.min(jnp.where(m2 == v2, lane, big), axis=1, keepdims=True)
    e2 = jnp.exp(v2 - v1)
    inv = 1.0 / (1.0 + e2)
    w1 = inv * pg_sel
    w2 = (e2 * inv) * pg_sel

    oh1 = lane == i1
    oh2 = lane == i2
    oh = jnp.where(oh1 | oh2, 1.0, 0.0)
    cnt = jnp.sum(oh, axis=0, keepdims=True)
    nblk = jnp.floor((cnt + (ROWS_PER_BLK - 1)) * (1.0 / ROWS_PER_BLK))
    lboff = _dot(jnp.broadcast_to(nblk, (8, LANES)).astype(BF16), su_ref[...])[0:1, :]
    lrank = _dot(tri_ref[...], oh.astype(BF16))
    posmat = lboff * ROWS_PER_BLK + lrank
    p1 = jnp.sum(jnp.where(oh1, posmat, 0.0), axis=1, keepdims=True)
    p2 = jnp.sum(jnp.where(oh2, posmat, 0.0), axis=1, keepdims=True)
    nb_ref[...] = nblk.astype(I32)
    lb_ref[...] = lboff.astype(I32)

    col = lax.broadcasted_iota(I32, (TM_OUT, PACK_ROWS), 1).astype(F32)
    place_t = jnp.where((col == p1) | (col == p2), 1.0, 0.0).astype(BF16)
    hs_ref[...] = lax.dot_general(place_t, h_hi, (((0,), (0,)), ((), ())), preferred_element_type=F32)

    rw = jnp.where(lane_i == 0, w1, jnp.where(lane_i == 1, w2,
                                               jnp.where(lane_i == 2, p1, jnp.where(lane_i == 3, p2, 0.0))))
    rw_ref[...] = rw[:, 0:8]


def _outproj(o_ctx, o_lat, f_ctx, f_lat, xp, xs, mod3, g_ffn, wo, wf, wr_hi, wr_lo, br, tri, su):
    nt = T_ALL // TM_OUT
    nctx = T_CTX // TM_OUT
    ctx_map = lambda i: (jnp.minimum(i, nctx - 1), 0)
    lat_map = lambda i: (jnp.maximum(i - nctx, 0), 0)
    const = lambda i: (0, 0)
    return pl.pallas_call(
        _outproj_kernel,
        grid=(nt,),
        in_specs=[pl.BlockSpec((TM_OUT, V_W), ctx_map),
                  pl.BlockSpec((TM_OUT, V_W), lat_map),
                  pl.BlockSpec((TM_OUT, FN_G * FN_C), ctx_map),
                  pl.BlockSpec((TM_OUT, FN_G * FN_C), lat_map),
                  pl.BlockSpec((TM_OUT, D_MODEL), ctx_map),
                  pl.BlockSpec((TM_OUT, D_MODEL), lat_map),
                  pl.BlockSpec((None, 1, 6 * D_MODEL), lambda i: (_cond_row(i, TM_OUT), 0, 0)),
                  pl.BlockSpec((1, D_MODEL), const),
                  pl.BlockSpec((V_W, D_MODEL), const),
                  pl.BlockSpec((FN_G * FN_C, D_MODEL), const),
                  pl.BlockSpec((D_MODEL, LANES), const),
                  pl.BlockSpec((D_MODEL, LANES), const),
                  pl.BlockSpec((1, LANES), const),
                  pl.BlockSpec((TM_OUT, TM_OUT), const),
                  pl.BlockSpec((LANES, LANES), const)],
        out_specs=[pl.BlockSpec((TM_OUT, D_MODEL), lambda i: (i, 0)),
                   pl.BlockSpec((PACK_ROWS, D_MODEL), lambda i: (i, 0)),
                   pl.BlockSpec((TM_OUT, 8), lambda i: (i, 0)),
                   pl.BlockSpec((None, 1, LANES), lambda i: (i, 0, 0)),
                   pl.BlockSpec((None, 1, LANES), lambda i: (i, 0, 0))],
        out_shape=[jax.ShapeDtypeStruct((T_ALL, D_MODEL), F32),
                   jax.ShapeDtypeStruct((HS_ROWS, D_MODEL), F32),
                   jax.ShapeDtypeStruct((T_ALL, 8), F32),
                   jax.ShapeDtypeStruct((nt, 1, LANES), I32),
                   jax.ShapeDtypeStruct((nt, 1, LANES), I32)],
        compiler_params=_cparams(("arbitrary",)),
        name="outproj",
    )(o_ctx, o_lat, f_ctx, f_lat, xp, xs, mod3, g_ffn, wo, wf, wr_hi, wr_lo, br, tri, su)


SRC_BITS = 16
SRC_MASK = (1 << SRC_BITS) - 1
Y_SLOTS = 3
N_UP_CHUNKS = 2
N_DN_CHUNKS = 8
NT_MOE = (2 * T_ALL + N_TOK_TILES * N_EXP * (ROWS_PER_BLK - 1)) // TM_MOE + N_EXP


def _moe_kernel(texp_ref, meta_ref, code_ref,
                h_hbm, wg_ref, wu_ref, wd_ref,
                out_hbm,
                xbuf, ybuf, wg_s, wu_s, wd_s, gsem, ssem):
    i = pl.program_id(0)
    nt = meta_ref[0]
    xs = i % 2

    def blk_rows(b):
        if isinstance(b, int):
            return pl.ds(b * ROWS_PER_BLK, ROWS_PER_BLK)
        return pl.ds(pl.multiple_of(b * ROWS_PER_BLK, ROWS_PER_BLK), ROWS_PER_BLK)

    def gather_row(tile, sl, j):
        src = code_ref[(tile + 2) * BLK_PER_TILE + j] & SRC_MASK
        pltpu.make_async_copy(h_hbm.at[blk_rows(src)], xbuf.at[sl, blk_rows(j)], gsem.at[sl]).start()

    def scatter_row(tile, sl, j):
        dst = code_ref[(tile + 2) * BLK_PER_TILE + j] >> SRC_BITS
        pltpu.make_async_copy(ybuf.at[sl, blk_rows(j)], out_hbm.at[blk_rows(dst)], ssem.at[sl]).start()

    def gather_wait(sl):
        pltpu.make_async_copy(h_hbm.at[pl.ds(0, TM_MOE)], xbuf.at[sl], gsem.at[sl]).wait()

    def scatter_wait(sl):
        pltpu.make_async_copy(ybuf.at[sl], out_hbm.at[pl.ds(0, TM_MOE)], ssem.at[sl]).wait()

    def y_slot(tile):
        return (tile + 1) % Y_SLOTS

    @pl.when(i == 0)
    def _():
        ybuf[y_slot(-2)] = jnp.zeros((TM_MOE, D_MODEL), F32)
        ybuf[y_slot(-1)] = jnp.zeros((TM_MOE, D_MODEL), F32)

        def body(j, c):
            gather_row(0, 0, j)
            scatter_row(-2, y_slot(-2), j)
            return c
        lax.fori_loop(0, BLK_PER_TILE, body, 0)

    @pl.when((i >= 1) & (i <= nt))
    def _():
        scatter_wait(y_slot(i - 3))

    @pl.when(i < nt)
    def _():
        prev = texp_ref[jnp.maximum(i - 1, 0)]

        @pl.when((i == 0) | (texp_ref[i] != prev))
        def _():
            wg_s[...] = wg_ref[...].astype(BF16)
            wu_s[...] = wu_ref[...].astype(BF16)
            wd_s[...] = wd_ref[...].astype(BF16)

        gather_wait(xs)
        x = xbuf[xs].astype(BF16)

        issues = []
        for j in range(BLK_PER_TILE):
            issues.append(functools.partial(gather_row, i + 1, 1 - xs, j))
            issues.append(functools.partial(scatter_row, i - 1, y_slot(i - 1), j))
        n_groups = N_UP_CHUNKS + N_DN_CHUNKS
        per_group = -(-len(issues) // n_groups)

        def issue_group(k):
            for fn in issues[k * per_group:(k + 1) * per_group]:
                fn()

        wu_c = D_EXP // N_UP_CHUNKS
        hid = []
        for n in range(N_UP_CHUNKS):
            issue_group(n)
            g = _dot(x, wg_s[:, n * wu_c:(n + 1) * wu_c])
            u = _dot(x, wu_s[:, n * wu_c:(n + 1) * wu_c])
            hid.append((_silu(g) * u).astype(BF16))
        hid = jnp.concatenate(hid, axis=1)
        wd_c = D_MODEL // N_DN_CHUNKS
        ys = y_slot(i)
        for n in range(N_DN_CHUNKS):
            issue_group(N_UP_CHUNKS + n)
            ybuf[ys, :, n * wd_c:(n + 1) * wd_c] = _dot(hid, wd_s[:, n * wd_c:(n + 1) * wd_c])

    @pl.when(i == nt)
    def _():
        gather_wait(xs)

        def body(j, c):
            scatter_row(nt - 1, y_slot(nt - 1), j)
            return c
        lax.fori_loop(0, BLK_PER_TILE, body, 0)
        scatter_wait(y_slot(nt - 2))
        scatter_wait(y_slot(nt - 1))


def _moe(texp, meta, code, hs, w_eg, w_eu, w_ed):
    wmap = lambda i, te, me, co: (te[i], 0, 0)
    grid_spec = pltpu.PrefetchScalarGridSpec(
        num_scalar_prefetch=3,
        grid=(NT_MOE + 1,),
        in_specs=[pl.BlockSpec(memory_space=pl.ANY),
                  pl.BlockSpec((None, D_MODEL, D_EXP), wmap),
                  pl.BlockSpec((None, D_MODEL, D_EXP), wmap),
                  pl.BlockSpec((None, D_EXP, D_MODEL), wmap)],
        out_specs=pl.BlockSpec(memory_space=pl.ANY),
        scratch_shapes=[pltpu.VMEM((2, TM_MOE, D_MODEL), F32),
                        pltpu.VMEM((Y_SLOTS, TM_MOE, D_MODEL), F32),
                        pltpu.VMEM((D_MODEL, D_EXP), BF16),
                        pltpu.VMEM((D_MODEL, D_EXP), BF16),
                        pltpu.VMEM((D_EXP, D_MODEL), BF16),
                        pltpu.SemaphoreType.DMA((2,)),
                        pltpu.SemaphoreType.DMA((Y_SLOTS,))])
    return pl.pallas_call(
        _moe_kernel,
        grid_spec=grid_spec,
        out_shape=jax.ShapeDtypeStruct((HS_ROWS, D_MODEL), F32),
        input_output_aliases={3: 0},
        compiler_params=_cparams(("arbitrary",)),
        name="moe",
    )(texp, meta, code, hs, w_eg, w_eu, w_ed)


TM_FIN = TM_OUT


def _final_kernel(x1_ref, ys_pack_ref, rw_ref, mod_ref, g_ref, yp_ref, ys_ref):
    i = pl.program_id(0)
    ga2 = mod_ref[:, 5 * D_MODEL:6 * D_MODEL]
    w0 = rw_ref[:, 0:1]
    w1 = rw_ref[:, 1:2]
    p0 = rw_ref[:, 2:3]
    p1 = rw_ref[:, 3:4]
    col = lax.broadcasted_iota(I32, (TM_FIN, PACK_ROWS), 1).astype(F32)
    comb = jnp.where(col == p0, w0, 0.0) + jnp.where(col == p1, w1, 0.0)
    y_moe = _dot(comb.astype(BF16), ys_pack_ref[...].astype(BF16))
    y = x1_ref[...] + ga2 * y_moe
    out = _rms(y, g_ref[...])

    @pl.when(i < T_CTX // TM_FIN)
    def _():
        yp_ref[...] = out

    @pl.when(i >= T_CTX // TM_FIN)
    def _():
        ys_ref[...] = out


def _final(x1, y2, rw, mod3, g_fin):
    nt = T_ALL // TM_FIN
    nctx = T_CTX // TM_FIN
    return pl.pallas_call(
        _final_kernel,
        grid=(nt,),
        in_specs=[pl.BlockSpec((TM_FIN, D_MODEL), lambda i: (i, 0)),
                  pl.BlockSpec((PACK_ROWS, D_MODEL), lambda i: (i, 0)),
                  pl.BlockSpec((TM_FIN, 8), lambda i: (i, 0)),
                  pl.BlockSpec((None, 1, 6 * D_MODEL), lambda i: (_cond_row(i, TM_FIN), 0, 0)),
                  pl.BlockSpec((1, D_MODEL), lambda i: (0, 0))],
        out_specs=[pl.BlockSpec((TM_FIN, D_MODEL), lambda i: (jnp.minimum(i, nctx - 1), 0)),
                   pl.BlockSpec((TM_FIN, D_MODEL), lambda i: (jnp.maximum(i - nctx, 0), 0))],
        out_shape=[jax.ShapeDtypeStruct((T_CTX, D_MODEL), F32),
                   jax.ShapeDtypeStruct((T_LAT, D_MODEL), F32)],
        compiler_params=_cparams(("arbitrary",)),
        name="final",
    )(x1, y2, rw, mod3, g_fin)


def _np_bf16(a):
    return jnp.asarray(np.asarray(a, np.float32), dtype=BF16)


def _np_f32(a):
    return jnp.asarray(np.asarray(a, np.float32))


@functools.lru_cache(maxsize=None)
def _constants():
    c = {}
    k = np.arange(FN_C)
    ang = 2.0 * np.pi * np.outer(k, k) / FN_C
    c["cs"] = np.concatenate([np.cos(ang), np.sin(ang)], axis=1) / np.sqrt(FN_C)
    p = np.arange(L_CTX)
    ang = 2.0 * np.pi * np.outer(p, p) / L_CTX
    c["p2"] = np.concatenate([np.cos(ang), -np.sin(ang)], axis=1) / np.sqrt(L_CTX)
    pos = np.arange(L_LAT)
    rr, cc = pos // GRID_W, pos % GRID_W
    num = (np.outer(rr, rr) * (GRID_W // GRID_H) + np.outer(cc, cc)) % GRID_W
    ang = 2.0 * np.pi * num / GRID_W
    c["kr"] = np.concatenate([np.cos(ang), -np.sin(ang)], axis=1) / np.sqrt(L_LAT)
    i = np.arange(SB)
    same = (i[:, None] // CHUNK) == (i[None, :] // CHUNK)
    tl = same & (i[:, None] >= i[None, :])
    tu = same & (i[:, None] <= i[None, :])
    c["tlu"] = np.concatenate([tl, tu], axis=1).astype(np.float32)
    c["tri"] = (i[:, None] > i[None, :]).astype(np.float32)
    k = np.arange(LANES)
    c["su"] = (k[:, None] < k[None, :]).astype(np.float32)
    return c


def kernel(x_prompt, x_sample, state_gla_fwd, state_gla_bwd, c, c_ctx, w_ada, b_ada, norm_attn, norm_ffn, w_in, w_gate_fwd, b_gate_fwd, w_gate_bwd, b_gate_bwd, norm_gla, w_out, w_router_group, b_router_group, w_router_expert, b_router_expert, w_expert_gate, w_expert_up, w_expert_down, norm_final):
    assert w_ada.shape[0] == 1, "single layer"
    cst = _constants()
    cs, p2, kr = _np_f32(cst["cs"]), _np_f32(cst["p2"]), _np_f32(cst["kr"])
    tlu, tri, su = _np_bf16(cst["tlu"]), _np_bf16(cst["tri"]), _np_bf16(cst["su"])

    xp = x_prompt.reshape(T_CTX, D_MODEL)
    xs = x_sample.reshape(T_LAT, D_MODEL)

    cond8 = jnp.concatenate([c_ctx[None, :], c, jnp.zeros((3, D_MODEL), F32)], axis=0)
    mod = _ada(cond8, w_ada[0], b_ada[0][None, :])
    mod3 = mod.reshape(8, 1, 6 * D_MODEL)

    wi = w_in[0]
    i_og = 2 * QK_W + 2 * V_W
    i_u = i_og + 2 * RANK
    w_main = jnp.concatenate([wi[:, :i_og], wi[:, i_u:]], axis=1).astype(BF16)
    w_r = jnp.pad(wi[:, i_og:i_u], ((0, 0), (0, LANES - 2 * RANK))).astype(BF16)

    wgf = w_gate_fwd[0].reshape(RANK, HEADS, DK)
    wgb = w_gate_bwd[0].reshape(RANK, HEADS, DK)
    zf = jnp.zeros_like(wgf)
    top = jnp.stack([wgf, zf], axis=2)
    bot = jnp.stack([zf, wgb], axis=2)
    wg = jnp.concatenate([top, bot], axis=0)
    wg = wg.reshape(2 * RANK, HEADS // 2, 4 * DK).transpose(1, 0, 2)
    wg = jnp.pad(wg, ((0, 0), (0, LANES - 2 * RANK), (0, 0)))
    wg_hi = wg.astype(BF16)
    wg_lo = (wg - wg_hi.astype(F32)).astype(BF16)
    bg = jnp.stack([b_gate_fwd[0].reshape(HEADS, DK), b_gate_bwd[0].reshape(HEADS, DK)], axis=1)
    bg = bg.reshape(HEADS // 2, 1, 4 * DK)
    ng = norm_gla[0].reshape(HEADS // 2, 1, 2 * DV)

    proj, r_all = _inproj(xp, xs, mod3, norm_attn, w_main, w_r)

    zero_state = jnp.zeros((N_CTX, HEADS, DK, DV), F32)
    o_ctx, sf_ctx, sb_ctx = _gla(proj, r_all, wg_hi, wg_lo, bg, ng, tlu, zero_state, zero_state,
                                 seq_len=L_CTX, n_seq=N_CTX, row_block0=0)
    o_lat, _, _ = _gla(proj, r_all, wg_hi, wg_lo, bg, ng, tlu,
                       state_gla_fwd[:, 0], state_gla_bwd[:, 0],
                       seq_len=L_LAT, n_seq=N_LAT, row_block0=T_CTX // L_LAT)

    f_ctx = _fnet_ctx(proj, cs, p2)
    f_lat = _fnet_lat(proj, cs, kr)

    wo = w_out[0][:V_W].astype(BF16)
    wf = w_out[0][V_W:].astype(BF16)
    wr = jnp.concatenate([w_router_group[0], w_router_expert[0]], axis=1)
    wr = jnp.pad(wr, ((0, 0), (0, LANES - N_GROUPS - N_EXP)))
    wr_hi = wr.astype(BF16)
    wr_lo = (wr - wr_hi.astype(F32)).astype(BF16)
    br = jnp.pad(jnp.concatenate([b_router_group[0], b_router_expert[0]]), (0, LANES - N_GROUPS - N_EXP))[None, :]

    x1, hs, rw, nb, lb = _outproj(o_ctx, o_lat, f_ctx, f_lat, xp, xs, mod3, norm_ffn, wo, wf,
                                  wr_hi, wr_lo, br, tri, su)

    nb_e = nb[:, 0, LANE_E0:LANE_E0 + N_EXP].T
    lb_e = lb[:, 0, LANE_E0:LANE_E0 + N_EXP].T
    run_end = jnp.cumsum(nb_e, axis=1)
    blocks_e = run_end[:, -1]
    tiles_e = (blocks_e + BLK_PER_TILE - 1) // BLK_PER_TILE
    tile_end = jnp.cumsum(tiles_e)
    tile_start = tile_end - tiles_e
    n_tiles = tile_end[-1]
    g = jnp.arange((NT_MOE + 3) * BLK_PER_TILE, dtype=I32)
    g_tile = g // BLK_PER_TILE - 2
    g_j = g % BLK_PER_TILE
    g_exp = jnp.sum(jnp.clip(g_tile, 0, n_tiles - 1)[:, None] >= tile_end[None, :], axis=1)
    bi = (g_tile - tile_start[g_exp]) * BLK_PER_TILE + g_j
    valid = (g_tile >= 0) & (g_tile < n_tiles) & (bi < blocks_e[g_exp])
    ends = run_end[g_exp]
    t_run = jnp.minimum(jnp.sum(ends <= bi[:, None], axis=1), N_TOK_TILES - 1)
    run_start = jnp.take_along_axis(ends - nb_e[g_exp], t_run[:, None], axis=1)[:, 0]
    blk = t_run * PACK_BLKS + lb_e[g_exp, t_run] + (bi - run_start)
    first_blk = blk.reshape(NT_MOE + 3, BLK_PER_TILE)[:, 0]
    pad_src = first_blk[jnp.clip(g_tile, 0, n_tiles - 1) + 2]
    spare = (((g // BLK_PER_TILE) % 2) * BLK_PER_TILE + g_j) * PACK_BLKS + USED_BLKS
    code = jnp.where(valid, (blk << SRC_BITS) | blk, (spare << SRC_BITS) | pad_src).astype(I32)
    tidx = jnp.minimum(jnp.arange(NT_MOE + 1, dtype=I32), n_tiles - 1)
    texp = jnp.sum(tidx[:, None] >= tile_end[None, :], axis=1).astype(I32)
    meta = n_tiles.reshape(1).astype(I32)

    y2 = _moe(texp, meta, code, hs, w_expert_gate[0], w_expert_up[0], w_expert_down[0])
    y_prompt, y_sample = _final(x1, y2, rw, mod3, norm_final[None, :])

    st_shape = (N_CTX, 1, HEADS, DK, DV)
    return (y_prompt.reshape(N_CTX, L_CTX, D_MODEL), y_sample.reshape(N_LAT, L_LAT, D_MODEL),
            sf_ctx.reshape(st_shape), sb_ctx.reshape(st_shape))
```

```python
import functools

import numpy as np
import jax
import jax.numpy as jnp
from jax import lax
from jax.experimental import pallas as pl
from jax.experimental.pallas import tpu as pltpu

F32 = jnp.float32
BF16 = jnp.bfloat16
I32 = jnp.int32

D_MODEL = 2048
N_CTX = 32
L_CTX = 256
N_LAT = 4
L_LAT = 2048
GRID_H = 32
GRID_W = 64
T_CTX = N_CTX * L_CTX
T_LAT = N_LAT * L_LAT
T_ALL = T_CTX + T_LAT
HEADS = 8
DK = 64
DV = 128
RANK = 16
TAU = 16.0
CHUNK = 64
FN_G = 8
FN_C = 128
QK_W = HEADS * DK
V_W = HEADS * DV
N_GROUPS = 4
EPG = 8
N_EXP = N_GROUPS * EPG
D_EXP = 512
EPS = 1e-6

LANES = 128
VMEM_LIMIT = 56 * 1024 * 1024

TM_MOE = 256


def _dot(a, b):
    return jnp.dot(a, b, preferred_element_type=F32)


def _dot_nt(a, b):
    return lax.dot_general(a, b, (((1,), (1,)), ((), ())), preferred_element_type=F32)


def _split2(x):
    hi = x.astype(BF16)
    lo = (x - hi.astype(F32)).astype(BF16)
    return hi, lo


def _silu(x):
    return x * (1.0 / (1.0 + jnp.exp(-x)))


def _rms(x, g):
    return x * lax.rsqrt(jnp.mean(x * x, axis=-1, keepdims=True) + EPS) * g


def _cparams(sem):
    return pltpu.CompilerParams(dimension_semantics=sem, vmem_limit_bytes=VMEM_LIMIT)


def _ada_kernel(c_ref, w_ref, b_ref, o_ref):
    s_hi, s_lo = _split2(_silu(c_ref[...]))
    w = w_ref[...]
    w_hi = w.astype(BF16)
    w_lo = (w - w_hi.astype(F32)).astype(BF16)
    o_ref[...] = _dot(s_hi, w_hi) + _dot(s_lo, w_hi) + _dot(s_hi, w_lo) + b_ref[...]


def _ada(cond8, w_ada, b_ada):
    tn = 768
    n6 = 6 * D_MODEL
    return pl.pallas_call(
        _ada_kernel,
        grid=(n6 // tn,),
        in_specs=[pl.BlockSpec((8, D_MODEL), lambda j: (0, 0)),
                  pl.BlockSpec((D_MODEL, tn), lambda j: (0, j)),
                  pl.BlockSpec((1, tn), lambda j: (0, j))],
        out_specs=pl.BlockSpec((8, tn), lambda j: (0, j)),
        out_shape=jax.ShapeDtypeStruct((8, n6), F32),
        compiler_params=_cparams(("arbitrary",)),
        name="ada",
    )(cond8, w_ada, b_ada)


TM_IN = 256
TN_IN = 1024
N_MAIN = 4096


def _cond_row(tile, tm):
    ctx_tiles = T_CTX // tm
    per_seq = L_LAT // tm
    return jnp.where(tile < ctx_tiles, 0, 1 + (jnp.maximum(tile - ctx_tiles, 0)) // per_seq)


def _inproj_kernel(xp_ref, xs_ref, mod_ref, g_ref, w_ref, wr_ref, proj_ref, r_ref):
    i = pl.program_id(0)
    x = jnp.where(i < T_CTX // TM_IN, xp_ref[...], xs_ref[...])
    sh1 = mod_ref[:, 0:D_MODEL]
    sc1 = mod_ref[:, D_MODEL:2 * D_MODEL]
    hb = (_rms(x, g_ref[...]) * (1.0 + sc1) + sh1).astype(BF16)
    r_ref[...] = _dot(hb, wr_ref[...])
    for n in range(N_MAIN // TN_IN):
        proj_ref[:, n * TN_IN:(n + 1) * TN_IN] = _dot(hb, w_ref[:, n * TN_IN:(n + 1) * TN_IN]).astype(BF16)


def _inproj(xp, xs, mod3, g_attn, w_main, w_r):
    nt = T_ALL // TM_IN
    nctx = T_CTX // TM_IN
    resident = pl.Buffered(1)
    return pl.pallas_call(
        _inproj_kernel,
        grid=(nt,),
        in_specs=[pl.BlockSpec((TM_IN, D_MODEL), lambda i: (jnp.minimum(i, nctx - 1), 0)),
                  pl.BlockSpec((TM_IN, D_MODEL), lambda i: (jnp.maximum(i - nctx, 0), 0)),
                  pl.BlockSpec((None, 1, 6 * D_MODEL), lambda i: (_cond_row(i, TM_IN), 0, 0)),
                  pl.BlockSpec((1, D_MODEL), lambda i: (0, 0)),
                  pl.BlockSpec((D_MODEL, N_MAIN), lambda i: (0, 0), pipeline_mode=resident),
                  pl.BlockSpec((D_MODEL, LANES), lambda i: (0, 0), pipeline_mode=resident)],
        out_specs=[pl.BlockSpec((TM_IN, N_MAIN), lambda i: (i, 0)),
                   pl.BlockSpec((TM_IN, LANES), lambda i: (i, 0))],
        out_shape=[jax.ShapeDtypeStruct((T_ALL, N_MAIN), BF16),
                   jax.ShapeDtypeStruct((T_ALL, LANES), F32)],
        compiler_params=_cparams(("arbitrary",)),
        name="inproj",
    )(xp, xs, mod3, g_attn, w_main, w_r)


SB = 256
CPB = SB // CHUNK


def _gla_kernel(q_ref, k_ref, v_ref, og_ref, r_ref, wgh_ref, wgl_ref, bg_ref, ng_ref,
                tlu_ref, s0f_ref, s0b_ref,
                o_ref, sf_ref, sb_ref,
                cum_s, last_s, qe_s, ke_s, kd_s, dec_s, kv_s, sbs_s, *, seq_len):
    n_chunks = seq_len // CHUNK
    n_sb = seq_len // SB
    lane = lax.broadcasted_iota(I32, (1, LANES), 1)
    m_f = lane < DK

    r_hi, r_lo = _split2(r_ref[...])
    z = _dot(r_hi, wgh_ref[...]) + _dot(r_lo, wgh_ref[...]) + _dot(r_hi, wgl_ref[...]) + bg_ref[...]
    g_all = (jnp.minimum(z, 0.0) - jnp.log(1.0 + jnp.exp(-jnp.abs(z)))) * (1.0 / TAU)

    q_pair = q_ref[...].astype(F32)
    k_pair = k_ref[...].astype(F32)
    q_roll = pltpu.roll(q_pair, DK, axis=1)
    k_roll = pltpu.roll(k_pair, DK, axis=1)

    row_b = lax.broadcasted_iota(I32, (SB, SB), 0)
    col_b = lax.broadcasted_iota(I32, (SB, SB), 1)
    same_chunk = (row_b // CHUNK) == (col_b // CHUNK)
    tril = same_chunk & (row_b >= col_b)
    triu = same_chunk & (row_b <= col_b)
    row_chunk = lax.broadcasted_iota(I32, (SB, 1), 0) // CHUNK
    col_chunk = lax.broadcasted_iota(I32, (1, SB), 1) // CHUNK

    m_f2 = (lax.broadcasted_iota(I32, (1, 2 * LANES), 1) % LANES) < DK
    cum_s[...] = g_all

    def cum_body(s, carry):
        rows = pl.ds(pl.multiple_of(s * SB, SB), SB)
        g = cum_s[rows, :]
        f_hi, f_lo = _split2(jnp.where(m_f2, g, 0.0))
        b_hi, b_lo = _split2(jnp.where(m_f2, 0.0, g))
        cum = (_dot(tlu_ref[...], jnp.concatenate([f_hi, b_hi], axis=0))
               + _dot(tlu_ref[...], jnp.concatenate([f_lo, b_lo], axis=0)))
        cum_s[rows, :] = cum
        tots = []
        for c in range(CPB):
            tot = jnp.where(m_f2, cum[(c + 1) * CHUNK - 1:(c + 1) * CHUNK, :], cum[c * CHUNK:c * CHUNK + 1, :])
            dec_s[s * CPB + c] = jnp.exp(tot)
            tots.append(jnp.broadcast_to(tot, (CHUNK, 2 * LANES)))
        last_s[rows, :] = jnp.concatenate(tots, axis=0)
        return carry

    lax.fori_loop(0, n_sb, cum_body, 0)

    for j in range(2):
        cum = cum_s[:, j * LANES:(j + 1) * LANES]
        last = last_s[:, j * LANES:(j + 1) * LANES]
        if j == 0:
            q2 = jnp.where(m_f, q_pair, q_roll)
            k2 = jnp.where(m_f, k_pair, k_roll)
        else:
            q2 = jnp.where(m_f, q_roll, q_pair)
            k2 = jnp.where(m_f, k_roll, k_pair)
        qe_s[...] = ((q2 * (DK ** -0.5)) * jnp.exp(cum)).astype(BF16)
        ke_s[...] = (k2 * jnp.exp(-cum)).astype(BF16)
        kd_s[...] = (k2 * jnp.exp(last - cum)).astype(BF16)

        lo = j * DV
        ng = ng_ref[:, lo:lo + DV]

        def kv_body(s, carry):
            rows = pl.ds(pl.multiple_of(s * SB, SB), SB)
            v_t = jnp.transpose(v_ref[rows, lo:lo + DV].astype(F32)).astype(BF16)
            zero = jnp.zeros_like(v_t)
            v_st = jnp.concatenate([jnp.where(col_chunk == c, v_t, zero) for c in range(CPB)], axis=0)
            kv = _dot(v_st, kd_s[rows, :])
            for c in range(CPB):
                kv_s[s * CPB + c] = kv[c * LANES:(c + 1) * LANES, :]
            return carry

        lax.fori_loop(0, n_sb, kv_body, 0)

        def dec_row(c):
            return dec_s[c][:, lo:lo + LANES]

        st0 = jnp.transpose(jnp.concatenate([s0f_ref[j], s0b_ref[j]], axis=0))

        def bwd_body(t, st_b):
            c = n_chunks - 1 - t
            sbs_s[c] = st_b
            return st_b * dec_row(c) + kv_s[c]

        st_b_fin = lax.fori_loop(0, n_chunks, bwd_body, st0, unroll=CPB)

        def fwd_body(s, st_f):
            rows = pl.ds(pl.multiple_of(s * SB, SB), SB)
            qe = qe_s[rows, :]
            ke = ke_s[rows, :]
            v_b = v_ref[rows, lo:lo + DV].astype(BF16)
            zero = jnp.zeros_like(qe)
            st = st_f
            q_parts, s_parts = [], []
            for c in range(CPB):
                ci = s * CPB + c
                s_parts.append(jnp.where(m_f, st, sbs_s[ci]).astype(BF16))
                q_parts.append(jnp.where(row_chunk == c, qe, zero))
                st = st * dec_row(ci) + kv_s[ci]
            o = _dot_nt(jnp.concatenate(q_parts, axis=1), jnp.concatenate(s_parts, axis=1))
            q_st = jnp.concatenate([jnp.where(m_f, qe, zero), jnp.where(m_f, zero, qe)], axis=0)
            sc = _dot_nt(q_st, ke)
            p = jnp.where(tril, sc[0:SB, :], 0.0) + jnp.where(triu, sc[SB:2 * SB, :], 0.0)
            o = o + _dot(p.astype(BF16), v_b)
            o = o * lax.rsqrt(jnp.mean(o * o, axis=-1, keepdims=True) + EPS) * ng
            o_ref[rows, lo:lo + DV] = o * _silu(og_ref[rows, lo:lo + DV].astype(F32))
            return st

        st_f_fin = lax.fori_loop(0, n_sb, fwd_body, st0)
        sf_ref[j] = jnp.transpose(st_f_fin)[0:DK, :]
        sb_ref[j] = jnp.transpose(st_b_fin)[DK:2 * DK, :]


def _gla(proj, r_all, wg_hi, wg_lo, bg, ng, tlu, s0f, s0b, *, seq_len, n_seq, row_block0):
    kern = functools.partial(_gla_kernel, seq_len=seq_len)
    n_chunks = seq_len // CHUNK
    rb = lambda b: b + row_block0
    return pl.pallas_call(
        kern,
        grid=(n_seq, HEADS // 2),
        in_specs=[pl.BlockSpec((seq_len, LANES), lambda b, h: (rb(b), h)),
                  pl.BlockSpec((seq_len, LANES), lambda b, h: (rb(b), QK_W // LANES + h)),
                  pl.BlockSpec((seq_len, 2 * DV), lambda b, h: (rb(b), 2 * QK_W // (2 * DV) + h)),
                  pl.BlockSpec((seq_len, 2 * DV), lambda b, h: (rb(b), (2 * QK_W + V_W) // (2 * DV) + h)),
                  pl.BlockSpec((seq_len, LANES), lambda b, h: (rb(b), 0)),
                  pl.BlockSpec((None, LANES, 2 * LANES), lambda b, h: (h, 0, 0)),
                  pl.BlockSpec((None, LANES, 2 * LANES), lambda b, h: (h, 0, 0)),
                  pl.BlockSpec((None, 1, 2 * LANES), lambda b, h: (h, 0, 0)),
                  pl.BlockSpec((None, 1, 2 * DV), lambda b, h: (h, 0, 0)),
                  pl.BlockSpec((SB, 2 * SB), lambda b, h: (0, 0)),
                  pl.BlockSpec((None, 2, DK, DV), lambda b, h: (b, h, 0, 0)),
                  pl.BlockSpec((None, 2, DK, DV), lambda b, h: (b, h, 0, 0))],
        out_specs=[pl.BlockSpec((seq_len, 2 * DV), lambda b, h: (b, h)),
                   pl.BlockSpec((None, 2, DK, DV), lambda b, h: (b, h, 0, 0)),
                   pl.BlockSpec((None, 2, DK, DV), lambda b, h: (b, h, 0, 0))],
        out_shape=[jax.ShapeDtypeStruct((n_seq * seq_len, V_W), F32),
                   jax.ShapeDtypeStruct((n_seq, HEADS, DK, DV), F32),
                   jax.ShapeDtypeStruct((n_seq, HEADS, DK, DV), F32)],
        scratch_shapes=[pltpu.VMEM((seq_len, 2 * LANES), F32),
                        pltpu.VMEM((seq_len, 2 * LANES), F32),
                        pltpu.VMEM((seq_len, LANES), BF16),
                        pltpu.VMEM((seq_len, LANES), BF16),
                        pltpu.VMEM((seq_len, LANES), BF16),
                        pltpu.VMEM((n_chunks, 1, 2 * LANES), F32),
                        pltpu.VMEM((n_chunks, DV, LANES), F32),
                        pltpu.VMEM((n_chunks, DV, LANES), F32)],
        compiler_params=_cparams(("arbitrary", "arbitrary")),
        name="gla_%d" % seq_len,
    )(proj, proj, proj, proj, r_all, wg_hi, wg_lo, bg, ng, tlu, s0f, s0b)


def _fnet_stage_a(u_bf, cs):
    cparts, sparts = [], []
    for g in range(FN_G):
        ab = _dot(u_bf[:, g * FN_C:(g + 1) * FN_C], cs)
        cparts.append(ab[:, 0:FN_C])
        sparts.append(ab[:, FN_C:2 * FN_C])
    return jnp.concatenate(cparts, axis=1), jnp.concatenate(sparts, axis=1)


def _fnet_ctx_kernel(u_ref, cs_ref, p2_ref, f_ref):
    uc, us = _fnet_stage_a(u_ref[...].astype(BF16), cs_ref[...].astype(BF16))
    ab = jnp.concatenate([uc, us], axis=0).astype(BF16)
    f_ref[...] = _dot(p2_ref[...].astype(BF16), ab)


def _fnet_ctx(proj, cs, p2):
    return pl.pallas_call(
        _fnet_ctx_kernel,
        grid=(N_CTX,),
        in_specs=[pl.BlockSpec((L_CTX, FN_G * FN_C), lambda b: (b, 3)),
                  pl.BlockSpec((FN_C, 2 * FN_C), lambda b: (0, 0)),
                  pl.BlockSpec((L_CTX, 2 * L_CTX), lambda b: (0, 0))],
        out_specs=pl.BlockSpec((L_CTX, FN_G * FN_C), lambda b: (b, 0)),
        out_shape=jax.ShapeDtypeStruct((T_CTX, FN_G * FN_C), F32),
        compiler_params=_cparams(("arbitrary",)),
        name="fnet_ctx",
    )(proj, cs, p2)


TM_FL = 256
RT_FL = 256


def _fnet_lat_kernel(u_ref, cs_ref, kr_ref, f_ref, ab_s):
    m = pl.program_id(1)

    @pl.when(m == 0)
    def _():
        def body(t, carry):
            rows = pl.ds(pl.multiple_of(t * RT_FL, RT_FL), RT_FL)
            uc, us = _fnet_stage_a(u_ref[rows, :].astype(BF16), cs_ref[...].astype(BF16))
            ab_s[rows, :] = uc.astype(BF16)
            ab_s[pl.ds(pl.multiple_of(L_LAT + t * RT_FL, RT_FL), RT_FL), :] = us.astype(BF16)
            return carry
        lax.fori_loop(0, L_LAT // RT_FL, body, 0)

    f_ref[...] = _dot(kr_ref[...].astype(BF16), ab_s[...])


def _fnet_lat(proj, cs, kr):
    nm = L_LAT // TM_FL
    return pl.pallas_call(
        _fnet_lat_kernel,
        grid=(N_LAT, nm),
        in_specs=[pl.BlockSpec((L_LAT, FN_G * FN_C), lambda b, m: (T_CTX // L_LAT + b, 3)),
                  pl.BlockSpec((FN_C, 2 * FN_C), lambda b, m: (0, 0)),
                  pl.BlockSpec((TM_FL, 2 * L_LAT), lambda b, m: (m, 0))],
        out_specs=pl.BlockSpec((TM_FL, FN_G * FN_C), lambda b, m: (b * nm + m, 0)),
        out_shape=jax.ShapeDtypeStruct((T_LAT, FN_G * FN_C), F32),
        scratch_shapes=[pltpu.VMEM((2 * L_LAT, FN_G * FN_C), BF16)],
        compiler_params=_cparams(("arbitrary", "arbitrary")),
        name="fnet_lat",
    )(proj, cs, kr)


TM_OUT = 256
LANE_E0 = N_GROUPS
ROWS_PER_BLK = 8
PACK_ROWS = -(-(2 * TM_OUT + N_EXP * (ROWS_PER_BLK - 1)) // 256) * 256
PACK_BLKS = PACK_ROWS // ROWS_PER_BLK
N_TOK_TILES = T_ALL // TM_OUT
BLK_PER_TILE = TM_MOE // ROWS_PER_BLK
USED_BLKS = (2 * TM_OUT + N_EXP * (ROWS_PER_BLK - 1)) // ROWS_PER_BLK
HS_ROWS = N_TOK_TILES * PACK_ROWS
assert USED_BLKS < PACK_BLKS and 2 * BLK_PER_TILE <= N_TOK_TILES


def _outproj_kernel(oc_ref, ol_ref, fc_ref, fl_ref, xp_ref, xs_ref, mod_ref, g_ref, wo_ref, wf_ref,
                    wrh_ref, wrl_ref, br_ref, tri_ref, su_ref,
                    x1_ref, hs_ref, rw_ref, nb_ref, lb_ref):
    i = pl.program_id(0)
    is_ctx = i < T_CTX // TM_OUT

    o = jnp.where(is_ctx, oc_ref[...], ol_ref[...]).astype(BF16)
    f = jnp.where(is_ctx, fc_ref[...], fl_ref[...]).astype(BF16)
    x = jnp.where(is_ctx, xp_ref[...], xs_ref[...])
    y = _dot(o, wo_ref[...]) + _dot(f, wf_ref[...])
    ga1 = mod_ref[:, 2 * D_MODEL:3 * D_MODEL]
    sh2 = mod_ref[:, 3 * D_MODEL:4 * D_MODEL]
    sc2 = mod_ref[:, 4 * D_MODEL:5 * D_MODEL]
    x1 = x + ga1 * y
    x1_ref[...] = x1
    h2 = _rms(x1, g_ref[...]) * (1.0 + sc2) + sh2

    h_hi, h_lo = _split2(h2)
    lg_all = _dot(h_hi, wrh_ref[...]) + _dot(h_lo, wrh_ref[...]) + _dot(h_hi, wrl_ref[...]) + br_ref[...]

    lane_i = lax.broadcasted_iota(I32, (TM_OUT, LANES), 1)
    lane = lane_i.astype(F32)
    neg = jnp.float32(-jnp.inf)
    big = jnp.float32(LANES)
    lg = jnp.where(lane_i < N_GROUPS, lg_all, neg)
    gmax = jnp.max(lg, axis=1, keepdims=True)
    gsel = jnp.min(jnp.where(lg == gmax, lane, big), axis=1, keepdims=True)
    den = jnp.sum(jnp.exp(lg - gmax), axis=1, keepdims=True)
    pg_sel = 1.0 / den

    e_idx = lane_i - LANE_E0
    egrp = (e_idx >> 3).astype(F32)
    emask = (e_idx >= 0) & (e_idx < N_EXP) & (egrp == gsel)
    m1 = jnp.where(emask, lg_all, neg)
    v1 = jnp.max(m1, axis=1, keepdims=True)
    i1 = jnp.min(jnp.where(m1 == v1, lane, big), axis=1, keepdims=True)
    m2 = jnp.where(lane == i1, neg, m1)
    v2 = jnp.max(m2, axis=1, keepdims=True)
    i2 = jnp.min(jnp.where(m2 == v2, lane, big), axis=1, keepdims=True)
    e2 = jnp.exp(v2 - v1)
    inv = 1.0 / (1.0 + e2)
    w1 = inv * pg_sel
    w2 = (e2 * inv) * pg_sel

    oh1 = lane == i1
    oh2 = lane == i2
    oh = jnp.where(oh1 | oh2, 1.0, 0.0)
    cnt = jnp.sum(oh, axis=0, keepdims=True)
    nblk = jnp.floor((cnt + (ROWS_PER_BLK - 1)) * (1.0 / ROWS_PER_BLK))
    lboff = _dot(jnp.broadcast_to(nblk, (8, LANES)).astype(BF16), su_ref[...])[0:1, :]
    lrank = _dot(tri_ref[...], oh.astype(BF16))
    posmat = lboff * ROWS_PER_BLK + lrank
    p1 = jnp.sum(jnp.where(oh1, posmat, 0.0), axis=1, keepdims=True)
    p2 = jnp.sum(jnp.where(oh2, posmat, 0.0), axis=1, keepdims=True)
    nb_ref[...] = nblk.astype(I32)
    lb_ref[...] = lboff.astype(I32)

    col = lax.broadcasted_iota(I32, (TM_OUT, PACK_ROWS), 1).astype(F32)
    place_t = jnp.where((col == p1) | (col == p2), 1.0, 0.0).astype(BF16)
    hs_ref[...] = lax.dot_general(place_t, h_hi, (((0,), (0,)), ((), ())), preferred_element_type=F32)

    rw = jnp.where(lane_i == 0, w1, jnp.where(lane_i == 1, w2,
                                               jnp.where(lane_i == 2, p1, jnp.where(lane_i == 3, p2, 0.0))))
    rw_ref[...] = rw[:, 0:8]


def _outproj(o_ctx, o_lat, f_ctx, f_lat, xp, xs, mod3, g_ffn, wo, wf, wr_hi, wr_lo, br, tri, su):
    nt = T_ALL // TM_OUT
    nctx = T_CTX // TM_OUT
    ctx_map = lambda i: (jnp.minimum(i, nctx - 1), 0)
    lat_map = lambda i: (jnp.maximum(i - nctx, 0), 0)
    const = lambda i: (0, 0)
    return pl.pallas_call(
        _outproj_kernel,
        grid=(nt,),
        in_specs=[pl.BlockSpec((TM_OUT, V_W), ctx_map),
                  pl.BlockSpec((TM_OUT, V_W), lat_map),
                  pl.BlockSpec((TM_OUT, FN_G * FN_C), ctx_map),
                  pl.BlockSpec((TM_OUT, FN_G * FN_C), lat_map),
                  pl.BlockSpec((TM_OUT, D_MODEL), ctx_map),
                  pl.BlockSpec((TM_OUT, D_MODEL), lat_map),
                  pl.BlockSpec((None, 1, 6 * D_MODEL), lambda i: (_cond_row(i, TM_OUT), 0, 0)),
                  pl.BlockSpec((1, D_MODEL), const),
                  pl.BlockSpec((V_W, D_MODEL), const),
                  pl.BlockSpec((FN_G * FN_C, D_MODEL), const),
                  pl.BlockSpec((D_MODEL, LANES), const),
                  pl.BlockSpec((D_MODEL, LANES), const),
                  pl.BlockSpec((1, LANES), const),
                  pl.BlockSpec((TM_OUT, TM_OUT), const),
                  pl.BlockSpec((LANES, LANES), const)],
        out_specs=[pl.BlockSpec((TM_OUT, D_MODEL), lambda i: (i, 0)),
                   pl.BlockSpec((PACK_ROWS, D_MODEL), lambda i: (i, 0)),
                   pl.BlockSpec((TM_OUT, 8), lambda i: (i, 0)),
                   pl.BlockSpec((None, 1, LANES), lambda i: (i, 0, 0)),
                   pl.BlockSpec((None, 1, LANES), lambda i: (i, 0, 0))],
        out_shape=[jax.ShapeDtypeStruct((T_ALL, D_MODEL), F32),
                   jax.ShapeDtypeStruct((HS_ROWS, D_MODEL), F32),
                   jax.ShapeDtypeStruct((T_ALL, 8), F32),
                   jax.ShapeDtypeStruct((nt, 1, LANES), I32),
                   jax.ShapeDtypeStruct((nt, 1, LANES), I32)],
        compiler_params=_cparams(("arbitrary",)),
        name="outproj",
    )(o_ctx, o_lat, f_ctx, f_lat, xp, xs, mod3, g_ffn, wo, wf, wr_hi, wr_lo, br, tri, su)


SRC_BITS = 16
SRC_MASK = (1 << SRC_BITS) - 1
X_SLOTS = 3
Y_SLOTS = 3
N_UP_CHUNKS = 2
N_DN_CHUNKS = 8
NT_MOE = (2 * T_ALL + N_TOK_TILES * N_EXP * (ROWS_PER_BLK - 1)) // TM_MOE + N_EXP


def _moe_kernel(texp_ref, meta_ref, code_ref,
                h_hbm, wg_ref, wu_ref, wd_ref,
                out_hbm,
                xbuf, ybuf, wg_s, wu_s, wd_s, gsem, ssem):
    i = pl.program_id(0)
    nt = meta_ref[0]
    xs = i % X_SLOTS

    def blk_rows(b):
        if isinstance(b, int):
            return pl.ds(b * ROWS_PER_BLK, ROWS_PER_BLK)
        return pl.ds(pl.multiple_of(b * ROWS_PER_BLK, ROWS_PER_BLK), ROWS_PER_BLK)

    def gather_row(tile, sl, j):
        src = code_ref[(tile + 2) * BLK_PER_TILE + j] & SRC_MASK
        pltpu.make_async_copy(h_hbm.at[blk_rows(src)], xbuf.at[sl, blk_rows(j)], gsem.at[sl]).start()

    def scatter_row(tile, sl, j):
        dst = code_ref[(tile + 2) * BLK_PER_TILE + j] >> SRC_BITS
        pltpu.make_async_copy(ybuf.at[sl, blk_rows(j)], out_hbm.at[blk_rows(dst)], ssem.at[sl]).start(priority=1)

    def gather_wait(sl):
        pltpu.make_async_copy(h_hbm.at[pl.ds(0, TM_MOE)], xbuf.at[sl], gsem.at[sl]).wait()

    def scatter_wait(sl):
        pltpu.make_async_copy(ybuf.at[sl], out_hbm.at[pl.ds(0, TM_MOE)], ssem.at[sl]).wait()

    def y_slot(tile):
        return (tile + 1) % Y_SLOTS

    @pl.when(i == 0)
    def _():
        ybuf[y_slot(-2)] = jnp.zeros((TM_MOE, D_MODEL), F32)
        ybuf[y_slot(-1)] = jnp.zeros((TM_MOE, D_MODEL), F32)

        def body(j, c):
            gather_row(0, 0, j)
            gather_row(1, 1, j)
            scatter_row(-2, y_slot(-2), j)
            return c
        lax.fori_loop(0, BLK_PER_TILE, body, 0)

    @pl.when((i >= 1) & (i <= nt))
    def _():
        scatter_wait(y_slot(i - 3))

    @pl.when(i < nt)
    def _():
        prev = texp_ref[jnp.maximum(i - 1, 0)]

        @pl.when((i == 0) | (texp_ref[i] != prev))
        def _():
            wg_s[...] = wg_ref[...].astype(BF16)
            wu_s[...] = wu_ref[...].astype(BF16)
            wd_s[...] = wd_ref[...].astype(BF16)

        gather_wait(xs)
        x = xbuf[xs].astype(BF16)

        issues = []
        for j in range(BLK_PER_TILE):
            issues.append(functools.partial(gather_row, i + 2, (i + 2) % X_SLOTS, j))
            issues.append(functools.partial(scatter_row, i - 1, y_slot(i - 1), j))
        n_groups = N_UP_CHUNKS + N_DN_CHUNKS
        per_group = -(-len(issues) // n_groups)

        def issue_group(k):
            for fn in issues[k * per_group:(k + 1) * per_group]:
                fn()

        wu_c = D_EXP // N_UP_CHUNKS
        hid = []
        for n in range(N_UP_CHUNKS):
            issue_group(n)
            g = _dot(x, wg_s[:, n * wu_c:(n + 1) * wu_c])
            u = _dot(x, wu_s[:, n * wu_c:(n + 1) * wu_c])
            hid.append((_silu(g) * u).astype(BF16))
        hid = jnp.concatenate(hid, axis=1)
        wd_c = D_MODEL // N_DN_CHUNKS
        ys = y_slot(i)
        for n in range(N_DN_CHUNKS):
            issue_group(N_UP_CHUNKS + n)
            ybuf[ys, :, n * wd_c:(n + 1) * wd_c] = _dot(hid, wd_s[:, n * wd_c:(n + 1) * wd_c])

    @pl.when(i == nt)
    def _():
        gather_wait(xs)
        gather_wait((i + 1) % X_SLOTS)

        def body(j, c):
            scatter_row(nt - 1, y_slot(nt - 1), j)
            return c
        lax.fori_loop(0, BLK_PER_TILE, body, 0)
        scatter_wait(y_slot(nt - 2))
        scatter_wait(y_slot(nt - 1))


def _moe(texp, meta, code, hs, w_eg, w_eu, w_ed):
    wmap = lambda i, te, me, co: (te[i], 0, 0)
    grid_spec = pltpu.PrefetchScalarGridSpec(
        num_scalar_prefetch=3,
        grid=(NT_MOE + 1,),
        in_specs=[pl.BlockSpec(memory_space=pl.ANY),
                  pl.BlockSpec((None, D_MODEL, D_EXP), wmap),
                  pl.BlockSpec((None, D_MODEL, D_EXP), wmap),
                  pl.BlockSpec((None, D_EXP, D_MODEL), wmap)],
        out_specs=pl.BlockSpec(memory_space=pl.ANY),
        scratch_shapes=[pltpu.VMEM((X_SLOTS, TM_MOE, D_MODEL), F32),
                        pltpu.VMEM((Y_SLOTS, TM_MOE, D_MODEL), F32),
                        pltpu.VMEM((D_MODEL, D_EXP), BF16),
                        pltpu.VMEM((D_MODEL, D_EXP), BF16),
                        pltpu.VMEM((D_EXP, D_MODEL), BF16),
                        pltpu.SemaphoreType.DMA((X_SLOTS,)),
                        pltpu.SemaphoreType.DMA((Y_SLOTS,))])
    return pl.pallas_call(
        _moe_kernel,
        grid_spec=grid_spec,
        out_shape=jax.ShapeDtypeStruct((HS_ROWS, D_MODEL), F32),
        input_output_aliases={3: 0},
        compiler_params=_cparams(("arbitrary",)),
        name="moe",
    )(texp, meta, code, hs, w_eg, w_eu, w_ed)


TM_FIN = TM_OUT


def _final_kernel(x1_ref, ys_pack_ref, rw_ref, mod_ref, g_ref, yp_ref, ys_ref):
    i = pl.program_id(0)
    ga2 = mod_ref[:, 5 * D_MODEL:6 * D_MODEL]
    w0 = rw_ref[:, 0:1]
    w1 = rw_ref[:, 1:2]
    p0 = rw_ref[:, 2:3]
    p1 = rw_ref[:, 3:4]
    col = lax.broadcasted_iota(I32, (TM_FIN, PACK_ROWS), 1).astype(F32)
    comb = jnp.where(col == p0, w0, 0.0) + jnp.where(col == p1, w1, 0.0)
    y_moe = _dot(comb.astype(BF16), ys_pack_ref[...].astype(BF16))
    y = x1_ref[...] + ga2 * y_moe
    out = _rms(y, g_ref[...])

    @pl.when(i < T_CTX // TM_FIN)
    def _():
        yp_ref[...] = out

    @pl.when(i >= T_CTX // TM_FIN)
    def _():
        ys_ref[...] = out


def _final(x1, y2, rw, mod3, g_fin):
    nt = T_ALL // TM_FIN
    nctx = T_CTX // TM_FIN
    return pl.pallas_call(
        _final_kernel,
        grid=(nt,),
        in_specs=[pl.BlockSpec((TM_FIN, D_MODEL), lambda i: (i, 0)),
                  pl.BlockSpec((PACK_ROWS, D_MODEL), lambda i: (i, 0)),
                  pl.BlockSpec((TM_FIN, 8), lambda i: (i, 0)),
                  pl.BlockSpec((None, 1, 6 * D_MODEL), lambda i: (_cond_row(i, TM_FIN), 0, 0)),
                  pl.BlockSpec((1, D_MODEL), lambda i: (0, 0))],
        out_specs=[pl.BlockSpec((TM_FIN, D_MODEL), lambda i: (jnp.minimum(i, nctx - 1), 0)),
                   pl.BlockSpec((TM_FIN, D_MODEL), lambda i: (jnp.maximum(i - nctx, 0), 0))],
        out_shape=[jax.ShapeDtypeStruct((T_CTX, D_MODEL), F32),
                   jax.ShapeDtypeStruct((T_LAT, D_MODEL), F32)],
        compiler_params=_cparams(("arbitrary",)),
        name="final",
    )(x1, y2, rw, mod3, g_fin)


def _np_bf16(a):
    return jnp.asarray(np.asarray(a, np.float32), dtype=BF16)


def _np_f32(a):
    return jnp.asarray(np.asarray(a, np.float32))


@functools.lru_cache(maxsize=None)
def _constants():
    c = {}
    k = np.arange(FN_C)
    ang = 2.0 * np.pi * np.outer(k, k) / FN_C
    c["cs"] = np.concatenate([np.cos(ang), np.sin(ang)], axis=1) / np.sqrt(FN_C)
    p = np.arange(L_CTX)
    ang = 2.0 * np.pi * np.outer(p, p) / L_CTX
    c["p2"] = np.concatenate([np.cos(ang), -np.sin(ang)], axis=1) / np.sqrt(L_CTX)
    pos = np.arange(L_LAT)
    rr, cc = pos // GRID_W, pos % GRID_W
    num = (np.outer(rr, rr) * (GRID_W // GRID_H) + np.outer(cc, cc)) % GRID_W
    ang = 2.0 * np.pi * num / GRID_W
    c["kr"] = np.concatenate([np.cos(ang), -np.sin(ang)], axis=1) / np.sqrt(L_LAT)
    i = np.arange(SB)
    same = (i[:, None] // CHUNK) == (i[None, :] // CHUNK)
    tl = same & (i[:, None] >= i[None, :])
    tu = same & (i[:, None] <= i[None, :])
    c["tlu"] = np.concatenate([tl, tu], axis=1).astype(np.float32)
    c["tri"] = (i[:, None] > i[None, :]).astype(np.float32)
    k = np.arange(LANES)
    c["su"] = (k[:, None] < k[None, :]).astype(np.float32)
    return c


def kernel(x_prompt, x_sample, state_gla_fwd, state_gla_bwd, c, c_ctx, w_ada, b_ada, norm_attn, norm_ffn, w_in, w_gate_fwd, b_gate_fwd, w_gate_bwd, b_gate_bwd, norm_gla, w_out, w_router_group, b_router_group, w_router_expert, b_router_expert, w_expert_gate, w_expert_up, w_expert_down, norm_final):
    assert w_ada.shape[0] == 1, "single layer"
    cst = _constants()
    cs, p2, kr = _np_f32(cst["cs"]), _np_f32(cst["p2"]), _np_f32(cst["kr"])
    tlu, tri, su = _np_bf16(cst["tlu"]), _np_bf16(cst["tri"]), _np_bf16(cst["su"])

    xp = x_prompt.reshape(T_CTX, D_MODEL)
    xs = x_sample.reshape(T_LAT, D_MODEL)

    cond8 = jnp.concatenate([c_ctx[None, :], c, jnp.zeros((3, D_MODEL), F32)], axis=0)
    mod = _ada(cond8, w_ada[0], b_ada[0][None, :])
    mod3 = mod.reshape(8, 1, 6 * D_MODEL)

    wi = w_in[0]
    i_og = 2 * QK_W + 2 * V_W
    i_u = i_og + 2 * RANK
    w_main = jnp.concatenate([wi[:, :i_og], wi[:, i_u:]], axis=1).astype(BF16)
    w_r = jnp.pad(wi[:, i_og:i_u], ((0, 0), (0, LANES - 2 * RANK))).astype(BF16)

    wgf = w_gate_fwd[0].reshape(RANK, HEADS, DK)
    wgb = w_gate_bwd[0].reshape(RANK, HEADS, DK)
    zf = jnp.zeros_like(wgf)
    top = jnp.stack([wgf, zf], axis=2)
    bot = jnp.stack([zf, wgb], axis=2)
    wg = jnp.concatenate([top, bot], axis=0)
    wg = wg.reshape(2 * RANK, HEADS // 2, 4 * DK).transpose(1, 0, 2)
    wg = jnp.pad(wg, ((0, 0), (0, LANES - 2 * RANK), (0, 0)))
    wg_hi = wg.astype(BF16)
    wg_lo = (wg - wg_hi.astype(F32)).astype(BF16)
    bg = jnp.stack([b_gate_fwd[0].reshape(HEADS, DK), b_gate_bwd[0].reshape(HEADS, DK)], axis=1)
    bg = bg.reshape(HEADS // 2, 1, 4 * DK)
    ng = norm_gla[0].reshape(HEADS // 2, 1, 2 * DV)

    proj, r_all = _inproj(xp, xs, mod3, norm_attn, w_main, w_r)

    zero_state = jnp.zeros((N_CTX, HEADS, DK, DV), F32)
    o_ctx, sf_ctx, sb_ctx = _gla(proj, r_all, wg_hi, wg_lo, bg, ng, tlu, zero_state, zero_state,
                                 seq_len=L_CTX, n_seq=N_CTX, row_block0=0)
    o_lat, _, _ = _gla(proj, r_all, wg_hi, wg_lo, bg, ng, tlu,
                       state_gla_fwd[:, 0], state_gla_bwd[:, 0],
                       seq_len=L_LAT, n_seq=N_LAT, row_block0=T_CTX // L_LAT)

    f_ctx = _fnet_ctx(proj, cs, p2)
    f_lat = _fnet_lat(proj, cs, kr)

    wo = w_out[0][:V_W].astype(BF16)
    wf = w_out[0][V_W:].astype(BF16)
    wr = jnp.concatenate([w_router_group[0], w_router_expert[0]], axis=1)
    wr = jnp.pad(wr, ((0, 0), (0, LANES - N_GROUPS - N_EXP)))
    wr_hi = wr.astype(BF16)
    wr_lo = (wr - wr_hi.astype(F32)).astype(BF16)
    br = jnp.pad(jnp.concatenate([b_router_group[0], b_router_expert[0]]), (0, LANES - N_GROUPS - N_EXP))[None, :]

    x1, hs, rw, nb, lb = _outproj(o_ctx, o_lat, f_ctx, f_lat, xp, xs, mod3, norm_ffn, wo, wf,
                                  wr_hi, wr_lo, br, tri, su)

    nb_e = nb[:, 0, LANE_E0:LANE_E0 + N_EXP].T
    lb_e = lb[:, 0, LANE_E0:LANE_E0 + N_EXP].T
    run_end = jnp.cumsum(nb_e, axis=1)
    blocks_e = run_end[:, -1]
    tiles_e = (blocks_e + BLK_PER_TILE - 1) // BLK_PER_TILE
    tile_end = jnp.cumsum(tiles_e)
    tile_start = tile_end - tiles_e
    n_tiles = tile_end[-1]
    n_code_tiles = NT_MOE + 4
    tile = jnp.arange(n_code_tiles, dtype=I32) - 2
    tile_c = jnp.clip(tile, 0, n_tiles - 1)
    t_exp = jnp.sum(tile_c[:, None] >= tile_end[None, :], axis=1)
    ends = run_end[t_exp]
    starts = ends - nb_e[t_exp]
    offs = lb_e[t_exp] + jnp.arange(N_TOK_TILES, dtype=I32)[None, :] * PACK_BLKS - starts
    j = jnp.arange(BLK_PER_TILE, dtype=I32)
    bi = ((tile_c - tile_start[t_exp]) * BLK_PER_TILE)[:, None] + j[None, :]
    in_run = (starts[:, None, :] <= bi[:, :, None]) & (bi[:, :, None] < ends[:, None, :])
    blk = bi + jnp.sum(jnp.where(in_run, offs[:, None, :], 0), axis=2)
    valid = (tile == tile_c)[:, None] & (bi < blocks_e[t_exp][:, None])
    spare = ((jnp.arange(n_code_tiles, dtype=I32) % 2)[:, None] * BLK_PER_TILE + j[None, :]) * PACK_BLKS + USED_BLKS
    code = jnp.where(valid, (blk << SRC_BITS) | blk, (spare << SRC_BITS) | blk[:, 0:1]).astype(I32).reshape(-1)
    tidx = jnp.minimum(jnp.arange(NT_MOE + 1, dtype=I32), n_tiles - 1)
    texp = jnp.sum(tidx[:, None] >= tile_end[None, :], axis=1).astype(I32)
    meta = n_tiles.reshape(1).astype(I32)

    y2 = _moe(texp, meta, code, hs, w_expert_gate[0], w_expert_up[0], w_expert_down[0])
    y_prompt, y_sample = _final(x1, y2, rw, mod3, norm_final[None, :])

    st_shape = (N_CTX, 1, HEADS, DK, DV)
    return (y_prompt.reshape(N_CTX, L_CTX, D_MODEL), y_sample.reshape(N_LAT, L_LAT, D_MODEL),
            sf_ctx.reshape(st_shape), sb_ctx.reshape(st_shape))
```

```python
import functools

import numpy as np
import jax
import jax.numpy as jnp
from jax import lax
from jax.experimental import pallas as pl
from jax.experimental.pallas import tpu as pltpu

F32 = jnp.float32
BF16 = jnp.bfloat16
I32 = jnp.int32

D_MODEL = 2048
N_CTX = 32
L_CTX = 256
N_LAT = 4
L_LAT = 2048
GRID_H = 32
GRID_W = 64
T_CTX = N_CTX * L_CTX
T_LAT = N_LAT * L_LAT
T_ALL = T_CTX + T_LAT
HEADS = 8
DK = 64
DV = 128
RANK = 16
TAU = 16.0
CHUNK = 64
FN_G = 8
FN_C = 128
QK_W = HEADS * DK
V_W = HEADS * DV
N_GROUPS = 4
EPG = 8
N_EXP = N_GROUPS * EPG
D_EXP = 512
EPS = 1e-6

LANES = 128
VMEM_LIMIT = 56 * 1024 * 1024

TM_MOE = 256


def _dot(a, b):
    return jnp.dot(a, b, preferred_element_type=F32)


def _dot_nt(a, b):
    return lax.dot_general(a, b, (((1,), (1,)), ((), ())), preferred_element_type=F32)


def _split2(x):
    hi = x.astype(BF16)
    lo = (x - hi.astype(F32)).astype(BF16)
    return hi, lo


def _silu(x):
    return x * (1.0 / (1.0 + jnp.exp(-x)))


def _rms(x, g):
    return x * lax.rsqrt(jnp.mean(x * x, axis=-1, keepdims=True) + EPS) * g


def _cparams(sem):
    return pltpu.CompilerParams(dimension_semantics=sem, vmem_limit_bytes=VMEM_LIMIT)


def _ada_kernel(c_ref, w_ref, b_ref, o_ref):
    s_hi, s_lo = _split2(_silu(c_ref[...]))
    w = w_ref[...]
    w_hi = w.astype(BF16)
    w_lo = (w - w_hi.astype(F32)).astype(BF16)
    o_ref[...] = _dot(s_hi, w_hi) + _dot(s_lo, w_hi) + _dot(s_hi, w_lo) + b_ref[...]


def _ada(cond8, w_ada, b_ada):
    tn = 768
    n6 = 6 * D_MODEL
    return pl.pallas_call(
        _ada_kernel,
        grid=(n6 // tn,),
        in_specs=[pl.BlockSpec((8, D_MODEL), lambda j: (0, 0)),
                  pl.BlockSpec((D_MODEL, tn), lambda j: (0, j)),
                  pl.BlockSpec((1, tn), lambda j: (0, j))],
        out_specs=pl.BlockSpec((8, tn), lambda j: (0, j)),
        out_shape=jax.ShapeDtypeStruct((8, n6), F32),
        compiler_params=_cparams(("arbitrary",)),
        name="ada",
    )(cond8, w_ada, b_ada)


TM_IN = 256
TN_IN = 1024
N_MAIN = 4096


def _cond_row(tile, tm):
    ctx_tiles = T_CTX // tm
    per_seq = L_LAT // tm
    return jnp.where(tile < ctx_tiles, 0, 1 + (jnp.maximum(tile - ctx_tiles, 0)) // per_seq)


def _inproj_kernel(xp_ref, xs_ref, mod_ref, g_ref, w_ref, wr_ref, proj_ref, r_ref):
    i = pl.program_id(0)
    x = jnp.where(i < T_CTX // TM_IN, xp_ref[...], xs_ref[...])
    sh1 = mod_ref[:, 0:D_MODEL]
    sc1 = mod_ref[:, D_MODEL:2 * D_MODEL]
    hb = (_rms(x, g_ref[...]) * (1.0 + sc1) + sh1).astype(BF16)
    r_ref[...] = _dot(hb, wr_ref[...])
    for n in range(N_MAIN // TN_IN):
        proj_ref[:, n * TN_IN:(n + 1) * TN_IN] = _dot(hb, w_ref[:, n * TN_IN:(n + 1) * TN_IN]).astype(BF16)


def _inproj(xp, xs, mod3, g_attn, w_main, w_r):
    nt = T_ALL // TM_IN
    nctx = T_CTX // TM_IN
    resident = pl.Buffered(1)
    return pl.pallas_call(
        _inproj_kernel,
        grid=(nt,),
        in_specs=[pl.BlockSpec((TM_IN, D_MODEL), lambda i: (jnp.minimum(i, nctx - 1), 0)),
                  pl.BlockSpec((TM_IN, D_MODEL), lambda i: (jnp.maximum(i - nctx, 0), 0)),
                  pl.BlockSpec((None, 1, 6 * D_MODEL), lambda i: (_cond_row(i, TM_IN), 0, 0)),
                  pl.BlockSpec((1, D_MODEL), lambda i: (0, 0)),
                  pl.BlockSpec((D_MODEL, N_MAIN), lambda i: (0, 0), pipeline_mode=resident),
                  pl.BlockSpec((D_MODEL, LANES), lambda i: (0, 0), pipeline_mode=resident)],
        out_specs=[pl.BlockSpec((TM_IN, N_MAIN), lambda i: (i, 0)),
                   pl.BlockSpec((TM_IN, LANES), lambda i: (i, 0))],
        out_shape=[jax.ShapeDtypeStruct((T_ALL, N_MAIN), BF16),
                   jax.ShapeDtypeStruct((T_ALL, LANES), F32)],
        compiler_params=_cparams(("arbitrary",)),
        name="inproj",
    )(xp, xs, mod3, g_attn, w_main, w_r)


SB = 256
CPB = SB // CHUNK


def _gla_kernel(q_ref, k_ref, v_ref, og_ref, r_ref, wgh_ref, wgl_ref, bg_ref, ng_ref,
                tlu_ref, s0f_ref, s0b_ref,
                o_ref, sf_ref, sb_ref,
                cum_s, last_s, qe_s, ke_s, kd_s, dec_s, kv_s, sbs_s, *, seq_len, n_in_step):
    sb_per_seq = seq_len // SB
    ch_per_seq = seq_len // CHUNK
    n_sb = n_in_step * sb_per_seq
    lane = lax.broadcasted_iota(I32, (1, LANES), 1)
    m_f = lane < DK

    r_hi, r_lo = _split2(r_ref[...])
    z = _dot(r_hi, wgh_ref[...]) + _dot(r_lo, wgh_ref[...]) + _dot(r_hi, wgl_ref[...]) + bg_ref[...]
    g_all = (jnp.minimum(z, 0.0) - jnp.log(1.0 + jnp.exp(-jnp.abs(z)))) * (1.0 / TAU)

    q_pair = q_ref[...].astype(F32)
    k_pair = k_ref[...].astype(F32)
    q_roll = pltpu.roll(q_pair, DK, axis=1)
    k_roll = pltpu.roll(k_pair, DK, axis=1)

    row_b = lax.broadcasted_iota(I32, (SB, SB), 0)
    col_b = lax.broadcasted_iota(I32, (SB, SB), 1)
    same_chunk = (row_b // CHUNK) == (col_b // CHUNK)
    tril = same_chunk & (row_b >= col_b)
    triu = same_chunk & (row_b <= col_b)
    row_chunk = lax.broadcasted_iota(I32, (SB, 1), 0) // CHUNK
    col_chunk = lax.broadcasted_iota(I32, (1, SB), 1) // CHUNK

    m_f2 = (lax.broadcasted_iota(I32, (1, 2 * LANES), 1) % LANES) < DK
    cum_s[...] = g_all

    def cum_body(s, carry):
        rows = pl.ds(pl.multiple_of(s * SB, SB), SB)
        g = cum_s[rows, :]
        f_hi, f_lo = _split2(jnp.where(m_f2, g, 0.0))
        b_hi, b_lo = _split2(jnp.where(m_f2, 0.0, g))
        cum = (_dot(tlu_ref[...], jnp.concatenate([f_hi, b_hi], axis=0))
               + _dot(tlu_ref[...], jnp.concatenate([f_lo, b_lo], axis=0)))
        cum_s[rows, :] = cum
        tots = []
        for c in range(CPB):
            tot = jnp.where(m_f2, cum[(c + 1) * CHUNK - 1:(c + 1) * CHUNK, :], cum[c * CHUNK:c * CHUNK + 1, :])
            dec_s[s * CPB + c] = jnp.exp(tot)
            tots.append(jnp.broadcast_to(tot, (CHUNK, 2 * LANES)))
        last_s[rows, :] = jnp.concatenate(tots, axis=0)
        return carry

    lax.fori_loop(0, n_sb, cum_body, 0)

    for j in range(2):
        cum = cum_s[:, j * LANES:(j + 1) * LANES]
        last = last_s[:, j * LANES:(j + 1) * LANES]
        if j == 0:
            q2 = jnp.where(m_f, q_pair, q_roll)
            k2 = jnp.where(m_f, k_pair, k_roll)
        else:
            q2 = jnp.where(m_f, q_roll, q_pair)
            k2 = jnp.where(m_f, k_roll, k_pair)
        qe_s[j] = ((q2 * (DK ** -0.5)) * jnp.exp(cum)).astype(BF16)
        ke_s[j] = (k2 * jnp.exp(-cum)).astype(BF16)
        kd_s[j] = (k2 * jnp.exp(last - cum)).astype(BF16)

    def kv_body(s, carry):
        rows = pl.ds(pl.multiple_of(s * SB, SB), SB)
        for j in range(2):
            v_t = jnp.transpose(v_ref[rows, j * DV:(j + 1) * DV].astype(F32)).astype(BF16)
            zero = jnp.zeros_like(v_t)
            v_st = jnp.concatenate([jnp.where(col_chunk == c, v_t, zero) for c in range(CPB)], axis=0)
            kv = _dot(v_st, kd_s[j, rows, :])
            for c in range(CPB):
                kv_s[j, s * CPB + c] = kv[c * LANES:(c + 1) * LANES, :]
        return carry

    lax.fori_loop(0, n_sb, kv_body, 0, unroll=min(2, n_sb))

    def dec_row(j, c):
        return dec_s[c][:, j * LANES:(j + 1) * LANES]

    def step_state(j, c, st):
        return st * dec_row(j, c) + kv_s[j, c]

    for q in range(n_in_step):
        st0 = tuple(jnp.transpose(jnp.concatenate([s0f_ref[q, j], s0b_ref[q, j]], axis=0)) for j in range(2))
        c0 = q * ch_per_seq
        s0 = q * sb_per_seq

        def bwd_body(t, sts, c0=c0):
            c = c0 + ch_per_seq - 1 - t
            for j in range(2):
                sbs_s[j, c] = sts[j]
            return tuple(step_state(j, c, sts[j]) for j in range(2))

        st_b_fin = lax.fori_loop(0, ch_per_seq, bwd_body, st0, unroll=CPB)

        def fwd_body(sl, sts, s0=s0):
            s = s0 + sl
            rows = pl.ds(pl.multiple_of(s * SB, SB), SB)
            new = []
            for j in range(2):
                lo = j * DV
                qe = qe_s[j, rows, :]
                ke = ke_s[j, rows, :]
                v_b = v_ref[rows, lo:lo + DV].astype(BF16)
                zero = jnp.zeros_like(qe)
                st = sts[j]
                q_parts, s_parts = [], []
                for c in range(CPB):
                    ci = s * CPB + c
                    s_parts.append(jnp.where(m_f, st, sbs_s[j, ci]).astype(BF16))
                    q_parts.append(jnp.where(row_chunk == c, qe, zero))
                    st = step_state(j, ci, st)
                o = _dot_nt(jnp.concatenate(q_parts, axis=1), jnp.concatenate(s_parts, axis=1))
                q_st = jnp.concatenate([jnp.where(m_f, qe, zero), jnp.where(m_f, zero, qe)], axis=0)
                sc = _dot_nt(q_st, ke)
                p = jnp.where(tril, sc[0:SB, :], 0.0) + jnp.where(triu, sc[SB:2 * SB, :], 0.0)
                o = o + _dot(p.astype(BF16), v_b)
                o = o * lax.rsqrt(jnp.mean(o * o, axis=-1, keepdims=True) + EPS) * ng_ref[:, lo:lo + DV]
                o_ref[rows, lo:lo + DV] = (o * _silu(og_ref[rows, lo:lo + DV].astype(F32))).astype(BF16)
                new.append(st)
            return tuple(new)

        st_f_fin = lax.fori_loop(0, sb_per_seq, fwd_body, st0, unroll=min(2, sb_per_seq))
        for j in range(2):
            sf_ref[q, j] = jnp.transpose(st_f_fin[j])[0:DK, :]
            sb_ref[q, j] = jnp.transpose(st_b_fin[j])[DK:2 * DK, :]


def _gla(proj, r_all, wg_hi, wg_lo, bg, ng, tlu, s0f, s0b, *, seq_len, n_seq, n_in_step, row0):
    kern = functools.partial(_gla_kernel, seq_len=seq_len, n_in_step=n_in_step)
    rows = n_in_step * seq_len
    n_chunks = rows // CHUNK
    rb = lambda b: b + row0 // rows
    st_spec = pl.BlockSpec((n_in_step, 2, DK, DV), lambda b, h: (b, h, 0, 0))
    return pl.pallas_call(
        kern,
        grid=(n_seq // n_in_step, HEADS // 2),
        in_specs=[pl.BlockSpec((rows, LANES), lambda b, h: (rb(b), h)),
                  pl.BlockSpec((rows, LANES), lambda b, h: (rb(b), QK_W // LANES + h)),
                  pl.BlockSpec((rows, 2 * DV), lambda b, h: (rb(b), 2 * QK_W // (2 * DV) + h)),
                  pl.BlockSpec((rows, 2 * DV), lambda b, h: (rb(b), (2 * QK_W + V_W) // (2 * DV) + h)),
                  pl.BlockSpec((rows, LANES), lambda b, h: (rb(b), 0)),
                  pl.BlockSpec((None, LANES, 2 * LANES), lambda b, h: (h, 0, 0)),
                  pl.BlockSpec((None, LANES, 2 * LANES), lambda b, h: (h, 0, 0)),
                  pl.BlockSpec((None, 1, 2 * LANES), lambda b, h: (h, 0, 0)),
                  pl.BlockSpec((None, 1, 2 * DV), lambda b, h: (h, 0, 0)),
                  pl.BlockSpec((SB, 2 * SB), lambda b, h: (0, 0)),
                  st_spec, st_spec],
        out_specs=[pl.BlockSpec((rows, 2 * DV), lambda b, h: (b, h)), st_spec, st_spec],
        out_shape=[jax.ShapeDtypeStruct((n_seq * seq_len, V_W), BF16),
                   jax.ShapeDtypeStruct((n_seq, HEADS, DK, DV), F32),
                   jax.ShapeDtypeStruct((n_seq, HEADS, DK, DV), F32)],
        scratch_shapes=[pltpu.VMEM((rows, 2 * LANES), F32),
                        pltpu.VMEM((rows, 2 * LANES), F32),
                        pltpu.VMEM((2, rows, LANES), BF16),
                        pltpu.VMEM((2, rows, LANES), BF16),
                        pltpu.VMEM((2, rows, LANES), BF16),
                        pltpu.VMEM((n_chunks, 1, 2 * LANES), F32),
                        pltpu.VMEM((2, n_chunks, DV, LANES), F32),
                        pltpu.VMEM((2, n_chunks, DV, LANES), F32)],
        compiler_params=_cparams(("arbitrary", "arbitrary")),
        name="gla_%d" % seq_len,
    )(proj, proj, proj, proj, r_all, wg_hi, wg_lo, bg, ng, tlu, s0f, s0b)


def _fnet_stage_a(u_bf, cs):
    cparts, sparts = [], []
    for g in range(FN_G):
        ab = _dot(u_bf[:, g * FN_C:(g + 1) * FN_C], cs)
        cparts.append(ab[:, 0:FN_C])
        sparts.append(ab[:, FN_C:2 * FN_C])
    return jnp.concatenate(cparts, axis=1), jnp.concatenate(sparts, axis=1)


def _fnet_ctx_kernel(u_ref, cs_ref, p2_ref, f_ref):
    uc, us = _fnet_stage_a(u_ref[...].astype(BF16), cs_ref[...].astype(BF16))
    ab = jnp.concatenate([uc, us], axis=0).astype(BF16)
    f_ref[...] = _dot(p2_ref[...].astype(BF16), ab).astype(BF16)


def _fnet_ctx(proj, cs, p2):
    return pl.pallas_call(
        _fnet_ctx_kernel,
        grid=(N_CTX,),
        in_specs=[pl.BlockSpec((L_CTX, FN_G * FN_C), lambda b: (b, 3)),
                  pl.BlockSpec((FN_C, 2 * FN_C), lambda b: (0, 0)),
                  pl.BlockSpec((L_CTX, 2 * L_CTX), lambda b: (0, 0))],
        out_specs=pl.BlockSpec((L_CTX, FN_G * FN_C), lambda b: (b, 0)),
        out_shape=jax.ShapeDtypeStruct((T_CTX, FN_G * FN_C), BF16),
        compiler_params=_cparams(("arbitrary",)),
        name="fnet_ctx",
    )(proj, cs, p2)


TM_FL = 256
RT_FL = 256


def _fnet_lat_kernel(u_ref, cs_ref, kr_ref, f_ref, ab_s):
    m = pl.program_id(1)

    @pl.when(m == 0)
    def _():
        def body(t, carry):
            rows = pl.ds(pl.multiple_of(t * RT_FL, RT_FL), RT_FL)
            uc, us = _fnet_stage_a(u_ref[rows, :].astype(BF16), cs_ref[...].astype(BF16))
            ab_s[rows, :] = uc.astype(BF16)
            ab_s[pl.ds(pl.multiple_of(L_LAT + t * RT_FL, RT_FL), RT_FL), :] = us.astype(BF16)
            return carry
        lax.fori_loop(0, L_LAT // RT_FL, body, 0)

    f_ref[...] = _dot(kr_ref[...].astype(BF16), ab_s[...]).astype(BF16)


def _fnet_lat(proj, cs, kr):
    nm = L_LAT // TM_FL
    return pl.pallas_call(
        _fnet_lat_kernel,
        grid=(N_LAT, nm),
        in_specs=[pl.BlockSpec((L_LAT, FN_G * FN_C), lambda b, m: (T_CTX // L_LAT + b, 3)),
                  pl.BlockSpec((FN_C, 2 * FN_C), lambda b, m: (0, 0)),
                  pl.BlockSpec((TM_FL, 2 * L_LAT), lambda b, m: (m, 0))],
        out_specs=pl.BlockSpec((TM_FL, FN_G * FN_C), lambda b, m: (b * nm + m, 0)),
        out_shape=jax.ShapeDtypeStruct((T_LAT, FN_G * FN_C), BF16),
        scratch_shapes=[pltpu.VMEM((2 * L_LAT, FN_G * FN_C), BF16)],
        compiler_params=_cparams(("arbitrary", "arbitrary")),
        name="fnet_lat",
    )(proj, cs, kr)


TM_OUT = 256
LANE_E0 = N_GROUPS
ROWS_PER_BLK = 8
PACK_ROWS = -(-(2 * TM_OUT + N_EXP * (ROWS_PER_BLK - 1)) // 256) * 256
PACK_BLKS = PACK_ROWS // ROWS_PER_BLK
N_TOK_TILES = T_ALL // TM_OUT
BLK_PER_TILE = TM_MOE // ROWS_PER_BLK
USED_BLKS = (2 * TM_OUT + N_EXP * (ROWS_PER_BLK - 1)) // ROWS_PER_BLK
HS_ROWS = N_TOK_TILES * PACK_ROWS
assert USED_BLKS < PACK_BLKS and 2 * BLK_PER_TILE <= N_TOK_TILES


def _outproj_kernel(oc_ref, ol_ref, fc_ref, fl_ref, xp_ref, xs_ref, mod_ref, g_ref, wo_ref, wf_ref,
                    wrh_ref, wrl_ref, br_ref, tri_ref, su_ref,
                    x1_ref, hs_ref, rw_ref, nb_ref, lb_ref):
    i = pl.program_id(0)
    is_ctx = i < T_CTX // TM_OUT

    o = jnp.where(is_ctx, oc_ref[...], ol_ref[...]).astype(BF16)
    f = jnp.where(is_ctx, fc_ref[...], fl_ref[...]).astype(BF16)
    x = jnp.where(is_ctx, xp_ref[...], xs_ref[...])
    y = _dot(o, wo_ref[...]) + _dot(f, wf_ref[...])
    ga1 = mod_ref[:, 2 * D_MODEL:3 * D_MODEL]
    sh2 = mod_ref[:, 3 * D_MODEL:4 * D_MODEL]
    sc2 = mod_ref[:, 4 * D_MODEL:5 * D_MODEL]
    x1 = x + ga1 * y
    x1_ref[...] = x1
    h2 = _rms(x1, g_ref[...]) * (1.0 + sc2) + sh2

    h_hi, h_lo = _split2(h2)
    lg_all = _dot(h_hi, wrh_ref[...]) + _dot(h_lo, wrh_ref[...]) + _dot(h_hi, wrl_ref[...]) + br_ref[...]

    lane_i = lax.broadcasted_iota(I32, (TM_OUT, LANES), 1)
    lane = lane_i.astype(F32)
    neg = jnp.float32(-jnp.inf)
    big = jnp.float32(LANES)
    lg = jnp.where(lane_i < N_GROUPS, lg_all, neg)
    gmax = jnp.max(lg, axis=1, keepdims=True)
    gsel = jnp.min(jnp.where(lg == gmax, lane, big), axis=1, keepdims=True)
    den = jnp.sum(jnp.exp(lg - gmax), axis=1, keepdims=True)
    pg_sel = 1.0 / den

    e_idx = lane_i - LANE_E0
    egrp = (e_idx >> 3).astype(F32)
    emask = (e_idx >= 0) & (e_idx < N_EXP) & (egrp == gsel)
    m1 = jnp.where(emask, lg_all, neg)
    v1 = jnp.max(m1, axis=1, keepdims=True)
    i1 = jnp.min(jnp.where(m1 == v1, lane, big), axis=1, keepdims=True)
    m2 = jnp.where(lane == i1, neg, m1)
    v2 = jnp.max(m2, axis=1, keepdims=True)
    i2 = jnp.min(jnp.where(m2 == v2, lane, big), axis=1, keepdims=True)
    e2 = jnp.exp(v2 - v1)
    inv = 1.0 / (1.0 + e2)
    w1 = inv * pg_sel
    w2 = (e2 * inv) * pg_sel

    oh1 = lane == i1
    oh2 = lane == i2
    oh = jnp.where(oh1 | oh2, 1.0, 0.0)
    cnt = jnp.sum(oh, axis=0, keepdims=True)
    nblk = jnp.floor((cnt + (ROWS_PER_BLK - 1)) * (1.0 / ROWS_PER_BLK))
    lboff = _dot(jnp.broadcast_to(nblk, (8, LANES)).astype(BF16), su_ref[...])[0:1, :]
    lrank = _dot(tri_ref[...], oh.astype(BF16))
    posmat = lboff * ROWS_PER_BLK + lrank
    p1 = jnp.sum(jnp.where(oh1, posmat, 0.0), axis=1, keepdims=True)
    p2 = jnp.sum(jnp.where(oh2, posmat, 0.0), axis=1, keepdims=True)
    nb_ref[...] = nblk.astype(I32)
    lb_ref[...] = lboff.astype(I32)

    col = lax.broadcasted_iota(I32, (TM_OUT, PACK_ROWS), 1).astype(F32)
    place_t = jnp.where((col == p1) | (col == p2), 1.0, 0.0).astype(BF16)
    hs_ref[...] = lax.dot_general(place_t, h_hi, (((0,), (0,)), ((), ())), preferred_element_type=F32)

    rw = jnp.where(lane_i == 0, w1, jnp.where(lane_i == 1, w2,
                                               jnp.where(lane_i == 2, p1, jnp.where(lane_i == 3, p2, 0.0))))
    rw_ref[...] = rw[:, 0:8]


def _outproj(o_ctx, o_lat, f_ctx, f_lat, xp, xs, mod3, g_ffn, wo, wf, wr_hi, wr_lo, br, tri, su):
    nt = T_ALL // TM_OUT
    nctx = T_CTX // TM_OUT
    ctx_map = lambda i: (jnp.minimum(i, nctx - 1), 0)
    lat_map = lambda i: (jnp.maximum(i - nctx, 0), 0)
    const = lambda i: (0, 0)
    return pl.pallas_call(
        _outproj_kernel,
        grid=(nt,),
        in_specs=[pl.BlockSpec((TM_OUT, V_W), ctx_map),
                  pl.BlockSpec((TM_OUT, V_W), lat_map),
                  pl.BlockSpec((TM_OUT, FN_G * FN_C), ctx_map),
                  pl.BlockSpec((TM_OUT, FN_G * FN_C), lat_map),
                  pl.BlockSpec((TM_OUT, D_MODEL), ctx_map),
                  pl.BlockSpec((TM_OUT, D_MODEL), lat_map),
                  pl.BlockSpec((None, 1, 6 * D_MODEL), lambda i: (_cond_row(i, TM_OUT), 0, 0)),
                  pl.BlockSpec((1, D_MODEL), const),
                  pl.BlockSpec((V_W, D_MODEL), const),
                  pl.BlockSpec((FN_G * FN_C, D_MODEL), const),
                  pl.BlockSpec((D_MODEL, LANES), const),
                  pl.BlockSpec((D_MODEL, LANES), const),
                  pl.BlockSpec((1, LANES), const),
                  pl.BlockSpec((TM_OUT, TM_OUT), const),
                  pl.BlockSpec((LANES, LANES), const)],
        out_specs=[pl.BlockSpec((TM_OUT, D_MODEL), lambda i: (i, 0)),
                   pl.BlockSpec((PACK_ROWS, D_MODEL), lambda i: (i, 0)),
                   pl.BlockSpec((TM_OUT, 8), lambda i: (i, 0)),
                   pl.BlockSpec((None, 1, LANES), lambda i: (i, 0, 0)),
                   pl.BlockSpec((None, 1, LANES), lambda i: (i, 0, 0))],
        out_shape=[jax.ShapeDtypeStruct((T_ALL, D_MODEL), F32),
                   jax.ShapeDtypeStruct((HS_ROWS, D_MODEL), F32),
                   jax.ShapeDtypeStruct((T_ALL, 8), F32),
                   jax.ShapeDtypeStruct((nt, 1, LANES), I32),
                   jax.ShapeDtypeStruct((nt, 1, LANES), I32)],
        compiler_params=_cparams(("arbitrary",)),
        name="outproj",
    )(o_ctx, o_lat, f_ctx, f_lat, xp, xs, mod3, g_ffn, wo, wf, wr_hi, wr_lo, br, tri, su)


SRC_BITS = 16
SRC_MASK = (1 << SRC_BITS) - 1
X_SLOTS = 3
Y_SLOTS = 3
N_UP_CHUNKS = 2
N_DN_CHUNKS = 8
NT_MOE = (2 * T_ALL + N_TOK_TILES * N_EXP * (ROWS_PER_BLK - 1)) // TM_MOE + N_EXP


def _moe_kernel(texp_ref, meta_ref, code_ref,
                h_hbm, wg_ref, wu_ref, wd_ref,
                out_hbm,
                xbuf, ybuf, wg_s, wu_s, wd_s, gsem, ssem):
    i = pl.program_id(0)
    nt = meta_ref[0]
    xs = i % X_SLOTS

    def blk_rows(b):
        if isinstance(b, int):
            return pl.ds(b * ROWS_PER_BLK, ROWS_PER_BLK)
        return pl.ds(pl.multiple_of(b * ROWS_PER_BLK, ROWS_PER_BLK), ROWS_PER_BLK)

    def gather_row(tile, sl, j):
        src = code_ref[(tile + 2) * BLK_PER_TILE + j] & SRC_MASK
        pltpu.make_async_copy(h_hbm.at[blk_rows(src)], xbuf.at[sl, blk_rows(j)], gsem.at[sl]).start()

    def scatter_row(tile, sl, j):
        dst = code_ref[(tile + 2) * BLK_PER_TILE + j] >> SRC_BITS
        pltpu.make_async_copy(ybuf.at[sl, blk_rows(j)], out_hbm.at[blk_rows(dst)], ssem.at[sl]).start(priority=1)

    def gather_wait(sl):
        pltpu.make_async_copy(h_hbm.at[pl.ds(0, TM_MOE)], xbuf.at[sl], gsem.at[sl]).wait()

    def scatter_wait(sl):
        pltpu.make_async_copy(ybuf.at[sl], out_hbm.at[pl.ds(0, TM_MOE)], ssem.at[sl]).wait()

    def y_slot(tile):
        return (tile + 1) % Y_SLOTS

    @pl.when(i == 0)
    def _():
        ybuf[y_slot(-2)] = jnp.zeros((TM_MOE, D_MODEL), F32)
        ybuf[y_slot(-1)] = jnp.zeros((TM_MOE, D_MODEL), F32)

        def body(j, c):
            gather_row(0, 0, j)
            gather_row(1, 1, j)
            scatter_row(-2, y_slot(-2), j)
            return c
        lax.fori_loop(0, BLK_PER_TILE, body, 0)

    @pl.when((i >= 1) & (i <= nt))
    def _():
        scatter_wait(y_slot(i - 3))

    @pl.when(i < nt)
    def _():
        prev = texp_ref[jnp.maximum(i - 1, 0)]

        @pl.when((i == 0) | (texp_ref[i] != prev))
        def _():
            wg_s[...] = wg_ref[...].astype(BF16)
            wu_s[...] = wu_ref[...].astype(BF16)
            wd_s[...] = wd_ref[...].astype(BF16)

        gather_wait(xs)
        x = xbuf[xs].astype(BF16)

        issues = []
        for j in range(BLK_PER_TILE):
            issues.append(functools.partial(gather_row, i + 2, (i + 2) % X_SLOTS, j))
            issues.append(functools.partial(scatter_row, i - 1, y_slot(i - 1), j))
        n_groups = N_UP_CHUNKS + N_DN_CHUNKS
        per_group = -(-len(issues) // n_groups)

        def issue_group(k):
            for fn in issues[k * per_group:(k + 1) * per_group]:
                fn()

        wu_c = D_EXP // N_UP_CHUNKS
        hid = []
        for n in range(N_UP_CHUNKS):
            issue_group(n)
            g = _dot(x, wg_s[:, n * wu_c:(n + 1) * wu_c])
            u = _dot(x, wu_s[:, n * wu_c:(n + 1) * wu_c])
            hid.append((_silu(g) * u).astype(BF16))
        hid = jnp.concatenate(hid, axis=1)
        wd_c = D_MODEL // N_DN_CHUNKS
        ys = y_slot(i)
        for n in range(N_DN_CHUNKS):
            issue_group(N_UP_CHUNKS + n)
            ybuf[ys, :, n * wd_c:(n + 1) * wd_c] = _dot(hid, wd_s[:, n * wd_c:(n + 1) * wd_c])

    @pl.when(i == nt)
    def _():
        gather_wait(xs)
        gather_wait((i + 1) % X_SLOTS)

        def body(j, c):
            scatter_row(nt - 1, y_slot(nt - 1), j)
            return c
        lax.fori_loop(0, BLK_PER_TILE, body, 0)
        scatter_wait(y_slot(nt - 2))
        scatter_wait(y_slot(nt - 1))


def _moe(texp, meta, code, hs, w_eg, w_eu, w_ed):
    wmap = lambda i, te, me, co: (te[i], 0, 0)
    grid_spec = pltpu.PrefetchScalarGridSpec(
        num_scalar_prefetch=3,
        grid=(NT_MOE + 1,),
        in_specs=[pl.BlockSpec(memory_space=pl.ANY),
                  pl.BlockSpec((None, D_MODEL, D_EXP), wmap),
                  pl.BlockSpec((None, D_MODEL, D_EXP), wmap),
                  pl.BlockSpec((None, D_EXP, D_MODEL), wmap)],
        out_specs=pl.BlockSpec(memory_space=pl.ANY),
        scratch_shapes=[pltpu.VMEM((X_SLOTS, TM_MOE, D_MODEL), F32),
                        pltpu.VMEM((Y_SLOTS, TM_MOE, D_MODEL), F32),
                        pltpu.VMEM((D_MODEL, D_EXP), BF16),
                        pltpu.VMEM((D_MODEL, D_EXP), BF16),
                        pltpu.VMEM((D_EXP, D_MODEL), BF16),
                        pltpu.SemaphoreType.DMA((X_SLOTS,)),
                        pltpu.SemaphoreType.DMA((Y_SLOTS,))])
    return pl.pallas_call(
        _moe_kernel,
        grid_spec=grid_spec,
        out_shape=jax.ShapeDtypeStruct((HS_ROWS, D_MODEL), F32),
        input_output_aliases={3: 0},
        compiler_params=_cparams(("arbitrary",)),
        name="moe",
    )(texp, meta, code, hs, w_eg, w_eu, w_ed)


TM_FIN = TM_OUT


def _final_kernel(x1_ref, ys_pack_ref, rw_ref, mod_ref, g_ref, yp_ref, ys_ref):
    i = pl.program_id(0)
    ga2 = mod_ref[:, 5 * D_MODEL:6 * D_MODEL]
    w0 = rw_ref[:, 0:1]
    w1 = rw_ref[:, 1:2]
    p0 = rw_ref[:, 2:3]
    p1 = rw_ref[:, 3:4]
    col = lax.broadcasted_iota(I32, (TM_FIN, PACK_ROWS), 1).astype(F32)
    comb = jnp.where(col == p0, w0, 0.0) + jnp.where(col == p1, w1, 0.0)
    y_moe = _dot(comb.astype(BF16), ys_pack_ref[...].astype(BF16))
    y = x1_ref[...] + ga2 * y_moe
    out = _rms(y, g_ref[...])

    @pl.when(i < T_CTX // TM_FIN)
    def _():
        yp_ref[...] = out

    @pl.when(i >= T_CTX // TM_FIN)
    def _():
        ys_ref[...] = out


def _final(x1, y2, rw, mod3, g_fin):
    nt = T_ALL // TM_FIN
    nctx = T_CTX // TM_FIN
    return pl.pallas_call(
        _final_kernel,
        grid=(nt,),
        in_specs=[pl.BlockSpec((TM_FIN, D_MODEL), lambda i: (i, 0)),
                  pl.BlockSpec((PACK_ROWS, D_MODEL), lambda i: (i, 0)),
                  pl.BlockSpec((TM_FIN, 8), lambda i: (i, 0)),
                  pl.BlockSpec((None, 1, 6 * D_MODEL), lambda i: (_cond_row(i, TM_FIN), 0, 0)),
                  pl.BlockSpec((1, D_MODEL), lambda i: (0, 0))],
        out_specs=[pl.BlockSpec((TM_FIN, D_MODEL), lambda i: (jnp.minimum(i, nctx - 1), 0)),
                   pl.BlockSpec((TM_FIN, D_MODEL), lambda i: (jnp.maximum(i - nctx, 0), 0))],
        out_shape=[jax.ShapeDtypeStruct((T_CTX, D_MODEL), F32),
                   jax.ShapeDtypeStruct((T_LAT, D_MODEL), F32)],
        compiler_params=_cparams(("arbitrary",)),
        name="final",
    )(x1, y2, rw, mod3, g_fin)


def _np_bf16(a):
    return jnp.asarray(np.asarray(a, np.float32), dtype=BF16)


def _np_f32(a):
    return jnp.asarray(np.asarray(a, np.float32))


@functools.lru_cache(maxsize=None)
def _constants():
    c = {}
    k = np.arange(FN_C)
    ang = 2.0 * np.pi * np.outer(k, k) / FN_C
    c["cs"] = np.concatenate([np.cos(ang), np.sin(ang)], axis=1) / np.sqrt(FN_C)
    p = np.arange(L_CTX)
    ang = 2.0 * np.pi * np.outer(p, p) / L_CTX
    c["p2"] = np.concatenate([np.cos(ang), -np.sin(ang)], axis=1) / np.sqrt(L_CTX)
    pos = np.arange(L_LAT)
    rr, cc = pos // GRID_W, pos % GRID_W
    num = (np.outer(rr, rr) * (GRID_W // GRID_H) + np.outer(cc, cc)) % GRID_W
    ang = 2.0 * np.pi * num / GRID_W
    c["kr"] = np.concatenate([np.cos(ang), -np.sin(ang)], axis=1) / np.sqrt(L_LAT)
    i = np.arange(SB)
    same = (i[:, None] // CHUNK) == (i[None, :] // CHUNK)
    tl = same & (i[:, None] >= i[None, :])
    tu = same & (i[:, None] <= i[None, :])
    c["tlu"] = np.concatenate([tl, tu], axis=1).astype(np.float32)
    c["tri"] = (i[:, None] > i[None, :]).astype(np.float32)
    k = np.arange(LANES)
    c["su"] = (k[:, None] < k[None, :]).astype(np.float32)
    return c


def kernel(x_prompt, x_sample, state_gla_fwd, state_gla_bwd, c, c_ctx, w_ada, b_ada, norm_attn, norm_ffn, w_in, w_gate_fwd, b_gate_fwd, w_gate_bwd, b_gate_bwd, norm_gla, w_out, w_router_group, b_router_group, w_router_expert, b_router_expert, w_expert_gate, w_expert_up, w_expert_down, norm_final):
    assert w_ada.shape[0] == 1, "single layer"
    cst = _constants()
    cs, p2, kr = _np_f32(cst["cs"]), _np_f32(cst["p2"]), _np_f32(cst["kr"])
    tlu, tri, su = _np_bf16(cst["tlu"]), _np_bf16(cst["tri"]), _np_bf16(cst["su"])

    xp = x_prompt.reshape(T_CTX, D_MODEL)
    xs = x_sample.reshape(T_LAT, D_MODEL)

    cond8 = jnp.concatenate([c_ctx[None, :], c, jnp.zeros((3, D_MODEL), F32)], axis=0)
    mod = _ada(cond8, w_ada[0], b_ada[0][None, :])
    mod3 = mod.reshape(8, 1, 6 * D_MODEL)

    wi = w_in[0]
    i_og = 2 * QK_W + 2 * V_W
    i_u = i_og + 2 * RANK
    w_main = jnp.concatenate([wi[:, :i_og], wi[:, i_u:]], axis=1).astype(BF16)
    w_r = jnp.pad(wi[:, i_og:i_u], ((0, 0), (0, LANES - 2 * RANK))).astype(BF16)

    wgf = w_gate_fwd[0].reshape(RANK, HEADS, DK)
    wgb = w_gate_bwd[0].reshape(RANK, HEADS, DK)
    zf = jnp.zeros_like(wgf)
    top = jnp.stack([wgf, zf], axis=2)
    bot = jnp.stack([zf, wgb], axis=2)
    wg = jnp.concatenate([top, bot], axis=0)
    wg = wg.reshape(2 * RANK, HEADS // 2, 4 * DK).transpose(1, 0, 2)
    wg = jnp.pad(wg, ((0, 0), (0, LANES - 2 * RANK), (0, 0)))
    wg_hi = wg.astype(BF16)
    wg_lo = (wg - wg_hi.astype(F32)).astype(BF16)
    bg = jnp.stack([b_gate_fwd[0].reshape(HEADS, DK), b_gate_bwd[0].reshape(HEADS, DK)], axis=1)
    bg = bg.reshape(HEADS // 2, 1, 4 * DK)
    ng = norm_gla[0].reshape(HEADS // 2, 1, 2 * DV)

    proj, r_all = _inproj(xp, xs, mod3, norm_attn, w_main, w_r)

    zero_state = jnp.zeros((N_CTX, HEADS, DK, DV), F32)
    o_ctx, sf_ctx, sb_ctx = _gla(proj, r_all, wg_hi, wg_lo, bg, ng, tlu, zero_state, zero_state,
                                 seq_len=L_CTX, n_seq=N_CTX, n_in_step=4, row0=0)
    o_lat, _, _ = _gla(proj, r_all, wg_hi, wg_lo, bg, ng, tlu,
                       state_gla_fwd[:, 0], state_gla_bwd[:, 0],
                       seq_len=L_LAT, n_seq=N_LAT, n_in_step=1, row0=T_CTX)

    f_ctx = _fnet_ctx(proj, cs, p2)
    f_lat = _fnet_lat(proj, cs, kr)

    wo = w_out[0][:V_W].astype(BF16)
    wf = w_out[0][V_W:].astype(BF16)
    wr = jnp.concatenate([w_router_group[0], w_router_expert[0]], axis=1)
    wr = jnp.pad(wr, ((0, 0), (0, LANES - N_GROUPS - N_EXP)))
    wr_hi = wr.astype(BF16)
    wr_lo = (wr - wr_hi.astype(F32)).astype(BF16)
    br = jnp.pad(jnp.concatenate([b_router_group[0], b_router_expert[0]]), (0, LANES - N_GROUPS - N_EXP))[None, :]

    x1, hs, rw, nb, lb = _outproj(o_ctx, o_lat, f_ctx, f_lat, xp, xs, mod3, norm_ffn, wo, wf,
                                  wr_hi, wr_lo, br, tri, su)

    nb_e = nb[:, 0, LANE_E0:LANE_E0 + N_EXP].T
    lb_e = lb[:, 0, LANE_E0:LANE_E0 + N_EXP].T
    run_end = jnp.cumsum(nb_e, axis=1)
    blocks_e = run_end[:, -1]
    tiles_e = (blocks_e + BLK_PER_TILE - 1) // BLK_PER_TILE
    tile_end = jnp.cumsum(tiles_e)
    tile_start = tile_end - tiles_e
    n_tiles = tile_end[-1]
    n_code_tiles = NT_MOE + 4
    tile = jnp.arange(n_code_tiles, dtype=I32) - 2
    tile_c = jnp.clip(tile, 0, n_tiles - 1)
    t_exp = jnp.sum(tile_c[:, None] >= tile_end[None, :], axis=1)
    ends = run_end[t_exp]
    starts = ends - nb_e[t_exp]
    offs = lb_e[t_exp] + jnp.arange(N_TOK_TILES, dtype=I32)[None, :] * PACK_BLKS - starts
    j = jnp.arange(BLK_PER_TILE, dtype=I32)
    bi = ((tile_c - tile_start[t_exp]) * BLK_PER_TILE)[:, None] + j[None, :]
    in_run = (starts[:, None, :] <= bi[:, :, None]) & (bi[:, :, None] < ends[:, None, :])
    blk = bi + jnp.sum(jnp.where(in_run, offs[:, None, :], 0), axis=2)
    valid = (tile == tile_c)[:, None] & (bi < blocks_e[t_exp][:, None])
    spare = ((jnp.arange(n_code_tiles, dtype=I32) % 2)[:, None] * BLK_PER_TILE + j[None, :]) * PACK_BLKS + USED_BLKS
    code = jnp.where(valid, (blk << SRC_BITS) | blk, (spare << SRC_BITS) | blk[:, 0:1]).astype(I32).reshape(-1)
    tidx = jnp.minimum(jnp.arange(NT_MOE + 1, dtype=I32), n_tiles - 1)
    texp = jnp.sum(tidx[:, None] >= tile_end[None, :], axis=1).astype(I32)
    meta = n_tiles.reshape(1).astype(I32)

    y2 = _moe(texp, meta, code, hs, w_expert_gate[0], w_expert_up[0], w_expert_down[0])
    y_prompt, y_sample = _final(x1, y2, rw, mod3, norm_final[None, :])

    st_shape = (N_CTX, 1, HEADS, DK, DV)
    return (y_prompt.reshape(N_CTX, L_CTX, D_MODEL), y_sample.reshape(N_LAT, L_LAT, D_MODEL),
            sf_ctx.reshape(st_shape), sb_ctx.reshape(st_shape))
```

```python
import functools

import numpy as np
import jax
import jax.numpy as jnp
from jax import lax
from jax.experimental import pallas as pl
from jax.experimental.pallas import tpu as pltpu

F32 = jnp.float32
BF16 = jnp.bfloat16
I32 = jnp.int32

D_MODEL = 2048
N_CTX = 32
L_CTX = 256
N_LAT = 4
L_LAT = 2048
GRID_H = 32
GRID_W = 64
T_CTX = N_CTX * L_CTX
T_LAT = N_LAT * L_LAT
T_ALL = T_CTX + T_LAT
HEADS = 8
DK = 64
DV = 128
RANK = 16
TAU = 16.0
CHUNK = 64
FN_G = 8
FN_C = 128
QK_W = HEADS * DK
V_W = HEADS * DV
N_GROUPS = 4
EPG = 8
N_EXP = N_GROUPS * EPG
D_EXP = 512
EPS = 1e-6

LANES = 128
VMEM_LIMIT = 56 * 1024 * 1024

TM_MOE = 256


def _dot(a, b):
    return jnp.dot(a, b, preferred_element_type=F32)


def _dot_nt(a, b):
    return lax.dot_general(a, b, (((1,), (1,)), ((), ())), preferred_element_type=F32)


def _split2(x):
    hi = x.astype(BF16)
    lo = (x - hi.astype(F32)).astype(BF16)
    return hi, lo


def _silu(x):
    return x * (1.0 / (1.0 + jnp.exp(-x)))


def _rms(x, g):
    return x * lax.rsqrt(jnp.mean(x * x, axis=-1, keepdims=True) + EPS) * g


def _cparams(sem):
    return pltpu.CompilerParams(dimension_semantics=sem, vmem_limit_bytes=VMEM_LIMIT)


def _ada_kernel(c_ref, w_ref, b_ref, o_ref):
    s_hi, s_lo = _split2(_silu(c_ref[...]))
    w = w_ref[...]
    w_hi = w.astype(BF16)
    w_lo = (w - w_hi.astype(F32)).astype(BF16)
    o_ref[...] = _dot(s_hi, w_hi) + _dot(s_lo, w_hi) + _dot(s_hi, w_lo) + b_ref[...]


def _ada(cond8, w_ada, b_ada):
    tn = 768
    n6 = 6 * D_MODEL
    return pl.pallas_call(
        _ada_kernel,
        grid=(n6 // tn,),
        in_specs=[pl.BlockSpec((8, D_MODEL), lambda j: (0, 0)),
                  pl.BlockSpec((D_MODEL, tn), lambda j: (0, j)),
                  pl.BlockSpec((1, tn), lambda j: (0, j))],
        out_specs=pl.BlockSpec((8, tn), lambda j: (0, j)),
        out_shape=jax.ShapeDtypeStruct((8, n6), F32),
        compiler_params=_cparams(("arbitrary",)),
        name="ada",
    )(cond8, w_ada, b_ada)


TM_IN = 256
TN_IN = 1024
N_MAIN = 4096


def _cond_row(tile, tm):
    ctx_tiles = T_CTX // tm
    per_seq = L_LAT // tm
    return jnp.where(tile < ctx_tiles, 0, 1 + (jnp.maximum(tile - ctx_tiles, 0)) // per_seq)


def _inproj_kernel(xp_ref, xs_ref, mod_ref, g_ref, w_ref, wr_ref, proj_ref, r_ref):
    i = pl.program_id(0)
    x = jnp.where(i < T_CTX // TM_IN, xp_ref[...], xs_ref[...])
    sh1 = mod_ref[:, 0:D_MODEL]
    sc1 = mod_ref[:, D_MODEL:2 * D_MODEL]
    hb = (_rms(x, g_ref[...]) * (1.0 + sc1) + sh1).astype(BF16)
    r_ref[...] = _dot(hb, wr_ref[...])
    for n in range(N_MAIN // TN_IN):
        proj_ref[:, n * TN_IN:(n + 1) * TN_IN] = _dot(hb, w_ref[:, n * TN_IN:(n + 1) * TN_IN]).astype(BF16)


def _inproj(xp, xs, mod3, g_attn, w_main, w_r):
    nt = T_ALL // TM_IN
    nctx = T_CTX // TM_IN
    resident = pl.Buffered(1)
    return pl.pallas_call(
        _inproj_kernel,
        grid=(nt,),
        in_specs=[pl.BlockSpec((TM_IN, D_MODEL), lambda i: (jnp.minimum(i, nctx - 1), 0)),
                  pl.BlockSpec((TM_IN, D_MODEL), lambda i: (jnp.maximum(i - nctx, 0), 0)),
                  pl.BlockSpec((None, 1, 6 * D_MODEL), lambda i: (_cond_row(i, TM_IN), 0, 0)),
                  pl.BlockSpec((1, D_MODEL), lambda i: (0, 0)),
                  pl.BlockSpec((D_MODEL, N_MAIN), lambda i: (0, 0), pipeline_mode=resident),
                  pl.BlockSpec((D_MODEL, LANES), lambda i: (0, 0), pipeline_mode=resident)],
        out_specs=[pl.BlockSpec((TM_IN, N_MAIN), lambda i: (i, 0)),
                   pl.BlockSpec((TM_IN, LANES), lambda i: (i, 0))],
        out_shape=[jax.ShapeDtypeStruct((T_ALL, N_MAIN), BF16),
                   jax.ShapeDtypeStruct((T_ALL, LANES), F32)],
        compiler_params=_cparams(("arbitrary",)),
        name="inproj",
    )(xp, xs, mod3, g_attn, w_main, w_r)


SB = 256
CPB = SB // CHUNK


def _gla_kernel(q_ref, k_ref, v_ref, og_ref, r_ref, wgh_ref, wgl_ref, bg_ref, ng_ref,
                tlu_ref, s0f_ref, s0b_ref,
                o_ref, sf_ref, sb_ref,
                cum_s, last_s, qe_s, ke_s, kd_s, dec_s, kv_s, sbs_s, *, seq_len, n_in_step):
    sb_per_seq = seq_len // SB
    ch_per_seq = seq_len // CHUNK
    n_sb = n_in_step * sb_per_seq
    lane = lax.broadcasted_iota(I32, (1, LANES), 1)
    m_f = lane < DK

    r_hi, r_lo = _split2(r_ref[...])
    z = _dot(r_hi, wgh_ref[...]) + _dot(r_lo, wgh_ref[...]) + _dot(r_hi, wgl_ref[...]) + bg_ref[...]
    g_all = (jnp.minimum(z, 0.0) - jnp.log(1.0 + jnp.exp(-jnp.abs(z)))) * (1.0 / TAU)

    q_pair = q_ref[...].astype(F32)
    k_pair = k_ref[...].astype(F32)
    q_roll = pltpu.roll(q_pair, DK, axis=1)
    k_roll = pltpu.roll(k_pair, DK, axis=1)

    row_b = lax.broadcasted_iota(I32, (SB, SB), 0)
    col_b = lax.broadcasted_iota(I32, (SB, SB), 1)
    same_chunk = (row_b // CHUNK) == (col_b // CHUNK)
    tril = same_chunk & (row_b >= col_b)
    triu = same_chunk & (row_b <= col_b)
    row_chunk = lax.broadcasted_iota(I32, (SB, 1), 0) // CHUNK
    col_chunk = lax.broadcasted_iota(I32, (1, SB), 1) // CHUNK

    m_f2 = (lax.broadcasted_iota(I32, (1, 2 * LANES), 1) % LANES) < DK
    cum_s[...] = g_all

    def cum_body(s, carry):
        rows = pl.ds(pl.multiple_of(s * SB, SB), SB)
        g = cum_s[rows, :]
        f_hi, f_lo = _split2(jnp.where(m_f2, g, 0.0))
        b_hi, b_lo = _split2(jnp.where(m_f2, 0.0, g))
        cum = (_dot(tlu_ref[...], jnp.concatenate([f_hi, b_hi], axis=0))
               + _dot(tlu_ref[...], jnp.concatenate([f_lo, b_lo], axis=0)))
        cum_s[rows, :] = cum
        tots = []
        for c in range(CPB):
            tot = jnp.where(m_f2, cum[(c + 1) * CHUNK - 1:(c + 1) * CHUNK, :], cum[c * CHUNK:c * CHUNK + 1, :])
            dec_s[s * CPB + c] = jnp.exp(tot)
            tots.append(jnp.broadcast_to(tot, (CHUNK, 2 * LANES)))
        last_s[rows, :] = jnp.concatenate(tots, axis=0)
        return carry

    lax.fori_loop(0, n_sb, cum_body, 0)

    for j in range(2):
        cum = cum_s[:, j * LANES:(j + 1) * LANES]
        last = last_s[:, j * LANES:(j + 1) * LANES]
        if j == 0:
            q2 = jnp.where(m_f, q_pair, q_roll)
            k2 = jnp.where(m_f, k_pair, k_roll)
        else:
            q2 = jnp.where(m_f, q_roll, q_pair)
            k2 = jnp.where(m_f, k_roll, k_pair)
        qe_s[j] = ((q2 * (DK ** -0.5)) * jnp.exp(cum)).astype(BF16)
        ke_s[j] = (k2 * jnp.exp(-cum)).astype(BF16)
        kd_s[j] = (k2 * jnp.exp(last - cum)).astype(BF16)

    def kv_body(s, carry):
        rows = pl.ds(pl.multiple_of(s * SB, SB), SB)
        for j in range(2):
            v_t = jnp.transpose(v_ref[rows, j * DV:(j + 1) * DV].astype(F32)).astype(BF16)
            zero = jnp.zeros_like(v_t)
            v_st = jnp.concatenate([jnp.where(col_chunk == c, v_t, zero) for c in range(CPB)], axis=0)
            kv = _dot(v_st, kd_s[j, rows, :])
            for c in range(CPB):
                kv_s[j, s * CPB + c] = kv[c * LANES:(c + 1) * LANES, :]
        return carry

    lax.fori_loop(0, n_sb, kv_body, 0, unroll=min(2, n_sb))

    def dec_row(j, c):
        return dec_s[c][:, j * LANES:(j + 1) * LANES]

    def step_state(j, c, st):
        return st * dec_row(j, c) + kv_s[j, c]

    for q in range(n_in_step):
        st0 = tuple(jnp.transpose(jnp.concatenate([s0f_ref[q, j], s0b_ref[q, j]], axis=0)) for j in range(2))
        c0 = q * ch_per_seq
        s0 = q * sb_per_seq

        def bwd_body(t, sts, c0=c0):
            c = c0 + ch_per_seq - 1 - t
            for j in range(2):
                sbs_s[j, c] = sts[j]
            return tuple(step_state(j, c, sts[j]) for j in range(2))

        st_b_fin = lax.fori_loop(0, ch_per_seq, bwd_body, st0, unroll=CPB)

        def fwd_body(sl, sts, s0=s0):
            s = s0 + sl
            rows = pl.ds(pl.multiple_of(s * SB, SB), SB)
            new = []
            for j in range(2):
                lo = j * DV
                qe = qe_s[j, rows, :]
                ke = ke_s[j, rows, :]
                v_b = v_ref[rows, lo:lo + DV].astype(BF16)
                zero = jnp.zeros_like(qe)
                st = sts[j]
                q_parts, s_parts = [], []
                for c in range(CPB):
                    ci = s * CPB + c
                    s_parts.append(jnp.where(m_f, st, sbs_s[j, ci]).astype(BF16))
                    q_parts.append(jnp.where(row_chunk == c, qe, zero))
                    st = step_state(j, ci, st)
                o = _dot_nt(jnp.concatenate(q_parts, axis=1), jnp.concatenate(s_parts, axis=1))
                q_st = jnp.concatenate([jnp.where(m_f, qe, zero), jnp.where(m_f, zero, qe)], axis=0)
                sc = _dot_nt(q_st, ke)
                p = jnp.where(tril, sc[0:SB, :], 0.0) + jnp.where(triu, sc[SB:2 * SB, :], 0.0)
                o = o + _dot(p.astype(BF16), v_b)
                o = o * lax.rsqrt(jnp.mean(o * o, axis=-1, keepdims=True) + EPS) * ng_ref[:, lo:lo + DV]
                o_ref[rows, lo:lo + DV] = (o * _silu(og_ref[rows, lo:lo + DV].astype(F32))).astype(BF16)
                new.append(st)
            return tuple(new)

        st_f_fin = lax.fori_loop(0, sb_per_seq, fwd_body, st0, unroll=min(2, sb_per_seq))
        for j in range(2):
            sf_ref[q, j] = jnp.transpose(st_f_fin[j])[0:DK, :]
            sb_ref[q, j] = jnp.transpose(st_b_fin[j])[DK:2 * DK, :]


def _gla(proj, r_all, wg_hi, wg_lo, bg, ng, tlu, s0f, s0b, *, seq_len, n_seq, n_in_step, row0):
    kern = functools.partial(_gla_kernel, seq_len=seq_len, n_in_step=n_in_step)
    rows = n_in_step * seq_len
    n_chunks = rows // CHUNK
    rb = lambda b: b + row0 // rows
    st_spec = pl.BlockSpec((n_in_step, 2, DK, DV), lambda b, h: (b, h, 0, 0))
    return pl.pallas_call(
        kern,
        grid=(n_seq // n_in_step, HEADS // 2),
        in_specs=[pl.BlockSpec((rows, LANES), lambda b, h: (rb(b), h)),
                  pl.BlockSpec((rows, LANES), lambda b, h: (rb(b), QK_W // LANES + h)),
                  pl.BlockSpec((rows, 2 * DV), lambda b, h: (rb(b), 2 * QK_W // (2 * DV) + h)),
                  pl.BlockSpec((rows, 2 * DV), lambda b, h: (rb(b), (2 * QK_W + V_W) // (2 * DV) + h)),
                  pl.BlockSpec((rows, LANES), lambda b, h: (rb(b), 0)),
                  pl.BlockSpec((None, LANES, 2 * LANES), lambda b, h: (h, 0, 0)),
                  pl.BlockSpec((None, LANES, 2 * LANES), lambda b, h: (h, 0, 0)),
                  pl.BlockSpec((None, 1, 2 * LANES), lambda b, h: (h, 0, 0)),
                  pl.BlockSpec((None, 1, 2 * DV), lambda b, h: (h, 0, 0)),
                  pl.BlockSpec((SB, 2 * SB), lambda b, h: (0, 0)),
                  st_spec, st_spec],
        out_specs=[pl.BlockSpec((rows, 2 * DV), lambda b, h: (b, h)), st_spec, st_spec],
        out_shape=[jax.ShapeDtypeStruct((n_seq * seq_len, V_W), BF16),
                   jax.ShapeDtypeStruct((n_seq, HEADS, DK, DV), F32),
                   jax.ShapeDtypeStruct((n_seq, HEADS, DK, DV), F32)],
        scratch_shapes=[pltpu.VMEM((rows, 2 * LANES), F32),
                        pltpu.VMEM((rows, 2 * LANES), F32),
                        pltpu.VMEM((2, rows, LANES), BF16),
                        pltpu.VMEM((2, rows, LANES), BF16),
                        pltpu.VMEM((2, rows, LANES), BF16),
                        pltpu.VMEM((n_chunks, 1, 2 * LANES), F32),
                        pltpu.VMEM((2, n_chunks, DV, LANES), F32),
                        pltpu.VMEM((2, n_chunks, DV, LANES), F32)],
        compiler_params=_cparams(("arbitrary", "arbitrary")),
        name="gla_%d" % seq_len,
    )(proj, proj, proj, proj, r_all, wg_hi, wg_lo, bg, ng, tlu, s0f, s0b)


def _fnet_stage_a(u_bf, cs):
    cparts, sparts = [], []
    for g in range(FN_G):
        ab = _dot(u_bf[:, g * FN_C:(g + 1) * FN_C], cs)
        cparts.append(ab[:, 0:FN_C])
        sparts.append(ab[:, FN_C:2 * FN_C])
    return jnp.concatenate(cparts, axis=1), jnp.concatenate(sparts, axis=1)


def _fnet_ctx_kernel(u_ref, cs_ref, p2_ref, f_ref):
    uc, us = _fnet_stage_a(u_ref[...].astype(BF16), cs_ref[...].astype(BF16))
    ab = jnp.concatenate([uc, us], axis=0).astype(BF16)
    f_ref[...] = _dot(p2_ref[...].astype(BF16), ab).astype(BF16)


def _fnet_ctx(proj, cs, p2):
    return pl.pallas_call(
        _fnet_ctx_kernel,
        grid=(N_CTX,),
        in_specs=[pl.BlockSpec((L_CTX, FN_G * FN_C), lambda b: (b, 3)),
                  pl.BlockSpec((FN_C, 2 * FN_C), lambda b: (0, 0)),
                  pl.BlockSpec((L_CTX, 2 * L_CTX), lambda b: (0, 0))],
        out_specs=pl.BlockSpec((L_CTX, FN_G * FN_C), lambda b: (b, 0)),
        out_shape=jax.ShapeDtypeStruct((T_CTX, FN_G * FN_C), BF16),
        compiler_params=_cparams(("arbitrary",)),
        name="fnet_ctx",
    )(proj, cs, p2)


TM_FL = 256
RT_FL = 256


def _fnet_lat_kernel(u_ref, cs_ref, kr_ref, f_ref, ab_s):
    m = pl.program_id(1)

    @pl.when(m == 0)
    def _():
        def body(t, carry):
            rows = pl.ds(pl.multiple_of(t * RT_FL, RT_FL), RT_FL)
            uc, us = _fnet_stage_a(u_ref[rows, :].astype(BF16), cs_ref[...].astype(BF16))
            ab_s[rows, :] = uc.astype(BF16)
            ab_s[pl.ds(pl.multiple_of(L_LAT + t * RT_FL, RT_FL), RT_FL), :] = us.astype(BF16)
            return carry
        lax.fori_loop(0, L_LAT // RT_FL, body, 0)

    f_ref[...] = _dot(kr_ref[...].astype(BF16), ab_s[...]).astype(BF16)


def _fnet_lat(proj, cs, kr):
    nm = L_LAT // TM_FL
    return pl.pallas_call(
        _fnet_lat_kernel,
        grid=(N_LAT, nm),
        in_specs=[pl.BlockSpec((L_LAT, FN_G * FN_C), lambda b, m: (T_CTX // L_LAT + b, 3)),
                  pl.BlockSpec((FN_C, 2 * FN_C), lambda b, m: (0, 0)),
                  pl.BlockSpec((TM_FL, 2 * L_LAT), lambda b, m: (m, 0))],
        out_specs=pl.BlockSpec((TM_FL, FN_G * FN_C), lambda b, m: (b * nm + m, 0)),
        out_shape=jax.ShapeDtypeStruct((T_LAT, FN_G * FN_C), BF16),
        scratch_shapes=[pltpu.VMEM((2 * L_LAT, FN_G * FN_C), BF16)],
        compiler_params=_cparams(("arbitrary", "arbitrary")),
        name="fnet_lat",
    )(proj, cs, kr)


TM_OUT = 256
LANE_E0 = N_GROUPS
ROWS_PER_BLK = 8
PACK_ROWS = -(-(2 * TM_OUT + N_EXP * (ROWS_PER_BLK - 1)) // 256) * 256
PACK_BLKS = PACK_ROWS // ROWS_PER_BLK
N_TOK_TILES = T_ALL // TM_OUT
BLK_PER_TILE = TM_MOE // ROWS_PER_BLK
USED_BLKS = (2 * TM_OUT + N_EXP * (ROWS_PER_BLK - 1)) // ROWS_PER_BLK
HS_ROWS = N_TOK_TILES * PACK_ROWS
assert USED_BLKS < PACK_BLKS and 2 * BLK_PER_TILE <= N_TOK_TILES


def _outproj_kernel(oc_ref, ol_ref, fc_ref, fl_ref, xp_ref, xs_ref, mod_ref, g_ref, wo_ref, wf_ref,
                    wrh_ref, wrl_ref, br_ref, tri_ref, su_ref,
                    x1_ref, hs_ref, rw_ref, nb_ref, lb_ref):
    i = pl.program_id(0)
    is_ctx = i < T_CTX // TM_OUT

    o = jnp.where(is_ctx, oc_ref[...], ol_ref[...]).astype(BF16)
    f = jnp.where(is_ctx, fc_ref[...], fl_ref[...]).astype(BF16)
    x = jnp.where(is_ctx, xp_ref[...], xs_ref[...])
    y = _dot(o, wo_ref[...]) + _dot(f, wf_ref[...])
    ga1 = mod_ref[:, 2 * D_MODEL:3 * D_MODEL]
    sh2 = mod_ref[:, 3 * D_MODEL:4 * D_MODEL]
    sc2 = mod_ref[:, 4 * D_MODEL:5 * D_MODEL]
    x1 = x + ga1 * y
    x1_ref[...] = x1
    h2 = _rms(x1, g_ref[...]) * (1.0 + sc2) + sh2

    h_hi, h_lo = _split2(h2)
    lg_all = _dot(h_hi, wrh_ref[...]) + _dot(h_lo, wrh_ref[...]) + _dot(h_hi, wrl_ref[...]) + br_ref[...]

    lane_i = lax.broadcasted_iota(I32, (TM_OUT, LANES), 1)
    lane = lane_i.astype(F32)
    neg = jnp.float32(-jnp.inf)
    big = jnp.float32(LANES)
    lg = jnp.where(lane_i < N_GROUPS, lg_all, neg)
    gmax = jnp.max(lg, axis=1, keepdims=True)
    gsel = jnp.min(jnp.where(lg == gmax, lane, big), axis=1, keepdims=True)
    den = jnp.sum(jnp.exp(lg - gmax), axis=1, keepdims=True)
    pg_sel = 1.0 / den

    e_idx = lane_i - LANE_E0
    egrp = (e_idx >> 3).astype(F32)
    emask = (e_idx >= 0) & (e_idx < N_EXP) & (egrp == gsel)
    m1 = jnp.where(emask, lg_all, neg)
    v1 = jnp.max(m1, axis=1, keepdims=True)
    i1 = jnp.min(jnp.where(m1 == v1, lane, big), axis=1, keepdims=True)
    m2 = jnp.where(lane == i1, neg, m1)
    v2 = jnp.max(m2, axis=1, keepdims=True)
    i2 = jnp.min(jnp.where(m2 == v2, lane, big), axis=1, keepdims=True)
    e2 = jnp.exp(v2 - v1)
    inv = 1.0 / (1.0 + e2)
    w1 = inv * pg_sel
    w2 = (e2 * inv) * pg_sel

    oh1 = lane == i1
    oh2 = lane == i2
    oh = jnp.where(oh1 | oh2, 1.0, 0.0)
    cnt = jnp.sum(oh, axis=0, keepdims=True)
    nblk = jnp.floor((cnt + (ROWS_PER_BLK - 1)) * (1.0 / ROWS_PER_BLK))
    lboff = _dot(jnp.broadcast_to(nblk, (8, LANES)).astype(BF16), su_ref[...])[0:1, :]
    lrank = _dot(tri_ref[...], oh.astype(BF16))
    posmat = lboff * ROWS_PER_BLK + lrank
    p1 = jnp.sum(jnp.where(oh1, posmat, 0.0), axis=1, keepdims=True)
    p2 = jnp.sum(jnp.where(oh2, posmat, 0.0), axis=1, keepdims=True)
    nb_ref[...] = nblk.astype(I32)
    lb_ref[...] = lboff.astype(I32)

    col = lax.broadcasted_iota(I32, (TM_OUT, PACK_ROWS), 1).astype(F32)
    place_t = jnp.where((col == p1) | (col == p2), 1.0, 0.0).astype(BF16)
    hs_ref[...] = lax.dot_general(place_t, h_hi, (((0,), (0,)), ((), ())), preferred_element_type=F32)

    rw = jnp.where(lane_i == 0, w1, jnp.where(lane_i == 1, w2,
                                               jnp.where(lane_i == 2, p1, jnp.where(lane_i == 3, p2, 0.0))))
    rw_ref[...] = rw[:, 0:8]


def _outproj(o_ctx, o_lat, f_ctx, f_lat, xp, xs, mod3, g_ffn, wo, wf, wr_hi, wr_lo, br, tri, su):
    nt = T_ALL // TM_OUT
    nctx = T_CTX // TM_OUT
    ctx_map = lambda i: (jnp.minimum(i, nctx - 1), 0)
    lat_map = lambda i: (jnp.maximum(i - nctx, 0), 0)
    const = lambda i: (0, 0)
    return pl.pallas_call(
        _outproj_kernel,
        grid=(nt,),
        in_specs=[pl.BlockSpec((TM_OUT, V_W), ctx_map),
                  pl.BlockSpec((TM_OUT, V_W), lat_map),
                  pl.BlockSpec((TM_OUT, FN_G * FN_C), ctx_map),
                  pl.BlockSpec((TM_OUT, FN_G * FN_C), lat_map),
                  pl.BlockSpec((TM_OUT, D_MODEL), ctx_map),
                  pl.BlockSpec((TM_OUT, D_MODEL), lat_map),
                  pl.BlockSpec((None, 1, 6 * D_MODEL), lambda i: (_cond_row(i, TM_OUT), 0, 0)),
                  pl.BlockSpec((1, D_MODEL), const),
                  pl.BlockSpec((V_W, D_MODEL), const),
                  pl.BlockSpec((FN_G * FN_C, D_MODEL), const),
                  pl.BlockSpec((D_MODEL, LANES), const),
                  pl.BlockSpec((D_MODEL, LANES), const),
                  pl.BlockSpec((1, LANES), const),
                  pl.BlockSpec((TM_OUT, TM_OUT), const),
                  pl.BlockSpec((LANES, LANES), const)],
        out_specs=[pl.BlockSpec((TM_OUT, D_MODEL), lambda i: (i, 0)),
                   pl.BlockSpec((PACK_ROWS, D_MODEL), lambda i: (i, 0)),
                   pl.BlockSpec((TM_OUT, 8), lambda i: (i, 0)),
                   pl.BlockSpec((None, 1, LANES), lambda i: (i, 0, 0)),
                   pl.BlockSpec((None, 1, LANES), lambda i: (i, 0, 0))],
        out_shape=[jax.ShapeDtypeStruct((T_ALL, D_MODEL), F32),
                   jax.ShapeDtypeStruct((HS_ROWS, D_MODEL), F32),
                   jax.ShapeDtypeStruct((T_ALL, 8), F32),
                   jax.ShapeDtypeStruct((nt, 1, LANES), I32),
                   jax.ShapeDtypeStruct((nt, 1, LANES), I32)],
        compiler_params=_cparams(("arbitrary",)),
        name="outproj",
    )(o_ctx, o_lat, f_ctx, f_lat, xp, xs, mod3, g_ffn, wo, wf, wr_hi, wr_lo, br, tri, su)


SRC_BITS = 16
SRC_MASK = (1 << SRC_BITS) - 1
X_SLOTS = 3
Y_SLOTS = 3
N_UP_CHUNKS = 2
N_DN_CHUNKS = 8
NT_MOE = (2 * T_ALL + N_TOK_TILES * N_EXP * (ROWS_PER_BLK - 1)) // TM_MOE + N_EXP


def _moe_kernel(texp_ref, nexp_ref, meta_ref, code_ref,
                h_hbm, wg_hbm, wu_hbm, wd_hbm,
                out_hbm,
                xbuf, ybuf, wg_f, wu_f, wd_f, wg_s, wu_s, wd_s, kcount, gsem, ssem, wsem):
    i = pl.program_id(0)
    nt = meta_ref[0]
    xs = i % X_SLOTS

    def blk_rows(b):
        if isinstance(b, int):
            return pl.ds(b * ROWS_PER_BLK, ROWS_PER_BLK)
        return pl.ds(pl.multiple_of(b * ROWS_PER_BLK, ROWS_PER_BLK), ROWS_PER_BLK)

    def gather_row(tile, sl, j):
        src = code_ref[(tile + 2) * BLK_PER_TILE + j] & SRC_MASK
        pltpu.make_async_copy(h_hbm.at[blk_rows(src)], xbuf.at[sl, blk_rows(j)], gsem.at[sl]).start()

    def scatter_row(tile, sl, j):
        dst = code_ref[(tile + 2) * BLK_PER_TILE + j] >> SRC_BITS
        pltpu.make_async_copy(ybuf.at[sl, blk_rows(j)], out_hbm.at[blk_rows(dst)], ssem.at[sl]).start(priority=1)

    def gather_wait(sl):
        pltpu.make_async_copy(h_hbm.at[pl.ds(0, TM_MOE)], xbuf.at[sl], gsem.at[sl]).wait()

    def scatter_wait(sl):
        pltpu.make_async_copy(ybuf.at[sl], out_hbm.at[pl.ds(0, TM_MOE)], ssem.at[sl]).wait()

    def y_slot(tile):
        return (tile + 1) % Y_SLOTS

    @pl.when(i == 0)
    def _():
        ybuf[y_slot(-2)] = jnp.zeros((TM_MOE, D_MODEL), F32)
        ybuf[y_slot(-1)] = jnp.zeros((TM_MOE, D_MODEL), F32)

        def body(j, c):
            gather_row(0, 0, j)
            gather_row(1, 1, j)
            scatter_row(-2, y_slot(-2), j)
            return c
        lax.fori_loop(0, BLK_PER_TILE, body, 0)

    @pl.when((i >= 1) & (i <= nt))
    def _():
        scatter_wait(y_slot(i - 3))

    @pl.when(i < nt)
    def _():
        prev = texp_ref[jnp.maximum(i - 1, 0)]

        def weight_copies(e, sl):
            return (pltpu.make_async_copy(wg_hbm.at[e], wg_f.at[sl], wsem.at[sl]),
                    pltpu.make_async_copy(wu_hbm.at[e], wu_f.at[sl], wsem.at[sl]),
                    pltpu.make_async_copy(wd_hbm.at[e], wd_f.at[sl], wsem.at[sl]))

        @pl.when(i == 0)
        def _():
            kcount[0] = 0
            for cp in weight_copies(texp_ref[0], 0):
                cp.start()

        @pl.when((i == 0) | (texp_ref[i] != prev))
        def _():
            k = kcount[0]
            for sl in range(2):
                @pl.when(k % 2 == sl)
                def _(sl=sl):
                    for cp in weight_copies(texp_ref[i], sl):
                        cp.wait()
                    def narrow(r, c):
                        up = pl.ds(pl.multiple_of(r * (D_MODEL // 16), D_MODEL // 16), D_MODEL // 16)
                        dn = pl.ds(pl.multiple_of(r * (D_EXP // 16), D_EXP // 16), D_EXP // 16)
                        wg_s[up, :] = wg_f[sl, up, :].astype(BF16)
                        wu_s[up, :] = wu_f[sl, up, :].astype(BF16)
                        wd_s[dn, :] = wd_f[sl, dn, :].astype(BF16)
                        return c
                    lax.fori_loop(0, 16, narrow, 0)

                    @pl.when(nexp_ref[i] >= 0)
                    def _():
                        for cp in weight_copies(nexp_ref[i], 1 - sl):
                            cp.start()
            kcount[0] = k + 1

        gather_wait(xs)
        x = xbuf[xs].astype(BF16)

        issues = []
        for j in range(BLK_PER_TILE):
            issues.append(functools.partial(gather_row, i + 2, (i + 2) % X_SLOTS, j))
            issues.append(functools.partial(scatter_row, i - 1, y_slot(i - 1), j))
        n_groups = N_UP_CHUNKS + N_DN_CHUNKS
        per_group = -(-len(issues) // n_groups)

        def issue_group(k):
            for fn in issues[k * per_group:(k + 1) * per_group]:
                fn()

        wu_c = D_EXP // N_UP_CHUNKS
        hid = []
        for n in range(N_UP_CHUNKS):
            issue_group(n)
            g = _dot(x, wg_s[:, n * wu_c:(n + 1) * wu_c])
            u = _dot(x, wu_s[:, n * wu_c:(n + 1) * wu_c])
            hid.append((_silu(g) * u).astype(BF16))
        hid = jnp.concatenate(hid, axis=1)
        wd_c = D_MODEL // N_DN_CHUNKS
        ys = y_slot(i)
        for n in range(N_DN_CHUNKS):
            issue_group(N_UP_CHUNKS + n)
            ybuf[ys, :, n * wd_c:(n + 1) * wd_c] = _dot(hid, wd_s[:, n * wd_c:(n + 1) * wd_c])

    @pl.when(i == nt)
    def _():
        gather_wait(xs)
        gather_wait((i + 1) % X_SLOTS)

        def body(j, c):
            scatter_row(nt - 1, y_slot(nt - 1), j)
            return c
        lax.fori_loop(0, BLK_PER_TILE, body, 0)
        scatter_wait(y_slot(nt - 2))
        scatter_wait(y_slot(nt - 1))


def _moe(texp, nexp, meta, code, hs, w_eg, w_eu, w_ed):
    hbm = pl.BlockSpec(memory_space=pl.ANY)
    grid_spec = pltpu.PrefetchScalarGridSpec(
        num_scalar_prefetch=4,
        grid=(NT_MOE + 1,),
        in_specs=[hbm, hbm, hbm, hbm],
        out_specs=hbm,
        scratch_shapes=[pltpu.VMEM((X_SLOTS, TM_MOE, D_MODEL), F32),
                        pltpu.VMEM((Y_SLOTS, TM_MOE, D_MODEL), F32),
                        pltpu.VMEM((2, D_MODEL, D_EXP), F32),
                        pltpu.VMEM((2, D_MODEL, D_EXP), F32),
                        pltpu.VMEM((2, D_EXP, D_MODEL), F32),
                        pltpu.VMEM((D_MODEL, D_EXP), BF16),
                        pltpu.VMEM((D_MODEL, D_EXP), BF16),
                        pltpu.VMEM((D_EXP, D_MODEL), BF16),
                        pltpu.SMEM((1,), I32),
                        pltpu.SemaphoreType.DMA((X_SLOTS,)),
                        pltpu.SemaphoreType.DMA((Y_SLOTS,)),
                        pltpu.SemaphoreType.DMA((2,))])
    return pl.pallas_call(
        _moe_kernel,
        grid_spec=grid_spec,
        out_shape=jax.ShapeDtypeStruct((HS_ROWS, D_MODEL), F32),
        input_output_aliases={4: 0},
        compiler_params=_cparams(("arbitrary",)),
        name="moe",
    )(texp, nexp, meta, code, hs, w_eg, w_eu, w_ed)


TM_FIN = TM_OUT


def _final_kernel(x1_ref, ys_pack_ref, rw_ref, mod_ref, g_ref, yp_ref, ys_ref):
    i = pl.program_id(0)
    ga2 = mod_ref[:, 5 * D_MODEL:6 * D_MODEL]
    w0 = rw_ref[:, 0:1]
    w1 = rw_ref[:, 1:2]
    p0 = rw_ref[:, 2:3]
    p1 = rw_ref[:, 3:4]
    col = lax.broadcasted_iota(I32, (TM_FIN, PACK_ROWS), 1).astype(F32)
    comb = jnp.where(col == p0, w0, 0.0) + jnp.where(col == p1, w1, 0.0)
    y_moe = _dot(comb.astype(BF16), ys_pack_ref[...].astype(BF16))
    y = x1_ref[...] + ga2 * y_moe
    out = _rms(y, g_ref[...])

    @pl.when(i < T_CTX // TM_FIN)
    def _():
        yp_ref[...] = out

    @pl.when(i >= T_CTX // TM_FIN)
    def _():
        ys_ref[...] = out


def _final(x1, y2, rw, mod3, g_fin):
    nt = T_ALL // TM_FIN
    nctx = T_CTX // TM_FIN
    return pl.pallas_call(
        _final_kernel,
        grid=(nt,),
        in_specs=[pl.BlockSpec((TM_FIN, D_MODEL), lambda i: (i, 0)),
                  pl.BlockSpec((PACK_ROWS, D_MODEL), lambda i: (i, 0)),
                  pl.BlockSpec((TM_FIN, 8), lambda i: (i, 0)),
                  pl.BlockSpec((None, 1, 6 * D_MODEL), lambda i: (_cond_row(i, TM_FIN), 0, 0)),
                  pl.BlockSpec((1, D_MODEL), lambda i: (0, 0))],
        out_specs=[pl.BlockSpec((TM_FIN, D_MODEL), lambda i: (jnp.minimum(i, nctx - 1), 0)),
                   pl.BlockSpec((TM_FIN, D_MODEL), lambda i: (jnp.maximum(i - nctx, 0), 0))],
        out_shape=[jax.ShapeDtypeStruct((T_CTX, D_MODEL), F32),
                   jax.ShapeDtypeStruct((T_LAT, D_MODEL), F32)],
        compiler_params=_cparams(("arbitrary",)),
        name="final",
    )(x1, y2, rw, mod3, g_fin)


def _np_bf16(a):
    return jnp.asarray(np.asarray(a, np.float32), dtype=BF16)


def _np_f32(a):
    return jnp.asarray(np.asarray(a, np.float32))


@functools.lru_cache(maxsize=None)
def _constants():
    c = {}
    k = np.arange(FN_C)
    ang = 2.0 * np.pi * np.outer(k, k) / FN_C
    c["cs"] = np.concatenate([np.cos(ang), np.sin(ang)], axis=1) / np.sqrt(FN_C)
    p = np.arange(L_CTX)
    ang = 2.0 * np.pi * np.outer(p, p) / L_CTX
    c["p2"] = np.concatenate([np.cos(ang), -np.sin(ang)], axis=1) / np.sqrt(L_CTX)
    pos = np.arange(L_LAT)
    rr, cc = pos // GRID_W, pos % GRID_W
    num = (np.outer(rr, rr) * (GRID_W // GRID_H) + np.outer(cc, cc)) % GRID_W
    ang = 2.0 * np.pi * num / GRID_W
    c["kr"] = np.concatenate([np.cos(ang), -np.sin(ang)], axis=1) / np.sqrt(L_LAT)
    i = np.arange(SB)
    same = (i[:, None] // CHUNK) == (i[None, :] // CHUNK)
    tl = same & (i[:, None] >= i[None, :])
    tu = same & (i[:, None] <= i[None, :])
    c["tlu"] = np.concatenate([tl, tu], axis=1).astype(np.float32)
    c["tri"] = (i[:, None] > i[None, :]).astype(np.float32)
    k = np.arange(LANES)
    c["su"] = (k[:, None] < k[None, :]).astype(np.float32)
    return c


def kernel(x_prompt, x_sample, state_gla_fwd, state_gla_bwd, c, c_ctx, w_ada, b_ada, norm_attn, norm_ffn, w_in, w_gate_fwd, b_gate_fwd, w_gate_bwd, b_gate_bwd, norm_gla, w_out, w_router_group, b_router_group, w_router_expert, b_router_expert, w_expert_gate, w_expert_up, w_expert_down, norm_final):
    assert w_ada.shape[0] == 1, "single layer"
    cst = _constants()
    cs, p2, kr = _np_f32(cst["cs"]), _np_f32(cst["p2"]), _np_f32(cst["kr"])
    tlu, tri, su = _np_bf16(cst["tlu"]), _np_bf16(cst["tri"]), _np_bf16(cst["su"])

    xp = x_prompt.reshape(T_CTX, D_MODEL)
    xs = x_sample.reshape(T_LAT, D_MODEL)

    cond8 = jnp.concatenate([c_ctx[None, :], c, jnp.zeros((3, D_MODEL), F32)], axis=0)
    mod = _ada(cond8, w_ada[0], b_ada[0][None, :])
    mod3 = mod.reshape(8, 1, 6 * D_MODEL)

    wi = w_in[0]
    i_og = 2 * QK_W + 2 * V_W
    i_u = i_og + 2 * RANK
    w_main = jnp.concatenate([wi[:, :i_og], wi[:, i_u:]], axis=1).astype(BF16)
    w_r = jnp.pad(wi[:, i_og:i_u], ((0, 0), (0, LANES - 2 * RANK))).astype(BF16)

    wgf = w_gate_fwd[0].reshape(RANK, HEADS, DK)
    wgb = w_gate_bwd[0].reshape(RANK, HEADS, DK)
    zf = jnp.zeros_like(wgf)
    top = jnp.stack([wgf, zf], axis=2)
    bot = jnp.stack([zf, wgb], axis=2)
    wg = jnp.concatenate([top, bot], axis=0)
    wg = wg.reshape(2 * RANK, HEADS // 2, 4 * DK).transpose(1, 0, 2)
    wg = jnp.pad(wg, ((0, 0), (0, LANES - 2 * RANK), (0, 0)))
    wg_hi = wg.astype(BF16)
    wg_lo = (wg - wg_hi.astype(F32)).astype(BF16)
    bg = jnp.stack([b_gate_fwd[0].reshape(HEADS, DK), b_gate_bwd[0].reshape(HEADS, DK)], axis=1)
    bg = bg.reshape(HEADS // 2, 1, 4 * DK)
    ng = norm_gla[0].reshape(HEADS // 2, 1, 2 * DV)

    proj, r_all = _inproj(xp, xs, mod3, norm_attn, w_main, w_r)

    zero_state = jnp.zeros((N_CTX, HEADS, DK, DV), F32)
    o_ctx, sf_ctx, sb_ctx = _gla(proj, r_all, wg_hi, wg_lo, bg, ng, tlu, zero_state, zero_state,
                                 seq_len=L_CTX, n_seq=N_CTX, n_in_step=4, row0=0)
    o_lat, _, _ = _gla(proj, r_all, wg_hi, wg_lo, bg, ng, tlu,
                       state_gla_fwd[:, 0], state_gla_bwd[:, 0],
                       seq_len=L_LAT, n_seq=N_LAT, n_in_step=1, row0=T_CTX)

    f_ctx = _fnet_ctx(proj, cs, p2)
    f_lat = _fnet_lat(proj, cs, kr)

    wo = w_out[0][:V_W].astype(BF16)
    wf = w_out[0][V_W:].astype(BF16)
    wr = jnp.concatenate([w_router_group[0], w_router_expert[0]], axis=1)
    wr = jnp.pad(wr, ((0, 0), (0, LANES - N_GROUPS - N_EXP)))
    wr_hi = wr.astype(BF16)
    wr_lo = (wr - wr_hi.astype(F32)).astype(BF16)
    br = jnp.pad(jnp.concatenate([b_router_group[0], b_router_expert[0]]), (0, LANES - N_GROUPS - N_EXP))[None, :]

    x1, hs, rw, nb, lb = _outproj(o_ctx, o_lat, f_ctx, f_lat, xp, xs, mod3, norm_ffn, wo, wf,
                                  wr_hi, wr_lo, br, tri, su)

    nb_e = nb[:, 0, LANE_E0:LANE_E0 + N_EXP].T
    lb_e = lb[:, 0, LANE_E0:LANE_E0 + N_EXP].T
    run_end = jnp.cumsum(nb_e, axis=1)
    blocks_e = run_end[:, -1]
    tiles_e = (blocks_e + BLK_PER_TILE - 1) // BLK_PER_TILE
    tile_end = jnp.cumsum(tiles_e)
    tile_start = tile_end - tiles_e
    n_tiles = tile_end[-1]
    n_code_tiles = NT_MOE + 4
    tile = jnp.arange(n_code_tiles, dtype=I32) - 2
    tile_c = jnp.clip(tile, 0, n_tiles - 1)
    t_exp = jnp.sum(tile_c[:, None] >= tile_end[None, :], axis=1)
    ends = run_end[t_exp]
    starts = ends - nb_e[t_exp]
    offs = lb_e[t_exp] + jnp.arange(N_TOK_TILES, dtype=I32)[None, :] * PACK_BLKS - starts
    j = jnp.arange(BLK_PER_TILE, dtype=I32)
    bi = ((tile_c - tile_start[t_exp]) * BLK_PER_TILE)[:, None] + j[None, :]
    in_run = (starts[:, None, :] <= bi[:, :, None]) & (bi[:, :, None] < ends[:, None, :])
    blk = bi + jnp.sum(jnp.where(in_run, offs[:, None, :], 0), axis=2)
    valid = (tile == tile_c)[:, None] & (bi < blocks_e[t_exp][:, None])
    spare = ((jnp.arange(n_code_tiles, dtype=I32) % 2)[:, None] * BLK_PER_TILE + j[None, :]) * PACK_BLKS + USED_BLKS
    code = jnp.where(valid, (blk << SRC_BITS) | blk, (spare << SRC_BITS) | blk[:, 0:1]).astype(I32).reshape(-1)
    tidx = jnp.minimum(jnp.arange(NT_MOE + 1, dtype=I32), n_tiles - 1)
    texp = jnp.sum(tidx[:, None] >= tile_end[None, :], axis=1).astype(I32)
    nxt = tile_end[texp]
    nexp = jnp.where(nxt < n_tiles, texp[jnp.minimum(nxt, NT_MOE)], -1).astype(I32)
    meta = n_tiles.reshape(1).astype(I32)

    y2 = _moe(texp, nexp, meta, code, hs, w_expert_gate[0], w_expert_up[0], w_expert_down[0])
    y_prompt, y_sample = _final(x1, y2, rw, mod3, norm_final[None, :])

    st_shape = (N_CTX, 1, HEADS, DK, DV)
    return (y_prompt.reshape(N_CTX, L_CTX, D_MODEL), y_sample.reshape(N_LAT, L_LAT, D_MODEL),
            sf_ctx.reshape(st_shape), sb_ctx.reshape(st_shape))
```

```python
import functools

import numpy as np
import jax
import jax.numpy as jnp
from jax import lax
from jax.experimental import pallas as pl
from jax.experimental.pallas import tpu as pltpu

F32 = jnp.float32
BF16 = jnp.bfloat16
I32 = jnp.int32

D_MODEL = 2048
N_CTX = 32
L_CTX = 256
N_LAT = 4
L_LAT = 2048
GRID_H = 32
GRID_W = 64
T_CTX = N_CTX * L_CTX
T_LAT = N_LAT * L_LAT
T_ALL = T_CTX + T_LAT
HEADS = 8
DK = 64
DV = 128
RANK = 16
TAU = 16.0
CHUNK = 64
FN_G = 8
FN_C = 128
QK_W = HEADS * DK
V_W = HEADS * DV
N_GROUPS = 4
EPG = 8
N_EXP = N_GROUPS * EPG
D_EXP = 512
EPS = 1e-6

LANES = 128
VMEM_LIMIT = 56 * 1024 * 1024

TM_MOE = 256


def _dot(a, b):
    return jnp.dot(a, b, preferred_element_type=F32)


def _dot_nt(a, b):
    return lax.dot_general(a, b, (((1,), (1,)), ((), ())), preferred_element_type=F32)


def _split2(x):
    hi = x.astype(BF16)
    lo = (x - hi.astype(F32)).astype(BF16)
    return hi, lo


def _silu(x):
    return x * (1.0 / (1.0 + jnp.exp(-x)))


def _rms(x, g):
    return x * lax.rsqrt(jnp.mean(x * x, axis=-1, keepdims=True) + EPS) * g


def _cparams(sem):
    return pltpu.CompilerParams(dimension_semantics=sem, vmem_limit_bytes=VMEM_LIMIT)


def _ada_kernel(c_ref, w_ref, b_ref, o_ref):
    s_hi, s_lo = _split2(_silu(c_ref[...]))
    w = w_ref[...]
    w_hi = w.astype(BF16)
    w_lo = (w - w_hi.astype(F32)).astype(BF16)
    o_ref[...] = _dot(s_hi, w_hi) + _dot(s_lo, w_hi) + _dot(s_hi, w_lo) + b_ref[...]


def _ada(cond8, w_ada, b_ada):
    tn = 768
    n6 = 6 * D_MODEL
    return pl.pallas_call(
        _ada_kernel,
        grid=(n6 // tn,),
        in_specs=[pl.BlockSpec((8, D_MODEL), lambda j: (0, 0)),
                  pl.BlockSpec((D_MODEL, tn), lambda j: (0, j)),
                  pl.BlockSpec((1, tn), lambda j: (0, j))],
        out_specs=pl.BlockSpec((8, tn), lambda j: (0, j)),
        out_shape=jax.ShapeDtypeStruct((8, n6), F32),
        compiler_params=_cparams(("arbitrary",)),
        name="ada",
    )(cond8, w_ada, b_ada)


TM_IN = 256
TN_IN = 1024
N_MAIN = 4096


def _cond_row(tile, tm):
    ctx_tiles = T_CTX // tm
    per_seq = L_LAT // tm
    return jnp.where(tile < ctx_tiles, 0, 1 + (jnp.maximum(tile - ctx_tiles, 0)) // per_seq)


def _inproj_kernel(xp_ref, xs_ref, mod_ref, g_ref, w_ref, wr_ref, proj_ref, r_ref):
    i = pl.program_id(0)
    x = jnp.where(i < T_CTX // TM_IN, xp_ref[...], xs_ref[...])
    sh1 = mod_ref[:, 0:D_MODEL]
    sc1 = mod_ref[:, D_MODEL:2 * D_MODEL]
    hb = (_rms(x, g_ref[...]) * (1.0 + sc1) + sh1).astype(BF16)
    r_ref[...] = _dot(hb, wr_ref[...])
    for n in range(N_MAIN // TN_IN):
        proj_ref[:, n * TN_IN:(n + 1) * TN_IN] = _dot(hb, w_ref[:, n * TN_IN:(n + 1) * TN_IN]).astype(BF16)


def _inproj(xp, xs, mod3, g_attn, w_main, w_r):
    nt = T_ALL // TM_IN
    nctx = T_CTX // TM_IN
    resident = pl.Buffered(1)
    return pl.pallas_call(
        _inproj_kernel,
        grid=(nt,),
        in_specs=[pl.BlockSpec((TM_IN, D_MODEL), lambda i: (jnp.minimum(i, nctx - 1), 0)),
                  pl.BlockSpec((TM_IN, D_MODEL), lambda i: (jnp.maximum(i - nctx, 0), 0)),
                  pl.BlockSpec((None, 1, 6 * D_MODEL), lambda i: (_cond_row(i, TM_IN), 0, 0)),
                  pl.BlockSpec((1, D_MODEL), lambda i: (0, 0)),
                  pl.BlockSpec((D_MODEL, N_MAIN), lambda i: (0, 0), pipeline_mode=resident),
                  pl.BlockSpec((D_MODEL, LANES), lambda i: (0, 0), pipeline_mode=resident)],
        out_specs=[pl.BlockSpec((TM_IN, N_MAIN), lambda i: (i, 0)),
                   pl.BlockSpec((TM_IN, LANES), lambda i: (i, 0))],
        out_shape=[jax.ShapeDtypeStruct((T_ALL, N_MAIN), BF16),
                   jax.ShapeDtypeStruct((T_ALL, LANES), F32)],
        compiler_params=_cparams(("arbitrary",)),
        name="inproj",
    )(xp, xs, mod3, g_attn, w_main, w_r)


SB = 256
CPB = SB // CHUNK


def _gla_kernel(q_ref, k_ref, v_ref, og_ref, r_ref, wgh_ref, wgl_ref, bg_ref, ng_ref,
                tlu_ref, s0f_ref, s0b_ref,
                o_ref, sf_ref, sb_ref,
                cum_s, last_s, qe_s, ke_s, kd_s, dec_s, kv_s, sbs_s, *, seq_len, n_in_step):
    sb_per_seq = seq_len // SB
    ch_per_seq = seq_len // CHUNK
    n_sb = n_in_step * sb_per_seq
    lane = lax.broadcasted_iota(I32, (1, LANES), 1)
    m_f = lane < DK

    r_hi, r_lo = _split2(r_ref[...])
    z = _dot(r_hi, wgh_ref[...]) + _dot(r_lo, wgh_ref[...]) + _dot(r_hi, wgl_ref[...]) + bg_ref[...]
    g_all = (jnp.minimum(z, 0.0) - jnp.log(1.0 + jnp.exp(-jnp.abs(z)))) * (1.0 / TAU)

    q_pair = q_ref[...].astype(F32)
    k_pair = k_ref[...].astype(F32)
    q_roll = pltpu.roll(q_pair, DK, axis=1)
    k_roll = pltpu.roll(k_pair, DK, axis=1)

    row_b = lax.broadcasted_iota(I32, (SB, SB), 0)
    col_b = lax.broadcasted_iota(I32, (SB, SB), 1)
    same_chunk = (row_b // CHUNK) == (col_b // CHUNK)
    tril = same_chunk & (row_b >= col_b)
    triu = same_chunk & (row_b <= col_b)
    row_chunk = lax.broadcasted_iota(I32, (SB, 1), 0) // CHUNK
    col_chunk = lax.broadcasted_iota(I32, (1, SB), 1) // CHUNK

    m_f2 = (lax.broadcasted_iota(I32, (1, 2 * LANES), 1) % LANES) < DK
    cum_s[...] = g_all

    def cum_body(s, carry):
        rows = pl.ds(pl.multiple_of(s * SB, SB), SB)
        g = cum_s[rows, :]
        f_hi, f_lo = _split2(jnp.where(m_f2, g, 0.0))
        b_hi, b_lo = _split2(jnp.where(m_f2, 0.0, g))
        cum = (_dot(tlu_ref[...], jnp.concatenate([f_hi, b_hi], axis=0))
               + _dot(tlu_ref[...], jnp.concatenate([f_lo, b_lo], axis=0)))
        cum_s[rows, :] = cum
        tots = []
        for c in range(CPB):
            tot = jnp.where(m_f2, cum[(c + 1) * CHUNK - 1:(c + 1) * CHUNK, :], cum[c * CHUNK:c * CHUNK + 1, :])
            dec_s[s * CPB + c] = jnp.exp(tot)
            tots.append(jnp.broadcast_to(tot, (CHUNK, 2 * LANES)))
        last_s[rows, :] = jnp.concatenate(tots, axis=0)
        return carry

    lax.fori_loop(0, n_sb, cum_body, 0)

    for j in range(2):
        cum = cum_s[:, j * LANES:(j + 1) * LANES]
        last = last_s[:, j * LANES:(j + 1) * LANES]
        if j == 0:
            q2 = jnp.where(m_f, q_pair, q_roll)
            k2 = jnp.where(m_f, k_pair, k_roll)
        else:
            q2 = jnp.where(m_f, q_roll, q_pair)
            k2 = jnp.where(m_f, k_roll, k_pair)
        qe_s[j] = ((q2 * (DK ** -0.5)) * jnp.exp(cum)).astype(BF16)
        ke_s[j] = (k2 * jnp.exp(-cum)).astype(BF16)
        kd_s[j] = (k2 * jnp.exp(last - cum)).astype(BF16)

    def kv_body(s, carry):
        rows = pl.ds(pl.multiple_of(s * SB, SB), SB)
        for j in range(2):
            v_t = jnp.transpose(v_ref[rows, j * DV:(j + 1) * DV].astype(F32)).astype(BF16)
            zero = jnp.zeros_like(v_t)
            v_st = jnp.concatenate([jnp.where(col_chunk == c, v_t, zero) for c in range(CPB)], axis=0)
            kv = _dot(v_st, kd_s[j, rows, :])
            for c in range(CPB):
                kv_s[j, s * CPB + c] = kv[c * LANES:(c + 1) * LANES, :]
        return carry

    lax.fori_loop(0, n_sb, kv_body, 0, unroll=min(2, n_sb))

    def dec_row(j, c):
        return dec_s[c][:, j * LANES:(j + 1) * LANES]

    def step_state(j, c, st):
        return st * dec_row(j, c) + kv_s[j, c]

    for q in range(n_in_step):
        st0 = tuple(jnp.transpose(jnp.concatenate([s0f_ref[q, j], s0b_ref[q, j]], axis=0)) for j in range(2))
        c0 = q * ch_per_seq
        s0 = q * sb_per_seq

        def bwd_body(t, sts, c0=c0):
            c = c0 + ch_per_seq - 1 - t
            for j in range(2):
                sbs_s[j, c] = sts[j]
            return tuple(step_state(j, c, sts[j]) for j in range(2))

        st_b_fin = lax.fori_loop(0, ch_per_seq, bwd_body, st0, unroll=CPB)

        def fwd_body(sl, sts, s0=s0):
            s = s0 + sl
            rows = pl.ds(pl.multiple_of(s * SB, SB), SB)
            new = []
            for j in range(2):
                lo = j * DV
                qe = qe_s[j, rows, :]
                ke = ke_s[j, rows, :]
                v_b = v_ref[rows, lo:lo + DV].astype(BF16)
                zero = jnp.zeros_like(qe)
                st = sts[j]
                q_parts, s_parts = [], []
                for c in range(CPB):
                    ci = s * CPB + c
                    s_parts.append(jnp.where(m_f, st, sbs_s[j, ci]).astype(BF16))
                    q_parts.append(jnp.where(row_chunk == c, qe, zero))
                    st = step_state(j, ci, st)
                o = _dot_nt(jnp.concatenate(q_parts, axis=1), jnp.concatenate(s_parts, axis=1))
                q_st = jnp.concatenate([jnp.where(m_f, qe, zero), jnp.where(m_f, zero, qe)], axis=0)
                sc = _dot_nt(q_st, ke)
                p = jnp.where(tril, sc[0:SB, :], 0.0) + jnp.where(triu, sc[SB:2 * SB, :], 0.0)
                o = o + _dot(p.astype(BF16), v_b)
                o = o * lax.rsqrt(jnp.mean(o * o, axis=-1, keepdims=True) + EPS) * ng_ref[:, lo:lo + DV]
                o_ref[rows, lo:lo + DV] = (o * _silu(og_ref[rows, lo:lo + DV].astype(F32))).astype(BF16)
                new.append(st)
            return tuple(new)

        st_f_fin = lax.fori_loop(0, sb_per_seq, fwd_body, st0, unroll=min(2, sb_per_seq))
        for j in range(2):
            sf_ref[q, j] = jnp.transpose(st_f_fin[j])[0:DK, :]
            sb_ref[q, j] = jnp.transpose(st_b_fin[j])[DK:2 * DK, :]


def _gla(proj, r_all, wg_hi, wg_lo, bg, ng, tlu, s0f, s0b, *, seq_len, n_seq, n_in_step, row0):
    kern = functools.partial(_gla_kernel, seq_len=seq_len, n_in_step=n_in_step)
    rows = n_in_step * seq_len
    n_chunks = rows // CHUNK
    rb = lambda b: b + row0 // rows
    st_spec = pl.BlockSpec((n_in_step, 2, DK, DV), lambda b, h: (b, h, 0, 0))
    return pl.pallas_call(
        kern,
        grid=(n_seq // n_in_step, HEADS // 2),
        in_specs=[pl.BlockSpec((rows, LANES), lambda b, h: (rb(b), h)),
                  pl.BlockSpec((rows, LANES), lambda b, h: (rb(b), QK_W // LANES + h)),
                  pl.BlockSpec((rows, 2 * DV), lambda b, h: (rb(b), 2 * QK_W // (2 * DV) + h)),
                  pl.BlockSpec((rows, 2 * DV), lambda b, h: (rb(b), (2 * QK_W + V_W) // (2 * DV) + h)),
                  pl.BlockSpec((rows, LANES), lambda b, h: (rb(b), 0)),
                  pl.BlockSpec((None, LANES, 2 * LANES), lambda b, h: (h, 0, 0)),
                  pl.BlockSpec((None, LANES, 2 * LANES), lambda b, h: (h, 0, 0)),
                  pl.BlockSpec((None, 1, 2 * LANES), lambda b, h: (h, 0, 0)),
                  pl.BlockSpec((None, 1, 2 * DV), lambda b, h: (h, 0, 0)),
                  pl.BlockSpec((SB, 2 * SB), lambda b, h: (0, 0)),
                  st_spec, st_spec],
        out_specs=[pl.BlockSpec((rows, 2 * DV), lambda b, h: (b, h)), st_spec, st_spec],
        out_shape=[jax.ShapeDtypeStruct((n_seq * seq_len, V_W), BF16),
                   jax.ShapeDtypeStruct((n_seq, HEADS, DK, DV), F32),
                   jax.ShapeDtypeStruct((n_seq, HEADS, DK, DV), F32)],
        scratch_shapes=[pltpu.VMEM((rows, 2 * LANES), F32),
                        pltpu.VMEM((rows, 2 * LANES), F32),
                        pltpu.VMEM((2, rows, LANES), BF16),
                        pltpu.VMEM((2, rows, LANES), BF16),
                        pltpu.VMEM((2, rows, LANES), BF16),
                        pltpu.VMEM((n_chunks, 1, 2 * LANES), F32),
                        pltpu.VMEM((2, n_chunks, DV, LANES), F32),
                        pltpu.VMEM((2, n_chunks, DV, LANES), F32)],
        compiler_params=_cparams(("arbitrary", "arbitrary")),
        name="gla_%d" % seq_len,
    )(proj, proj, proj, proj, r_all, wg_hi, wg_lo, bg, ng, tlu, s0f, s0b)


def _fnet_stage_a(u_bf, cs):
    cparts, sparts = [], []
    for g in range(FN_G):
        ab = _dot(u_bf[:, g * FN_C:(g + 1) * FN_C], cs)
        cparts.append(ab[:, 0:FN_C])
        sparts.append(ab[:, FN_C:2 * FN_C])
    return jnp.concatenate(cparts, axis=1), jnp.concatenate(sparts, axis=1)


def _fnet_ctx_kernel(u_ref, cs_ref, p2_ref, f_ref):
    uc, us = _fnet_stage_a(u_ref[...].astype(BF16), cs_ref[...].astype(BF16))
    ab = jnp.concatenate([uc, us], axis=0).astype(BF16)
    f_ref[...] = _dot(p2_ref[...].astype(BF16), ab).astype(BF16)


def _fnet_ctx(proj, cs, p2):
    return pl.pallas_call(
        _fnet_ctx_kernel,
        grid=(N_CTX,),
        in_specs=[pl.BlockSpec((L_CTX, FN_G * FN_C), lambda b: (b, 3)),
                  pl.BlockSpec((FN_C, 2 * FN_C), lambda b: (0, 0)),
                  pl.BlockSpec((L_CTX, 2 * L_CTX), lambda b: (0, 0))],
        out_specs=pl.BlockSpec((L_CTX, FN_G * FN_C), lambda b: (b, 0)),
        out_shape=jax.ShapeDtypeStruct((T_CTX, FN_G * FN_C), BF16),
        compiler_params=_cparams(("arbitrary",)),
        name="fnet_ctx",
    )(proj, cs, p2)


TM_FL = 256
RT_FL = 256


def _fnet_lat_kernel(u_ref, cs_ref, kr_ref, f_ref, ab_s):
    m = pl.program_id(1)

    @pl.when(m == 0)
    def _():
        def body(t, carry):
            rows = pl.ds(pl.multiple_of(t * RT_FL, RT_FL), RT_FL)
            uc, us = _fnet_stage_a(u_ref[rows, :].astype(BF16), cs_ref[...].astype(BF16))
            ab_s[rows, :] = uc.astype(BF16)
            ab_s[pl.ds(pl.multiple_of(L_LAT + t * RT_FL, RT_FL), RT_FL), :] = us.astype(BF16)
            return carry
        lax.fori_loop(0, L_LAT // RT_FL, body, 0)

    f_ref[...] = _dot(kr_ref[...].astype(BF16), ab_s[...]).astype(BF16)


def _fnet_lat(proj, cs, kr):
    nm = L_LAT // TM_FL
    return pl.pallas_call(
        _fnet_lat_kernel,
        grid=(N_LAT, nm),
        in_specs=[pl.BlockSpec((L_LAT, FN_G * FN_C), lambda b, m: (T_CTX // L_LAT + b, 3)),
                  pl.BlockSpec((FN_C, 2 * FN_C), lambda b, m: (0, 0)),
                  pl.BlockSpec((TM_FL, 2 * L_LAT), lambda b, m: (m, 0))],
        out_specs=pl.BlockSpec((TM_FL, FN_G * FN_C), lambda b, m: (b * nm + m, 0)),
        out_shape=jax.ShapeDtypeStruct((T_LAT, FN_G * FN_C), BF16),
        scratch_shapes=[pltpu.VMEM((2 * L_LAT, FN_G * FN_C), BF16)],
        compiler_params=_cparams(("arbitrary", "arbitrary")),
        name="fnet_lat",
    )(proj, cs, kr)


TM_OUT = 256
LANE_E0 = N_GROUPS
ROWS_PER_BLK = 16
PACK_ROWS = -(-(2 * TM_OUT + N_EXP * (ROWS_PER_BLK - 1)) // 256) * 256
PACK_BLKS = PACK_ROWS // ROWS_PER_BLK
N_TOK_TILES = T_ALL // TM_OUT
BLK_PER_TILE = TM_MOE // ROWS_PER_BLK
USED_BLKS = (2 * TM_OUT + N_EXP * (ROWS_PER_BLK - 1)) // ROWS_PER_BLK
HS_ROWS = N_TOK_TILES * PACK_ROWS
assert USED_BLKS < PACK_BLKS and 2 * BLK_PER_TILE <= N_TOK_TILES


def _outproj_kernel(oc_ref, ol_ref, fc_ref, fl_ref, xp_ref, xs_ref, mod_ref, g_ref, wo_ref, wf_ref,
                    wrh_ref, wrl_ref, br_ref, tri_ref, su_ref,
                    x1_ref, hs_ref, rw_ref, nb_ref, lb_ref):
    i = pl.program_id(0)
    is_ctx = i < T_CTX // TM_OUT

    o = jnp.where(is_ctx, oc_ref[...], ol_ref[...]).astype(BF16)
    f = jnp.where(is_ctx, fc_ref[...], fl_ref[...]).astype(BF16)
    x = jnp.where(is_ctx, xp_ref[...], xs_ref[...])
    y = _dot(o, wo_ref[...]) + _dot(f, wf_ref[...])
    ga1 = mod_ref[:, 2 * D_MODEL:3 * D_MODEL]
    sh2 = mod_ref[:, 3 * D_MODEL:4 * D_MODEL]
    sc2 = mod_ref[:, 4 * D_MODEL:5 * D_MODEL]
    x1 = x + ga1 * y
    x1_ref[...] = x1
    h2 = _rms(x1, g_ref[...]) * (1.0 + sc2) + sh2

    h_hi, h_lo = _split2(h2)
    lg_all = _dot(h_hi, wrh_ref[...]) + _dot(h_lo, wrh_ref[...]) + _dot(h_hi, wrl_ref[...]) + br_ref[...]

    lane_i = lax.broadcasted_iota(I32, (TM_OUT, LANES), 1)
    lane = lane_i.astype(F32)
    neg = jnp.float32(-jnp.inf)
    big = jnp.float32(LANES)
    lg = jnp.where(lane_i < N_GROUPS, lg_all, neg)
    gmax = jnp.max(lg, axis=1, keepdims=True)
    gsel = jnp.min(jnp.where(lg == gmax, lane, big), axis=1, keepdims=True)
    den = jnp.sum(jnp.exp(lg - gmax), axis=1, keepdims=True)
    pg_sel = 1.0 / den

    e_idx = lane_i - LANE_E0
    egrp = (e_idx >> 3).astype(F32)
    emask = (e_idx >= 0) & (e_idx < N_EXP) & (egrp == gsel)
    m1 = jnp.where(emask, lg_all, neg)
    v1 = jnp.max(m1, axis=1, keepdims=True)
    i1 = jnp.min(jnp.where(m1 == v1, lane, big), axis=1, keepdims=True)
    m2 = jnp.where(lane == i1, neg, m1)
    v2 = jnp.max(m2, axis=1, keepdims=True)
    i2 = jnp.min(jnp.where(m2 == v2, lane, big), axis=1, keepdims=True)
    e2 = jnp.exp(v2 - v1)
    inv = 1.0 / (1.0 + e2)
    w1 = inv * pg_sel
    w2 = (e2 * inv) * pg_sel

    oh1 = lane == i1
    oh2 = lane == i2
    oh = jnp.where(oh1 | oh2, 1.0, 0.0)
    cnt = jnp.sum(oh, axis=0, keepdims=True)
    nblk = jnp.floor((cnt + (ROWS_PER_BLK - 1)) * (1.0 / ROWS_PER_BLK))
    lboff = _dot(jnp.broadcast_to(nblk, (8, LANES)).astype(BF16), su_ref[...])[0:1, :]
    lrank = _dot(tri_ref[...], oh.astype(BF16))
    posmat = lboff * ROWS_PER_BLK + lrank
    p1 = jnp.sum(jnp.where(oh1, posmat, 0.0), axis=1, keepdims=True)
    p2 = jnp.sum(jnp.where(oh2, posmat, 0.0), axis=1, keepdims=True)
    nb_ref[...] = nblk.astype(I32)
    lb_ref[...] = lboff.astype(I32)

    col = lax.broadcasted_iota(I32, (TM_OUT, PACK_ROWS), 1).astype(F32)
    place_t = jnp.where((col == p1) | (col == p2), 1.0, 0.0).astype(BF16)
    hs_ref[...] = lax.dot_general(place_t, h_hi, (((0,), (0,)), ((), ())),
                                  preferred_element_type=F32).astype(BF16)

    rw = jnp.where(lane_i == 0, w1, jnp.where(lane_i == 1, w2,
                                               jnp.where(lane_i == 2, p1, jnp.where(lane_i == 3, p2, 0.0))))
    rw_ref[...] = rw[:, 0:8]


def _outproj(o_ctx, o_lat, f_ctx, f_lat, xp, xs, mod3, g_ffn, wo, wf, wr_hi, wr_lo, br, tri, su):
    nt = T_ALL // TM_OUT
    nctx = T_CTX // TM_OUT
    ctx_map = lambda i: (jnp.minimum(i, nctx - 1), 0)
    lat_map = lambda i: (jnp.maximum(i - nctx, 0), 0)
    const = lambda i: (0, 0)
    return pl.pallas_call(
        _outproj_kernel,
        grid=(nt,),
        in_specs=[pl.BlockSpec((TM_OUT, V_W), ctx_map),
                  pl.BlockSpec((TM_OUT, V_W), lat_map),
                  pl.BlockSpec((TM_OUT, FN_G * FN_C), ctx_map),
                  pl.BlockSpec((TM_OUT, FN_G * FN_C), lat_map),
                  pl.BlockSpec((TM_OUT, D_MODEL), ctx_map),
                  pl.BlockSpec((TM_OUT, D_MODEL), lat_map),
                  pl.BlockSpec((None, 1, 6 * D_MODEL), lambda i: (_cond_row(i, TM_OUT), 0, 0)),
                  pl.BlockSpec((1, D_MODEL), const),
                  pl.BlockSpec((V_W, D_MODEL), const),
                  pl.BlockSpec((FN_G * FN_C, D_MODEL), const),
                  pl.BlockSpec((D_MODEL, LANES), const),
                  pl.BlockSpec((D_MODEL, LANES), const),
                  pl.BlockSpec((1, LANES), const),
                  pl.BlockSpec((TM_OUT, TM_OUT), const),
                  pl.BlockSpec((LANES, LANES), const)],
        out_specs=[pl.BlockSpec((TM_OUT, D_MODEL), lambda i: (i, 0)),
                   pl.BlockSpec((PACK_ROWS, D_MODEL), lambda i: (i, 0)),
                   pl.BlockSpec((TM_OUT, 8), lambda i: (i, 0)),
                   pl.BlockSpec((None, 1, LANES), lambda i: (i, 0, 0)),
                   pl.BlockSpec((None, 1, LANES), lambda i: (i, 0, 0))],
        out_shape=[jax.ShapeDtypeStruct((T_ALL, D_MODEL), F32),
                   jax.ShapeDtypeStruct((HS_ROWS, D_MODEL), BF16),
                   jax.ShapeDtypeStruct((T_ALL, 8), F32),
                   jax.ShapeDtypeStruct((nt, 1, LANES), I32),
                   jax.ShapeDtypeStruct((nt, 1, LANES), I32)],
        compiler_params=_cparams(("arbitrary",)),
        name="outproj",
    )(o_ctx, o_lat, f_ctx, f_lat, xp, xs, mod3, g_ffn, wo, wf, wr_hi, wr_lo, br, tri, su)


SRC_BITS = 16
SRC_MASK = (1 << SRC_BITS) - 1
X_SLOTS = 3
Y_SLOTS = 3
N_UP_CHUNKS = 2
N_DN_CHUNKS = 8
NT_MOE = (2 * T_ALL + N_TOK_TILES * N_EXP * (ROWS_PER_BLK - 1)) // TM_MOE + N_EXP


def _moe_kernel(texp_ref, nexp_ref, meta_ref, code_ref,
                h_hbm, wg_hbm, wu_hbm, wd_hbm,
                out_hbm,
                xbuf, ybuf, wg_f, wu_f, wd_f, wg_s, wu_s, wd_s, kcount, gsem, ssem, wsem):
    i = pl.program_id(0)
    nt = meta_ref[0]
    xs = i % X_SLOTS

    def blk_rows(b):
        if isinstance(b, int):
            return pl.ds(b * ROWS_PER_BLK, ROWS_PER_BLK)
        return pl.ds(pl.multiple_of(b * ROWS_PER_BLK, ROWS_PER_BLK), ROWS_PER_BLK)

    def gather_row(tile, sl, j):
        src = code_ref[(tile + 2) * BLK_PER_TILE + j] & SRC_MASK
        pltpu.make_async_copy(h_hbm.at[blk_rows(src)], xbuf.at[sl, blk_rows(j)], gsem.at[sl]).start()

    def scatter_row(tile, sl, j):
        dst = code_ref[(tile + 2) * BLK_PER_TILE + j] >> SRC_BITS
        pltpu.make_async_copy(ybuf.at[sl, blk_rows(j)], out_hbm.at[blk_rows(dst)], ssem.at[sl]).start(priority=1)

    def gather_wait(sl):
        pltpu.make_async_copy(h_hbm.at[pl.ds(0, TM_MOE)], xbuf.at[sl], gsem.at[sl]).wait()

    def scatter_wait(sl):
        pltpu.make_async_copy(ybuf.at[sl], out_hbm.at[pl.ds(0, TM_MOE)], ssem.at[sl]).wait()

    def y_slot(tile):
        return (tile + 1) % Y_SLOTS

    @pl.when(i == 0)
    def _():
        ybuf[y_slot(-2)] = jnp.zeros((TM_MOE, D_MODEL), BF16)
        ybuf[y_slot(-1)] = jnp.zeros((TM_MOE, D_MODEL), BF16)

        def body(j, c):
            gather_row(0, 0, j)
            gather_row(1, 1, j)
            scatter_row(-2, y_slot(-2), j)
            return c
        lax.fori_loop(0, BLK_PER_TILE, body, 0)

    @pl.when((i >= 1) & (i <= nt))
    def _():
        scatter_wait(y_slot(i - 3))

    @pl.when(i < nt)
    def _():
        prev = texp_ref[jnp.maximum(i - 1, 0)]

        def weight_copies(e, sl):
            return (pltpu.make_async_copy(wg_hbm.at[e], wg_f.at[sl], wsem.at[sl]),
                    pltpu.make_async_copy(wu_hbm.at[e], wu_f.at[sl], wsem.at[sl]),
                    pltpu.make_async_copy(wd_hbm.at[e], wd_f.at[sl], wsem.at[sl]))

        @pl.when(i == 0)
        def _():
            kcount[0] = 0
            for cp in weight_copies(texp_ref[0], 0):
                cp.start()

        @pl.when((i == 0) | (texp_ref[i] != prev))
        def _():
            k = kcount[0]
            for sl in range(2):
                @pl.when(k % 2 == sl)
                def _(sl=sl):
                    for cp in weight_copies(texp_ref[i], sl):
                        cp.wait()
                    def narrow(r, c):
                        up = pl.ds(pl.multiple_of(r * (D_MODEL // 16), D_MODEL // 16), D_MODEL // 16)
                        dn = pl.ds(pl.multiple_of(r * (D_EXP // 16), D_EXP // 16), D_EXP // 16)
                        wg_s[up, :] = wg_f[sl, up, :].astype(BF16)
                        wu_s[up, :] = wu_f[sl, up, :].astype(BF16)
                        wd_s[dn, :] = wd_f[sl, dn, :].astype(BF16)
                        return c
                    lax.fori_loop(0, 16, narrow, 0)

                    @pl.when(nexp_ref[i] >= 0)
                    def _():
                        for cp in weight_copies(nexp_ref[i], 1 - sl):
                            cp.start()
            kcount[0] = k + 1

        gather_wait(xs)
        x = xbuf[xs].astype(BF16)

        issues = []
        for j in range(BLK_PER_TILE):
            issues.append(functools.partial(gather_row, i + 2, (i + 2) % X_SLOTS, j))
            issues.append(functools.partial(scatter_row, i - 1, y_slot(i - 1), j))
        n_groups = N_UP_CHUNKS + N_DN_CHUNKS
        per_group = -(-len(issues) // n_groups)

        def issue_group(k):
            for fn in issues[k * per_group:(k + 1) * per_group]:
                fn()

        wu_c = D_EXP // N_UP_CHUNKS
        hid = []
        for n in range(N_UP_CHUNKS):
            issue_group(n)
            g = _dot(x, wg_s[:, n * wu_c:(n + 1) * wu_c])
            u = _dot(x, wu_s[:, n * wu_c:(n + 1) * wu_c])
            hid.append((_silu(g) * u).astype(BF16))
        hid = jnp.concatenate(hid, axis=1)
        wd_c = D_MODEL // N_DN_CHUNKS
        ys = y_slot(i)
        for n in range(N_DN_CHUNKS):
            issue_group(N_UP_CHUNKS + n)
            ybuf[ys, :, n * wd_c:(n + 1) * wd_c] = _dot(hid, wd_s[:, n * wd_c:(n + 1) * wd_c]).astype(BF16)

    @pl.when(i == nt)
    def _():
        gather_wait(xs)
        gather_wait((i + 1) % X_SLOTS)

        def body(j, c):
            scatter_row(nt - 1, y_slot(nt - 1), j)
            return c
        lax.fori_loop(0, BLK_PER_TILE, body, 0)
        scatter_wait(y_slot(nt - 2))
        scatter_wait(y_slot(nt - 1))


def _moe(texp, nexp, meta, code, hs, w_eg, w_eu, w_ed):
    hbm = pl.BlockSpec(memory_space=pl.ANY)
    grid_spec = pltpu.PrefetchScalarGridSpec(
        num_scalar_prefetch=4,
        grid=(NT_MOE + 1,),
        in_specs=[hbm, hbm, hbm, hbm],
        out_specs=hbm,
        scratch_shapes=[pltpu.VMEM((X_SLOTS, TM_MOE, D_MODEL), BF16),
                        pltpu.VMEM((Y_SLOTS, TM_MOE, D_MODEL), BF16),
                        pltpu.VMEM((2, D_MODEL, D_EXP), F32),
                        pltpu.VMEM((2, D_MODEL, D_EXP), F32),
                        pltpu.VMEM((2, D_EXP, D_MODEL), F32),
                        pltpu.VMEM((D_MODEL, D_EXP), BF16),
                        pltpu.VMEM((D_MODEL, D_EXP), BF16),
                        pltpu.VMEM((D_EXP, D_MODEL), BF16),
                        pltpu.SMEM((1,), I32),
                        pltpu.SemaphoreType.DMA((X_SLOTS,)),
                        pltpu.SemaphoreType.DMA((Y_SLOTS,)),
                        pltpu.SemaphoreType.DMA((2,))])
    return pl.pallas_call(
        _moe_kernel,
        grid_spec=grid_spec,
        out_shape=jax.ShapeDtypeStruct((HS_ROWS, D_MODEL), BF16),
        input_output_aliases={4: 0},
        compiler_params=_cparams(("arbitrary",)),
        name="moe",
    )(texp, nexp, meta, code, hs, w_eg, w_eu, w_ed)


TM_FIN = TM_OUT


def _final_kernel(x1_ref, ys_pack_ref, rw_ref, mod_ref, g_ref, yp_ref, ys_ref):
    i = pl.program_id(0)
    ga2 = mod_ref[:, 5 * D_MODEL:6 * D_MODEL]
    w0 = rw_ref[:, 0:1]
    w1 = rw_ref[:, 1:2]
    p0 = rw_ref[:, 2:3]
    p1 = rw_ref[:, 3:4]
    col = lax.broadcasted_iota(I32, (TM_FIN, PACK_ROWS), 1).astype(F32)
    comb = jnp.where(col == p0, w0, 0.0) + jnp.where(col == p1, w1, 0.0)
    y_moe = _dot(comb.astype(BF16), ys_pack_ref[...].astype(BF16))
    y = x1_ref[...] + ga2 * y_moe
    out = _rms(y, g_ref[...])

    @pl.when(i < T_CTX // TM_FIN)
    def _():
        yp_ref[...] = out

    @pl.when(i >= T_CTX // TM_FIN)
    def _():
        ys_ref[...] = out


def _final(x1, y2, rw, mod3, g_fin):
    nt = T_ALL // TM_FIN
    nctx = T_CTX // TM_FIN
    return pl.pallas_call(
        _final_kernel,
        grid=(nt,),
        in_specs=[pl.BlockSpec((TM_FIN, D_MODEL), lambda i: (i, 0)),
                  pl.BlockSpec((PACK_ROWS, D_MODEL), lambda i: (i, 0)),
                  pl.BlockSpec((TM_FIN, 8), lambda i: (i, 0)),
                  pl.BlockSpec((None, 1, 6 * D_MODEL), lambda i: (_cond_row(i, TM_FIN), 0, 0)),
                  pl.BlockSpec((1, D_MODEL), lambda i: (0, 0))],
        out_specs=[pl.BlockSpec((TM_FIN, D_MODEL), lambda i: (jnp.minimum(i, nctx - 1), 0)),
                   pl.BlockSpec((TM_FIN, D_MODEL), lambda i: (jnp.maximum(i - nctx, 0), 0))],
        out_shape=[jax.ShapeDtypeStruct((T_CTX, D_MODEL), F32),
                   jax.ShapeDtypeStruct((T_LAT, D_MODEL), F32)],
        compiler_params=_cparams(("arbitrary",)),
        name="final",
    )(x1, y2, rw, mod3, g_fin)


def _np_bf16(a):
    return jnp.asarray(np.asarray(a, np.float32), dtype=BF16)


def _np_f32(a):
    return jnp.asarray(np.asarray(a, np.float32))


@functools.lru_cache(maxsize=None)
def _constants():
    c = {}
    k = np.arange(FN_C)
    ang = 2.0 * np.pi * np.outer(k, k) / FN_C
    c["cs"] = np.concatenate([np.cos(ang), np.sin(ang)], axis=1) / np.sqrt(FN_C)
    p = np.arange(L_CTX)
    ang = 2.0 * np.pi * np.outer(p, p) / L_CTX
    c["p2"] = np.concatenate([np.cos(ang), -np.sin(ang)], axis=1) / np.sqrt(L_CTX)
    pos = np.arange(L_LAT)
    rr, cc = pos // GRID_W, pos % GRID_W
    num = (np.outer(rr, rr) * (GRID_W // GRID_H) + np.outer(cc, cc)) % GRID_W
    ang = 2.0 * np.pi * num / GRID_W
    c["kr"] = np.concatenate([np.cos(ang), -np.sin(ang)], axis=1) / np.sqrt(L_LAT)
    i = np.arange(SB)
    same = (i[:, None] // CHUNK) == (i[None, :] // CHUNK)
    tl = same & (i[:, None] >= i[None, :])
    tu = same & (i[:, None] <= i[None, :])
    c["tlu"] = np.concatenate([tl, tu], axis=1).astype(np.float32)
    c["tri"] = (i[:, None] > i[None, :]).astype(np.float32)
    k = np.arange(LANES)
    c["su"] = (k[:, None] < k[None, :]).astype(np.float32)
    return c


def kernel(x_prompt, x_sample, state_gla_fwd, state_gla_bwd, c, c_ctx, w_ada, b_ada, norm_attn, norm_ffn, w_in, w_gate_fwd, b_gate_fwd, w_gate_bwd, b_gate_bwd, norm_gla, w_out, w_router_group, b_router_group, w_router_expert, b_router_expert, w_expert_gate, w_expert_up, w_expert_down, norm_final):
    assert w_ada.shape[0] == 1, "single layer"
    cst = _constants()
    cs, p2, kr = _np_f32(cst["cs"]), _np_f32(cst["p2"]), _np_f32(cst["kr"])
    tlu, tri, su = _np_bf16(cst["tlu"]), _np_bf16(cst["tri"]), _np_bf16(cst["su"])

    xp = x_prompt.reshape(T_CTX, D_MODEL)
    xs = x_sample.reshape(T_LAT, D_MODEL)

    cond8 = jnp.concatenate([c_ctx[None, :], c, jnp.zeros((3, D_MODEL), F32)], axis=0)
    mod = _ada(cond8, w_ada[0], b_ada[0][None, :])
    mod3 = mod.reshape(8, 1, 6 * D_MODEL)

    wi = w_in[0]
    i_og = 2 * QK_W + 2 * V_W
    i_u = i_og + 2 * RANK
    w_main = jnp.concatenate([wi[:, :i_og], wi[:, i_u:]], axis=1).astype(BF16)
    w_r = jnp.pad(wi[:, i_og:i_u], ((0, 0), (0, LANES - 2 * RANK))).astype(BF16)

    wgf = w_gate_fwd[0].reshape(RANK, HEADS, DK)
    wgb = w_gate_bwd[0].reshape(RANK, HEADS, DK)
    zf = jnp.zeros_like(wgf)
    top = jnp.stack([wgf, zf], axis=2)
    bot = jnp.stack([zf, wgb], axis=2)
    wg = jnp.concatenate([top, bot], axis=0)
    wg = wg.reshape(2 * RANK, HEADS // 2, 4 * DK).transpose(1, 0, 2)
    wg = jnp.pad(wg, ((0, 0), (0, LANES - 2 * RANK), (0, 0)))
    wg_hi = wg.astype(BF16)
    wg_lo = (wg - wg_hi.astype(F32)).astype(BF16)
    bg = jnp.stack([b_gate_fwd[0].reshape(HEADS, DK), b_gate_bwd[0].reshape(HEADS, DK)], axis=1)
    bg = bg.reshape(HEADS // 2, 1, 4 * DK)
    ng = norm_gla[0].reshape(HEADS // 2, 1, 2 * DV)

    proj, r_all = _inproj(xp, xs, mod3, norm_attn, w_main, w_r)

    zero_state = jnp.zeros((N_CTX, HEADS, DK, DV), F32)
    o_ctx, sf_ctx, sb_ctx = _gla(proj, r_all, wg_hi, wg_lo, bg, ng, tlu, zero_state, zero_state,
                                 seq_len=L_CTX, n_seq=N_CTX, n_in_step=4, row0=0)
    o_lat, _, _ = _gla(proj, r_all, wg_hi, wg_lo, bg, ng, tlu,
                       state_gla_fwd[:, 0], state_gla_bwd[:, 0],
                       seq_len=L_LAT, n_seq=N_LAT, n_in_step=1, row0=T_CTX)

    f_ctx = _fnet_ctx(proj, cs, p2)
    f_lat = _fnet_lat(proj, cs, kr)

    wo = w_out[0][:V_W].astype(BF16)
    wf = w_out[0][V_W:].astype(BF16)
    wr = jnp.concatenate([w_router_group[0], w_router_expert[0]], axis=1)
    wr = jnp.pad(wr, ((0, 0), (0, LANES - N_GROUPS - N_EXP)))
    wr_hi = wr.astype(BF16)
    wr_lo = (wr - wr_hi.astype(F32)).astype(BF16)
    br = jnp.pad(jnp.concatenate([b_router_group[0], b_router_expert[0]]), (0, LANES - N_GROUPS - N_EXP))[None, :]

    x1, hs, rw, nb, lb = _outproj(o_ctx, o_lat, f_ctx, f_lat, xp, xs, mod3, norm_ffn, wo, wf,
                                  wr_hi, wr_lo, br, tri, su)

    nb_e = nb[:, 0, LANE_E0:LANE_E0 + N_EXP].T
    lb_e = lb[:, 0, LANE_E0:LANE_E0 + N_EXP].T
    run_end = jnp.cumsum(nb_e, axis=1)
    blocks_e = run_end[:, -1]
    tiles_e = (blocks_e + BLK_PER_TILE - 1) // BLK_PER_TILE
    tile_end = jnp.cumsum(tiles_e)
    tile_start = tile_end - tiles_e
    n_tiles = tile_end[-1]
    n_code_tiles = NT_MOE + 4
    tile = jnp.arange(n_code_tiles, dtype=I32) - 2
    tile_c = jnp.clip(tile, 0, n_tiles - 1)
    t_exp = jnp.sum(tile_c[:, None] >= tile_end[None, :], axis=1)
    ends = run_end[t_exp]
    starts = ends - nb_e[t_exp]
    offs = lb_e[t_exp] + jnp.arange(N_TOK_TILES, dtype=I32)[None, :] * PACK_BLKS - starts
    j = jnp.arange(BLK_PER_TILE, dtype=I32)
    bi = ((tile_c - tile_start[t_exp]) * BLK_PER_TILE)[:, None] + j[None, :]
    in_run = (starts[:, None, :] <= bi[:, :, None]) & (bi[:, :, None] < ends[:, None, :])
    blk = bi + jnp.sum(jnp.where(in_run, offs[:, None, :], 0), axis=2)
    valid = (tile == tile_c)[:, None] & (bi < blocks_e[t_exp][:, None])
    spare = ((jnp.arange(n_code_tiles, dtype=I32) % 2)[:, None] * BLK_PER_TILE + j[None, :]) * PACK_BLKS + USED_BLKS
    code = jnp.where(valid, (blk << SRC_BITS) | blk, (spare << SRC_BITS) | blk[:, 0:1]).astype(I32).reshape(-1)
    tidx = jnp.minimum(jnp.arange(NT_MOE + 1, dtype=I32), n_tiles - 1)
    texp = jnp.sum(tidx[:, None] >= tile_end[None, :], axis=1).astype(I32)
    nxt = tile_end[texp]
    nexp = jnp.where(nxt < n_tiles, texp[jnp.minimum(nxt, NT_MOE)], -1).astype(I32)
    meta = n_tiles.reshape(1).astype(I32)

    y2 = _moe(texp, nexp, meta, code, hs, w_expert_gate[0], w_expert_up[0], w_expert_down[0])
    y_prompt, y_sample = _final(x1, y2, rw, mod3, norm_final[None, :])

    st_shape = (N_CTX, 1, HEADS, DK, DV)
    return (y_prompt.reshape(N_CTX, L_CTX, D_MODEL), y_sample.reshape(N_LAT, L_LAT, D_MODEL),
            sf_ctx.reshape(st_shape), sb_ctx.reshape(st_shape))
```

```python
import functools

import numpy as np
import jax
import jax.numpy as jnp
from jax import lax
from jax.experimental import pallas as pl
from jax.experimental.pallas import tpu as pltpu

F32 = jnp.float32
BF16 = jnp.bfloat16
I32 = jnp.int32

D_MODEL = 2048
N_CTX = 32
L_CTX = 256
N_LAT = 4
L_LAT = 2048
GRID_H = 32
GRID_W = 64
T_CTX = N_CTX * L_CTX
T_LAT = N_LAT * L_LAT
T_ALL = T_CTX + T_LAT
HEADS = 8
DK = 64
DV = 128
RANK = 16
TAU = 16.0
CHUNK = 64
FN_G = 8
FN_C = 128
QK_W = HEADS * DK
V_W = HEADS * DV
N_GROUPS = 4
EPG = 8
N_EXP = N_GROUPS * EPG
D_EXP = 512
EPS = 1e-6

LANES = 128
VMEM_LIMIT = 56 * 1024 * 1024

TM_MOE = 256


def _dot(a, b):
    return jnp.dot(a, b, preferred_element_type=F32)


def _dot_nt(a, b):
    return lax.dot_general(a, b, (((1,), (1,)), ((), ())), preferred_element_type=F32)


def _split2(x):
    hi = x.astype(BF16)
    lo = (x - hi.astype(F32)).astype(BF16)
    return hi, lo


def _silu(x):
    return x * (1.0 / (1.0 + jnp.exp(-x)))


def _rms(x, g):
    return x * lax.rsqrt(jnp.mean(x * x, axis=-1, keepdims=True) + EPS) * g


def _cparams(sem):
    return pltpu.CompilerParams(dimension_semantics=sem, vmem_limit_bytes=VMEM_LIMIT)


def _ada_kernel(c_ref, w_ref, b_ref, o_ref):
    s_hi, s_lo = _split2(_silu(c_ref[...]))
    w = w_ref[...]
    w_hi = w.astype(BF16)
    w_lo = (w - w_hi.astype(F32)).astype(BF16)
    o_ref[...] = _dot(s_hi, w_hi) + _dot(s_lo, w_hi) + _dot(s_hi, w_lo) + b_ref[...]


def _ada(cond8, w_ada, b_ada):
    tn = 768
    n6 = 6 * D_MODEL
    return pl.pallas_call(
        _ada_kernel,
        grid=(n6 // tn,),
        in_specs=[pl.BlockSpec((8, D_MODEL), lambda j: (0, 0)),
                  pl.BlockSpec((D_MODEL, tn), lambda j: (0, j)),
                  pl.BlockSpec((1, tn), lambda j: (0, j))],
        out_specs=pl.BlockSpec((8, tn), lambda j: (0, j)),
        out_shape=jax.ShapeDtypeStruct((8, n6), F32),
        compiler_params=_cparams(("arbitrary",)),
        name="ada",
    )(cond8, w_ada, b_ada)


TM_IN = 256
TN_IN = 1024
N_MAIN = 4096


def _cond_row(tile, tm):
    ctx_tiles = T_CTX // tm
    per_seq = L_LAT // tm
    return jnp.where(tile < ctx_tiles, 0, 1 + (jnp.maximum(tile - ctx_tiles, 0)) // per_seq)


def _inproj_kernel(xp_ref, xs_ref, mod_ref, g_ref, w_ref, wr_ref, proj_ref, r_ref):
    i = pl.program_id(0)
    x = jnp.where(i < T_CTX // TM_IN, xp_ref[...], xs_ref[...])
    sh1 = mod_ref[:, 0:D_MODEL]
    sc1 = mod_ref[:, D_MODEL:2 * D_MODEL]
    hb = (_rms(x, g_ref[...]) * (1.0 + sc1) + sh1).astype(BF16)
    r_ref[...] = _dot(hb, wr_ref[...])
    for n in range(N_MAIN // TN_IN):
        proj_ref[:, n * TN_IN:(n + 1) * TN_IN] = _dot(hb, w_ref[:, n * TN_IN:(n + 1) * TN_IN]).astype(BF16)


def _inproj(xp, xs, mod3, g_attn, w_main, w_r):
    nt = T_ALL // TM_IN
    nctx = T_CTX // TM_IN
    resident = pl.Buffered(1)
    return pl.pallas_call(
        _inproj_kernel,
        grid=(nt,),
        in_specs=[pl.BlockSpec((TM_IN, D_MODEL), lambda i: (jnp.minimum(i, nctx - 1), 0)),
                  pl.BlockSpec((TM_IN, D_MODEL), lambda i: (jnp.maximum(i - nctx, 0), 0)),
                  pl.BlockSpec((None, 1, 6 * D_MODEL), lambda i: (_cond_row(i, TM_IN), 0, 0)),
                  pl.BlockSpec((1, D_MODEL), lambda i: (0, 0)),
                  pl.BlockSpec((D_MODEL, N_MAIN), lambda i: (0, 0), pipeline_mode=resident),
                  pl.BlockSpec((D_MODEL, LANES), lambda i: (0, 0), pipeline_mode=resident)],
        out_specs=[pl.BlockSpec((TM_IN, N_MAIN), lambda i: (i, 0)),
                   pl.BlockSpec((TM_IN, LANES), lambda i: (i, 0))],
        out_shape=[jax.ShapeDtypeStruct((T_ALL, N_MAIN), BF16),
                   jax.ShapeDtypeStruct((T_ALL, LANES), F32)],
        compiler_params=_cparams(("arbitrary",)),
        name="inproj",
    )(xp, xs, mod3, g_attn, w_main, w_r)


SB = 256
CPB = SB // CHUNK


def _gla_kernel(q_ref, k_ref, v_ref, og_ref, r_ref, wgh_ref, wgl_ref, bg_ref, ng_ref,
                tlu_ref, s0f_ref, s0b_ref,
                o_ref, sf_ref, sb_ref,
                cum_s, last_s, qe_s, ke_s, kd_s, dec_s, kv_s, sbs_s, *, seq_len, n_in_step):
    sb_per_seq = seq_len // SB
    ch_per_seq = seq_len // CHUNK
    n_sb = n_in_step * sb_per_seq
    lane = lax.broadcasted_iota(I32, (1, LANES), 1)
    m_f = lane < DK

    r_hi, r_lo = _split2(r_ref[...])
    z = _dot(r_hi, wgh_ref[...]) + _dot(r_lo, wgh_ref[...]) + _dot(r_hi, wgl_ref[...]) + bg_ref[...]
    g_all = (jnp.minimum(z, 0.0) - jnp.log(1.0 + jnp.exp(-jnp.abs(z)))) * (1.0 / TAU)

    q_pair = q_ref[...].astype(F32)
    k_pair = k_ref[...].astype(F32)
    q_roll = pltpu.roll(q_pair, DK, axis=1)
    k_roll = pltpu.roll(k_pair, DK, axis=1)

    row_b = lax.broadcasted_iota(I32, (SB, SB), 0)
    col_b = lax.broadcasted_iota(I32, (SB, SB), 1)
    same_chunk = (row_b // CHUNK) == (col_b // CHUNK)
    tril = same_chunk & (row_b >= col_b)
    triu = same_chunk & (row_b <= col_b)
    row_chunk = lax.broadcasted_iota(I32, (SB, 1), 0) // CHUNK
    col_chunk = lax.broadcasted_iota(I32, (1, SB), 1) // CHUNK

    m_f2 = (lax.broadcasted_iota(I32, (1, 2 * LANES), 1) % LANES) < DK
    cum_s[...] = g_all

    def cum_body(s, carry):
        rows = pl.ds(pl.multiple_of(s * SB, SB), SB)
        g = cum_s[rows, :]
        f_hi, f_lo = _split2(jnp.where(m_f2, g, 0.0))
        b_hi, b_lo = _split2(jnp.where(m_f2, 0.0, g))
        cum = (_dot(tlu_ref[...], jnp.concatenate([f_hi, b_hi], axis=0))
               + _dot(tlu_ref[...], jnp.concatenate([f_lo, b_lo], axis=0)))
        cum_s[rows, :] = cum
        tots = []
        for c in range(CPB):
            tot = jnp.where(m_f2, cum[(c + 1) * CHUNK - 1:(c + 1) * CHUNK, :], cum[c * CHUNK:c * CHUNK + 1, :])
            dec_s[s * CPB + c] = jnp.exp(tot)
            tots.append(jnp.broadcast_to(tot, (CHUNK, 2 * LANES)))
        last_s[rows, :] = jnp.concatenate(tots, axis=0)
        return carry

    lax.fori_loop(0, n_sb, cum_body, 0)

    for j in range(2):
        cum = cum_s[:, j * LANES:(j + 1) * LANES]
        last = last_s[:, j * LANES:(j + 1) * LANES]
        if j == 0:
            q2 = jnp.where(m_f, q_pair, q_roll)
            k2 = jnp.where(m_f, k_pair, k_roll)
        else:
            q2 = jnp.where(m_f, q_roll, q_pair)
            k2 = jnp.where(m_f, k_roll, k_pair)
        qe_s[j] = ((q2 * (DK ** -0.5)) * jnp.exp(cum)).astype(BF16)
        ke_s[j] = (k2 * jnp.exp(-cum)).astype(BF16)
        kd_s[j] = (k2 * jnp.exp(last - cum)).astype(BF16)

    def kv_body(s, carry):
        rows = pl.ds(pl.multiple_of(s * SB, SB), SB)
        for j in range(2):
            v_t = jnp.transpose(v_ref[rows, j * DV:(j + 1) * DV].astype(F32)).astype(BF16)
            zero = jnp.zeros_like(v_t)
            v_st = jnp.concatenate([jnp.where(col_chunk == c, v_t, zero) for c in range(CPB)], axis=0)
            kv = _dot(v_st, kd_s[j, rows, :])
            for c in range(CPB):
                kv_s[j, s * CPB + c] = kv[c * LANES:(c + 1) * LANES, :]
        return carry

    lax.fori_loop(0, n_sb, kv_body, 0, unroll=min(2, n_sb))

    def dec_row(j, c):
        return dec_s[c][:, j * LANES:(j + 1) * LANES]

    def step_state(j, c, st):
        return st * dec_row(j, c) + kv_s[j, c]

    for q in range(n_in_step):
        st0 = tuple(jnp.transpose(jnp.concatenate([s0f_ref[q, j], s0b_ref[q, j]], axis=0)) for j in range(2))
        c0 = q * ch_per_seq
        s0 = q * sb_per_seq

        def bwd_body(t, sts, c0=c0):
            c = c0 + ch_per_seq - 1 - t
            for j in range(2):
                sbs_s[j, c] = sts[j]
            return tuple(step_state(j, c, sts[j]) for j in range(2))

        st_b_fin = lax.fori_loop(0, ch_per_seq, bwd_body, st0, unroll=CPB)

        def fwd_body(sl, sts, s0=s0):
            s = s0 + sl
            rows = pl.ds(pl.multiple_of(s * SB, SB), SB)
            new = []
            for j in range(2):
                lo = j * DV
                qe = qe_s[j, rows, :]
                ke = ke_s[j, rows, :]
                v_b = v_ref[rows, lo:lo + DV].astype(BF16)
                zero = jnp.zeros_like(qe)
                st = sts[j]
                q_parts, s_parts = [], []
                for c in range(CPB):
                    ci = s * CPB + c
                    s_parts.append(jnp.where(m_f, st, sbs_s[j, ci]).astype(BF16))
                    q_parts.append(jnp.where(row_chunk == c, qe, zero))
                    st = step_state(j, ci, st)
                o = _dot_nt(jnp.concatenate(q_parts, axis=1), jnp.concatenate(s_parts, axis=1))
                q_st = jnp.concatenate([jnp.where(m_f, qe, zero), jnp.where(m_f, zero, qe)], axis=0)
                sc = _dot_nt(q_st, ke)
                p = jnp.where(tril, sc[0:SB, :], 0.0) + jnp.where(triu, sc[SB:2 * SB, :], 0.0)
                o = o + _dot(p.astype(BF16), v_b)
                o = o * lax.rsqrt(jnp.mean(o * o, axis=-1, keepdims=True) + EPS) * ng_ref[:, lo:lo + DV]
                o_ref[rows, lo:lo + DV] = (o * _silu(og_ref[rows, lo:lo + DV].astype(F32))).astype(BF16)
                new.append(st)
            return tuple(new)

        st_f_fin = lax.fori_loop(0, sb_per_seq, fwd_body, st0, unroll=min(2, sb_per_seq))
        for j in range(2):
            sf_ref[q, j] = jnp.transpose(st_f_fin[j])[0:DK, :]
            sb_ref[q, j] = jnp.transpose(st_b_fin[j])[DK:2 * DK, :]


def _gla(proj, r_all, wg_hi, wg_lo, bg, ng, tlu, s0f, s0b, *, seq_len, n_seq, n_in_step, row0):
    kern = functools.partial(_gla_kernel, seq_len=seq_len, n_in_step=n_in_step)
    rows = n_in_step * seq_len
    n_chunks = rows // CHUNK
    rb = lambda b: b + row0 // rows
    st_spec = pl.BlockSpec((n_in_step, 2, DK, DV), lambda b, h: (b, h, 0, 0))
    return pl.pallas_call(
        kern,
        grid=(n_seq // n_in_step, HEADS // 2),
        in_specs=[pl.BlockSpec((rows, LANES), lambda b, h: (rb(b), h)),
                  pl.BlockSpec((rows, LANES), lambda b, h: (rb(b), QK_W // LANES + h)),
                  pl.BlockSpec((rows, 2 * DV), lambda b, h: (rb(b), 2 * QK_W // (2 * DV) + h)),
                  pl.BlockSpec((rows, 2 * DV), lambda b, h: (rb(b), (2 * QK_W + V_W) // (2 * DV) + h)),
                  pl.BlockSpec((rows, LANES), lambda b, h: (rb(b), 0)),
                  pl.BlockSpec((None, LANES, 2 * LANES), lambda b, h: (h, 0, 0)),
                  pl.BlockSpec((None, LANES, 2 * LANES), lambda b, h: (h, 0, 0)),
                  pl.BlockSpec((None, 1, 2 * LANES), lambda b, h: (h, 0, 0)),
                  pl.BlockSpec((None, 1, 2 * DV), lambda b, h: (h, 0, 0)),
                  pl.BlockSpec((SB, 2 * SB), lambda b, h: (0, 0)),
                  st_spec, st_spec],
        out_specs=[pl.BlockSpec((rows, 2 * DV), lambda b, h: (b, h)), st_spec, st_spec],
        out_shape=[jax.ShapeDtypeStruct((n_seq * seq_len, V_W), BF16),
                   jax.ShapeDtypeStruct((n_seq, HEADS, DK, DV), F32),
                   jax.ShapeDtypeStruct((n_seq, HEADS, DK, DV), F32)],
        scratch_shapes=[pltpu.VMEM((rows, 2 * LANES), F32),
                        pltpu.VMEM((rows, 2 * LANES), F32),
                        pltpu.VMEM((2, rows, LANES), BF16),
                        pltpu.VMEM((2, rows, LANES), BF16),
                        pltpu.VMEM((2, rows, LANES), BF16),
                        pltpu.VMEM((n_chunks, 1, 2 * LANES), F32),
                        pltpu.VMEM((2, n_chunks, DV, LANES), F32),
                        pltpu.VMEM((2, n_chunks, DV, LANES), F32)],
        compiler_params=_cparams(("arbitrary", "arbitrary")),
        name="gla_%d" % seq_len,
    )(proj, proj, proj, proj, r_all, wg_hi, wg_lo, bg, ng, tlu, s0f, s0b)


def _fnet_stage_a(u_bf, cs):
    cparts, sparts = [], []
    for g in range(FN_G):
        ab = _dot(u_bf[:, g * FN_C:(g + 1) * FN_C], cs)
        cparts.append(ab[:, 0:FN_C])
        sparts.append(ab[:, FN_C:2 * FN_C])
    return jnp.concatenate(cparts, axis=1), jnp.concatenate(sparts, axis=1)


def _fnet_ctx_kernel(u_ref, cs_ref, p2_ref, f_ref):
    uc, us = _fnet_stage_a(u_ref[...].astype(BF16), cs_ref[...].astype(BF16))
    ab = jnp.concatenate([uc, us], axis=0).astype(BF16)
    f_ref[...] = _dot(p2_ref[...].astype(BF16), ab).astype(BF16)


def _fnet_ctx(proj, cs, p2):
    return pl.pallas_call(
        _fnet_ctx_kernel,
        grid=(N_CTX,),
        in_specs=[pl.BlockSpec((L_CTX, FN_G * FN_C), lambda b: (b, 3)),
                  pl.BlockSpec((FN_C, 2 * FN_C), lambda b: (0, 0)),
                  pl.BlockSpec((L_CTX, 2 * L_CTX), lambda b: (0, 0))],
        out_specs=pl.BlockSpec((L_CTX, FN_G * FN_C), lambda b: (b, 0)),
        out_shape=jax.ShapeDtypeStruct((T_CTX, FN_G * FN_C), BF16),
        compiler_params=_cparams(("arbitrary",)),
        name="fnet_ctx",
    )(proj, cs, p2)


TM_FL = 256
RT_FL = 256


def _fnet_lat_kernel(u_ref, cs_ref, kr_ref, f_ref, ab_s):
    m = pl.program_id(1)

    @pl.when(m == 0)
    def _():
        def body(t, carry):
            rows = pl.ds(pl.multiple_of(t * RT_FL, RT_FL), RT_FL)
            uc, us = _fnet_stage_a(u_ref[rows, :].astype(BF16), cs_ref[...].astype(BF16))
            ab_s[rows, :] = uc.astype(BF16)
            ab_s[pl.ds(pl.multiple_of(L_LAT + t * RT_FL, RT_FL), RT_FL), :] = us.astype(BF16)
            return carry
        lax.fori_loop(0, L_LAT // RT_FL, body, 0)

    f_ref[...] = _dot(kr_ref[...].astype(BF16), ab_s[...]).astype(BF16)


def _fnet_lat(proj, cs, kr):
    nm = L_LAT // TM_FL
    return pl.pallas_call(
        _fnet_lat_kernel,
        grid=(N_LAT, nm),
        in_specs=[pl.BlockSpec((L_LAT, FN_G * FN_C), lambda b, m: (T_CTX // L_LAT + b, 3)),
                  pl.BlockSpec((FN_C, 2 * FN_C), lambda b, m: (0, 0)),
                  pl.BlockSpec((TM_FL, 2 * L_LAT), lambda b, m: (m, 0))],
        out_specs=pl.BlockSpec((TM_FL, FN_G * FN_C), lambda b, m: (b * nm + m, 0)),
        out_shape=jax.ShapeDtypeStruct((T_LAT, FN_G * FN_C), BF16),
        scratch_shapes=[pltpu.VMEM((2 * L_LAT, FN_G * FN_C), BF16)],
        compiler_params=_cparams(("arbitrary", "arbitrary")),
        name="fnet_lat",
    )(proj, cs, kr)


TM_OUT = 256
LANE_E0 = N_GROUPS
ROWS_PER_BLK = 8
PACK_ROWS = -(-(2 * TM_OUT + N_EXP * (ROWS_PER_BLK - 1)) // 256) * 256
PACK_BLKS = PACK_ROWS // ROWS_PER_BLK
N_TOK_TILES = T_ALL // TM_OUT
BLK_PER_TILE = TM_MOE // ROWS_PER_BLK
USED_BLKS = (2 * TM_OUT + N_EXP * (ROWS_PER_BLK - 1)) // ROWS_PER_BLK
HS_ROWS = N_TOK_TILES * PACK_ROWS
assert USED_BLKS < PACK_BLKS and 2 * BLK_PER_TILE <= N_TOK_TILES


def _outproj_kernel(oc_ref, ol_ref, fc_ref, fl_ref, xp_ref, xs_ref, mod_ref, g_ref, wo_ref, wf_ref,
                    wrh_ref, wrl_ref, br_ref, tri_ref, su_ref,
                    x1_ref, hs_ref, rw_ref, nb_ref, lb_ref):
    i = pl.program_id(0)
    is_ctx = i < T_CTX // TM_OUT

    o = jnp.where(is_ctx, oc_ref[...], ol_ref[...]).astype(BF16)
    f = jnp.where(is_ctx, fc_ref[...], fl_ref[...]).astype(BF16)
    x = jnp.where(is_ctx, xp_ref[...], xs_ref[...])
    y = _dot(o, wo_ref[...]) + _dot(f, wf_ref[...])
    ga1 = mod_ref[:, 2 * D_MODEL:3 * D_MODEL]
    sh2 = mod_ref[:, 3 * D_MODEL:4 * D_MODEL]
    sc2 = mod_ref[:, 4 * D_MODEL:5 * D_MODEL]
    x1 = x + ga1 * y
    x1_ref[...] = x1
    h2 = _rms(x1, g_ref[...]) * (1.0 + sc2) + sh2

    h_hi, h_lo = _split2(h2)
    lg_all = _dot(h_hi, wrh_ref[...]) + _dot(h_lo, wrh_ref[...]) + _dot(h_hi, wrl_ref[...]) + br_ref[...]

    lane_i = lax.broadcasted_iota(I32, (TM_OUT, LANES), 1)
    lane = lane_i.astype(F32)
    neg = jnp.float32(-jnp.inf)
    big = jnp.float32(LANES)
    lg = jnp.where(lane_i < N_GROUPS, lg_all, neg)
    gmax = jnp.max(lg, axis=1, keepdims=True)
    gsel = jnp.min(jnp.where(lg == gmax, lane, big), axis=1, keepdims=True)
    den = jnp.sum(jnp.exp(lg - gmax), axis=1, keepdims=True)
    pg_sel = 1.0 / den

    e_idx = lane_i - LANE_E0
    egrp = (e_idx >> 3).astype(F32)
    emask = (e_idx >= 0) & (e_idx < N_EXP) & (egrp == gsel)
    m1 = jnp.where(emask, lg_all, neg)
    v1 = jnp.max(m1, axis=1, keepdims=True)
    i1 = jnp.min(jnp.where(m1 == v1, lane, big), axis=1, keepdims=True)
    m2 = jnp.where(lane == i1, neg, m1)
    v2 = jnp.max(m2, axis=1, keepdims=True)
    i2 = jnp.min(jnp.where(m2 == v2, lane, big), axis=1, keepdims=True)
    e2 = jnp.exp(v2 - v1)
    inv = 1.0 / (1.0 + e2)
    w1 = inv * pg_sel
    w2 = (e2 * inv) * pg_sel

    oh1 = lane == i1
    oh2 = lane == i2
    oh = jnp.where(oh1 | oh2, 1.0, 0.0)
    cnt = jnp.sum(oh, axis=0, keepdims=True)
    nblk = jnp.floor((cnt + (ROWS_PER_BLK - 1)) * (1.0 / ROWS_PER_BLK))
    lboff = _dot(jnp.broadcast_to(nblk, (8, LANES)).astype(BF16), su_ref[...])[0:1, :]
    lrank = _dot(tri_ref[...], oh.astype(BF16))
    posmat = lboff * ROWS_PER_BLK + lrank
    p1 = jnp.sum(jnp.where(oh1, posmat, 0.0), axis=1, keepdims=True)
    p2 = jnp.sum(jnp.where(oh2, posmat, 0.0), axis=1, keepdims=True)
    nb_ref[...] = nblk.astype(I32)
    lb_ref[...] = lboff.astype(I32)

    col = lax.broadcasted_iota(I32, (TM_OUT, PACK_ROWS), 1).astype(F32)
    place_t = jnp.where((col == p1) | (col == p2), 1.0, 0.0).astype(BF16)
    hs_ref[...] = lax.dot_general(place_t, h_hi, (((0,), (0,)), ((), ())),
                                  preferred_element_type=F32).astype(BF16)

    rw = jnp.where(lane_i == 0, w1, jnp.where(lane_i == 1, w2,
                                               jnp.where(lane_i == 2, p1, jnp.where(lane_i == 3, p2, 0.0))))
    rw_ref[...] = rw[:, 0:8]


def _outproj(o_ctx, o_lat, f_ctx, f_lat, xp, xs, mod3, g_ffn, wo, wf, wr_hi, wr_lo, br, tri, su):
    nt = T_ALL // TM_OUT
    nctx = T_CTX // TM_OUT
    ctx_map = lambda i: (jnp.minimum(i, nctx - 1), 0)
    lat_map = lambda i: (jnp.maximum(i - nctx, 0), 0)
    const = lambda i: (0, 0)
    return pl.pallas_call(
        _outproj_kernel,
        grid=(nt,),
        in_specs=[pl.BlockSpec((TM_OUT, V_W), ctx_map),
                  pl.BlockSpec((TM_OUT, V_W), lat_map),
                  pl.BlockSpec((TM_OUT, FN_G * FN_C), ctx_map),
                  pl.BlockSpec((TM_OUT, FN_G * FN_C), lat_map),
                  pl.BlockSpec((TM_OUT, D_MODEL), ctx_map),
                  pl.BlockSpec((TM_OUT, D_MODEL), lat_map),
                  pl.BlockSpec((None, 1, 6 * D_MODEL), lambda i: (_cond_row(i, TM_OUT), 0, 0)),
                  pl.BlockSpec((1, D_MODEL), const),
                  pl.BlockSpec((V_W, D_MODEL), const),
                  pl.BlockSpec((FN_G * FN_C, D_MODEL), const),
                  pl.BlockSpec((D_MODEL, LANES), const),
                  pl.BlockSpec((D_MODEL, LANES), const),
                  pl.BlockSpec((1, LANES), const),
                  pl.BlockSpec((TM_OUT, TM_OUT), const),
                  pl.BlockSpec((LANES, LANES), const)],
        out_specs=[pl.BlockSpec((TM_OUT, D_MODEL), lambda i: (i, 0)),
                   pl.BlockSpec((PACK_ROWS, D_MODEL), lambda i: (i, 0)),
                   pl.BlockSpec((TM_OUT, 8), lambda i: (i, 0)),
                   pl.BlockSpec((None, 1, LANES), lambda i: (i, 0, 0)),
                   pl.BlockSpec((None, 1, LANES), lambda i: (i, 0, 0))],
        out_shape=[jax.ShapeDtypeStruct((T_ALL, D_MODEL), F32),
                   jax.ShapeDtypeStruct((HS_ROWS, D_MODEL), BF16),
                   jax.ShapeDtypeStruct((T_ALL, 8), F32),
                   jax.ShapeDtypeStruct((nt, 1, LANES), I32),
                   jax.ShapeDtypeStruct((nt, 1, LANES), I32)],
        compiler_params=_cparams(("arbitrary",)),
        name="outproj",
    )(o_ctx, o_lat, f_ctx, f_lat, xp, xs, mod3, g_ffn, wo, wf, wr_hi, wr_lo, br, tri, su)


SRC_BITS = 16
SRC_MASK = (1 << SRC_BITS) - 1
X_SLOTS = 3
Y_SLOTS = 3
N_UP_CHUNKS = 2
N_DN_CHUNKS = 8
NT_MOE = (2 * T_ALL + N_TOK_TILES * N_EXP * (ROWS_PER_BLK - 1)) // TM_MOE + N_EXP


def _moe_kernel(texp_ref, nexp_ref, meta_ref, code_ref,
                h_hbm, wg_hbm, wu_hbm, wd_hbm,
                out_hbm,
                xbuf, ybuf, wg_f, wu_f, wd_f, wg_s, wu_s, wd_s, kcount, gsem, ssem, wsem):
    i = pl.program_id(0)
    nt = meta_ref[0]
    xs = i % X_SLOTS

    def blk_rows(b):
        if isinstance(b, int):
            return pl.ds(b * ROWS_PER_BLK, ROWS_PER_BLK)
        return pl.ds(pl.multiple_of(b * ROWS_PER_BLK, ROWS_PER_BLK), ROWS_PER_BLK)

    def gather_row(tile, sl, j):
        src = code_ref[(tile + 2) * BLK_PER_TILE + j] & SRC_MASK
        pltpu.make_async_copy(h_hbm.at[blk_rows(src)], xbuf.at[sl, blk_rows(j)], gsem.at[sl]).start()

    def scatter_row(tile, sl, j):
        dst = code_ref[(tile + 2) * BLK_PER_TILE + j] >> SRC_BITS
        pltpu.make_async_copy(ybuf.at[sl, blk_rows(j)], out_hbm.at[blk_rows(dst)], ssem.at[sl]).start(priority=1)

    def gather_wait(sl):
        pltpu.make_async_copy(h_hbm.at[pl.ds(0, TM_MOE)], xbuf.at[sl], gsem.at[sl]).wait()

    def scatter_wait(sl):
        pltpu.make_async_copy(ybuf.at[sl], out_hbm.at[pl.ds(0, TM_MOE)], ssem.at[sl]).wait()

    def y_slot(tile):
        return (tile + 1) % Y_SLOTS

    @pl.when(i == 0)
    def _():
        ybuf[y_slot(-2)] = jnp.zeros((TM_MOE, D_MODEL), BF16)
        ybuf[y_slot(-1)] = jnp.zeros((TM_MOE, D_MODEL), BF16)

        def body(j, c):
            gather_row(0, 0, j)
            gather_row(1, 1, j)
            scatter_row(-2, y_slot(-2), j)
            return c
        lax.fori_loop(0, BLK_PER_TILE, body, 0)

    @pl.when((i >= 1) & (i <= nt))
    def _():
        scatter_wait(y_slot(i - 3))

    @pl.when(i < nt)
    def _():
        prev = texp_ref[jnp.maximum(i - 1, 0)]

        def weight_copies(e, sl):
            return (pltpu.make_async_copy(wg_hbm.at[e], wg_f.at[sl], wsem.at[sl]),
                    pltpu.make_async_copy(wu_hbm.at[e], wu_f.at[sl], wsem.at[sl]),
                    pltpu.make_async_copy(wd_hbm.at[e], wd_f.at[sl], wsem.at[sl]))

        @pl.when(i == 0)
        def _():
            kcount[0] = 0
            for cp in weight_copies(texp_ref[0], 0):
                cp.start()

        @pl.when((i == 0) | (texp_ref[i] != prev))
        def _():
            k = kcount[0]
            for sl in range(2):
                @pl.when(k % 2 == sl)
                def _(sl=sl):
                    for cp in weight_copies(texp_ref[i], sl):
                        cp.wait()
                    def narrow(r, c):
                        up = pl.ds(pl.multiple_of(r * (D_MODEL // 16), D_MODEL // 16), D_MODEL // 16)
                        dn = pl.ds(pl.multiple_of(r * (D_EXP // 16), D_EXP // 16), D_EXP // 16)
                        wg_s[up, :] = wg_f[sl, up, :].astype(BF16)
                        wu_s[up, :] = wu_f[sl, up, :].astype(BF16)
                        wd_s[dn, :] = wd_f[sl, dn, :].astype(BF16)
                        return c
                    lax.fori_loop(0, 16, narrow, 0)

                    @pl.when(nexp_ref[i] >= 0)
                    def _():
                        for cp in weight_copies(nexp_ref[i], 1 - sl):
                            cp.start()
            kcount[0] = k + 1

        gather_wait(xs)
        x = xbuf[xs].astype(BF16)

        issues = []
        for j in range(BLK_PER_TILE):
            issues.append(functools.partial(gather_row, i + 2, (i + 2) % X_SLOTS, j))
            issues.append(functools.partial(scatter_row, i - 1, y_slot(i - 1), j))
        n_groups = N_UP_CHUNKS + N_DN_CHUNKS
        per_group = -(-len(issues) // n_groups)

        def issue_group(k):
            for fn in issues[k * per_group:(k + 1) * per_group]:
                fn()

        wu_c = D_EXP // N_UP_CHUNKS
        hid = []
        for n in range(N_UP_CHUNKS):
            issue_group(n)
            g = _dot(x, wg_s[:, n * wu_c:(n + 1) * wu_c])
            u = _dot(x, wu_s[:, n * wu_c:(n + 1) * wu_c])
            hid.append((_silu(g) * u).astype(BF16))
        hid = jnp.concatenate(hid, axis=1)
        wd_c = D_MODEL // N_DN_CHUNKS
        ys = y_slot(i)
        for n in range(N_DN_CHUNKS):
            issue_group(N_UP_CHUNKS + n)
            ybuf[ys, :, n * wd_c:(n + 1) * wd_c] = _dot(hid, wd_s[:, n * wd_c:(n + 1) * wd_c]).astype(BF16)

    @pl.when(i == nt)
    def _():
        gather_wait(xs)
        gather_wait((i + 1) % X_SLOTS)

        def body(j, c):
            scatter_row(nt - 1, y_slot(nt - 1), j)
            return c
        lax.fori_loop(0, BLK_PER_TILE, body, 0)
        scatter_wait(y_slot(nt - 2))
        scatter_wait(y_slot(nt - 1))


def _moe(texp, nexp, meta, code, hs, w_eg, w_eu, w_ed):
    hbm = pl.BlockSpec(memory_space=pl.ANY)
    grid_spec = pltpu.PrefetchScalarGridSpec(
        num_scalar_prefetch=4,
        grid=(NT_MOE + 1,),
        in_specs=[hbm, hbm, hbm, hbm],
        out_specs=hbm,
        scratch_shapes=[pltpu.VMEM((X_SLOTS, TM_MOE, D_MODEL), BF16),
                        pltpu.VMEM((Y_SLOTS, TM_MOE, D_MODEL), BF16),
                        pltpu.VMEM((2, D_MODEL, D_EXP), F32),
                        pltpu.VMEM((2, D_MODEL, D_EXP), F32),
                        pltpu.VMEM((2, D_EXP, D_MODEL), F32),
                        pltpu.VMEM((D_MODEL, D_EXP), BF16),
                        pltpu.VMEM((D_MODEL, D_EXP), BF16),
                        pltpu.VMEM((D_EXP, D_MODEL), BF16),
                        pltpu.SMEM((1,), I32),
                        pltpu.SemaphoreType.DMA((X_SLOTS,)),
                        pltpu.SemaphoreType.DMA((Y_SLOTS,)),
                        pltpu.SemaphoreType.DMA((2,))])
    return pl.pallas_call(
        _moe_kernel,
        grid_spec=grid_spec,
        out_shape=jax.ShapeDtypeStruct((HS_ROWS, D_MODEL), BF16),
        input_output_aliases={4: 0},
        compiler_params=_cparams(("arbitrary",)),
        name="moe",
    )(texp, nexp, meta, code, hs, w_eg, w_eu, w_ed)


TM_FIN = TM_OUT


def _final_kernel(x1_ref, ys_pack_ref, rw_ref, mod_ref, g_ref, yp_ref, ys_ref):
    i = pl.program_id(0)
    ga2 = mod_ref[:, 5 * D_MODEL:6 * D_MODEL]
    w0 = rw_ref[:, 0:1]
    w1 = rw_ref[:, 1:2]
    p0 = rw_ref[:, 2:3]
    p1 = rw_ref[:, 3:4]
    col = lax.broadcasted_iota(I32, (TM_FIN, PACK_ROWS), 1).astype(F32)
    comb = jnp.where(col == p0, w0, 0.0) + jnp.where(col == p1, w1, 0.0)
    y_moe = _dot(comb.astype(BF16), ys_pack_ref[...].astype(BF16))
    y = x1_ref[...] + ga2 * y_moe
    out = _rms(y, g_ref[...])

    @pl.when(i < T_CTX // TM_FIN)
    def _():
        yp_ref[...] = out

    @pl.when(i >= T_CTX // TM_FIN)
    def _():
        ys_ref[...] = out


def _final(x1, y2, rw, mod3, g_fin):
    nt = T_ALL // TM_FIN
    nctx = T_CTX // TM_FIN
    return pl.pallas_call(
        _final_kernel,
        grid=(nt,),
        in_specs=[pl.BlockSpec((TM_FIN, D_MODEL), lambda i: (i, 0)),
                  pl.BlockSpec((PACK_ROWS, D_MODEL), lambda i: (i, 0)),
                  pl.BlockSpec((TM_FIN, 8), lambda i: (i, 0)),
                  pl.BlockSpec((None, 1, 6 * D_MODEL), lambda i: (_cond_row(i, TM_FIN), 0, 0)),
                  pl.BlockSpec((1, D_MODEL), lambda i: (0, 0))],
        out_specs=[pl.BlockSpec((TM_FIN, D_MODEL), lambda i: (jnp.minimum(i, nctx - 1), 0)),
                   pl.BlockSpec((TM_FIN, D_MODEL), lambda i: (jnp.maximum(i - nctx, 0), 0))],
        out_shape=[jax.ShapeDtypeStruct((T_CTX, D_MODEL), F32),
                   jax.ShapeDtypeStruct((T_LAT, D_MODEL), F32)],
        compiler_params=_cparams(("arbitrary",)),
        name="final",
    )(x1, y2, rw, mod3, g_fin)


def _np_bf16(a):
    return jnp.asarray(np.asarray(a, np.float32), dtype=BF16)


def _np_f32(a):
    return jnp.asarray(np.asarray(a, np.float32))


@functools.lru_cache(maxsize=None)
def _constants():
    c = {}
    k = np.arange(FN_C)
    ang = 2.0 * np.pi * np.outer(k, k) / FN_C
    c["cs"] = np.concatenate([np.cos(ang), np.sin(ang)], axis=1) / np.sqrt(FN_C)
    p = np.arange(L_CTX)
    ang = 2.0 * np.pi * np.outer(p, p) / L_CTX
    c["p2"] = np.concatenate([np.cos(ang), -np.sin(ang)], axis=1) / np.sqrt(L_CTX)
    pos = np.arange(L_LAT)
    rr, cc = pos // GRID_W, pos % GRID_W
    num = (np.outer(rr, rr) * (GRID_W // GRID_H) + np.outer(cc, cc)) % GRID_W
    ang = 2.0 * np.pi * num / GRID_W
    c["kr"] = np.concatenate([np.cos(ang), -np.sin(ang)], axis=1) / np.sqrt(L_LAT)
    i = np.arange(SB)
    same = (i[:, None] // CHUNK) == (i[None, :] // CHUNK)
    tl = same & (i[:, None] >= i[None, :])
    tu = same & (i[:, None] <= i[None, :])
    c["tlu"] = np.concatenate([tl, tu], axis=1).astype(np.float32)
    c["tri"] = (i[:, None] > i[None, :]).astype(np.float32)
    k = np.arange(LANES)
    c["su"] = (k[:, None] < k[None, :]).astype(np.float32)
    return c


def kernel(x_prompt, x_sample, state_gla_fwd, state_gla_bwd, c, c_ctx, w_ada, b_ada, norm_attn, norm_ffn, w_in, w_gate_fwd, b_gate_fwd, w_gate_bwd, b_gate_bwd, norm_gla, w_out, w_router_group, b_router_group, w_router_expert, b_router_expert, w_expert_gate, w_expert_up, w_expert_down, norm_final):
    assert w_ada.shape[0] == 1, "single layer"
    cst = _constants()
    cs, p2, kr = _np_f32(cst["cs"]), _np_f32(cst["p2"]), _np_f32(cst["kr"])
    tlu, tri, su = _np_bf16(cst["tlu"]), _np_bf16(cst["tri"]), _np_bf16(cst["su"])

    xp = x_prompt.reshape(T_CTX, D_MODEL)
    xs = x_sample.reshape(T_LAT, D_MODEL)

    cond8 = jnp.concatenate([c_ctx[None, :], c, jnp.zeros((3, D_MODEL), F32)], axis=0)
    mod = _ada(cond8, w_ada[0], b_ada[0][None, :])
    mod3 = mod.reshape(8, 1, 6 * D_MODEL)

    wi = w_in[0]
    i_og = 2 * QK_W + 2 * V_W
    i_u = i_og + 2 * RANK
    w_main = jnp.concatenate([wi[:, :i_og], wi[:, i_u:]], axis=1).astype(BF16)
    w_r = jnp.pad(wi[:, i_og:i_u], ((0, 0), (0, LANES - 2 * RANK))).astype(BF16)

    wgf = w_gate_fwd[0].reshape(RANK, HEADS, DK)
    wgb = w_gate_bwd[0].reshape(RANK, HEADS, DK)
    zf = jnp.zeros_like(wgf)
    top = jnp.stack([wgf, zf], axis=2)
    bot = jnp.stack([zf, wgb], axis=2)
    wg = jnp.concatenate([top, bot], axis=0)
    wg = wg.reshape(2 * RANK, HEADS // 2, 4 * DK).transpose(1, 0, 2)
    wg = jnp.pad(wg, ((0, 0), (0, LANES - 2 * RANK), (0, 0)))
    wg_hi = wg.astype(BF16)
    wg_lo = (wg - wg_hi.astype(F32)).astype(BF16)
    bg = jnp.stack([b_gate_fwd[0].reshape(HEADS, DK), b_gate_bwd[0].reshape(HEADS, DK)], axis=1)
    bg = bg.reshape(HEADS // 2, 1, 4 * DK)
    ng = norm_gla[0].reshape(HEADS // 2, 1, 2 * DV)

    proj, r_all = _inproj(xp, xs, mod3, norm_attn, w_main, w_r)

    zero_state = jnp.zeros((N_CTX, HEADS, DK, DV), F32)
    o_ctx, sf_ctx, sb_ctx = _gla(proj, r_all, wg_hi, wg_lo, bg, ng, tlu, zero_state, zero_state,
                                 seq_len=L_CTX, n_seq=N_CTX, n_in_step=4, row0=0)
    o_lat, _, _ = _gla(proj, r_all, wg_hi, wg_lo, bg, ng, tlu,
                       state_gla_fwd[:, 0], state_gla_bwd[:, 0],
                       seq_len=L_LAT, n_seq=N_LAT, n_in_step=1, row0=T_CTX)

    f_ctx = _fnet_ctx(proj, cs, p2)
    f_lat = _fnet_lat(proj, cs, kr)

    wo = w_out[0][:V_W].astype(BF16)
    wf = w_out[0][V_W:].astype(BF16)
    wr = jnp.concatenate([w_router_group[0], w_router_expert[0]], axis=1)
    wr = jnp.pad(wr, ((0, 0), (0, LANES - N_GROUPS - N_EXP)))
    wr_hi = wr.astype(BF16)
    wr_lo = (wr - wr_hi.astype(F32)).astype(BF16)
    br = jnp.pad(jnp.concatenate([b_router_group[0], b_router_expert[0]]), (0, LANES - N_GROUPS - N_EXP))[None, :]

    x1, hs, rw, nb, lb = _outproj(o_ctx, o_lat, f_ctx, f_lat, xp, xs, mod3, norm_ffn, wo, wf,
                                  wr_hi, wr_lo, br, tri, su)

    nb_e = nb[:, 0, LANE_E0:LANE_E0 + N_EXP].T
    lb_e = lb[:, 0, LANE_E0:LANE_E0 + N_EXP].T
    run_end = jnp.cumsum(nb_e, axis=1)
    blocks_e = run_end[:, -1]
    tiles_e = (blocks_e + BLK_PER_TILE - 1) // BLK_PER_TILE
    tile_end = jnp.cumsum(tiles_e)
    tile_start = tile_end - tiles_e
    n_tiles = tile_end[-1]
    n_code_tiles = NT_MOE + 4
    tile = jnp.arange(n_code_tiles, dtype=I32) - 2
    tile_c = jnp.clip(tile, 0, n_tiles - 1)
    t_exp = jnp.sum(tile_c[:, None] >= tile_end[None, :], axis=1)
    ends = run_end[t_exp]
    starts = ends - nb_e[t_exp]
    offs = lb_e[t_exp] + jnp.arange(N_TOK_TILES, dtype=I32)[None, :] * PACK_BLKS - starts
    j = jnp.arange(BLK_PER_TILE, dtype=I32)
    bi = ((tile_c - tile_start[t_exp]) * BLK_PER_TILE)[:, None] + j[None, :]
    in_run = (starts[:, None, :] <= bi[:, :, None]) & (bi[:, :, None] < ends[:, None, :])
    blk = bi + jnp.sum(jnp.where(in_run, offs[:, None, :], 0), axis=2)
    valid = (tile == tile_c)[:, None] & (bi < blocks_e[t_exp][:, None])
    spare = ((jnp.arange(n_code_tiles, dtype=I32) % 2)[:, None] * BLK_PER_TILE + j[None, :]) * PACK_BLKS + USED_BLKS
    code = jnp.where(valid, (blk << SRC_BITS) | blk, (spare << SRC_BITS) | blk[:, 0:1]).astype(I32).reshape(-1)
    tidx = jnp.minimum(jnp.arange(NT_MOE + 1, dtype=I32), n_tiles - 1)
    texp = jnp.sum(tidx[:, None] >= tile_end[None, :], axis=1).astype(I32)
    nxt = tile_end[texp]
    nexp = jnp.where(nxt < n_tiles, texp[jnp.minimum(nxt, NT_MOE)], -1).astype(I32)
    meta = n_tiles.reshape(1).astype(I32)

    y2 = _moe(texp, nexp, meta, code, hs, w_expert_gate[0], w_expert_up[0], w_expert_down[0])
    y_prompt, y_sample = _final(x1, y2, rw, mod3, norm_final[None, :])

    st_shape = (N_CTX, 1, HEADS, DK, DV)
    return (y_prompt.reshape(N_CTX, L_CTX, D_MODEL), y_sample.reshape(N_LAT, L_LAT, D_MODEL),
            sf_ctx.reshape(st_shape), sb_ctx.reshape(st_shape))
```

```python
import functools

import numpy as np
import jax
import jax.numpy as jnp
from jax import lax
from jax.experimental import pallas as pl
from jax.experimental.pallas import tpu as pltpu

F32 = jnp.float32
BF16 = jnp.bfloat16
I32 = jnp.int32

D_MODEL = 2048
N_CTX = 32
L_CTX = 256
N_LAT = 4
L_LAT = 2048
GRID_H = 32
GRID_W = 64
T_CTX = N_CTX * L_CTX
T_LAT = N_LAT * L_LAT
T_ALL = T_CTX + T_LAT
HEADS = 8
DK = 64
DV = 128
RANK = 16
TAU = 16.0
CHUNK = 64
FN_G = 8
FN_C = 128
QK_W = HEADS * DK
V_W = HEADS * DV
N_GROUPS = 4
EPG = 8
N_EXP = N_GROUPS * EPG
D_EXP = 512
EPS = 1e-6

LANES = 128
VMEM_LIMIT = 56 * 1024 * 1024

TM_MOE = 256


def _dot(a, b):
    return jnp.dot(a, b, preferred_element_type=F32)


def _dot_nt(a, b):
    return lax.dot_general(a, b, (((1,), (1,)), ((), ())), preferred_element_type=F32)


def _split2(x):
    hi = x.astype(BF16)
    lo = (x - hi.astype(F32)).astype(BF16)
    return hi, lo


def _silu(x):
    return x * (1.0 / (1.0 + jnp.exp(-x)))


def _rms(x, g):
    return x * lax.rsqrt(jnp.mean(x * x, axis=-1, keepdims=True) + EPS) * g


def _cparams(sem):
    return pltpu.CompilerParams(dimension_semantics=sem, vmem_limit_bytes=VMEM_LIMIT)


def _ada_kernel(c_ref, w_ref, b_ref, o_ref):
    s_hi, s_lo = _split2(_silu(c_ref[...]))
    w = w_ref[...]
    w_hi = w.astype(BF16)
    w_lo = (w - w_hi.astype(F32)).astype(BF16)
    o_ref[...] = _dot(s_hi, w_hi) + _dot(s_lo, w_hi) + _dot(s_hi, w_lo) + b_ref[...]


def _ada(cond8, w_ada, b_ada):
    tn = 768
    n6 = 6 * D_MODEL
    return pl.pallas_call(
        _ada_kernel,
        grid=(n6 // tn,),
        in_specs=[pl.BlockSpec((8, D_MODEL), lambda j: (0, 0)),
                  pl.BlockSpec((D_MODEL, tn), lambda j: (0, j)),
                  pl.BlockSpec((1, tn), lambda j: (0, j))],
        out_specs=pl.BlockSpec((8, tn), lambda j: (0, j)),
        out_shape=jax.ShapeDtypeStruct((8, n6), F32),
        compiler_params=_cparams(("arbitrary",)),
        name="ada",
    )(cond8, w_ada, b_ada)


TM_IN = 256
TN_IN = 1024
N_MAIN = 4096


def _cond_row(tile, tm):
    ctx_tiles = T_CTX // tm
    per_seq = L_LAT // tm
    return jnp.where(tile < ctx_tiles, 0, 1 + (jnp.maximum(tile - ctx_tiles, 0)) // per_seq)


def _inproj_kernel(xp_ref, xs_ref, mod_ref, g_ref, w_ref, wr_ref, proj_ref, r_ref):
    i = pl.program_id(0)
    x = jnp.where(i < T_CTX // TM_IN, xp_ref[...], xs_ref[...])
    sh1 = mod_ref[:, 0:D_MODEL]
    sc1 = mod_ref[:, D_MODEL:2 * D_MODEL]
    hb = (_rms(x, g_ref[...]) * (1.0 + sc1) + sh1).astype(BF16)
    r_ref[...] = _dot(hb, wr_ref[...])
    for n in range(N_MAIN // TN_IN):
        proj_ref[:, n * TN_IN:(n + 1) * TN_IN] = _dot(hb, w_ref[:, n * TN_IN:(n + 1) * TN_IN]).astype(BF16)


def _inproj(xp, xs, mod3, g_attn, w_main, w_r):
    nt = T_ALL // TM_IN
    nctx = T_CTX // TM_IN
    resident = pl.Buffered(1)
    return pl.pallas_call(
        _inproj_kernel,
        grid=(nt,),
        in_specs=[pl.BlockSpec((TM_IN, D_MODEL), lambda i: (jnp.minimum(i, nctx - 1), 0)),
                  pl.BlockSpec((TM_IN, D_MODEL), lambda i: (jnp.maximum(i - nctx, 0), 0)),
                  pl.BlockSpec((None, 1, 6 * D_MODEL), lambda i: (_cond_row(i, TM_IN), 0, 0)),
                  pl.BlockSpec((1, D_MODEL), lambda i: (0, 0)),
                  pl.BlockSpec((D_MODEL, N_MAIN), lambda i: (0, 0), pipeline_mode=resident),
                  pl.BlockSpec((D_MODEL, LANES), lambda i: (0, 0), pipeline_mode=resident)],
        out_specs=[pl.BlockSpec((TM_IN, N_MAIN), lambda i: (i, 0)),
                   pl.BlockSpec((TM_IN, LANES), lambda i: (i, 0))],
        out_shape=[jax.ShapeDtypeStruct((T_ALL, N_MAIN), BF16),
                   jax.ShapeDtypeStruct((T_ALL, LANES), F32)],
        compiler_params=_cparams(("arbitrary",)),
        name="inproj",
    )(xp, xs, mod3, g_attn, w_main, w_r)


SB = 256
CPB = SB // CHUNK


def _gla_kernel(q_ref, k_ref, v_ref, og_ref, r_ref, wgh_ref, wgl_ref, bg_ref, ng_ref,
                tlu_ref, s0f_ref, s0b_ref,
                o_ref, sf_ref, sb_ref,
                cum_s, last_s, qe_s, ke_s, kd_s, dec_s, kv_s, sbs_s, *, seq_len, n_in_step):
    sb_per_seq = seq_len // SB
    ch_per_seq = seq_len // CHUNK
    n_sb = n_in_step * sb_per_seq
    lane = lax.broadcasted_iota(I32, (1, LANES), 1)
    m_f = lane < DK

    r_hi, r_lo = _split2(r_ref[...])
    z = _dot(r_hi, wgh_ref[...]) + _dot(r_lo, wgh_ref[...]) + _dot(r_hi, wgl_ref[...]) + bg_ref[...]
    g_all = (jnp.minimum(z, 0.0) - jnp.log(1.0 + jnp.exp(-jnp.abs(z)))) * (1.0 / TAU)

    q_pair = q_ref[...].astype(F32)
    k_pair = k_ref[...].astype(F32)
    q_roll = pltpu.roll(q_pair, DK, axis=1)
    k_roll = pltpu.roll(k_pair, DK, axis=1)

    row_b = lax.broadcasted_iota(I32, (SB, SB), 0)
    col_b = lax.broadcasted_iota(I32, (SB, SB), 1)
    same_chunk = (row_b // CHUNK) == (col_b // CHUNK)
    tril = same_chunk & (row_b >= col_b)
    triu = same_chunk & (row_b <= col_b)
    row_chunk = lax.broadcasted_iota(I32, (SB, 1), 0) // CHUNK
    col_chunk = lax.broadcasted_iota(I32, (1, SB), 1) // CHUNK

    m_f2 = (lax.broadcasted_iota(I32, (1, 2 * LANES), 1) % LANES) < DK
    cum_s[...] = g_all

    def cum_body(s, carry):
        rows = pl.ds(pl.multiple_of(s * SB, SB), SB)
        g = cum_s[rows, :]
        f_hi, f_lo = _split2(jnp.where(m_f2, g, 0.0))
        b_hi, b_lo = _split2(jnp.where(m_f2, 0.0, g))
        cum = (_dot(tlu_ref[...], jnp.concatenate([f_hi, b_hi], axis=0))
               + _dot(tlu_ref[...], jnp.concatenate([f_lo, b_lo], axis=0)))
        cum_s[rows, :] = cum
        tots = []
        for c in range(CPB):
            tot = jnp.where(m_f2, cum[(c + 1) * CHUNK - 1:(c + 1) * CHUNK, :], cum[c * CHUNK:c * CHUNK + 1, :])
            dec_s[s * CPB + c] = jnp.exp(tot)
            tots.append(jnp.broadcast_to(tot, (CHUNK, 2 * LANES)))
        last_s[rows, :] = jnp.concatenate(tots, axis=0)
        return carry

    lax.fori_loop(0, n_sb, cum_body, 0)

    for j in range(2):
        cum = cum_s[:, j * LANES:(j + 1) * LANES]
        last = last_s[:, j * LANES:(j + 1) * LANES]
        if j == 0:
            q2 = jnp.where(m_f, q_pair, q_roll)
            k2 = jnp.where(m_f, k_pair, k_roll)
        else:
            q2 = jnp.where(m_f, q_roll, q_pair)
            k2 = jnp.where(m_f, k_roll, k_pair)
        qe_s[j] = ((q2 * (DK ** -0.5)) * jnp.exp(cum)).astype(BF16)
        ke_s[j] = (k2 * jnp.exp(-cum)).astype(BF16)
        kd_s[j] = (k2 * jnp.exp(last - cum)).astype(BF16)

    def kv_body(s, carry):
        rows = pl.ds(pl.multiple_of(s * SB, SB), SB)
        for j in range(2):
            v_t = jnp.transpose(v_ref[rows, j * DV:(j + 1) * DV].astype(F32)).astype(BF16)
            zero = jnp.zeros_like(v_t)
            v_st = jnp.concatenate([jnp.where(col_chunk == c, v_t, zero) for c in range(CPB)], axis=0)
            kv = _dot(v_st, kd_s[j, rows, :])
            for c in range(CPB):
                kv_s[j, s * CPB + c] = kv[c * LANES:(c + 1) * LANES, :]
        return carry

    lax.fori_loop(0, n_sb, kv_body, 0, unroll=min(2, n_sb))

    def dec_row(j, c):
        return dec_s[c][:, j * LANES:(j + 1) * LANES]

    def step_state(j, c, st):
        return st * dec_row(j, c) + kv_s[j, c]

    for q in range(n_in_step):
        st0 = tuple(jnp.transpose(jnp.concatenate([s0f_ref[q, j], s0b_ref[q, j]], axis=0)) for j in range(2))
        c0 = q * ch_per_seq
        s0 = q * sb_per_seq

        def bwd_body(t, sts, c0=c0):
            c = c0 + ch_per_seq - 1 - t
            for j in range(2):
                sbs_s[j, c] = sts[j]
            return tuple(step_state(j, c, sts[j]) for j in range(2))

        st_b_fin = lax.fori_loop(0, ch_per_seq, bwd_body, st0, unroll=CPB)

        def fwd_body(sl, sts, s0=s0):
            s = s0 + sl
            rows = pl.ds(pl.multiple_of(s * SB, SB), SB)
            new = []
            for j in range(2):
                lo = j * DV
                qe = qe_s[j, rows, :]
                ke = ke_s[j, rows, :]
                v_b = v_ref[rows, lo:lo + DV].astype(BF16)
                zero = jnp.zeros_like(qe)
                st = sts[j]
                q_parts, s_parts = [], []
                for c in range(CPB):
                    ci = s * CPB + c
                    s_parts.append(jnp.where(m_f, st, sbs_s[j, ci]).astype(BF16))
                    q_parts.append(jnp.where(row_chunk == c, qe, zero))
                    st = step_state(j, ci, st)
                o = _dot_nt(jnp.concatenate(q_parts, axis=1), jnp.concatenate(s_parts, axis=1))
                q_st = jnp.concatenate([jnp.where(m_f, qe, zero), jnp.where(m_f, zero, qe)], axis=0)
                sc = _dot_nt(q_st, ke)
                p = jnp.where(tril, sc[0:SB, :], 0.0) + jnp.where(triu, sc[SB:2 * SB, :], 0.0)
                o = o + _dot(p.astype(BF16), v_b)
                o = o * lax.rsqrt(jnp.mean(o * o, axis=-1, keepdims=True) + EPS) * ng_ref[:, lo:lo + DV]
                o_ref[rows, lo:lo + DV] = (o * _silu(og_ref[rows, lo:lo + DV].astype(F32))).astype(BF16)
                new.append(st)
            return tuple(new)

        st_f_fin = lax.fori_loop(0, sb_per_seq, fwd_body, st0, unroll=min(2, sb_per_seq))
        for j in range(2):
            sf_ref[q, j] = jnp.transpose(st_f_fin[j])[0:DK, :]
            sb_ref[q, j] = jnp.transpose(st_b_fin[j])[DK:2 * DK, :]


def _gla(proj, r_all, wg_hi, wg_lo, bg, ng, tlu, s0f, s0b, *, seq_len, n_seq, n_in_step, row0):
    kern = functools.partial(_gla_kernel, seq_len=seq_len, n_in_step=n_in_step)
    rows = n_in_step * seq_len
    n_chunks = rows // CHUNK
    rb = lambda b: b + row0 // rows
    st_spec = pl.BlockSpec((n_in_step, 2, DK, DV), lambda b, h: (b, h, 0, 0))
    return pl.pallas_call(
        kern,
        grid=(n_seq // n_in_step, HEADS // 2),
        in_specs=[pl.BlockSpec((rows, LANES), lambda b, h: (rb(b), h)),
                  pl.BlockSpec((rows, LANES), lambda b, h: (rb(b), QK_W // LANES + h)),
                  pl.BlockSpec((rows, 2 * DV), lambda b, h: (rb(b), 2 * QK_W // (2 * DV) + h)),
                  pl.BlockSpec((rows, 2 * DV), lambda b, h: (rb(b), (2 * QK_W + V_W) // (2 * DV) + h)),
                  pl.BlockSpec((rows, LANES), lambda b, h: (rb(b), 0)),
                  pl.BlockSpec((None, LANES, 2 * LANES), lambda b, h: (h, 0, 0)),
                  pl.BlockSpec((None, LANES, 2 * LANES), lambda b, h: (h, 0, 0)),
                  pl.BlockSpec((None, 1, 2 * LANES), lambda b, h: (h, 0, 0)),
                  pl.BlockSpec((None, 1, 2 * DV), lambda b, h: (h, 0, 0)),
                  pl.BlockSpec((SB, 2 * SB), lambda b, h: (0, 0)),
                  st_spec, st_spec],
        out_specs=[pl.BlockSpec((rows, 2 * DV), lambda b, h: (b, h)), st_spec, st_spec],
        out_shape=[jax.ShapeDtypeStruct((n_seq * seq_len, V_W), BF16),
                   jax.ShapeDtypeStruct((n_seq, HEADS, DK, DV), F32),
                   jax.ShapeDtypeStruct((n_seq, HEADS, DK, DV), F32)],
        scratch_shapes=[pltpu.VMEM((rows, 2 * LANES), F32),
                        pltpu.VMEM((rows, 2 * LANES), F32),
                        pltpu.VMEM((2, rows, LANES), BF16),
                        pltpu.VMEM((2, rows, LANES), BF16),
                        pltpu.VMEM((2, rows, LANES), BF16),
                        pltpu.VMEM((n_chunks, 1, 2 * LANES), F32),
                        pltpu.VMEM((2, n_chunks, DV, LANES), F32),
                        pltpu.VMEM((2, n_chunks, DV, LANES), F32)],
        compiler_params=_cparams(("arbitrary", "arbitrary")),
        name="gla_%d" % seq_len,
    )(proj, proj, proj, proj, r_all, wg_hi, wg_lo, bg, ng, tlu, s0f, s0b)


def _fnet_stage_a(u_bf, cs):
    cparts, sparts = [], []
    for g in range(FN_G):
        ab = _dot(u_bf[:, g * FN_C:(g + 1) * FN_C], cs)
        cparts.append(ab[:, 0:FN_C])
        sparts.append(ab[:, FN_C:2 * FN_C])
    return jnp.concatenate(cparts, axis=1), jnp.concatenate(sparts, axis=1)


def _fnet_ctx_kernel(u_ref, cs_ref, p2_ref, f_ref):
    uc, us = _fnet_stage_a(u_ref[...].astype(BF16), cs_ref[...].astype(BF16))
    ab = jnp.concatenate([uc, us], axis=0).astype(BF16)
    f_ref[...] = _dot(p2_ref[...].astype(BF16), ab).astype(BF16)


def _fnet_ctx(proj, cs, p2):
    return pl.pallas_call(
        _fnet_ctx_kernel,
        grid=(N_CTX,),
        in_specs=[pl.BlockSpec((L_CTX, FN_G * FN_C), lambda b: (b, 3)),
                  pl.BlockSpec((FN_C, 2 * FN_C), lambda b: (0, 0)),
                  pl.BlockSpec((L_CTX, 2 * L_CTX), lambda b: (0, 0))],
        out_specs=pl.BlockSpec((L_CTX, FN_G * FN_C), lambda b: (b, 0)),
        out_shape=jax.ShapeDtypeStruct((T_CTX, FN_G * FN_C), BF16),
        compiler_params=_cparams(("arbitrary",)),
        name="fnet_ctx",
    )(proj, cs, p2)


TM_FL = 256
RT_FL = 256


def _fnet_lat_kernel(u_ref, cs_ref, kr_ref, f_ref, ab_s):
    m = pl.program_id(1)

    @pl.when(m == 0)
    def _():
        def body(t, carry):
            rows = pl.ds(pl.multiple_of(t * RT_FL, RT_FL), RT_FL)
            uc, us = _fnet_stage_a(u_ref[rows, :].astype(BF16), cs_ref[...].astype(BF16))
            ab_s[rows, :] = uc.astype(BF16)
            ab_s[pl.ds(pl.multiple_of(L_LAT + t * RT_FL, RT_FL), RT_FL), :] = us.astype(BF16)
            return carry
        lax.fori_loop(0, L_LAT // RT_FL, body, 0)

    f_ref[...] = _dot(kr_ref[...].astype(BF16), ab_s[...]).astype(BF16)


def _fnet_lat(proj, cs, kr):
    nm = L_LAT // TM_FL
    return pl.pallas_call(
        _fnet_lat_kernel,
        grid=(N_LAT, nm),
        in_specs=[pl.BlockSpec((L_LAT, FN_G * FN_C), lambda b, m: (T_CTX // L_LAT + b, 3)),
                  pl.BlockSpec((FN_C, 2 * FN_C), lambda b, m: (0, 0)),
                  pl.BlockSpec((TM_FL, 2 * L_LAT), lambda b, m: (m, 0))],
        out_specs=pl.BlockSpec((TM_FL, FN_G * FN_C), lambda b, m: (b * nm + m, 0)),
        out_shape=jax.ShapeDtypeStruct((T_LAT, FN_G * FN_C), BF16),
        scratch_shapes=[pltpu.VMEM((2 * L_LAT, FN_G * FN_C), BF16)],
        compiler_params=_cparams(("arbitrary", "arbitrary")),
        name="fnet_lat",
    )(proj, cs, kr)


TM_OUT = 256
LANE_E0 = N_GROUPS
ROWS_PER_BLK = 8
PACK_ROWS = -(-(2 * TM_OUT + N_EXP * (ROWS_PER_BLK - 1)) // 256) * 256
PACK_BLKS = PACK_ROWS // ROWS_PER_BLK
N_TOK_TILES = T_ALL // TM_OUT
BLK_PER_TILE = TM_MOE // ROWS_PER_BLK
USED_BLKS = (2 * TM_OUT + N_EXP * (ROWS_PER_BLK - 1)) // ROWS_PER_BLK
HS_ROWS = N_TOK_TILES * PACK_ROWS
assert USED_BLKS < PACK_BLKS and 2 * BLK_PER_TILE <= N_TOK_TILES


def _outproj_kernel(oc_ref, ol_ref, fc_ref, fl_ref, xp_ref, xs_ref, mod_ref, g_ref, wo_ref, wf_ref,
                    wrh_ref, wrl_ref, br_ref, sut_ref, sl_ref,
                    x1_ref, hs_ref, rw_ref, nb_ref, lb_ref):
    i = pl.program_id(0)
    is_ctx = i < T_CTX // TM_OUT

    o = jnp.where(is_ctx, oc_ref[...], ol_ref[...]).astype(BF16)
    f = jnp.where(is_ctx, fc_ref[...], fl_ref[...]).astype(BF16)
    x = jnp.where(is_ctx, xp_ref[...], xs_ref[...])
    y = _dot(o, wo_ref[...]) + _dot(f, wf_ref[...])
    ga1 = mod_ref[:, 2 * D_MODEL:3 * D_MODEL]
    sh2 = mod_ref[:, 3 * D_MODEL:4 * D_MODEL]
    sc2 = mod_ref[:, 4 * D_MODEL:5 * D_MODEL]
    x1 = x + ga1 * y
    x1_ref[...] = x1
    h2 = _rms(x1, g_ref[...]) * (1.0 + sc2) + sh2

    h_hi, h_lo = _split2(h2)
    lg_all = jnp.transpose(_dot(h_hi, wrh_ref[...]) + _dot(h_lo, wrh_ref[...]) + _dot(h_hi, wrl_ref[...])
                           + br_ref[...])

    row_i = lax.broadcasted_iota(I32, (LANES, TM_OUT), 0)
    row = row_i.astype(F32)
    neg = jnp.float32(-jnp.inf)
    big = jnp.float32(LANES)
    lg = jnp.where(row_i < N_GROUPS, lg_all, neg)
    gmax = jnp.max(lg, axis=0, keepdims=True)
    gsel = jnp.min(jnp.where(lg == gmax, row, big), axis=0, keepdims=True)
    den = jnp.sum(jnp.exp(lg - gmax), axis=0, keepdims=True)
    pg_sel = 1.0 / den

    e_idx = row_i - LANE_E0
    egrp = (e_idx >> 3).astype(F32)
    emask = (e_idx >= 0) & (e_idx < N_EXP) & (egrp == gsel)
    m1 = jnp.where(emask, lg_all, neg)
    v1 = jnp.max(m1, axis=0, keepdims=True)
    i1 = jnp.min(jnp.where(m1 == v1, row, big), axis=0, keepdims=True)
    m2 = jnp.where(row == i1, neg, m1)
    v2 = jnp.max(m2, axis=0, keepdims=True)
    i2 = jnp.min(jnp.where(m2 == v2, row, big), axis=0, keepdims=True)
    e2 = jnp.exp(v2 - v1)
    inv = 1.0 / (1.0 + e2)
    w1 = inv * pg_sel
    w2 = (e2 * inv) * pg_sel

    oh1 = row == i1
    oh2 = row == i2
    oh = jnp.where(oh1 | oh2, 1.0, 0.0).astype(BF16)
    cnt = _dot(oh, jnp.ones((TM_OUT, LANES), BF16))
    nblk = jnp.floor((cnt + (ROWS_PER_BLK - 1)) * (1.0 / ROWS_PER_BLK))
    lboff = _dot(sl_ref[...], nblk.astype(BF16))
    lrank = _dot(oh, sut_ref[...])
    posmat = jnp.concatenate([lboff, lboff], axis=1) * ROWS_PER_BLK + lrank
    p1 = jnp.sum(jnp.where(oh1, posmat, 0.0), axis=0, keepdims=True)
    p2 = jnp.sum(jnp.where(oh2, posmat, 0.0), axis=0, keepdims=True)
    nb_ref[...] = nblk[:, 0:8].astype(I32)
    lb_ref[...] = lboff[:, 0:8].astype(I32)

    prow = lax.broadcasted_iota(I32, (PACK_ROWS, TM_OUT), 0).astype(F32)
    place = jnp.where((prow == p1) | (prow == p2), 1.0, 0.0).astype(BF16)
    hs_ref[...] = _dot(place, h_hi).astype(BF16)

    rw_ref[...] = jnp.concatenate([w1, w2, p1, p2, jnp.zeros((4, TM_OUT), F32)], axis=0)


def _outproj(o_ctx, o_lat, f_ctx, f_lat, xp, xs, mod3, g_ffn, wo, wf, wr_hi, wr_lo, br, sut, sl):
    nt = T_ALL // TM_OUT
    nctx = T_CTX // TM_OUT
    ctx_map = lambda i: (jnp.minimum(i, nctx - 1), 0)
    lat_map = lambda i: (jnp.maximum(i - nctx, 0), 0)
    const = lambda i: (0, 0)
    return pl.pallas_call(
        _outproj_kernel,
        grid=(nt,),
        in_specs=[pl.BlockSpec((TM_OUT, V_W), ctx_map),
                  pl.BlockSpec((TM_OUT, V_W), lat_map),
                  pl.BlockSpec((TM_OUT, FN_G * FN_C), ctx_map),
                  pl.BlockSpec((TM_OUT, FN_G * FN_C), lat_map),
                  pl.BlockSpec((TM_OUT, D_MODEL), ctx_map),
                  pl.BlockSpec((TM_OUT, D_MODEL), lat_map),
                  pl.BlockSpec((None, 1, 6 * D_MODEL), lambda i: (_cond_row(i, TM_OUT), 0, 0)),
                  pl.BlockSpec((1, D_MODEL), const),
                  pl.BlockSpec((V_W, D_MODEL), const),
                  pl.BlockSpec((FN_G * FN_C, D_MODEL), const),
                  pl.BlockSpec((D_MODEL, LANES), const),
                  pl.BlockSpec((D_MODEL, LANES), const),
                  pl.BlockSpec((1, LANES), const),
                  pl.BlockSpec((TM_OUT, TM_OUT), const),
                  pl.BlockSpec((LANES, LANES), const)],
        out_specs=[pl.BlockSpec((TM_OUT, D_MODEL), lambda i: (i, 0)),
                   pl.BlockSpec((PACK_ROWS, D_MODEL), lambda i: (i, 0)),
                   pl.BlockSpec((8, TM_OUT), lambda i: (0, i)),
                   pl.BlockSpec((None, LANES, 8), lambda i: (i, 0, 0)),
                   pl.BlockSpec((None, LANES, 8), lambda i: (i, 0, 0))],
        out_shape=[jax.ShapeDtypeStruct((T_ALL, D_MODEL), F32),
                   jax.ShapeDtypeStruct((HS_ROWS, D_MODEL), BF16),
                   jax.ShapeDtypeStruct((8, T_ALL), F32),
                   jax.ShapeDtypeStruct((nt, LANES, 8), I32),
                   jax.ShapeDtypeStruct((nt, LANES, 8), I32)],
        compiler_params=_cparams(("arbitrary",)),
        name="outproj",
    )(o_ctx, o_lat, f_ctx, f_lat, xp, xs, mod3, g_ffn, wo, wf, wr_hi, wr_lo, br, sut, sl)


SRC_BITS = 16
SRC_MASK = (1 << SRC_BITS) - 1
X_SLOTS = 3
Y_SLOTS = 3
N_UP_CHUNKS = 2
N_DN_CHUNKS = 8
NT_MOE = (2 * T_ALL + N_TOK_TILES * N_EXP * (ROWS_PER_BLK - 1)) // TM_MOE + N_EXP


def _moe_kernel(texp_ref, nexp_ref, meta_ref, code_ref,
                h_hbm, wg_hbm, wu_hbm, wd_hbm,
                out_hbm,
                xbuf, ybuf, wg_f, wu_f, wd_f, wg_s, wu_s, wd_s, kcount, gsem, ssem, wsem):
    i = pl.program_id(0)
    nt = meta_ref[0]
    xs = i % X_SLOTS

    def blk_rows(b):
        if isinstance(b, int):
            return pl.ds(b * ROWS_PER_BLK, ROWS_PER_BLK)
        return pl.ds(pl.multiple_of(b * ROWS_PER_BLK, ROWS_PER_BLK), ROWS_PER_BLK)

    def gather_row(tile, sl, j):
        src = code_ref[(tile + 2) * BLK_PER_TILE + j] & SRC_MASK
        pltpu.make_async_copy(h_hbm.at[blk_rows(src)], xbuf.at[sl, blk_rows(j)], gsem.at[sl]).start()

    def scatter_row(tile, sl, j):
        dst = code_ref[(tile + 2) * BLK_PER_TILE + j] >> SRC_BITS
        pltpu.make_async_copy(ybuf.at[sl, blk_rows(j)], out_hbm.at[blk_rows(dst)], ssem.at[sl]).start(priority=1)

    def gather_wait(sl):
        pltpu.make_async_copy(h_hbm.at[pl.ds(0, TM_MOE)], xbuf.at[sl], gsem.at[sl]).wait()

    def scatter_wait(sl):
        pltpu.make_async_copy(ybuf.at[sl], out_hbm.at[pl.ds(0, TM_MOE)], ssem.at[sl]).wait()

    def y_slot(tile):
        return (tile + 1) % Y_SLOTS

    @pl.when(i == 0)
    def _():
        ybuf[y_slot(-2)] = jnp.zeros((TM_MOE, D_MODEL), BF16)
        ybuf[y_slot(-1)] = jnp.zeros((TM_MOE, D_MODEL), BF16)

        def body(j, c):
            gather_row(0, 0, j)
            gather_row(1, 1, j)
            scatter_row(-2, y_slot(-2), j)
            return c
        lax.fori_loop(0, BLK_PER_TILE, body, 0)

    @pl.when((i >= 1) & (i <= nt))
    def _():
        scatter_wait(y_slot(i - 3))

    @pl.when(i < nt)
    def _():
        prev = texp_ref[jnp.maximum(i - 1, 0)]

        def weight_copies(e, sl):
            return (pltpu.make_async_copy(wg_hbm.at[e], wg_f.at[sl], wsem.at[sl]),
                    pltpu.make_async_copy(wu_hbm.at[e], wu_f.at[sl], wsem.at[sl]),
                    pltpu.make_async_copy(wd_hbm.at[e], wd_f.at[sl], wsem.at[sl]))

        @pl.when(i == 0)
        def _():
            kcount[0] = 0
            for cp in weight_copies(texp_ref[0], 0):
                cp.start()

        @pl.when((i == 0) | (texp_ref[i] != prev))
        def _():
            k = kcount[0]
            for sl in range(2):
                @pl.when(k % 2 == sl)
                def _(sl=sl):
                    for cp in weight_copies(texp_ref[i], sl):
                        cp.wait()
                    def narrow(r, c):
                        up = pl.ds(pl.multiple_of(r * (D_MODEL // 16), D_MODEL // 16), D_MODEL // 16)
                        dn = pl.ds(pl.multiple_of(r * (D_EXP // 16), D_EXP // 16), D_EXP // 16)
                        wg_s[up, :] = wg_f[sl, up, :].astype(BF16)
                        wu_s[up, :] = wu_f[sl, up, :].astype(BF16)
                        wd_s[dn, :] = wd_f[sl, dn, :].astype(BF16)
                        return c
                    lax.fori_loop(0, 16, narrow, 0)

                    @pl.when(nexp_ref[i] >= 0)
                    def _():
                        for cp in weight_copies(nexp_ref[i], 1 - sl):
                            cp.start()
            kcount[0] = k + 1

        gather_wait(xs)
        x = xbuf[xs].astype(BF16)

        issues = []
        for j in range(BLK_PER_TILE):
            issues.append(functools.partial(gather_row, i + 2, (i + 2) % X_SLOTS, j))
            issues.append(functools.partial(scatter_row, i - 1, y_slot(i - 1), j))
        n_groups = N_UP_CHUNKS + N_DN_CHUNKS
        per_group = -(-len(issues) // n_groups)

        def issue_group(k):
            for fn in issues[k * per_group:(k + 1) * per_group]:
                fn()

        wu_c = D_EXP // N_UP_CHUNKS
        hid = []
        for n in range(N_UP_CHUNKS):
            issue_group(n)
            g = _dot(x, wg_s[:, n * wu_c:(n + 1) * wu_c])
            u = _dot(x, wu_s[:, n * wu_c:(n + 1) * wu_c])
            hid.append((_silu(g) * u).astype(BF16))
        hid = jnp.concatenate(hid, axis=1)
        wd_c = D_MODEL // N_DN_CHUNKS
        ys = y_slot(i)
        for n in range(N_DN_CHUNKS):
            issue_group(N_UP_CHUNKS + n)
            ybuf[ys, :, n * wd_c:(n + 1) * wd_c] = _dot(hid, wd_s[:, n * wd_c:(n + 1) * wd_c]).astype(BF16)

    @pl.when(i == nt)
    def _():
        gather_wait(xs)
        gather_wait((i + 1) % X_SLOTS)

        def body(j, c):
            scatter_row(nt - 1, y_slot(nt - 1), j)
            return c
        lax.fori_loop(0, BLK_PER_TILE, body, 0)
        scatter_wait(y_slot(nt - 2))
        scatter_wait(y_slot(nt - 1))


def _moe(texp, nexp, meta, code, hs, w_eg, w_eu, w_ed):
    hbm = pl.BlockSpec(memory_space=pl.ANY)
    grid_spec = pltpu.PrefetchScalarGridSpec(
        num_scalar_prefetch=4,
        grid=(NT_MOE + 1,),
        in_specs=[hbm, hbm, hbm, hbm],
        out_specs=hbm,
        scratch_shapes=[pltpu.VMEM((X_SLOTS, TM_MOE, D_MODEL), BF16),
                        pltpu.VMEM((Y_SLOTS, TM_MOE, D_MODEL), BF16),
                        pltpu.VMEM((2, D_MODEL, D_EXP), F32),
                        pltpu.VMEM((2, D_MODEL, D_EXP), F32),
                        pltpu.VMEM((2, D_EXP, D_MODEL), F32),
                        pltpu.VMEM((D_MODEL, D_EXP), BF16),
                        pltpu.VMEM((D_MODEL, D_EXP), BF16),
                        pltpu.VMEM((D_EXP, D_MODEL), BF16),
                        pltpu.SMEM((1,), I32),
                        pltpu.SemaphoreType.DMA((X_SLOTS,)),
                        pltpu.SemaphoreType.DMA((Y_SLOTS,)),
                        pltpu.SemaphoreType.DMA((2,))])
    return pl.pallas_call(
        _moe_kernel,
        grid_spec=grid_spec,
        out_shape=jax.ShapeDtypeStruct((HS_ROWS, D_MODEL), BF16),
        input_output_aliases={4: 0},
        compiler_params=_cparams(("arbitrary",)),
        name="moe",
    )(texp, nexp, meta, code, hs, w_eg, w_eu, w_ed)


TM_FIN = TM_OUT


def _final_kernel(x1_ref, ys_pack_ref, rw_ref, mod_ref, g_ref, yp_ref, ys_ref):
    i = pl.program_id(0)
    ga2 = mod_ref[:, 5 * D_MODEL:6 * D_MODEL]
    w0 = rw_ref[0:1, :]
    w1 = rw_ref[1:2, :]
    p0 = rw_ref[2:3, :]
    p1 = rw_ref[3:4, :]
    prow = lax.broadcasted_iota(I32, (PACK_ROWS, TM_FIN), 0).astype(F32)
    comb_t = jnp.where(prow == p0, w0, 0.0) + jnp.where(prow == p1, w1, 0.0)
    y_moe = lax.dot_general(comb_t.astype(BF16), ys_pack_ref[...].astype(BF16), (((0,), (0,)), ((), ())),
                            preferred_element_type=F32)
    y = x1_ref[...] + ga2 * y_moe
    out = _rms(y, g_ref[...])

    @pl.when(i < T_CTX // TM_FIN)
    def _():
        yp_ref[...] = out

    @pl.when(i >= T_CTX // TM_FIN)
    def _():
        ys_ref[...] = out


def _final(x1, y2, rw, mod3, g_fin):
    nt = T_ALL // TM_FIN
    nctx = T_CTX // TM_FIN
    return pl.pallas_call(
        _final_kernel,
        grid=(nt,),
        in_specs=[pl.BlockSpec((TM_FIN, D_MODEL), lambda i: (i, 0)),
                  pl.BlockSpec((PACK_ROWS, D_MODEL), lambda i: (i, 0)),
                  pl.BlockSpec((8, TM_FIN), lambda i: (0, i)),
                  pl.BlockSpec((None, 1, 6 * D_MODEL), lambda i: (_cond_row(i, TM_FIN), 0, 0)),
                  pl.BlockSpec((1, D_MODEL), lambda i: (0, 0))],
        out_specs=[pl.BlockSpec((TM_FIN, D_MODEL), lambda i: (jnp.minimum(i, nctx - 1), 0)),
                   pl.BlockSpec((TM_FIN, D_MODEL), lambda i: (jnp.maximum(i - nctx, 0), 0))],
        out_shape=[jax.ShapeDtypeStruct((T_CTX, D_MODEL), F32),
                   jax.ShapeDtypeStruct((T_LAT, D_MODEL), F32)],
        compiler_params=_cparams(("arbitrary",)),
        name="final",
    )(x1, y2, rw, mod3, g_fin)


def _np_bf16(a):
    return jnp.asarray(np.asarray(a, np.float32), dtype=BF16)


def _np_f32(a):
    return jnp.asarray(np.asarray(a, np.float32))


@functools.lru_cache(maxsize=None)
def _constants():
    c = {}
    k = np.arange(FN_C)
    ang = 2.0 * np.pi * np.outer(k, k) / FN_C
    c["cs"] = np.concatenate([np.cos(ang), np.sin(ang)], axis=1) / np.sqrt(FN_C)
    p = np.arange(L_CTX)
    ang = 2.0 * np.pi * np.outer(p, p) / L_CTX
    c["p2"] = np.concatenate([np.cos(ang), -np.sin(ang)], axis=1) / np.sqrt(L_CTX)
    pos = np.arange(L_LAT)
    rr, cc = pos // GRID_W, pos % GRID_W
    num = (np.outer(rr, rr) * (GRID_W // GRID_H) + np.outer(cc, cc)) % GRID_W
    ang = 2.0 * np.pi * num / GRID_W
    c["kr"] = np.concatenate([np.cos(ang), -np.sin(ang)], axis=1) / np.sqrt(L_LAT)
    i = np.arange(SB)
    same = (i[:, None] // CHUNK) == (i[None, :] // CHUNK)
    tl = same & (i[:, None] >= i[None, :])
    tu = same & (i[:, None] <= i[None, :])
    c["tlu"] = np.concatenate([tl, tu], axis=1).astype(np.float32)
    c["sut"] = (i[:, None] < i[None, :]).astype(np.float32)
    k = np.arange(LANES)
    c["sl"] = (k[:, None] > k[None, :]).astype(np.float32)
    return c


def kernel(x_prompt, x_sample, state_gla_fwd, state_gla_bwd, c, c_ctx, w_ada, b_ada, norm_attn, norm_ffn, w_in, w_gate_fwd, b_gate_fwd, w_gate_bwd, b_gate_bwd, norm_gla, w_out, w_router_group, b_router_group, w_router_expert, b_router_expert, w_expert_gate, w_expert_up, w_expert_down, norm_final):
    assert w_ada.shape[0] == 1, "single layer"
    cst = _constants()
    cs, p2, kr = _np_f32(cst["cs"]), _np_f32(cst["p2"]), _np_f32(cst["kr"])
    tlu, sut, sl = _np_bf16(cst["tlu"]), _np_bf16(cst["sut"]), _np_bf16(cst["sl"])

    xp = x_prompt.reshape(T_CTX, D_MODEL)
    xs = x_sample.reshape(T_LAT, D_MODEL)

    cond8 = jnp.concatenate([c_ctx[None, :], c, jnp.zeros((3, D_MODEL), F32)], axis=0)
    mod = _ada(cond8, w_ada[0], b_ada[0][None, :])
    mod3 = mod.reshape(8, 1, 6 * D_MODEL)

    wi = w_in[0]
    i_og = 2 * QK_W + 2 * V_W
    i_u = i_og + 2 * RANK
    w_main = jnp.concatenate([wi[:, :i_og], wi[:, i_u:]], axis=1).astype(BF16)
    w_r = jnp.pad(wi[:, i_og:i_u], ((0, 0), (0, LANES - 2 * RANK))).astype(BF16)

    wgf = w_gate_fwd[0].reshape(RANK, HEADS, DK)
    wgb = w_gate_bwd[0].reshape(RANK, HEADS, DK)
    zf = jnp.zeros_like(wgf)
    top = jnp.stack([wgf, zf], axis=2)
    bot = jnp.stack([zf, wgb], axis=2)
    wg = jnp.concatenate([top, bot], axis=0)
    wg = wg.reshape(2 * RANK, HEADS // 2, 4 * DK).transpose(1, 0, 2)
    wg = jnp.pad(wg, ((0, 0), (0, LANES - 2 * RANK), (0, 0)))
    wg_hi = wg.astype(BF16)
    wg_lo = (wg - wg_hi.astype(F32)).astype(BF16)
    bg = jnp.stack([b_gate_fwd[0].reshape(HEADS, DK), b_gate_bwd[0].reshape(HEADS, DK)], axis=1)
    bg = bg.reshape(HEADS // 2, 1, 4 * DK)
    ng = norm_gla[0].reshape(HEADS // 2, 1, 2 * DV)

    proj, r_all = _inproj(xp, xs, mod3, norm_attn, w_main, w_r)

    zero_state = jnp.zeros((N_CTX, HEADS, DK, DV), F32)
    o_ctx, sf_ctx, sb_ctx = _gla(proj, r_all, wg_hi, wg_lo, bg, ng, tlu, zero_state, zero_state,
                                 seq_len=L_CTX, n_seq=N_CTX, n_in_step=4, row0=0)
    o_lat, _, _ = _gla(proj, r_all, wg_hi, wg_lo, bg, ng, tlu,
                       state_gla_fwd[:, 0], state_gla_bwd[:, 0],
                       seq_len=L_LAT, n_seq=N_LAT, n_in_step=1, row0=T_CTX)

    f_ctx = _fnet_ctx(proj, cs, p2)
    f_lat = _fnet_lat(proj, cs, kr)

    wo = w_out[0][:V_W].astype(BF16)
    wf = w_out[0][V_W:].astype(BF16)
    wr = jnp.concatenate([w_router_group[0], w_router_expert[0]], axis=1)
    wr = jnp.pad(wr, ((0, 0), (0, LANES - N_GROUPS - N_EXP)))
    wr_hi = wr.astype(BF16)
    wr_lo = (wr - wr_hi.astype(F32)).astype(BF16)
    br = jnp.pad(jnp.concatenate([b_router_group[0], b_router_expert[0]]), (0, LANES - N_GROUPS - N_EXP))[None, :]

    x1, hs, rw, nb, lb = _outproj(o_ctx, o_lat, f_ctx, f_lat, xp, xs, mod3, norm_ffn, wo, wf,
                                  wr_hi, wr_lo, br, sut, sl)

    nb_e = nb[:, LANE_E0:LANE_E0 + N_EXP, 0].T
    lb_e = lb[:, LANE_E0:LANE_E0 + N_EXP, 0].T
    run_end = jnp.cumsum(nb_e, axis=1)
    blocks_e = run_end[:, -1]
    tiles_e = (blocks_e + BLK_PER_TILE - 1) // BLK_PER_TILE
    tile_end = jnp.cumsum(tiles_e)
    tile_start = tile_end - tiles_e
    n_tiles = tile_end[-1]
    n_code_tiles = NT_MOE + 4
    tile = jnp.arange(n_code_tiles, dtype=I32) - 2
    tile_c = jnp.clip(tile, 0, n_tiles - 1)
    t_exp = jnp.sum(tile_c[:, None] >= tile_end[None, :], axis=1)
    ends = run_end[t_exp]
    starts = ends - nb_e[t_exp]
    offs = lb_e[t_exp] + jnp.arange(N_TOK_TILES, dtype=I32)[None, :] * PACK_BLKS - starts
    j = jnp.arange(BLK_PER_TILE, dtype=I32)
    bi = ((tile_c - tile_start[t_exp]) * BLK_PER_TILE)[:, None] + j[None, :]
    in_run = (starts.T[:, :, None] <= bi[None, :, :]) & (bi[None, :, :] < ends.T[:, :, None])
    blk = bi + jnp.sum(jnp.where(in_run, offs.T[:, :, None], 0), axis=0)
    valid = (tile == tile_c)[:, None] & (bi < blocks_e[t_exp][:, None])
    spare = ((jnp.arange(n_code_tiles, dtype=I32) % 2)[:, None] * BLK_PER_TILE + j[None, :]) * PACK_BLKS + USED_BLKS
    code = jnp.where(valid, (blk << SRC_BITS) | blk, (spare << SRC_BITS) | blk[:, 0:1]).astype(I32).reshape(-1)
    tidx = jnp.minimum(jnp.arange(NT_MOE + 1, dtype=I32), n_tiles - 1)
    texp = jnp.sum(tidx[:, None] >= tile_end[None, :], axis=1).astype(I32)
    nxt = tile_end[texp]
    nexp = jnp.where(nxt < n_tiles, texp[jnp.minimum(nxt, NT_MOE)], -1).astype(I32)
    meta = n_tiles.reshape(1).astype(I32)

    y2 = _moe(texp, nexp, meta, code, hs, w_expert_gate[0], w_expert_up[0], w_expert_down[0])
    y_prompt, y_sample = _final(x1, y2, rw, mod3, norm_final[None, :])

    st_shape = (N_CTX, 1, HEADS, DK, DV)
    return (y_prompt.reshape(N_CTX, L_CTX, D_MODEL), y_sample.reshape(N_LAT, L_LAT, D_MODEL),
            sf_ctx.reshape(st_shape), sb_ctx.reshape(st_shape))
```

```python
import functools

import numpy as np
import jax
import jax.numpy as jnp
from jax import lax
from jax.experimental import pallas as pl
from jax.experimental.pallas import tpu as pltpu

F32 = jnp.float32
BF16 = jnp.bfloat16
I32 = jnp.int32

D_MODEL = 2048
N_CTX = 32
L_CTX = 256
N_LAT = 4
L_LAT = 2048
GRID_H = 32
GRID_W = 64
T_CTX = N_CTX * L_CTX
T_LAT = N_LAT * L_LAT
T_ALL = T_CTX + T_LAT
HEADS = 8
DK = 64
DV = 128
RANK = 16
TAU = 16.0
CHUNK = 64
FN_G = 8
FN_C = 128
QK_W = HEADS * DK
V_W = HEADS * DV
N_GROUPS = 4
EPG = 8
N_EXP = N_GROUPS * EPG
D_EXP = 512
EPS = 1e-6

LANES = 128
VMEM_LIMIT = 56 * 1024 * 1024

TM_MOE = 256


def _dot(a, b):
    return jnp.dot(a, b, preferred_element_type=F32)


def _dot_nt(a, b):
    return lax.dot_general(a, b, (((1,), (1,)), ((), ())), preferred_element_type=F32)


def _split2(x):
    hi = x.astype(BF16)
    lo = (x - hi.astype(F32)).astype(BF16)
    return hi, lo


def _silu(x):
    return x * (1.0 / (1.0 + jnp.exp(-x)))


def _rms(x, g):
    return x * lax.rsqrt(jnp.mean(x * x, axis=-1, keepdims=True) + EPS) * g


def _cparams(sem):
    return pltpu.CompilerParams(dimension_semantics=sem, vmem_limit_bytes=VMEM_LIMIT)


def _ada_kernel(c_ref, w_ref, b_ref, o_ref):
    s_hi, s_lo = _split2(_silu(c_ref[...]))
    w = w_ref[...]
    w_hi = w.astype(BF16)
    w_lo = (w - w_hi.astype(F32)).astype(BF16)
    o_ref[...] = _dot(s_hi, w_hi) + _dot(s_lo, w_hi) + _dot(s_hi, w_lo) + b_ref[...]


def _ada(cond8, w_ada, b_ada):
    tn = 768
    n6 = 6 * D_MODEL
    return pl.pallas_call(
        _ada_kernel,
        grid=(n6 // tn,),
        in_specs=[pl.BlockSpec((8, D_MODEL), lambda j: (0, 0)),
                  pl.BlockSpec((D_MODEL, tn), lambda j: (0, j)),
                  pl.BlockSpec((1, tn), lambda j: (0, j))],
        out_specs=pl.BlockSpec((8, tn), lambda j: (0, j)),
        out_shape=jax.ShapeDtypeStruct((8, n6), F32),
        compiler_params=_cparams(("arbitrary",)),
        name="ada",
    )(cond8, w_ada, b_ada)


TM_IN = 256
TN_IN = 1024
N_MAIN = 4096


def _cond_row(tile, tm):
    ctx_tiles = T_CTX // tm
    per_seq = L_LAT // tm
    return jnp.where(tile < ctx_tiles, 0, 1 + (jnp.maximum(tile - ctx_tiles, 0)) // per_seq)


def _inproj_kernel(xp_ref, xs_ref, mod_ref, g_ref, w_ref, wr_ref, proj_ref, r_ref):
    i = pl.program_id(0)
    x = jnp.where(i < T_CTX // TM_IN, xp_ref[...], xs_ref[...])
    sh1 = mod_ref[:, 0:D_MODEL]
    sc1 = mod_ref[:, D_MODEL:2 * D_MODEL]
    hb = (_rms(x, g_ref[...]) * (1.0 + sc1) + sh1).astype(BF16)
    r_ref[...] = _dot(hb, wr_ref[...])
    for n in range(N_MAIN // TN_IN):
        proj_ref[:, n * TN_IN:(n + 1) * TN_IN] = _dot(hb, w_ref[:, n * TN_IN:(n + 1) * TN_IN]).astype(BF16)


def _inproj(xp, xs, mod3, g_attn, w_main, w_r):
    nt = T_ALL // TM_IN
    nctx = T_CTX // TM_IN
    resident = pl.Buffered(1)
    return pl.pallas_call(
        _inproj_kernel,
        grid=(nt,),
        in_specs=[pl.BlockSpec((TM_IN, D_MODEL), lambda i: (jnp.minimum(i, nctx - 1), 0)),
                  pl.BlockSpec((TM_IN, D_MODEL), lambda i: (jnp.maximum(i - nctx, 0), 0)),
                  pl.BlockSpec((None, 1, 6 * D_MODEL), lambda i: (_cond_row(i, TM_IN), 0, 0)),
                  pl.BlockSpec((1, D_MODEL), lambda i: (0, 0)),
                  pl.BlockSpec((D_MODEL, N_MAIN), lambda i: (0, 0), pipeline_mode=resident),
                  pl.BlockSpec((D_MODEL, LANES), lambda i: (0, 0), pipeline_mode=resident)],
        out_specs=[pl.BlockSpec((TM_IN, N_MAIN), lambda i: (i, 0)),
                   pl.BlockSpec((TM_IN, LANES), lambda i: (i, 0))],
        out_shape=[jax.ShapeDtypeStruct((T_ALL, N_MAIN), BF16),
                   jax.ShapeDtypeStruct((T_ALL, LANES), F32)],
        compiler_params=_cparams(("arbitrary",)),
        name="inproj",
    )(xp, xs, mod3, g_attn, w_main, w_r)


SB = 256
CPB = SB // CHUNK


def _gla_kernel(q_ref, k_ref, v_ref, og_ref, r_ref, wgh_ref, wgl_ref, bg_ref, ng_ref,
                tlu_ref, s0f_ref, s0b_ref,
                o_ref, sf_ref, sb_ref,
                cum_s, last_s, qe_s, ke_s, kd_s, dec_s, kv_s, sbs_s, *, seq_len, n_in_step):
    sb_per_seq = seq_len // SB
    ch_per_seq = seq_len // CHUNK
    n_sb = n_in_step * sb_per_seq
    lane = lax.broadcasted_iota(I32, (1, LANES), 1)
    m_f = lane < DK

    r_hi, r_lo = _split2(r_ref[...])
    z = _dot(r_hi, wgh_ref[...]) + _dot(r_lo, wgh_ref[...]) + _dot(r_hi, wgl_ref[...]) + bg_ref[...]
    g_all = (jnp.minimum(z, 0.0) - jnp.log(1.0 + jnp.exp(-jnp.abs(z)))) * (1.0 / TAU)

    q_pair = q_ref[...].astype(F32)
    k_pair = k_ref[...].astype(F32)
    q_roll = pltpu.roll(q_pair, DK, axis=1)
    k_roll = pltpu.roll(k_pair, DK, axis=1)

    row_b = lax.broadcasted_iota(I32, (SB, SB), 0)
    col_b = lax.broadcasted_iota(I32, (SB, SB), 1)
    same_chunk = (row_b // CHUNK) == (col_b // CHUNK)
    tril = same_chunk & (row_b >= col_b)
    triu = same_chunk & (row_b <= col_b)
    row_chunk = lax.broadcasted_iota(I32, (SB, 1), 0) // CHUNK
    col_chunk = lax.broadcasted_iota(I32, (1, SB), 1) // CHUNK

    m_f2 = (lax.broadcasted_iota(I32, (1, 2 * LANES), 1) % LANES) < DK
    cum_s[...] = g_all

    def cum_body(s, carry):
        rows = pl.ds(pl.multiple_of(s * SB, SB), SB)
        g = cum_s[rows, :]
        f_hi, f_lo = _split2(jnp.where(m_f2, g, 0.0))
        b_hi, b_lo = _split2(jnp.where(m_f2, 0.0, g))
        cum = (_dot(tlu_ref[...], jnp.concatenate([f_hi, b_hi], axis=0))
               + _dot(tlu_ref[...], jnp.concatenate([f_lo, b_lo], axis=0)))
        cum_s[rows, :] = cum
        tots = []
        for c in range(CPB):
            tot = jnp.where(m_f2, cum[(c + 1) * CHUNK - 1:(c + 1) * CHUNK, :], cum[c * CHUNK:c * CHUNK + 1, :])
            dec_s[s * CPB + c] = jnp.exp(tot)
            tots.append(jnp.broadcast_to(tot, (CHUNK, 2 * LANES)))
        last_s[rows, :] = jnp.concatenate(tots, axis=0)
        return carry

    lax.fori_loop(0, n_sb, cum_body, 0)

    for j in range(2):
        cum = cum_s[:, j * LANES:(j + 1) * LANES]
        last = last_s[:, j * LANES:(j + 1) * LANES]
        if j == 0:
            q2 = jnp.where(m_f, q_pair, q_roll)
            k2 = jnp.where(m_f, k_pair, k_roll)
        else:
            q2 = jnp.where(m_f, q_roll, q_pair)
            k2 = jnp.where(m_f, k_roll, k_pair)
        qe_s[j] = ((q2 * (DK ** -0.5)) * jnp.exp(cum)).astype(BF16)
        ke_s[j] = (k2 * jnp.exp(-cum)).astype(BF16)
        kd_s[j] = (k2 * jnp.exp(last - cum)).astype(BF16)

    def kv_body(s, carry):
        rows = pl.ds(pl.multiple_of(s * SB, SB), SB)
        for j in range(2):
            v_t = jnp.transpose(v_ref[rows, j * DV:(j + 1) * DV].astype(F32)).astype(BF16)
            zero = jnp.zeros_like(v_t)
            v_st = jnp.concatenate([jnp.where(col_chunk == c, v_t, zero) for c in range(CPB)], axis=0)
            kv = _dot(v_st, kd_s[j, rows, :])
            for c in range(CPB):
                kv_s[j, s * CPB + c] = kv[c * LANES:(c + 1) * LANES, :]
        return carry

    lax.fori_loop(0, n_sb, kv_body, 0, unroll=min(2, n_sb))

    def dec_row(j, c):
        return dec_s[c][:, j * LANES:(j + 1) * LANES]

    def step_state(j, c, st):
        return st * dec_row(j, c) + kv_s[j, c]

    for q in range(n_in_step):
        st0 = tuple(jnp.transpose(jnp.concatenate([s0f_ref[q, j], s0b_ref[q, j]], axis=0)) for j in range(2))
        c0 = q * ch_per_seq
        s0 = q * sb_per_seq

        def bwd_body(t, sts, c0=c0):
            c = c0 + ch_per_seq - 1 - t
            for j in range(2):
                sbs_s[j, c] = sts[j]
            return tuple(step_state(j, c, sts[j]) for j in range(2))

        st_b_fin = lax.fori_loop(0, ch_per_seq, bwd_body, st0, unroll=CPB)

        def fwd_body(sl, sts, s0=s0):
            s = s0 + sl
            rows = pl.ds(pl.multiple_of(s * SB, SB), SB)
            new = []
            for j in range(2):
                lo = j * DV
                qe = qe_s[j, rows, :]
                ke = ke_s[j, rows, :]
                v_b = v_ref[rows, lo:lo + DV].astype(BF16)
                zero = jnp.zeros_like(qe)
                st = sts[j]
                q_parts, s_parts = [], []
                for c in range(CPB):
                    ci = s * CPB + c
                    s_parts.append(jnp.where(m_f, st, sbs_s[j, ci]).astype(BF16))
                    q_parts.append(jnp.where(row_chunk == c, qe, zero))
                    st = step_state(j, ci, st)
                o = _dot_nt(jnp.concatenate(q_parts, axis=1), jnp.concatenate(s_parts, axis=1))
                q_st = jnp.concatenate([jnp.where(m_f, qe, zero), jnp.where(m_f, zero, qe)], axis=0)
                sc = _dot_nt(q_st, ke)
                p = jnp.where(tril, sc[0:SB, :], 0.0) + jnp.where(triu, sc[SB:2 * SB, :], 0.0)
                o = o + _dot(p.astype(BF16), v_b)
                o = o * lax.rsqrt(jnp.mean(o * o, axis=-1, keepdims=True) + EPS) * ng_ref[:, lo:lo + DV]
                o_ref[rows, lo:lo + DV] = (o * _silu(og_ref[rows, lo:lo + DV].astype(F32))).astype(BF16)
                new.append(st)
            return tuple(new)

        st_f_fin = lax.fori_loop(0, sb_per_seq, fwd_body, st0, unroll=min(2, sb_per_seq))
        for j in range(2):
            sf_ref[q, j] = jnp.transpose(st_f_fin[j])[0:DK, :]
            sb_ref[q, j] = jnp.transpose(st_b_fin[j])[DK:2 * DK, :]


def _gla(proj, r_all, wg_hi, wg_lo, bg, ng, tlu, s0f, s0b, *, seq_len, n_seq, n_in_step, row0):
    kern = functools.partial(_gla_kernel, seq_len=seq_len, n_in_step=n_in_step)
    rows = n_in_step * seq_len
    n_chunks = rows // CHUNK
    rb = lambda b: b + row0 // rows
    st_spec = pl.BlockSpec((n_in_step, 2, DK, DV), lambda b, h: (b, h, 0, 0))
    return pl.pallas_call(
        kern,
        grid=(n_seq // n_in_step, HEADS // 2),
        in_specs=[pl.BlockSpec((rows, LANES), lambda b, h: (rb(b), h)),
                  pl.BlockSpec((rows, LANES), lambda b, h: (rb(b), QK_W // LANES + h)),
                  pl.BlockSpec((rows, 2 * DV), lambda b, h: (rb(b), 2 * QK_W // (2 * DV) + h)),
                  pl.BlockSpec((rows, 2 * DV), lambda b, h: (rb(b), (2 * QK_W + V_W) // (2 * DV) + h)),
                  pl.BlockSpec((rows, LANES), lambda b, h: (rb(b), 0)),
                  pl.BlockSpec((None, LANES, 2 * LANES), lambda b, h: (h, 0, 0)),
                  pl.BlockSpec((None, LANES, 2 * LANES), lambda b, h: (h, 0, 0)),
                  pl.BlockSpec((None, 1, 2 * LANES), lambda b, h: (h, 0, 0)),
                  pl.BlockSpec((None, 1, 2 * DV), lambda b, h: (h, 0, 0)),
                  pl.BlockSpec((SB, 2 * SB), lambda b, h: (0, 0)),
                  st_spec, st_spec],
        out_specs=[pl.BlockSpec((rows, 2 * DV), lambda b, h: (b, h)), st_spec, st_spec],
        out_shape=[jax.ShapeDtypeStruct((n_seq * seq_len, V_W), BF16),
                   jax.ShapeDtypeStruct((n_seq, HEADS, DK, DV), F32),
                   jax.ShapeDtypeStruct((n_seq, HEADS, DK, DV), F32)],
        scratch_shapes=[pltpu.VMEM((rows, 2 * LANES), F32),
                        pltpu.VMEM((rows, 2 * LANES), F32),
                        pltpu.VMEM((2, rows, LANES), BF16),
                        pltpu.VMEM((2, rows, LANES), BF16),
                        pltpu.VMEM((2, rows, LANES), BF16),
                        pltpu.VMEM((n_chunks, 1, 2 * LANES), F32),
                        pltpu.VMEM((2, n_chunks, DV, LANES), F32),
                        pltpu.VMEM((2, n_chunks, DV, LANES), F32)],
        compiler_params=_cparams(("arbitrary", "arbitrary")),
        name="gla_%d" % seq_len,
    )(proj, proj, proj, proj, r_all, wg_hi, wg_lo, bg, ng, tlu, s0f, s0b)


def _fnet_stage_a(u_bf, cs):
    cparts, sparts = [], []
    for g in range(FN_G):
        ab = _dot(u_bf[:, g * FN_C:(g + 1) * FN_C], cs)
        cparts.append(ab[:, 0:FN_C])
        sparts.append(ab[:, FN_C:2 * FN_C])
    return jnp.concatenate(cparts, axis=1), jnp.concatenate(sparts, axis=1)


def _fnet_ctx_kernel(u_ref, cs_ref, p2_ref, f_ref):
    uc, us = _fnet_stage_a(u_ref[...].astype(BF16), cs_ref[...].astype(BF16))
    ab = jnp.concatenate([uc, us], axis=0).astype(BF16)
    f_ref[...] = _dot(p2_ref[...].astype(BF16), ab).astype(BF16)


def _fnet_ctx(proj, cs, p2):
    return pl.pallas_call(
        _fnet_ctx_kernel,
        grid=(N_CTX,),
        in_specs=[pl.BlockSpec((L_CTX, FN_G * FN_C), lambda b: (b, 3)),
                  pl.BlockSpec((FN_C, 2 * FN_C), lambda b: (0, 0)),
                  pl.BlockSpec((L_CTX, 2 * L_CTX), lambda b: (0, 0))],
        out_specs=pl.BlockSpec((L_CTX, FN_G * FN_C), lambda b: (b, 0)),
        out_shape=jax.ShapeDtypeStruct((T_CTX, FN_G * FN_C), BF16),
        compiler_params=_cparams(("arbitrary",)),
        name="fnet_ctx",
    )(proj, cs, p2)


TM_FL = 256
RT_FL = 256


def _fnet_lat_kernel(u_ref, cs_ref, kr_ref, f_ref, ab_s):
    m = pl.program_id(1)

    @pl.when(m == 0)
    def _():
        def body(t, carry):
            rows = pl.ds(pl.multiple_of(t * RT_FL, RT_FL), RT_FL)
            uc, us = _fnet_stage_a(u_ref[rows, :].astype(BF16), cs_ref[...].astype(BF16))
            ab_s[rows, :] = uc.astype(BF16)
            ab_s[pl.ds(pl.multiple_of(L_LAT + t * RT_FL, RT_FL), RT_FL), :] = us.astype(BF16)
            return carry
        lax.fori_loop(0, L_LAT // RT_FL, body, 0)

    f_ref[...] = _dot(kr_ref[...].astype(BF16), ab_s[...]).astype(BF16)


def _fnet_lat(proj, cs, kr):
    nm = L_LAT // TM_FL
    return pl.pallas_call(
        _fnet_lat_kernel,
        grid=(N_LAT, nm),
        in_specs=[pl.BlockSpec((L_LAT, FN_G * FN_C), lambda b, m: (T_CTX // L_LAT + b, 3)),
                  pl.BlockSpec((FN_C, 2 * FN_C), lambda b, m: (0, 0)),
                  pl.BlockSpec((TM_FL, 2 * L_LAT), lambda b, m: (m, 0))],
        out_specs=pl.BlockSpec((TM_FL, FN_G * FN_C), lambda b, m: (b * nm + m, 0)),
        out_shape=jax.ShapeDtypeStruct((T_LAT, FN_G * FN_C), BF16),
        scratch_shapes=[pltpu.VMEM((2 * L_LAT, FN_G * FN_C), BF16)],
        compiler_params=_cparams(("arbitrary", "arbitrary")),
        name="fnet_lat",
    )(proj, cs, kr)


TM_OUT = 256
LANE_E0 = N_GROUPS
ROWS_PER_BLK = 4
PACK_ROWS = -(-(2 * TM_OUT + N_EXP * (ROWS_PER_BLK - 1)) // 256) * 256
PACK_BLKS = PACK_ROWS // ROWS_PER_BLK
N_TOK_TILES = T_ALL // TM_OUT
BLK_PER_TILE = TM_MOE // ROWS_PER_BLK
USED_BLKS = (2 * TM_OUT + N_EXP * (ROWS_PER_BLK - 1)) // ROWS_PER_BLK
HS_ROWS = N_TOK_TILES * PACK_ROWS
assert (PACK_BLKS - USED_BLKS) * N_TOK_TILES >= 2 * BLK_PER_TILE


def _outproj_kernel(oc_ref, ol_ref, fc_ref, fl_ref, xp_ref, xs_ref, mod_ref, g_ref, wo_ref, wf_ref,
                    wrh_ref, wrl_ref, br_ref, sut_ref, sl_ref,
                    x1_ref, hs_ref, rw_ref, nb_ref, lb_ref):
    i = pl.program_id(0)
    is_ctx = i < T_CTX // TM_OUT

    o = jnp.where(is_ctx, oc_ref[...], ol_ref[...]).astype(BF16)
    f = jnp.where(is_ctx, fc_ref[...], fl_ref[...]).astype(BF16)
    x = jnp.where(is_ctx, xp_ref[...], xs_ref[...])
    y = _dot(o, wo_ref[...]) + _dot(f, wf_ref[...])
    ga1 = mod_ref[:, 2 * D_MODEL:3 * D_MODEL]
    sh2 = mod_ref[:, 3 * D_MODEL:4 * D_MODEL]
    sc2 = mod_ref[:, 4 * D_MODEL:5 * D_MODEL]
    x1 = x + ga1 * y
    x1_ref[...] = x1
    h2 = _rms(x1, g_ref[...]) * (1.0 + sc2) + sh2

    h_hi, h_lo = _split2(h2)
    lg_all = jnp.transpose(_dot(h_hi, wrh_ref[...]) + _dot(h_lo, wrh_ref[...]) + _dot(h_hi, wrl_ref[...])
                           + br_ref[...])

    row_i = lax.broadcasted_iota(I32, (LANES, TM_OUT), 0)
    row = row_i.astype(F32)
    neg = jnp.float32(-jnp.inf)
    big = jnp.float32(LANES)
    lg = jnp.where(row_i < N_GROUPS, lg_all, neg)
    gmax = jnp.max(lg, axis=0, keepdims=True)
    gsel = jnp.min(jnp.where(lg == gmax, row, big), axis=0, keepdims=True)
    den = jnp.sum(jnp.exp(lg - gmax), axis=0, keepdims=True)
    pg_sel = 1.0 / den

    e_idx = row_i - LANE_E0
    egrp = (e_idx >> 3).astype(F32)
    emask = (e_idx >= 0) & (e_idx < N_EXP) & (egrp == gsel)
    m1 = jnp.where(emask, lg_all, neg)
    v1 = jnp.max(m1, axis=0, keepdims=True)
    i1 = jnp.min(jnp.where(m1 == v1, row, big), axis=0, keepdims=True)
    m2 = jnp.where(row == i1, neg, m1)
    v2 = jnp.max(m2, axis=0, keepdims=True)
    i2 = jnp.min(jnp.where(m2 == v2, row, big), axis=0, keepdims=True)
    e2 = jnp.exp(v2 - v1)
    inv = 1.0 / (1.0 + e2)
    w1 = inv * pg_sel
    w2 = (e2 * inv) * pg_sel

    oh1 = row == i1
    oh2 = row == i2
    oh = jnp.where(oh1 | oh2, 1.0, 0.0).astype(BF16)
    cnt = _dot(oh, jnp.ones((TM_OUT, LANES), BF16))
    nblk = jnp.floor((cnt + (ROWS_PER_BLK - 1)) * (1.0 / ROWS_PER_BLK))
    lboff = _dot(sl_ref[...], nblk.astype(BF16))
    lrank = _dot(oh, sut_ref[...])
    posmat = jnp.concatenate([lboff, lboff], axis=1) * ROWS_PER_BLK + lrank
    p1 = jnp.sum(jnp.where(oh1, posmat, 0.0), axis=0, keepdims=True)
    p2 = jnp.sum(jnp.where(oh2, posmat, 0.0), axis=0, keepdims=True)
    nb_ref[...] = nblk[:, 0:8].astype(I32)
    lb_ref[...] = lboff[:, 0:8].astype(I32)

    prow = lax.broadcasted_iota(I32, (PACK_ROWS, TM_OUT), 0).astype(F32)
    place = jnp.where((prow == p1) | (prow == p2), 1.0, 0.0).astype(BF16)
    hs_ref[...] = _dot(place, h_hi).astype(BF16)

    rw_ref[...] = jnp.concatenate([w1, w2, p1, p2, jnp.zeros((4, TM_OUT), F32)], axis=0)


def _outproj(o_ctx, o_lat, f_ctx, f_lat, xp, xs, mod3, g_ffn, wo, wf, wr_hi, wr_lo, br, sut, sl):
    nt = T_ALL // TM_OUT
    nctx = T_CTX // TM_OUT
    ctx_map = lambda i: (jnp.minimum(i, nctx - 1), 0)
    lat_map = lambda i: (jnp.maximum(i - nctx, 0), 0)
    const = lambda i: (0, 0)
    return pl.pallas_call(
        _outproj_kernel,
        grid=(nt,),
        in_specs=[pl.BlockSpec((TM_OUT, V_W), ctx_map),
                  pl.BlockSpec((TM_OUT, V_W), lat_map),
                  pl.BlockSpec((TM_OUT, FN_G * FN_C), ctx_map),
                  pl.BlockSpec((TM_OUT, FN_G * FN_C), lat_map),
                  pl.BlockSpec((TM_OUT, D_MODEL), ctx_map),
                  pl.BlockSpec((TM_OUT, D_MODEL), lat_map),
                  pl.BlockSpec((None, 1, 6 * D_MODEL), lambda i: (_cond_row(i, TM_OUT), 0, 0)),
                  pl.BlockSpec((1, D_MODEL), const),
                  pl.BlockSpec((V_W, D_MODEL), const),
                  pl.BlockSpec((FN_G * FN_C, D_MODEL), const),
                  pl.BlockSpec((D_MODEL, LANES), const),
                  pl.BlockSpec((D_MODEL, LANES), const),
                  pl.BlockSpec((1, LANES), const),
                  pl.BlockSpec((TM_OUT, TM_OUT), const),
                  pl.BlockSpec((LANES, LANES), const)],
        out_specs=[pl.BlockSpec((TM_OUT, D_MODEL), lambda i: (i, 0)),
                   pl.BlockSpec((PACK_ROWS, D_MODEL), lambda i: (i, 0)),
                   pl.BlockSpec((8, TM_OUT), lambda i: (0, i)),
                   pl.BlockSpec((None, LANES, 8), lambda i: (i, 0, 0)),
                   pl.BlockSpec((None, LANES, 8), lambda i: (i, 0, 0))],
        out_shape=[jax.ShapeDtypeStruct((T_ALL, D_MODEL), F32),
                   jax.ShapeDtypeStruct((HS_ROWS, D_MODEL), BF16),
                   jax.ShapeDtypeStruct((8, T_ALL), F32),
                   jax.ShapeDtypeStruct((nt, LANES, 8), I32),
                   jax.ShapeDtypeStruct((nt, LANES, 8), I32)],
        compiler_params=_cparams(("arbitrary",)),
        name="outproj",
    )(o_ctx, o_lat, f_ctx, f_lat, xp, xs, mod3, g_ffn, wo, wf, wr_hi, wr_lo, br, sut, sl)


SRC_BITS = 16
SRC_MASK = (1 << SRC_BITS) - 1
X_SLOTS = 3
Y_SLOTS = 3
N_UP_CHUNKS = 2
N_DN_CHUNKS = 8
NT_MOE = (2 * T_ALL + N_TOK_TILES * N_EXP * (ROWS_PER_BLK - 1)) // TM_MOE + N_EXP


def _moe_kernel(texp_ref, nexp_ref, meta_ref, code_ref,
                h_hbm, wg_hbm, wu_hbm, wd_hbm,
                out_hbm,
                xbuf, ybuf, wg_f, wu_f, wd_f, wg_s, wu_s, wd_s, kcount, gsem, ssem, wsem):
    i = pl.program_id(0)
    nt = meta_ref[0]
    xs = i % X_SLOTS

    def blk_rows(b):
        if isinstance(b, int):
            return pl.ds(b * ROWS_PER_BLK, ROWS_PER_BLK)
        return pl.ds(pl.multiple_of(b * ROWS_PER_BLK, ROWS_PER_BLK), ROWS_PER_BLK)

    def gather_row(tile, sl, j):
        src = code_ref[(tile + 2) * BLK_PER_TILE + j] & SRC_MASK
        pltpu.make_async_copy(h_hbm.at[blk_rows(src)], xbuf.at[sl, blk_rows(j)], gsem.at[sl]).start()

    def scatter_row(tile, sl, j):
        dst = code_ref[(tile + 2) * BLK_PER_TILE + j] >> SRC_BITS
        pltpu.make_async_copy(ybuf.at[sl, blk_rows(j)], out_hbm.at[blk_rows(dst)], ssem.at[sl]).start(priority=1)

    def gather_wait(sl):
        pltpu.make_async_copy(h_hbm.at[pl.ds(0, TM_MOE)], xbuf.at[sl], gsem.at[sl]).wait()

    def scatter_wait(sl):
        pltpu.make_async_copy(ybuf.at[sl], out_hbm.at[pl.ds(0, TM_MOE)], ssem.at[sl]).wait()

    def y_slot(tile):
        return (tile + 1) % Y_SLOTS

    @pl.when(i == 0)
    def _():
        ybuf[y_slot(-2)] = jnp.zeros((TM_MOE, D_MODEL), BF16)
        ybuf[y_slot(-1)] = jnp.zeros((TM_MOE, D_MODEL), BF16)

        def body(j, c):
            gather_row(0, 0, j)
            gather_row(1, 1, j)
            scatter_row(-2, y_slot(-2), j)
            return c
        lax.fori_loop(0, BLK_PER_TILE, body, 0)

    @pl.when((i >= 1) & (i <= nt))
    def _():
        scatter_wait(y_slot(i - 3))

    @pl.when(i < nt)
    def _():
        prev = texp_ref[jnp.maximum(i - 1, 0)]

        def weight_copies(e, sl):
            return (pltpu.make_async_copy(wg_hbm.at[e], wg_f.at[sl], wsem.at[sl]),
                    pltpu.make_async_copy(wu_hbm.at[e], wu_f.at[sl], wsem.at[sl]),
                    pltpu.make_async_copy(wd_hbm.at[e], wd_f.at[sl], wsem.at[sl]))

        @pl.when(i == 0)
        def _():
            kcount[0] = 0
            for cp in weight_copies(texp_ref[0], 0):
                cp.start()

        @pl.when((i == 0) | (texp_ref[i] != prev))
        def _():
            k = kcount[0]
            for sl in range(2):
                @pl.when(k % 2 == sl)
                def _(sl=sl):
                    for cp in weight_copies(texp_ref[i], sl):
                        cp.wait()
                    def narrow(r, c):
                        up = pl.ds(pl.multiple_of(r * (D_MODEL // 16), D_MODEL // 16), D_MODEL // 16)
                        dn = pl.ds(pl.multiple_of(r * (D_EXP // 16), D_EXP // 16), D_EXP // 16)
                        wg_s[up, :] = wg_f[sl, up, :].astype(BF16)
                        wu_s[up, :] = wu_f[sl, up, :].astype(BF16)
                        wd_s[dn, :] = wd_f[sl, dn, :].astype(BF16)
                        return c
                    lax.fori_loop(0, 16, narrow, 0)

                    @pl.when(nexp_ref[i] >= 0)
                    def _():
                        for cp in weight_copies(nexp_ref[i], 1 - sl):
                            cp.start()
            kcount[0] = k + 1

        gather_wait(xs)
        x = xbuf[xs].astype(BF16)

        issues = []
        for j in range(BLK_PER_TILE):
            issues.append(functools.partial(gather_row, i + 2, (i + 2) % X_SLOTS, j))
            issues.append(functools.partial(scatter_row, i - 1, y_slot(i - 1), j))
        n_groups = N_UP_CHUNKS + N_DN_CHUNKS
        per_group = -(-len(issues) // n_groups)

        def issue_group(k):
            for fn in issues[k * per_group:(k + 1) * per_group]:
                fn()

        wu_c = D_EXP // N_UP_CHUNKS
        hid = []
        for n in range(N_UP_CHUNKS):
            issue_group(n)
            g = _dot(x, wg_s[:, n * wu_c:(n + 1) * wu_c])
            u = _dot(x, wu_s[:, n * wu_c:(n + 1) * wu_c])
            hid.append((_silu(g) * u).astype(BF16))
        hid = jnp.concatenate(hid, axis=1)
        wd_c = D_MODEL // N_DN_CHUNKS
        ys = y_slot(i)
        for n in range(N_DN_CHUNKS):
            issue_group(N_UP_CHUNKS + n)
            ybuf[ys, :, n * wd_c:(n + 1) * wd_c] = _dot(hid, wd_s[:, n * wd_c:(n + 1) * wd_c]).astype(BF16)

    @pl.when(i == nt)
    def _():
        gather_wait(xs)
        gather_wait((i + 1) % X_SLOTS)

        def body(j, c):
            scatter_row(nt - 1, y_slot(nt - 1), j)
            return c
        lax.fori_loop(0, BLK_PER_TILE, body, 0)
        scatter_wait(y_slot(nt - 2))
        scatter_wait(y_slot(nt - 1))


def _moe(texp, nexp, meta, code, hs, w_eg, w_eu, w_ed):
    hbm = pl.BlockSpec(memory_space=pl.ANY)
    grid_spec = pltpu.PrefetchScalarGridSpec(
        num_scalar_prefetch=4,
        grid=(NT_MOE + 1,),
        in_specs=[hbm, hbm, hbm, hbm],
        out_specs=hbm,
        scratch_shapes=[pltpu.VMEM((X_SLOTS, TM_MOE, D_MODEL), BF16),
                        pltpu.VMEM((Y_SLOTS, TM_MOE, D_MODEL), BF16),
                        pltpu.VMEM((2, D_MODEL, D_EXP), F32),
                        pltpu.VMEM((2, D_MODEL, D_EXP), F32),
                        pltpu.VMEM((2, D_EXP, D_MODEL), F32),
                        pltpu.VMEM((D_MODEL, D_EXP), BF16),
                        pltpu.VMEM((D_MODEL, D_EXP), BF16),
                        pltpu.VMEM((D_EXP, D_MODEL), BF16),
                        pltpu.SMEM((1,), I32),
                        pltpu.SemaphoreType.DMA((X_SLOTS,)),
                        pltpu.SemaphoreType.DMA((Y_SLOTS,)),
                        pltpu.SemaphoreType.DMA((2,))])
    return pl.pallas_call(
        _moe_kernel,
        grid_spec=grid_spec,
        out_shape=jax.ShapeDtypeStruct((HS_ROWS, D_MODEL), BF16),
        input_output_aliases={4: 0},
        compiler_params=_cparams(("arbitrary",)),
        name="moe",
    )(texp, nexp, meta, code, hs, w_eg, w_eu, w_ed)


TM_FIN = TM_OUT


def _final_kernel(x1_ref, ys_pack_ref, rw_ref, mod_ref, g_ref, yp_ref, ys_ref):
    i = pl.program_id(0)
    ga2 = mod_ref[:, 5 * D_MODEL:6 * D_MODEL]
    w0 = rw_ref[0:1, :]
    w1 = rw_ref[1:2, :]
    p0 = rw_ref[2:3, :]
    p1 = rw_ref[3:4, :]
    prow = lax.broadcasted_iota(I32, (PACK_ROWS, TM_FIN), 0).astype(F32)
    comb_t = jnp.where(prow == p0, w0, 0.0) + jnp.where(prow == p1, w1, 0.0)
    y_moe = lax.dot_general(comb_t.astype(BF16), ys_pack_ref[...].astype(BF16), (((0,), (0,)), ((), ())),
                            preferred_element_type=F32)
    y = x1_ref[...] + ga2 * y_moe
    out = _rms(y, g_ref[...])

    @pl.when(i < T_CTX // TM_FIN)
    def _():
        yp_ref[...] = out

    @pl.when(i >= T_CTX // TM_FIN)
    def _():
        ys_ref[...] = out


def _final(x1, y2, rw, mod3, g_fin):
    nt = T_ALL // TM_FIN
    nctx = T_CTX // TM_FIN
    return pl.pallas_call(
        _final_kernel,
        grid=(nt,),
        in_specs=[pl.BlockSpec((TM_FIN, D_MODEL), lambda i: (i, 0)),
                  pl.BlockSpec((PACK_ROWS, D_MODEL), lambda i: (i, 0)),
                  pl.BlockSpec((8, TM_FIN), lambda i: (0, i)),
                  pl.BlockSpec((None, 1, 6 * D_MODEL), lambda i: (_cond_row(i, TM_FIN), 0, 0)),
                  pl.BlockSpec((1, D_MODEL), lambda i: (0, 0))],
        out_specs=[pl.BlockSpec((TM_FIN, D_MODEL), lambda i: (jnp.minimum(i, nctx - 1), 0)),
                   pl.BlockSpec((TM_FIN, D_MODEL), lambda i: (jnp.maximum(i - nctx, 0), 0))],
        out_shape=[jax.ShapeDtypeStruct((T_CTX, D_MODEL), F32),
                   jax.ShapeDtypeStruct((T_LAT, D_MODEL), F32)],
        compiler_params=_cparams(("arbitrary",)),
        name="final",
    )(x1, y2, rw, mod3, g_fin)


def _np_bf16(a):
    return jnp.asarray(np.asarray(a, np.float32), dtype=BF16)


def _np_f32(a):
    return jnp.asarray(np.asarray(a, np.float32))


@functools.lru_cache(maxsize=None)
def _constants():
    c = {}
    k = np.arange(FN_C)
    ang = 2.0 * np.pi * np.outer(k, k) / FN_C
    c["cs"] = np.concatenate([np.cos(ang), np.sin(ang)], axis=1) / np.sqrt(FN_C)
    p = np.arange(L_CTX)
    ang = 2.0 * np.pi * np.outer(p, p) / L_CTX
    c["p2"] = np.concatenate([np.cos(ang), -np.sin(ang)], axis=1) / np.sqrt(L_CTX)
    pos = np.arange(L_LAT)
    rr, cc = pos // GRID_W, pos % GRID_W
    num = (np.outer(rr, rr) * (GRID_W // GRID_H) + np.outer(cc, cc)) % GRID_W
    ang = 2.0 * np.pi * num / GRID_W
    c["kr"] = np.concatenate([np.cos(ang), -np.sin(ang)], axis=1) / np.sqrt(L_LAT)
    i = np.arange(SB)
    same = (i[:, None] // CHUNK) == (i[None, :] // CHUNK)
    tl = same & (i[:, None] >= i[None, :])
    tu = same & (i[:, None] <= i[None, :])
    c["tlu"] = np.concatenate([tl, tu], axis=1).astype(np.float32)
    c["sut"] = (i[:, None] < i[None, :]).astype(np.float32)
    k = np.arange(LANES)
    c["sl"] = (k[:, None] > k[None, :]).astype(np.float32)
    return c


def kernel(x_prompt, x_sample, state_gla_fwd, state_gla_bwd, c, c_ctx, w_ada, b_ada, norm_attn, norm_ffn, w_in, w_gate_fwd, b_gate_fwd, w_gate_bwd, b_gate_bwd, norm_gla, w_out, w_router_group, b_router_group, w_router_expert, b_router_expert, w_expert_gate, w_expert_up, w_expert_down, norm_final):
    assert w_ada.shape[0] == 1, "single layer"
    cst = _constants()
    cs, p2, kr = _np_f32(cst["cs"]), _np_f32(cst["p2"]), _np_f32(cst["kr"])
    tlu, sut, sl = _np_bf16(cst["tlu"]), _np_bf16(cst["sut"]), _np_bf16(cst["sl"])

    xp = x_prompt.reshape(T_CTX, D_MODEL)
    xs = x_sample.reshape(T_LAT, D_MODEL)

    cond8 = jnp.concatenate([c_ctx[None, :], c, jnp.zeros((3, D_MODEL), F32)], axis=0)
    mod = _ada(cond8, w_ada[0], b_ada[0][None, :])
    mod3 = mod.reshape(8, 1, 6 * D_MODEL)

    wi = w_in[0]
    i_og = 2 * QK_W + 2 * V_W
    i_u = i_og + 2 * RANK
    w_main = jnp.concatenate([wi[:, :i_og], wi[:, i_u:]], axis=1).astype(BF16)
    w_r = jnp.pad(wi[:, i_og:i_u], ((0, 0), (0, LANES - 2 * RANK))).astype(BF16)

    wgf = w_gate_fwd[0].reshape(RANK, HEADS, DK)
    wgb = w_gate_bwd[0].reshape(RANK, HEADS, DK)
    zf = jnp.zeros_like(wgf)
    top = jnp.stack([wgf, zf], axis=2)
    bot = jnp.stack([zf, wgb], axis=2)
    wg = jnp.concatenate([top, bot], axis=0)
    wg = wg.reshape(2 * RANK, HEADS // 2, 4 * DK).transpose(1, 0, 2)
    wg = jnp.pad(wg, ((0, 0), (0, LANES - 2 * RANK), (0, 0)))
    wg_hi = wg.astype(BF16)
    wg_lo = (wg - wg_hi.astype(F32)).astype(BF16)
    bg = jnp.stack([b_gate_fwd[0].reshape(HEADS, DK), b_gate_bwd[0].reshape(HEADS, DK)], axis=1)
    bg = bg.reshape(HEADS // 2, 1, 4 * DK)
    ng = norm_gla[0].reshape(HEADS // 2, 1, 2 * DV)

    proj, r_all = _inproj(xp, xs, mod3, norm_attn, w_main, w_r)

    zero_state = jnp.zeros((N_CTX, HEADS, DK, DV), F32)
    o_ctx, sf_ctx, sb_ctx = _gla(proj, r_all, wg_hi, wg_lo, bg, ng, tlu, zero_state, zero_state,
                                 seq_len=L_CTX, n_seq=N_CTX, n_in_step=4, row0=0)
    o_lat, _, _ = _gla(proj, r_all, wg_hi, wg_lo, bg, ng, tlu,
                       state_gla_fwd[:, 0], state_gla_bwd[:, 0],
                       seq_len=L_LAT, n_seq=N_LAT, n_in_step=1, row0=T_CTX)

    f_ctx = _fnet_ctx(proj, cs, p2)
    f_lat = _fnet_lat(proj, cs, kr)

    wo = w_out[0][:V_W].astype(BF16)
    wf = w_out[0][V_W:].astype(BF16)
    wr = jnp.concatenate([w_router_group[0], w_router_expert[0]], axis=1)
    wr = jnp.pad(wr, ((0, 0), (0, LANES - N_GROUPS - N_EXP)))
    wr_hi = wr.astype(BF16)
    wr_lo = (wr - wr_hi.astype(F32)).astype(BF16)
    br = jnp.pad(jnp.concatenate([b_router_group[0], b_router_expert[0]]), (0, LANES - N_GROUPS - N_EXP))[None, :]

    x1, hs, rw, nb, lb = _outproj(o_ctx, o_lat, f_ctx, f_lat, xp, xs, mod3, norm_ffn, wo, wf,
                                  wr_hi, wr_lo, br, sut, sl)

    nb_e = nb[:, LANE_E0:LANE_E0 + N_EXP, 0].T
    lb_e = lb[:, LANE_E0:LANE_E0 + N_EXP, 0].T
    run_end = jnp.cumsum(nb_e, axis=1)
    blocks_e = run_end[:, -1]
    tiles_e = (blocks_e + BLK_PER_TILE - 1) // BLK_PER_TILE
    tile_end = jnp.cumsum(tiles_e)
    tile_start = tile_end - tiles_e
    n_tiles = tile_end[-1]
    n_code_tiles = NT_MOE + 4
    tile = jnp.arange(n_code_tiles, dtype=I32) - 2
    tile_c = jnp.clip(tile, 0, n_tiles - 1)
    t_exp = jnp.sum(tile_c[:, None] >= tile_end[None, :], axis=1)
    ends = run_end[t_exp]
    starts = ends - nb_e[t_exp]
    offs = lb_e[t_exp] + jnp.arange(N_TOK_TILES, dtype=I32)[None, :] * PACK_BLKS - starts
    j = jnp.arange(BLK_PER_TILE, dtype=I32)
    bi = ((tile_c - tile_start[t_exp]) * BLK_PER_TILE)[:, None] + j[None, :]
    in_run = (starts.T[:, :, None] <= bi[None, :, :]) & (bi[None, :, :] < ends.T[:, :, None])
    blk = bi + jnp.sum(jnp.where(in_run, offs.T[:, :, None], 0), axis=0)
    valid = (tile == tile_c)[:, None] & (bi < blocks_e[t_exp][:, None])
    spare_ix = (jnp.arange(n_code_tiles, dtype=I32) % 2)[:, None] * BLK_PER_TILE + j[None, :]
    spare = (spare_ix % N_TOK_TILES) * PACK_BLKS + USED_BLKS + spare_ix // N_TOK_TILES
    code = jnp.where(valid, (blk << SRC_BITS) | blk, (spare << SRC_BITS) | blk[:, 0:1]).astype(I32).reshape(-1)
    tidx = jnp.minimum(jnp.arange(NT_MOE + 1, dtype=I32), n_tiles - 1)
    texp = jnp.sum(tidx[:, None] >= tile_end[None, :], axis=1).astype(I32)
    nxt = tile_end[texp]
    nexp = jnp.where(nxt < n_tiles, texp[jnp.minimum(nxt, NT_MOE)], -1).astype(I32)
    meta = n_tiles.reshape(1).astype(I32)

    y2 = _moe(texp, nexp, meta, code, hs, w_expert_gate[0], w_expert_up[0], w_expert_down[0])
    y_prompt, y_sample = _final(x1, y2, rw, mod3, norm_final[None, :])

    st_shape = (N_CTX, 1, HEADS, DK, DV)
    return (y_prompt.reshape(N_CTX, L_CTX, D_MODEL), y_sample.reshape(N_LAT, L_LAT, D_MODEL),
            sf_ctx.reshape(st_shape), sb_ctx.reshape(st_shape))
```

```python
import functools

import numpy as np
import jax
import jax.numpy as jnp
from jax import lax
from jax.experimental import pallas as pl
from jax.experimental.pallas import tpu as pltpu

F32 = jnp.float32
BF16 = jnp.bfloat16
I32 = jnp.int32

D_MODEL = 2048
N_CTX = 32
L_CTX = 256
N_LAT = 4
L_LAT = 2048
GRID_H = 32
GRID_W = 64
T_CTX = N_CTX * L_CTX
T_LAT = N_LAT * L_LAT
T_ALL = T_CTX + T_LAT
HEADS = 8
DK = 64
DV = 128
RANK = 16
TAU = 16.0
CHUNK = 64
FN_G = 8
FN_C = 128
QK_W = HEADS * DK
V_W = HEADS * DV
N_GROUPS = 4
EPG = 8
N_EXP = N_GROUPS * EPG
D_EXP = 512
EPS = 1e-6

LANES = 128
VMEM_LIMIT = 56 * 1024 * 1024

TM_MOE = 256


def _dot(a, b):
    return jnp.dot(a, b, preferred_element_type=F32)


def _dot_nt(a, b):
    return lax.dot_general(a, b, (((1,), (1,)), ((), ())), preferred_element_type=F32)


def _split2(x):
    hi = x.astype(BF16)
    lo = (x - hi.astype(F32)).astype(BF16)
    return hi, lo


def _silu(x):
    return x * (1.0 / (1.0 + jnp.exp(-x)))


def _rms(x, g):
    return x * lax.rsqrt(jnp.mean(x * x, axis=-1, keepdims=True) + EPS) * g


def _cparams(sem):
    return pltpu.CompilerParams(dimension_semantics=sem, vmem_limit_bytes=VMEM_LIMIT)


def _ada_kernel(c_ref, w_ref, b_ref, o_ref):
    s_hi, s_lo = _split2(_silu(c_ref[...]))
    w = w_ref[...]
    w_hi = w.astype(BF16)
    w_lo = (w - w_hi.astype(F32)).astype(BF16)
    o_ref[...] = _dot(s_hi, w_hi) + _dot(s_lo, w_hi) + _dot(s_hi, w_lo) + b_ref[...]


def _ada(cond8, w_ada, b_ada):
    tn = 1024
    n6 = 6 * D_MODEL
    return pl.pallas_call(
        _ada_kernel,
        grid=(n6 // tn,),
        in_specs=[pl.BlockSpec((8, D_MODEL), lambda j: (0, 0)),
                  pl.BlockSpec((D_MODEL, tn), lambda j: (0, j)),
                  pl.BlockSpec((1, tn), lambda j: (0, j))],
        out_specs=pl.BlockSpec((8, tn), lambda j: (0, j)),
        out_shape=jax.ShapeDtypeStruct((8, n6), F32),
        compiler_params=_cparams(("arbitrary",)),
        name="ada",
    )(cond8, w_ada, b_ada)


TM_IN = 512
TM_IN_HALF = 256
TN_IN = 1024
N_MAIN = 4096
N_QKVG = 3072


def _cond_row(tile, tm):
    ctx_tiles = T_CTX // tm
    per_seq = L_LAT // tm
    return jnp.where(tile < ctx_tiles, 0, 1 + (jnp.maximum(tile - ctx_tiles, 0)) // per_seq)


def _inproj_kernel(xp_ref, xs_ref, mod_ref, g_ref, wa_ref, wu_ref, wr_ref, proj_ref, r_ref):
    i = pl.program_id(0)
    is_ctx = i < T_CTX // TM_IN
    sh1 = mod_ref[:, 0:D_MODEL]
    sc1 = mod_ref[:, D_MODEL:2 * D_MODEL]
    for hf in range(TM_IN // TM_IN_HALF):
        rows = slice(hf * TM_IN_HALF, (hf + 1) * TM_IN_HALF)
        x = jnp.where(is_ctx, xp_ref[rows, :], xs_ref[rows, :])
        hb = (_rms(x, g_ref[...]) * (1.0 + sc1) + sh1).astype(BF16)
        r_ref[rows, :] = _dot(hb, wr_ref[...])
        for n in range(N_QKVG // TN_IN):
            cols = slice(n * TN_IN, (n + 1) * TN_IN)
            proj_ref[rows, cols] = _dot(hb, wa_ref[:, cols]).astype(BF16)
        for n in range((N_MAIN - N_QKVG) // TN_IN):
            cols = slice(N_QKVG + n * TN_IN, N_QKVG + (n + 1) * TN_IN)
            proj_ref[rows, cols] = _dot(hb, wu_ref[:, n * TN_IN:(n + 1) * TN_IN]).astype(BF16)


def _inproj(xp, xs, mod3, g_attn, w_a, w_u, w_r):
    nt = T_ALL // TM_IN
    nctx = T_CTX // TM_IN
    resident = pl.Buffered(1)
    return pl.pallas_call(
        _inproj_kernel,
        grid=(nt,),
        in_specs=[pl.BlockSpec((TM_IN, D_MODEL), lambda i: (jnp.minimum(i, nctx - 1), 0)),
                  pl.BlockSpec((TM_IN, D_MODEL), lambda i: (jnp.maximum(i - nctx, 0), 0)),
                  pl.BlockSpec((None, 1, 6 * D_MODEL), lambda i: (_cond_row(i, TM_IN), 0, 0)),
                  pl.BlockSpec((1, D_MODEL), lambda i: (0, 0)),
                  pl.BlockSpec((D_MODEL, N_QKVG), lambda i: (0, 0), pipeline_mode=resident),
                  pl.BlockSpec((D_MODEL, N_MAIN - N_QKVG), lambda i: (0, 0), pipeline_mode=resident),
                  pl.BlockSpec((D_MODEL, LANES), lambda i: (0, 0), pipeline_mode=resident)],
        out_specs=[pl.BlockSpec((TM_IN, N_MAIN), lambda i: (i, 0)),
                   pl.BlockSpec((TM_IN, LANES), lambda i: (i, 0))],
        out_shape=[jax.ShapeDtypeStruct((T_ALL, N_MAIN), BF16),
                   jax.ShapeDtypeStruct((T_ALL, LANES), F32)],
        compiler_params=_cparams(("arbitrary",)),
        name="inproj",
    )(xp, xs, mod3, g_attn, w_a, w_u, w_r)


SB = 256
CPB = SB // CHUNK


def _gla_kernel(q_ref, k_ref, v_ref, og_ref, r_ref, wgh_ref, wgl_ref, bg_ref, ng_ref,
                tlu_ref, s0f_ref, s0b_ref,
                o_ref, sf_ref, sb_ref,
                cum_s, last_s, qe_s, ke_s, kd_s, dec_s, kv_s, sbs_s, *, seq_len, n_in_step):
    sb_per_seq = seq_len // SB
    ch_per_seq = seq_len // CHUNK
    n_sb = n_in_step * sb_per_seq
    lane = lax.broadcasted_iota(I32, (1, LANES), 1)
    m_f = lane < DK

    r_hi, r_lo = _split2(r_ref[...])
    z = _dot(r_hi, wgh_ref[...]) + _dot(r_lo, wgh_ref[...]) + _dot(r_hi, wgl_ref[...]) + bg_ref[...]
    g_all = (jnp.minimum(z, 0.0) - jnp.log(1.0 + jnp.exp(-jnp.abs(z)))) * (1.0 / TAU)

    q_pair = q_ref[...].astype(F32)
    k_pair = k_ref[...].astype(F32)
    q_roll = pltpu.roll(q_pair, DK, axis=1)
    k_roll = pltpu.roll(k_pair, DK, axis=1)

    row_b = lax.broadcasted_iota(I32, (SB, SB), 0)
    col_b = lax.broadcasted_iota(I32, (SB, SB), 1)
    same_chunk = (row_b // CHUNK) == (col_b // CHUNK)
    tril = same_chunk & (row_b >= col_b)
    triu = same_chunk & (row_b <= col_b)
    row_chunk = lax.broadcasted_iota(I32, (SB, 1), 0) // CHUNK
    col_chunk = lax.broadcasted_iota(I32, (1, SB), 1) // CHUNK

    m_f2 = (lax.broadcasted_iota(I32, (1, 2 * LANES), 1) % LANES) < DK
    cum_s[...] = g_all

    def cum_body(s, carry):
        rows = pl.ds(pl.multiple_of(s * SB, SB), SB)
        g = cum_s[rows, :]
        f_hi, f_lo = _split2(jnp.where(m_f2, g, 0.0))
        b_hi, b_lo = _split2(jnp.where(m_f2, 0.0, g))
        cum = (_dot(tlu_ref[...], jnp.concatenate([f_hi, b_hi], axis=0))
               + _dot(tlu_ref[...], jnp.concatenate([f_lo, b_lo], axis=0)))
        cum_s[rows, :] = cum
        tots = []
        for c in range(CPB):
            tot = jnp.where(m_f2, cum[(c + 1) * CHUNK - 1:(c + 1) * CHUNK, :], cum[c * CHUNK:c * CHUNK + 1, :])
            dec_s[s * CPB + c] = jnp.exp(tot)
            tots.append(jnp.broadcast_to(tot, (CHUNK, 2 * LANES)))
        last_s[rows, :] = jnp.concatenate(tots, axis=0)
        return carry

    lax.fori_loop(0, n_sb, cum_body, 0)

    for j in range(2):
        cum = cum_s[:, j * LANES:(j + 1) * LANES]
        last = last_s[:, j * LANES:(j + 1) * LANES]
        if j == 0:
            q2 = jnp.where(m_f, q_pair, q_roll)
            k2 = jnp.where(m_f, k_pair, k_roll)
        else:
            q2 = jnp.where(m_f, q_roll, q_pair)
            k2 = jnp.where(m_f, k_roll, k_pair)
        qe_s[j] = ((q2 * (DK ** -0.5)) * jnp.exp(cum)).astype(BF16)
        ke_s[j] = (k2 * jnp.exp(-cum)).astype(BF16)
        kd_s[j] = (k2 * jnp.exp(last - cum)).astype(BF16)

    def kv_body(s, carry):
        rows = pl.ds(pl.multiple_of(s * SB, SB), SB)
        for j in range(2):
            v_t = jnp.transpose(v_ref[rows, j * DV:(j + 1) * DV].astype(F32)).astype(BF16)
            zero = jnp.zeros_like(v_t)
            v_st = jnp.concatenate([jnp.where(col_chunk == c, v_t, zero) for c in range(CPB)], axis=0)
            kv = _dot(v_st, kd_s[j, rows, :])
            for c in range(CPB):
                kv_s[j, s * CPB + c] = kv[c * LANES:(c + 1) * LANES, :]
        return carry

    lax.fori_loop(0, n_sb, kv_body, 0, unroll=min(2, n_sb))

    def dec_row(j, c):
        return dec_s[c][:, j * LANES:(j + 1) * LANES]

    def step_state(j, c, st):
        return st * dec_row(j, c) + kv_s[j, c]

    for q in range(n_in_step):
        st0 = tuple(jnp.transpose(jnp.concatenate([s0f_ref[q, j], s0b_ref[q, j]], axis=0)) for j in range(2))
        c0 = q * ch_per_seq
        s0 = q * sb_per_seq

        def bwd_body(t, sts, c0=c0):
            c = c0 + ch_per_seq - 1 - t
            for j in range(2):
                sbs_s[j, c] = sts[j]
            return tuple(step_state(j, c, sts[j]) for j in range(2))

        st_b_fin = lax.fori_loop(0, ch_per_seq, bwd_body, st0, unroll=CPB)

        def fwd_body(sl, sts, s0=s0):
            s = s0 + sl
            rows = pl.ds(pl.multiple_of(s * SB, SB), SB)
            new = []
            for j in range(2):
                lo = j * DV
                qe = qe_s[j, rows, :]
                ke = ke_s[j, rows, :]
                v_b = v_ref[rows, lo:lo + DV].astype(BF16)
                zero = jnp.zeros_like(qe)
                st = sts[j]
                q_parts, s_parts = [], []
                for c in range(CPB):
                    ci = s * CPB + c
                    s_parts.append(jnp.where(m_f, st, sbs_s[j, ci]).astype(BF16))
                    q_parts.append(jnp.where(row_chunk == c, qe, zero))
                    st = step_state(j, ci, st)
                o = _dot_nt(jnp.concatenate(q_parts, axis=1), jnp.concatenate(s_parts, axis=1))
                q_st = jnp.concatenate([jnp.where(m_f, qe, zero), jnp.where(m_f, zero, qe)], axis=0)
                sc = _dot_nt(q_st, ke)
                p = jnp.where(tril, sc[0:SB, :], 0.0) + jnp.where(triu, sc[SB:2 * SB, :], 0.0)
                o = o + _dot(p.astype(BF16), v_b)
                o = o * lax.rsqrt(jnp.mean(o * o, axis=-1, keepdims=True) + EPS) * ng_ref[:, lo:lo + DV]
                o_ref[rows, lo:lo + DV] = (o * _silu(og_ref[rows, lo:lo + DV].astype(F32))).astype(BF16)
                new.append(st)
            return tuple(new)

        st_f_fin = lax.fori_loop(0, sb_per_seq, fwd_body, st0, unroll=min(2, sb_per_seq))
        for j in range(2):
            sf_ref[q, j] = jnp.transpose(st_f_fin[j])[0:DK, :]
            sb_ref[q, j] = jnp.transpose(st_b_fin[j])[DK:2 * DK, :]


def _gla(proj, r_all, wg_hi, wg_lo, bg, ng, tlu, s0f, s0b, *, seq_len, n_seq, n_in_step, row0):
    kern = functools.partial(_gla_kernel, seq_len=seq_len, n_in_step=n_in_step)
    rows = n_in_step * seq_len
    n_chunks = rows // CHUNK
    rb = lambda b: b + row0 // rows
    st_spec = pl.BlockSpec((n_in_step, 2, DK, DV), lambda b, h: (b, h, 0, 0))
    return pl.pallas_call(
        kern,
        grid=(n_seq // n_in_step, HEADS // 2),
        in_specs=[pl.BlockSpec((rows, LANES), lambda b, h: (rb(b), h)),
                  pl.BlockSpec((rows, LANES), lambda b, h: (rb(b), QK_W // LANES + h)),
                  pl.BlockSpec((rows, 2 * DV), lambda b, h: (rb(b), 2 * QK_W // (2 * DV) + h)),
                  pl.BlockSpec((rows, 2 * DV), lambda b, h: (rb(b), (2 * QK_W + V_W) // (2 * DV) + h)),
                  pl.BlockSpec((rows, LANES), lambda b, h: (rb(b), 0)),
                  pl.BlockSpec((None, LANES, 2 * LANES), lambda b, h: (h, 0, 0)),
                  pl.BlockSpec((None, LANES, 2 * LANES), lambda b, h: (h, 0, 0)),
                  pl.BlockSpec((None, 1, 2 * LANES), lambda b, h: (h, 0, 0)),
                  pl.BlockSpec((None, 1, 2 * DV), lambda b, h: (h, 0, 0)),
                  pl.BlockSpec((SB, 2 * SB), lambda b, h: (0, 0)),
                  st_spec, st_spec],
        out_specs=[pl.BlockSpec((rows, 2 * DV), lambda b, h: (b, h)), st_spec, st_spec],
        out_shape=[jax.ShapeDtypeStruct((n_seq * seq_len, V_W), BF16),
                   jax.ShapeDtypeStruct((n_seq, HEADS, DK, DV), F32),
                   jax.ShapeDtypeStruct((n_seq, HEADS, DK, DV), F32)],
        scratch_shapes=[pltpu.VMEM((rows, 2 * LANES), F32),
                        pltpu.VMEM((rows, 2 * LANES), F32),
                        pltpu.VMEM((2, rows, LANES), BF16),
                        pltpu.VMEM((2, rows, LANES), BF16),
                        pltpu.VMEM((2, rows, LANES), BF16),
                        pltpu.VMEM((n_chunks, 1, 2 * LANES), F32),
                        pltpu.VMEM((2, n_chunks, DV, LANES), F32),
                        pltpu.VMEM((2, n_chunks, DV, LANES), F32)],
        compiler_params=_cparams(("arbitrary", "arbitrary")),
        name="gla_%d" % seq_len,
    )(proj, proj, proj, proj, r_all, wg_hi, wg_lo, bg, ng, tlu, s0f, s0b)


def _fnet_stage_a(u_bf, cs):
    cparts, sparts = [], []
    for g in range(FN_G):
        ab = _dot(u_bf[:, g * FN_C:(g + 1) * FN_C], cs)
        cparts.append(ab[:, 0:FN_C])
        sparts.append(ab[:, FN_C:2 * FN_C])
    return jnp.concatenate(cparts, axis=1), jnp.concatenate(sparts, axis=1)


def _fnet_ctx_kernel(u_ref, cs_ref, p2_ref, f_ref):
    uc, us = _fnet_stage_a(u_ref[...].astype(BF16), cs_ref[...].astype(BF16))
    ab = jnp.concatenate([uc, us], axis=0).astype(BF16)
    f_ref[...] = _dot(p2_ref[...].astype(BF16), ab).astype(BF16)


def _fnet_ctx(proj, cs, p2):
    return pl.pallas_call(
        _fnet_ctx_kernel,
        grid=(N_CTX,),
        in_specs=[pl.BlockSpec((L_CTX, FN_G * FN_C), lambda b: (b, 3)),
                  pl.BlockSpec((FN_C, 2 * FN_C), lambda b: (0, 0)),
                  pl.BlockSpec((L_CTX, 2 * L_CTX), lambda b: (0, 0))],
        out_specs=pl.BlockSpec((L_CTX, FN_G * FN_C), lambda b: (b, 0)),
        out_shape=jax.ShapeDtypeStruct((T_CTX, FN_G * FN_C), BF16),
        compiler_params=_cparams(("arbitrary",)),
        name="fnet_ctx",
    )(proj, cs, p2)


TM_FL = 256
RT_FL = 256


def _fnet_lat_kernel(u_ref, cs_ref, kr_ref, f_ref, ab_s):
    m = pl.program_id(1)

    @pl.when(m == 0)
    def _():
        def body(t, carry):
            rows = pl.ds(pl.multiple_of(t * RT_FL, RT_FL), RT_FL)
            uc, us = _fnet_stage_a(u_ref[rows, :].astype(BF16), cs_ref[...].astype(BF16))
            ab_s[rows, :] = uc.astype(BF16)
            ab_s[pl.ds(pl.multiple_of(L_LAT + t * RT_FL, RT_FL), RT_FL), :] = us.astype(BF16)
            return carry
        lax.fori_loop(0, L_LAT // RT_FL, body, 0)

    f_ref[...] = _dot(kr_ref[...].astype(BF16), ab_s[...]).astype(BF16)


def _fnet_lat(proj, cs, kr):
    nm = L_LAT // TM_FL
    return pl.pallas_call(
        _fnet_lat_kernel,
        grid=(N_LAT, nm),
        in_specs=[pl.BlockSpec((L_LAT, FN_G * FN_C), lambda b, m: (T_CTX // L_LAT + b, 3)),
                  pl.BlockSpec((FN_C, 2 * FN_C), lambda b, m: (0, 0)),
                  pl.BlockSpec((TM_FL, 2 * L_LAT), lambda b, m: (m, 0))],
        out_specs=pl.BlockSpec((TM_FL, FN_G * FN_C), lambda b, m: (b * nm + m, 0)),
        out_shape=jax.ShapeDtypeStruct((T_LAT, FN_G * FN_C), BF16),
        scratch_shapes=[pltpu.VMEM((2 * L_LAT, FN_G * FN_C), BF16)],
        compiler_params=_cparams(("arbitrary", "arbitrary")),
        name="fnet_lat",
    )(proj, cs, kr)


TM_OUT = 256
LANE_E0 = N_GROUPS
ROWS_PER_BLK = 8
PACK_ROWS = -(-(2 * TM_OUT + N_EXP * (ROWS_PER_BLK - 1)) // 256) * 256
PACK_BLKS = PACK_ROWS // ROWS_PER_BLK
N_TOK_TILES = T_ALL // TM_OUT
BLK_PER_TILE = TM_MOE // ROWS_PER_BLK
USED_BLKS = (2 * TM_OUT + N_EXP * (ROWS_PER_BLK - 1)) // ROWS_PER_BLK
HS_ROWS = N_TOK_TILES * PACK_ROWS
assert (PACK_BLKS - USED_BLKS) * N_TOK_TILES >= 2 * BLK_PER_TILE


def _outproj_kernel(oc_ref, ol_ref, fc_ref, fl_ref, xp_ref, xs_ref, mod_ref, g_ref, wo_ref, wf_ref,
                    wrc_ref, br_ref, sut_ref, sl_ref,
                    x1_ref, hs_ref, rw_ref, nb_ref, lb_ref):
    i = pl.program_id(0)
    is_ctx = i < T_CTX // TM_OUT

    o = jnp.where(is_ctx, oc_ref[...], ol_ref[...]).astype(BF16)
    f = jnp.where(is_ctx, fc_ref[...], fl_ref[...]).astype(BF16)
    x = jnp.where(is_ctx, xp_ref[...], xs_ref[...])
    y = _dot(o, wo_ref[...]) + _dot(f, wf_ref[...])
    ga1 = mod_ref[:, 2 * D_MODEL:3 * D_MODEL]
    sh2 = mod_ref[:, 3 * D_MODEL:4 * D_MODEL]
    sc2 = mod_ref[:, 4 * D_MODEL:5 * D_MODEL]
    x1 = x + ga1 * y
    x1_ref[...] = x1
    h2 = _rms(x1, g_ref[...]) * (1.0 + sc2) + sh2

    h_hi, h_lo = _split2(h2)
    hw = _dot(h_hi, wrc_ref[...])
    lg_all = jnp.transpose(hw[:, 0:LANES] + _dot(h_lo, wrc_ref[:, 0:LANES]) + hw[:, LANES:2 * LANES]
                           + br_ref[...])

    row_i = lax.broadcasted_iota(I32, (LANES, TM_OUT), 0)
    row = row_i.astype(F32)
    neg = jnp.float32(-jnp.inf)
    big = jnp.float32(LANES)
    lg = jnp.where(row_i < N_GROUPS, lg_all, neg)
    gmax = jnp.max(lg, axis=0, keepdims=True)
    gsel = jnp.min(jnp.where(lg == gmax, row, big), axis=0, keepdims=True)
    den = jnp.sum(jnp.exp(lg - gmax), axis=0, keepdims=True)
    pg_sel = 1.0 / den

    e_idx = row_i - LANE_E0
    egrp = (e_idx >> 3).astype(F32)
    emask = (e_idx >= 0) & (e_idx < N_EXP) & (egrp == gsel)
    m1 = jnp.where(emask, lg_all, neg)
    v1 = jnp.max(m1, axis=0, keepdims=True)
    i1 = jnp.min(jnp.where(m1 == v1, row, big), axis=0, keepdims=True)
    m2 = jnp.where(row == i1, neg, m1)
    v2 = jnp.max(m2, axis=0, keepdims=True)
    i2 = jnp.min(jnp.where(m2 == v2, row, big), axis=0, keepdims=True)
    e2 = jnp.exp(v2 - v1)
    inv = 1.0 / (1.0 + e2)
    w1 = inv * pg_sel
    w2 = (e2 * inv) * pg_sel

    oh1 = row == i1
    oh2 = row == i2
    oh = jnp.where(oh1 | oh2, 1.0, 0.0).astype(BF16)
    cnt = _dot(oh, jnp.ones((TM_OUT, LANES), BF16))
    nblk = jnp.floor((cnt + (ROWS_PER_BLK - 1)) * (1.0 / ROWS_PER_BLK))
    lboff = _dot(sl_ref[...], nblk.astype(BF16))
    lrank = _dot(oh, sut_ref[...])
    posmat = jnp.concatenate([lboff, lboff], axis=1) * ROWS_PER_BLK + lrank
    p1 = jnp.sum(jnp.where(oh1, posmat, 0.0), axis=0, keepdims=True)
    p2 = jnp.sum(jnp.where(oh2, posmat, 0.0), axis=0, keepdims=True)
    nb_ref[...] = nblk[:, 0:8].astype(I32)
    lb_ref[...] = lboff[:, 0:8].astype(I32)

    prow = lax.broadcasted_iota(I32, (PACK_ROWS, TM_OUT), 0).astype(F32)
    place = jnp.where((prow == p1) | (prow == p2), 1.0, 0.0).astype(BF16)
    hs_ref[...] = _dot(place, h_hi).astype(BF16)

    rw_ref[...] = jnp.concatenate([w1, w2, p1, p2, jnp.zeros((4, TM_OUT), F32)], axis=0)


def _outproj(o_ctx, o_lat, f_ctx, f_lat, xp, xs, mod3, g_ffn, wo, wf, wr_cat, br, sut, sl):
    nt = T_ALL // TM_OUT
    nctx = T_CTX // TM_OUT
    ctx_map = lambda i: (jnp.minimum(i, nctx - 1), 0)
    lat_map = lambda i: (jnp.maximum(i - nctx, 0), 0)
    const = lambda i: (0, 0)
    return pl.pallas_call(
        _outproj_kernel,
        grid=(nt,),
        in_specs=[pl.BlockSpec((TM_OUT, V_W), ctx_map),
                  pl.BlockSpec((TM_OUT, V_W), lat_map),
                  pl.BlockSpec((TM_OUT, FN_G * FN_C), ctx_map),
                  pl.BlockSpec((TM_OUT, FN_G * FN_C), lat_map),
                  pl.BlockSpec((TM_OUT, D_MODEL), ctx_map),
                  pl.BlockSpec((TM_OUT, D_MODEL), lat_map),
                  pl.BlockSpec((None, 1, 6 * D_MODEL), lambda i: (_cond_row(i, TM_OUT), 0, 0)),
                  pl.BlockSpec((1, D_MODEL), const),
                  pl.BlockSpec((V_W, D_MODEL), const),
                  pl.BlockSpec((FN_G * FN_C, D_MODEL), const),
                  pl.BlockSpec((D_MODEL, 2 * LANES), const),
                  pl.BlockSpec((1, LANES), const),
                  pl.BlockSpec((TM_OUT, TM_OUT), const),
                  pl.BlockSpec((LANES, LANES), const)],
        out_specs=[pl.BlockSpec((TM_OUT, D_MODEL), lambda i: (i, 0)),
                   pl.BlockSpec((PACK_ROWS, D_MODEL), lambda i: (i, 0)),
                   pl.BlockSpec((8, TM_OUT), lambda i: (0, i)),
                   pl.BlockSpec((None, LANES, 8), lambda i: (i, 0, 0)),
                   pl.BlockSpec((None, LANES, 8), lambda i: (i, 0, 0))],
        out_shape=[jax.ShapeDtypeStruct((T_ALL, D_MODEL), F32),
                   jax.ShapeDtypeStruct((HS_ROWS, D_MODEL), BF16),
                   jax.ShapeDtypeStruct((8, T_ALL), F32),
                   jax.ShapeDtypeStruct((nt, LANES, 8), I32),
                   jax.ShapeDtypeStruct((nt, LANES, 8), I32)],
        compiler_params=_cparams(("arbitrary",)),
        name="outproj",
    )(o_ctx, o_lat, f_ctx, f_lat, xp, xs, mod3, g_ffn, wo, wf, wr_cat, br, sut, sl)


SRC_BITS = 16
SRC_MASK = (1 << SRC_BITS) - 1
X_SLOTS = 3
Y_SLOTS = 3
N_UP_CHUNKS = 2
N_DN_CHUNKS = 8
NT_MOE = (2 * T_ALL + N_TOK_TILES * N_EXP * (ROWS_PER_BLK - 1)) // TM_MOE + N_EXP


def _moe_kernel(texp_ref, nexp_ref, meta_ref, code_ref,
                h_hbm, wg_hbm, wu_hbm, wd_hbm,
                out_hbm,
                xbuf, ybuf, wg_f, wu_f, wd_f, wg_s, wu_s, wd_s, kcount, gsem, ssem, wsem):
    i = pl.program_id(0)
    nt = meta_ref[0]
    xs = i % X_SLOTS

    def blk_rows(b):
        if isinstance(b, int):
            return pl.ds(b * ROWS_PER_BLK, ROWS_PER_BLK)
        return pl.ds(pl.multiple_of(b * ROWS_PER_BLK, ROWS_PER_BLK), ROWS_PER_BLK)

    def gather_row(tile, sl, j):
        src = code_ref[(tile + 2) * BLK_PER_TILE + j] & SRC_MASK
        pltpu.make_async_copy(h_hbm.at[blk_rows(src)], xbuf.at[sl, blk_rows(j)], gsem.at[sl]).start()

    def scatter_row(tile, sl, j):
        dst = code_ref[(tile + 2) * BLK_PER_TILE + j] >> SRC_BITS
        pltpu.make_async_copy(ybuf.at[sl, blk_rows(j)], out_hbm.at[blk_rows(dst)], ssem.at[sl]).start(priority=1)

    def gather_wait(sl):
        pltpu.make_async_copy(h_hbm.at[pl.ds(0, TM_MOE)], xbuf.at[sl], gsem.at[sl]).wait()

    def scatter_wait(sl):
        pltpu.make_async_copy(ybuf.at[sl], out_hbm.at[pl.ds(0, TM_MOE)], ssem.at[sl]).wait()

    def y_slot(tile):
        return (tile + 1) % Y_SLOTS

    @pl.when(i == 0)
    def _():
        ybuf[y_slot(-2)] = jnp.zeros((TM_MOE, D_MODEL), BF16)
        ybuf[y_slot(-1)] = jnp.zeros((TM_MOE, D_MODEL), BF16)

        def body(j, c):
            gather_row(0, 0, j)
            gather_row(1, 1, j)
            scatter_row(-2, y_slot(-2), j)
            return c
        lax.fori_loop(0, BLK_PER_TILE, body, 0)

    @pl.when((i >= 1) & (i <= nt))
    def _():
        scatter_wait(y_slot(i - 3))

    @pl.when(i < nt)
    def _():
        prev = texp_ref[jnp.maximum(i - 1, 0)]

        def weight_copies(e, sl):
            return (pltpu.make_async_copy(wg_hbm.at[e], wg_f.at[sl], wsem.at[sl]),
                    pltpu.make_async_copy(wu_hbm.at[e], wu_f.at[sl], wsem.at[sl]),
                    pltpu.make_async_copy(wd_hbm.at[e], wd_f.at[sl], wsem.at[sl]))

        @pl.when(i == 0)
        def _():
            kcount[0] = 0
            for cp in weight_copies(texp_ref[0], 0):
                cp.start()

        @pl.when((i == 0) | (texp_ref[i] != prev))
        def _():
            k = kcount[0]
            for sl in range(2):
                @pl.when(k % 2 == sl)
                def _(sl=sl):
                    for cp in weight_copies(texp_ref[i], sl):
                        cp.wait()
                    def narrow(r, c):
                        up = pl.ds(pl.multiple_of(r * (D_MODEL // 16), D_MODEL // 16), D_MODEL // 16)
                        dn = pl.ds(pl.multiple_of(r * (D_EXP // 16), D_EXP // 16), D_EXP // 16)
                        wg_s[up, :] = wg_f[sl, up, :].astype(BF16)
                        wu_s[up, :] = wu_f[sl, up, :].astype(BF16)
                        wd_s[dn, :] = wd_f[sl, dn, :].astype(BF16)
                        return c
                    lax.fori_loop(0, 16, narrow, 0)

                    @pl.when(nexp_ref[i] >= 0)
                    def _():
                        for cp in weight_copies(nexp_ref[i], 1 - sl):
                            cp.start()
            kcount[0] = k + 1

        gather_wait(xs)
        x = xbuf[xs].astype(BF16)

        issues = []
        for j in range(BLK_PER_TILE):
            issues.append(functools.partial(gather_row, i + 2, (i + 2) % X_SLOTS, j))
            issues.append(functools.partial(scatter_row, i - 1, y_slot(i - 1), j))
        n_groups = N_UP_CHUNKS + N_DN_CHUNKS
        per_group = -(-len(issues) // n_groups)

        def issue_group(k):
            for fn in issues[k * per_group:(k + 1) * per_group]:
                fn()

        wu_c = D_EXP // N_UP_CHUNKS
        hid = []
        for n in range(N_UP_CHUNKS):
            issue_group(n)
            g = _dot(x, wg_s[:, n * wu_c:(n + 1) * wu_c])
            u = _dot(x, wu_s[:, n * wu_c:(n + 1) * wu_c])
            hid.append((_silu(g) * u).astype(BF16))
        hid = jnp.concatenate(hid, axis=1)
        wd_c = D_MODEL // N_DN_CHUNKS
        ys = y_slot(i)
        for n in range(N_DN_CHUNKS):
            issue_group(N_UP_CHUNKS + n)
            ybuf[ys, :, n * wd_c:(n + 1) * wd_c] = _dot(hid, wd_s[:, n * wd_c:(n + 1) * wd_c]).astype(BF16)

    @pl.when(i == nt)
    def _():
        gather_wait(xs)
        gather_wait((i + 1) % X_SLOTS)

        def body(j, c):
            scatter_row(nt - 1, y_slot(nt - 1), j)
            return c
        lax.fori_loop(0, BLK_PER_TILE, body, 0)
        scatter_wait(y_slot(nt - 2))
        scatter_wait(y_slot(nt - 1))


def _moe(texp, nexp, meta, code, hs, w_eg, w_eu, w_ed):
    hbm = pl.BlockSpec(memory_space=pl.ANY)
    grid_spec = pltpu.PrefetchScalarGridSpec(
        num_scalar_prefetch=4,
        grid=(NT_MOE + 1,),
        in_specs=[hbm, hbm, hbm, hbm],
        out_specs=hbm,
        scratch_shapes=[pltpu.VMEM((X_SLOTS, TM_MOE, D_MODEL), BF16),
                        pltpu.VMEM((Y_SLOTS, TM_MOE, D_MODEL), BF16),
                        pltpu.VMEM((2, D_MODEL, D_EXP), F32),
                        pltpu.VMEM((2, D_MODEL, D_EXP), F32),
                        pltpu.VMEM((2, D_EXP, D_MODEL), F32),
                        pltpu.VMEM((D_MODEL, D_EXP), BF16),
                        pltpu.VMEM((D_MODEL, D_EXP), BF16),
                        pltpu.VMEM((D_EXP, D_MODEL), BF16),
                        pltpu.SMEM((1,), I32),
                        pltpu.SemaphoreType.DMA((X_SLOTS,)),
                        pltpu.SemaphoreType.DMA((Y_SLOTS,)),
                        pltpu.SemaphoreType.DMA((2,))])
    return pl.pallas_call(
        _moe_kernel,
        grid_spec=grid_spec,
        out_shape=jax.ShapeDtypeStruct((HS_ROWS, D_MODEL), BF16),
        input_output_aliases={4: 0},
        compiler_params=_cparams(("arbitrary",)),
        name="moe",
    )(texp, nexp, meta, code, hs, w_eg, w_eu, w_ed)


TM_FIN = TM_OUT


def _final_kernel(x1_ref, ys_pack_ref, rw_ref, mod_ref, g_ref, yp_ref, ys_ref):
    i = pl.program_id(0)
    ga2 = mod_ref[:, 5 * D_MODEL:6 * D_MODEL]
    w0 = rw_ref[0:1, :]
    w1 = rw_ref[1:2, :]
    p0 = rw_ref[2:3, :]
    p1 = rw_ref[3:4, :]
    prow = lax.broadcasted_iota(I32, (PACK_ROWS, TM_FIN), 0).astype(F32)
    comb_t = jnp.where(prow == p0, w0, 0.0) + jnp.where(prow == p1, w1, 0.0)
    y_moe = lax.dot_general(comb_t.astype(BF16), ys_pack_ref[...].astype(BF16), (((0,), (0,)), ((), ())),
                            preferred_element_type=F32)
    y = x1_ref[...] + ga2 * y_moe
    out = _rms(y, g_ref[...])

    @pl.when(i < T_CTX // TM_FIN)
    def _():
        yp_ref[...] = out

    @pl.when(i >= T_CTX // TM_FIN)
    def _():
        ys_ref[...] = out


def _final(x1, y2, rw, mod3, g_fin):
    nt = T_ALL // TM_FIN
    nctx = T_CTX // TM_FIN
    return pl.pallas_call(
        _final_kernel,
        grid=(nt,),
        in_specs=[pl.BlockSpec((TM_FIN, D_MODEL), lambda i: (i, 0)),
                  pl.BlockSpec((PACK_ROWS, D_MODEL), lambda i: (i, 0)),
                  pl.BlockSpec((8, TM_FIN), lambda i: (0, i)),
                  pl.BlockSpec((None, 1, 6 * D_MODEL), lambda i: (_cond_row(i, TM_FIN), 0, 0)),
                  pl.BlockSpec((1, D_MODEL), lambda i: (0, 0))],
        out_specs=[pl.BlockSpec((TM_FIN, D_MODEL), lambda i: (jnp.minimum(i, nctx - 1), 0)),
                   pl.BlockSpec((TM_FIN, D_MODEL), lambda i: (jnp.maximum(i - nctx, 0), 0))],
        out_shape=[jax.ShapeDtypeStruct((T_CTX, D_MODEL), F32),
                   jax.ShapeDtypeStruct((T_LAT, D_MODEL), F32)],
        compiler_params=_cparams(("arbitrary",)),
        name="final",
    )(x1, y2, rw, mod3, g_fin)


def _np_bf16(a):
    return jnp.asarray(np.asarray(a, np.float32), dtype=BF16)


def _np_f32(a):
    return jnp.asarray(np.asarray(a, np.float32))


@functools.lru_cache(maxsize=None)
def _constants():
    c = {}
    k = np.arange(FN_C)
    ang = 2.0 * np.pi * np.outer(k, k) / FN_C
    c["cs"] = np.concatenate([np.cos(ang), np.sin(ang)], axis=1) / np.sqrt(FN_C)
    p = np.arange(L_CTX)
    ang = 2.0 * np.pi * np.outer(p, p) / L_CTX
    c["p2"] = np.concatenate([np.cos(ang), -np.sin(ang)], axis=1) / np.sqrt(L_CTX)
    pos = np.arange(L_LAT)
    rr, cc = pos // GRID_W, pos % GRID_W
    num = (np.outer(rr, rr) * (GRID_W // GRID_H) + np.outer(cc, cc)) % GRID_W
    ang = 2.0 * np.pi * num / GRID_W
    c["kr"] = np.concatenate([np.cos(ang), -np.sin(ang)], axis=1) / np.sqrt(L_LAT)
    i = np.arange(SB)
    same = (i[:, None] // CHUNK) == (i[None, :] // CHUNK)
    tl = same & (i[:, None] >= i[None, :])
    tu = same & (i[:, None] <= i[None, :])
    c["tlu"] = np.concatenate([tl, tu], axis=1).astype(np.float32)
    c["sut"] = (i[:, None] < i[None, :]).astype(np.float32)
    k = np.arange(LANES)
    c["sl"] = (k[:, None] > k[None, :]).astype(np.float32)
    return c


def kernel(x_prompt, x_sample, state_gla_fwd, state_gla_bwd, c, c_ctx, w_ada, b_ada, norm_attn, norm_ffn, w_in, w_gate_fwd, b_gate_fwd, w_gate_bwd, b_gate_bwd, norm_gla, w_out, w_router_group, b_router_group, w_router_expert, b_router_expert, w_expert_gate, w_expert_up, w_expert_down, norm_final):
    assert w_ada.shape[0] == 1, "single layer"
    cst = _constants()
    cs, p2, kr = _np_f32(cst["cs"]), _np_f32(cst["p2"]), _np_f32(cst["kr"])
    tlu, sut, sl = _np_bf16(cst["tlu"]), _np_bf16(cst["sut"]), _np_bf16(cst["sl"])

    xp = x_prompt.reshape(T_CTX, D_MODEL)
    xs = x_sample.reshape(T_LAT, D_MODEL)

    cond8 = jnp.concatenate([c_ctx[None, :], c, jnp.zeros((3, D_MODEL), F32)], axis=0)
    mod = _ada(cond8, w_ada[0], b_ada[0][None, :])
    mod3 = mod.reshape(8, 1, 6 * D_MODEL)

    wi = w_in[0]
    i_og = 2 * QK_W + 2 * V_W
    i_u = i_og + 2 * RANK
    w_a = wi[:, :i_og].astype(BF16)
    w_u = wi[:, i_u:].astype(BF16)
    w_r = jnp.pad(wi[:, i_og:i_u], ((0, 0), (0, LANES - 2 * RANK))).astype(BF16)

    wgf = w_gate_fwd[0].reshape(RANK, HEADS, DK)
    wgb = w_gate_bwd[0].reshape(RANK, HEADS, DK)
    zf = jnp.zeros_like(wgf)
    top = jnp.stack([wgf, zf], axis=2)
    bot = jnp.stack([zf, wgb], axis=2)
    wg = jnp.concatenate([top, bot], axis=0)
    wg = wg.reshape(2 * RANK, HEADS // 2, 4 * DK).transpose(1, 0, 2)
    wg = jnp.pad(wg, ((0, 0), (0, LANES - 2 * RANK), (0, 0)))
    wg_hi = wg.astype(BF16)
    wg_lo = (wg - wg_hi.astype(F32)).astype(BF16)
    bg = jnp.stack([b_gate_fwd[0].reshape(HEADS, DK), b_gate_bwd[0].reshape(HEADS, DK)], axis=1)
    bg = bg.reshape(HEADS // 2, 1, 4 * DK)
    ng = norm_gla[0].reshape(HEADS // 2, 1, 2 * DV)

    proj, r_all = _inproj(xp, xs, mod3, norm_attn, w_a, w_u, w_r)

    zero_state = jnp.zeros((N_CTX, HEADS, DK, DV), F32)
    o_ctx, sf_ctx, sb_ctx = _gla(proj, r_all, wg_hi, wg_lo, bg, ng, tlu, zero_state, zero_state,
                                 seq_len=L_CTX, n_seq=N_CTX, n_in_step=4, row0=0)
    o_lat, _, _ = _gla(proj, r_all, wg_hi, wg_lo, bg, ng, tlu,
                       state_gla_fwd[:, 0], state_gla_bwd[:, 0],
                       seq_len=L_LAT, n_seq=N_LAT, n_in_step=1, row0=T_CTX)

    f_ctx = _fnet_ctx(proj, cs, p2)
    f_lat = _fnet_lat(proj, cs, kr)

    wo = w_out[0][:V_W].astype(BF16)
    wf = w_out[0][V_W:].astype(BF16)
    wr = jnp.concatenate([w_router_group[0], w_router_expert[0]], axis=1)
    wr = jnp.pad(wr, ((0, 0), (0, LANES - N_GROUPS - N_EXP)))
    wr_hi = wr.astype(BF16)
    wr_lo = (wr - wr_hi.astype(F32)).astype(BF16)
    wr_cat = jnp.concatenate([wr_hi, wr_lo], axis=1)
    br = jnp.pad(jnp.concatenate([b_router_group[0], b_router_expert[0]]), (0, LANES - N_GROUPS - N_EXP))[None, :]

    x1, hs, rw, nb, lb = _outproj(o_ctx, o_lat, f_ctx, f_lat, xp, xs, mod3, norm_ffn, wo, wf,
                                  wr_cat, br, sut, sl)

    nb_e = nb[:, LANE_E0:LANE_E0 + N_EXP, 0].T
    lb_e = lb[:, LANE_E0:LANE_E0 + N_EXP, 0].T
    run_end = jnp.cumsum(nb_e, axis=1)
    blocks_e = run_end[:, -1]
    tiles_e = (blocks_e + BLK_PER_TILE - 1) // BLK_PER_TILE
    tile_end = jnp.cumsum(tiles_e)
    tile_start = tile_end - tiles_e
    n_tiles = tile_end[-1]
    n_code_tiles = NT_MOE + 4
    tile = jnp.arange(n_code_tiles, dtype=I32) - 2
    tile_c = jnp.clip(tile, 0, n_tiles - 1)
    t_exp = jnp.sum(tile_c[:, None] >= tile_end[None, :], axis=1)
    ends = run_end[t_exp]
    starts = ends - nb_e[t_exp]
    offs = lb_e[t_exp] + jnp.arange(N_TOK_TILES, dtype=I32)[None, :] * PACK_BLKS - starts
    j = jnp.arange(BLK_PER_TILE, dtype=I32)
    bi = ((tile_c - tile_start[t_exp]) * BLK_PER_TILE)[:, None] + j[None, :]
    in_run = (starts.T[:, :, None] <= bi[None, :, :]) & (bi[None, :, :] < ends.T[:, :, None])
    blk = bi + jnp.sum(jnp.where(in_run, offs.T[:, :, None], 0), axis=0)
    valid = (tile == tile_c)[:, None] & (bi < blocks_e[t_exp][:, None])
    spare_ix = (jnp.arange(n_code_tiles, dtype=I32) % 2)[:, None] * BLK_PER_TILE + j[None, :]
    spare = (spare_ix % N_TOK_TILES) * PACK_BLKS + USED_BLKS + spare_ix // N_TOK_TILES
    code = jnp.where(valid, (blk << SRC_BITS) | blk, (spare << SRC_BITS) | blk[:, 0:1]).astype(I32).reshape(-1)
    tidx = jnp.minimum(jnp.arange(NT_MOE + 1, dtype=I32), n_tiles - 1)
    texp = jnp.sum(tidx[:, None] >= tile_end[None, :], axis=1).astype(I32)
    nxt = tile_end[texp]
    nexp = jnp.where(nxt < n_tiles, texp[jnp.minimum(nxt, NT_MOE)], -1).astype(I32)
    meta = n_tiles.reshape(1).astype(I32)

    y2 = _moe(texp, nexp, meta, code, hs, w_expert_gate[0], w_expert_up[0], w_expert_down[0])
    y_prompt, y_sample = _final(x1, y2, rw, mod3, norm_final[None, :])

    st_shape = (N_CTX, 1, HEADS, DK, DV)
    return (y_prompt.reshape(N_CTX, L_CTX, D_MODEL), y_sample.reshape(N_LAT, L_LAT, D_MODEL),
            sf_ctx.reshape(st_shape), sb_ctx.reshape(st_shape))
```

```python
import functools

import numpy as np
import jax
import jax.numpy as jnp
from jax import lax
from jax.experimental import pallas as pl
from jax.experimental.pallas import tpu as pltpu

F32 = jnp.float32
BF16 = jnp.bfloat16
I32 = jnp.int32

D_MODEL = 2048
N_CTX = 32
L_CTX = 256
N_LAT = 4
L_LAT = 2048
GRID_H = 32
GRID_W = 64
T_CTX = N_CTX * L_CTX
T_LAT = N_LAT * L_LAT
T_ALL = T_CTX + T_LAT
HEADS = 8
DK = 64
DV = 128
RANK = 16
TAU = 16.0
CHUNK = 64
FN_G = 8
FN_C = 128
QK_W = HEADS * DK
V_W = HEADS * DV
N_GROUPS = 4
EPG = 8
N_EXP = N_GROUPS * EPG
D_EXP = 512
EPS = 1e-6

LANES = 128
VMEM_LIMIT = 56 * 1024 * 1024

TM_MOE = 256


def _dot(a, b):
    return jnp.dot(a, b, preferred_element_type=F32)


def _dot_nt(a, b):
    return lax.dot_general(a, b, (((1,), (1,)), ((), ())), preferred_element_type=F32)


def _split2(x):
    hi = x.astype(BF16)
    lo = (x - hi.astype(F32)).astype(BF16)
    return hi, lo


def _silu(x):
    return x * (1.0 / (1.0 + jnp.exp(-x)))


def _rms(x, g):
    return x * lax.rsqrt(jnp.mean(x * x, axis=-1, keepdims=True) + EPS) * g


def _cparams(sem):
    return pltpu.CompilerParams(dimension_semantics=sem, vmem_limit_bytes=VMEM_LIMIT)


def _ada_kernel(c_ref, w_ref, b_ref, o_ref):
    s_hi, s_lo = _split2(_silu(c_ref[...]))
    w = w_ref[...]
    w_hi = w.astype(BF16)
    w_lo = (w - w_hi.astype(F32)).astype(BF16)
    o_ref[...] = _dot(s_hi, w_hi) + _dot(s_lo, w_hi) + _dot(s_hi, w_lo) + b_ref[...]


def _ada(cond8, w_ada, b_ada):
    tn = 1024
    n6 = 6 * D_MODEL
    return pl.pallas_call(
        _ada_kernel,
        grid=(n6 // tn,),
        in_specs=[pl.BlockSpec((8, D_MODEL), lambda j: (0, 0)),
                  pl.BlockSpec((D_MODEL, tn), lambda j: (0, j)),
                  pl.BlockSpec((1, tn), lambda j: (0, j))],
        out_specs=pl.BlockSpec((8, tn), lambda j: (0, j)),
        out_shape=jax.ShapeDtypeStruct((8, n6), F32),
        compiler_params=_cparams(("arbitrary",)),
        name="ada",
    )(cond8, w_ada, b_ada)


TM_IN = 512
TM_IN_HALF = 256
TN_IN = 1024
N_MAIN = 4096
N_QKVG = 3072


def _cond_row(tile, tm):
    ctx_tiles = T_CTX // tm
    per_seq = L_LAT // tm
    return jnp.where(tile < ctx_tiles, 0, 1 + (jnp.maximum(tile - ctx_tiles, 0)) // per_seq)


def _inproj_kernel(xp_ref, xs_ref, mod_ref, g_ref, wa_ref, wu_ref, wr_ref, proj_ref, r_ref):
    i = pl.program_id(0)
    is_ctx = i < T_CTX // TM_IN
    sh1 = mod_ref[:, 0:D_MODEL]
    sc1 = mod_ref[:, D_MODEL:2 * D_MODEL]
    for hf in range(TM_IN // TM_IN_HALF):
        rows = slice(hf * TM_IN_HALF, (hf + 1) * TM_IN_HALF)
        x = jnp.where(is_ctx, xp_ref[rows, :], xs_ref[rows, :])
        hb = (_rms(x, g_ref[...]) * (1.0 + sc1) + sh1).astype(BF16)
        r_ref[rows, :] = _dot(hb, wr_ref[...])
        for n in range(N_QKVG // TN_IN):
            cols = slice(n * TN_IN, (n + 1) * TN_IN)
            proj_ref[rows, cols] = _dot(hb, wa_ref[:, cols]).astype(BF16)
        for n in range((N_MAIN - N_QKVG) // TN_IN):
            cols = slice(N_QKVG + n * TN_IN, N_QKVG + (n + 1) * TN_IN)
            proj_ref[rows, cols] = _dot(hb, wu_ref[:, n * TN_IN:(n + 1) * TN_IN]).astype(BF16)


def _inproj(xp, xs, mod3, g_attn, w_a, w_u, w_r):
    nt = T_ALL // TM_IN
    nctx = T_CTX // TM_IN
    resident = pl.Buffered(1)
    return pl.pallas_call(
        _inproj_kernel,
        grid=(nt,),
        in_specs=[pl.BlockSpec((TM_IN, D_MODEL), lambda i: (jnp.minimum(i, nctx - 1), 0)),
                  pl.BlockSpec((TM_IN, D_MODEL), lambda i: (jnp.maximum(i - nctx, 0), 0)),
                  pl.BlockSpec((None, 1, 6 * D_MODEL), lambda i: (_cond_row(i, TM_IN), 0, 0)),
                  pl.BlockSpec((1, D_MODEL), lambda i: (0, 0)),
                  pl.BlockSpec((D_MODEL, N_QKVG), lambda i: (0, 0), pipeline_mode=resident),
                  pl.BlockSpec((D_MODEL, N_MAIN - N_QKVG), lambda i: (0, 0), pipeline_mode=resident),
                  pl.BlockSpec((D_MODEL, LANES), lambda i: (0, 0), pipeline_mode=resident)],
        out_specs=[pl.BlockSpec((TM_IN, N_MAIN), lambda i: (i, 0)),
                   pl.BlockSpec((TM_IN, LANES), lambda i: (i, 0))],
        out_shape=[jax.ShapeDtypeStruct((T_ALL, N_MAIN), BF16),
                   jax.ShapeDtypeStruct((T_ALL, LANES), F32)],
        compiler_params=_cparams(("arbitrary",)),
        name="inproj",
    )(xp, xs, mod3, g_attn, w_a, w_u, w_r)


SB = 256
CPB = SB // CHUNK


def _gla_kernel(q_ref, k_ref, v_ref, og_ref, r_ref, wgh_ref, wgl_ref, bg_ref, ng_ref,
                tlu_ref, s0f_ref, s0b_ref,
                o_ref, sf_ref, sb_ref,
                cum_s, last_s, qe_s, ke_s, kd_s, dec_s, kv_s, sbs_s, *, seq_len, n_in_step):
    sb_per_seq = seq_len // SB
    ch_per_seq = seq_len // CHUNK
    n_sb = n_in_step * sb_per_seq
    lane = lax.broadcasted_iota(I32, (1, LANES), 1)
    m_f = lane < DK

    r_hi, r_lo = _split2(r_ref[...])
    z = _dot(r_hi, wgh_ref[...]) + _dot(r_lo, wgh_ref[...]) + _dot(r_hi, wgl_ref[...]) + bg_ref[...]
    g_all = (jnp.minimum(z, 0.0) - jnp.log(1.0 + jnp.exp(-jnp.abs(z)))) * (1.0 / TAU)

    q_pair = q_ref[...].astype(F32)
    k_pair = k_ref[...].astype(F32)
    q_roll = pltpu.roll(q_pair, DK, axis=1)
    k_roll = pltpu.roll(k_pair, DK, axis=1)

    row_b = lax.broadcasted_iota(I32, (SB, SB), 0)
    col_b = lax.broadcasted_iota(I32, (SB, SB), 1)
    same_chunk = (row_b // CHUNK) == (col_b // CHUNK)
    tril = same_chunk & (row_b >= col_b)
    triu = same_chunk & (row_b <= col_b)
    row_chunk = lax.broadcasted_iota(I32, (SB, 1), 0) // CHUNK
    col_chunk = lax.broadcasted_iota(I32, (1, SB), 1) // CHUNK

    m_f2 = (lax.broadcasted_iota(I32, (1, 2 * LANES), 1) % LANES) < DK
    cum_s[...] = g_all

    def cum_body(s, carry):
        rows = pl.ds(pl.multiple_of(s * SB, SB), SB)
        g = cum_s[rows, :]
        f_hi, f_lo = _split2(jnp.where(m_f2, g, 0.0))
        b_hi, b_lo = _split2(jnp.where(m_f2, 0.0, g))
        cum = (_dot(tlu_ref[...], jnp.concatenate([f_hi, b_hi], axis=0))
               + _dot(tlu_ref[...], jnp.concatenate([f_lo, b_lo], axis=0)))
        cum_s[rows, :] = cum
        tots = []
        for c in range(CPB):
            tot = jnp.where(m_f2, cum[(c + 1) * CHUNK - 1:(c + 1) * CHUNK, :], cum[c * CHUNK:c * CHUNK + 1, :])
            dec_s[s * CPB + c] = jnp.exp(tot)
            tots.append(jnp.broadcast_to(tot, (CHUNK, 2 * LANES)))
        last_s[rows, :] = jnp.concatenate(tots, axis=0)
        return carry

    lax.fori_loop(0, n_sb, cum_body, 0)

    for j in range(2):
        cum = cum_s[:, j * LANES:(j + 1) * LANES]
        last = last_s[:, j * LANES:(j + 1) * LANES]
        if j == 0:
            q2 = jnp.where(m_f, q_pair, q_roll)
            k2 = jnp.where(m_f, k_pair, k_roll)
        else:
            q2 = jnp.where(m_f, q_roll, q_pair)
            k2 = jnp.where(m_f, k_roll, k_pair)
        qe_s[j] = ((q2 * (DK ** -0.5)) * jnp.exp(cum)).astype(BF16)
        ke_s[j] = (k2 * jnp.exp(-cum)).astype(BF16)
        kd_s[j] = (k2 * jnp.exp(last - cum)).astype(BF16)

    def kv_body(s, carry):
        rows = pl.ds(pl.multiple_of(s * SB, SB), SB)
        for j in range(2):
            v_t = jnp.transpose(v_ref[rows, j * DV:(j + 1) * DV].astype(F32)).astype(BF16)
            zero = jnp.zeros_like(v_t)
            v_st = jnp.concatenate([jnp.where(col_chunk == c, v_t, zero) for c in range(CPB)], axis=0)
            kv = _dot(v_st, kd_s[j, rows, :])
            for c in range(CPB):
                kv_s[j, s * CPB + c] = kv[c * LANES:(c + 1) * LANES, :]
        return carry

    lax.fori_loop(0, n_sb, kv_body, 0, unroll=min(2, n_sb))

    def dec_row(j, c):
        return dec_s[c][:, j * LANES:(j + 1) * LANES]

    def step_state(j, c, st):
        return st * dec_row(j, c) + kv_s[j, c]

    for q in range(n_in_step):
        st0 = tuple(jnp.transpose(jnp.concatenate([s0f_ref[q, j], s0b_ref[q, j]], axis=0)) for j in range(2))
        c0 = q * ch_per_seq
        s0 = q * sb_per_seq

        def bwd_body(t, sts, c0=c0):
            c = c0 + ch_per_seq - 1 - t
            for j in range(2):
                sbs_s[j, c] = sts[j]
            return tuple(step_state(j, c, sts[j]) for j in range(2))

        st_b_fin = lax.fori_loop(0, ch_per_seq, bwd_body, st0, unroll=CPB)

        def fwd_body(sl, sts, s0=s0):
            s = s0 + sl
            rows = pl.ds(pl.multiple_of(s * SB, SB), SB)
            new = []
            for j in range(2):
                lo = j * DV
                qe = qe_s[j, rows, :]
                ke = ke_s[j, rows, :]
                v_b = v_ref[rows, lo:lo + DV].astype(BF16)
                zero = jnp.zeros_like(qe)
                st = sts[j]
                q_parts, s_parts = [], []
                for c in range(CPB):
                    ci = s * CPB + c
                    s_parts.append(jnp.where(m_f, st, sbs_s[j, ci]).astype(BF16))
                    q_parts.append(jnp.where(row_chunk == c, qe, zero))
                    st = step_state(j, ci, st)
                o = _dot_nt(jnp.concatenate(q_parts, axis=1), jnp.concatenate(s_parts, axis=1))
                q_st = jnp.concatenate([jnp.where(m_f, qe, zero), jnp.where(m_f, zero, qe)], axis=0)
                sc = _dot_nt(q_st, ke)
                p = jnp.where(tril, sc[0:SB, :], 0.0) + jnp.where(triu, sc[SB:2 * SB, :], 0.0)
                o = o + _dot(p.astype(BF16), v_b)
                o = o * lax.rsqrt(jnp.mean(o * o, axis=-1, keepdims=True) + EPS) * ng_ref[:, lo:lo + DV]
                o_ref[rows, lo:lo + DV] = (o * _silu(og_ref[rows, lo:lo + DV].astype(F32))).astype(BF16)
                new.append(st)
            return tuple(new)

        st_f_fin = lax.fori_loop(0, sb_per_seq, fwd_body, st0, unroll=min(2, sb_per_seq))
        for j in range(2):
            sf_ref[q, j] = jnp.transpose(st_f_fin[j])[0:DK, :]
            sb_ref[q, j] = jnp.transpose(st_b_fin[j])[DK:2 * DK, :]


def _gla(proj, r_all, wg_hi, wg_lo, bg, ng, tlu, s0f, s0b, *, seq_len, n_seq, n_in_step, row0):
    kern = functools.partial(_gla_kernel, seq_len=seq_len, n_in_step=n_in_step)
    rows = n_in_step * seq_len
    n_chunks = rows // CHUNK
    rb = lambda b: b + row0 // rows
    st_spec = pl.BlockSpec((n_in_step, 2, DK, DV), lambda b, h: (b, h, 0, 0))
    return pl.pallas_call(
        kern,
        grid=(n_seq // n_in_step, HEADS // 2),
        in_specs=[pl.BlockSpec((rows, LANES), lambda b, h: (rb(b), h)),
                  pl.BlockSpec((rows, LANES), lambda b, h: (rb(b), QK_W // LANES + h)),
                  pl.BlockSpec((rows, 2 * DV), lambda b, h: (rb(b), 2 * QK_W // (2 * DV) + h)),
                  pl.BlockSpec((rows, 2 * DV), lambda b, h: (rb(b), (2 * QK_W + V_W) // (2 * DV) + h)),
                  pl.BlockSpec((rows, LANES), lambda b, h: (rb(b), 0)),
                  pl.BlockSpec((None, LANES, 2 * LANES), lambda b, h: (h, 0, 0)),
                  pl.BlockSpec((None, LANES, 2 * LANES), lambda b, h: (h, 0, 0)),
                  pl.BlockSpec((None, 1, 2 * LANES), lambda b, h: (h, 0, 0)),
                  pl.BlockSpec((None, 1, 2 * DV), lambda b, h: (h, 0, 0)),
                  pl.BlockSpec((SB, 2 * SB), lambda b, h: (0, 0)),
                  st_spec, st_spec],
        out_specs=[pl.BlockSpec((rows, 2 * DV), lambda b, h: (b, h)), st_spec, st_spec],
        out_shape=[jax.ShapeDtypeStruct((n_seq * seq_len, V_W), BF16),
                   jax.ShapeDtypeStruct((n_seq, HEADS, DK, DV), F32),
                   jax.ShapeDtypeStruct((n_seq, HEADS, DK, DV), F32)],
        scratch_shapes=[pltpu.VMEM((rows, 2 * LANES), F32),
                        pltpu.VMEM((rows, 2 * LANES), F32),
                        pltpu.VMEM((2, rows, LANES), BF16),
                        pltpu.VMEM((2, rows, LANES), BF16),
                        pltpu.VMEM((2, rows, LANES), BF16),
                        pltpu.VMEM((n_chunks, 1, 2 * LANES), F32),
                        pltpu.VMEM((2, n_chunks, DV, LANES), F32),
                        pltpu.VMEM((2, n_chunks, DV, LANES), F32)],
        compiler_params=_cparams(("arbitrary", "arbitrary")),
        name="gla_%d" % seq_len,
    )(proj, proj, proj, proj, r_all, wg_hi, wg_lo, bg, ng, tlu, s0f, s0b)


def _fnet_stage_a(u_bf, cs):
    cparts, sparts = [], []
    for g in range(FN_G):
        ab = _dot(u_bf[:, g * FN_C:(g + 1) * FN_C], cs)
        cparts.append(ab[:, 0:FN_C])
        sparts.append(ab[:, FN_C:2 * FN_C])
    return jnp.concatenate(cparts, axis=1), jnp.concatenate(sparts, axis=1)


def _fnet_ctx_kernel(u_ref, cs_ref, p2_ref, f_ref):
    uc, us = _fnet_stage_a(u_ref[...].astype(BF16), cs_ref[...].astype(BF16))
    ab = jnp.concatenate([uc, us], axis=0).astype(BF16)
    f_ref[...] = _dot(p2_ref[...].astype(BF16), ab).astype(BF16)


def _fnet_ctx(proj, cs, p2):
    return pl.pallas_call(
        _fnet_ctx_kernel,
        grid=(N_CTX,),
        in_specs=[pl.BlockSpec((L_CTX, FN_G * FN_C), lambda b: (b, 3)),
                  pl.BlockSpec((FN_C, 2 * FN_C), lambda b: (0, 0)),
                  pl.BlockSpec((L_CTX, 2 * L_CTX), lambda b: (0, 0))],
        out_specs=pl.BlockSpec((L_CTX, FN_G * FN_C), lambda b: (b, 0)),
        out_shape=jax.ShapeDtypeStruct((T_CTX, FN_G * FN_C), BF16),
        compiler_params=_cparams(("arbitrary",)),
        name="fnet_ctx",
    )(proj, cs, p2)


TM_FL = 256
RT_FL = 256


def _fnet_lat_kernel(u_ref, cs_ref, kr_ref, f_ref, ab_s):
    m = pl.program_id(1)

    @pl.when(m == 0)
    def _():
        def body(t, carry):
            rows = pl.ds(pl.multiple_of(t * RT_FL, RT_FL), RT_FL)
            uc, us = _fnet_stage_a(u_ref[rows, :].astype(BF16), cs_ref[...].astype(BF16))
            ab_s[rows, :] = uc.astype(BF16)
            ab_s[pl.ds(pl.multiple_of(L_LAT + t * RT_FL, RT_FL), RT_FL), :] = us.astype(BF16)
            return carry
        lax.fori_loop(0, L_LAT // RT_FL, body, 0)

    f_ref[...] = _dot(kr_ref[...].astype(BF16), ab_s[...]).astype(BF16)


def _fnet_lat(proj, cs, kr):
    nm = L_LAT // TM_FL
    return pl.pallas_call(
        _fnet_lat_kernel,
        grid=(N_LAT, nm),
        in_specs=[pl.BlockSpec((L_LAT, FN_G * FN_C), lambda b, m: (T_CTX // L_LAT + b, 3)),
                  pl.BlockSpec((FN_C, 2 * FN_C), lambda b, m: (0, 0)),
                  pl.BlockSpec((TM_FL, 2 * L_LAT), lambda b, m: (m, 0))],
        out_specs=pl.BlockSpec((TM_FL, FN_G * FN_C), lambda b, m: (b * nm + m, 0)),
        out_shape=jax.ShapeDtypeStruct((T_LAT, FN_G * FN_C), BF16),
        scratch_shapes=[pltpu.VMEM((2 * L_LAT, FN_G * FN_C), BF16)],
        compiler_params=_cparams(("arbitrary", "arbitrary")),
        name="fnet_lat",
    )(proj, cs, kr)


TM_OUT = 256
LANE_E0 = N_GROUPS
ROWS_PER_BLK = 8
PACK_ROWS = -(-(2 * TM_OUT + N_EXP * (ROWS_PER_BLK - 1)) // 256) * 256
PACK_BLKS = PACK_ROWS // ROWS_PER_BLK
N_TOK_TILES = T_ALL // TM_OUT
BLK_PER_TILE = TM_MOE // ROWS_PER_BLK
USED_BLKS = (2 * TM_OUT + N_EXP * (ROWS_PER_BLK - 1)) // ROWS_PER_BLK
HS_ROWS = N_TOK_TILES * PACK_ROWS
assert (PACK_BLKS - USED_BLKS) * N_TOK_TILES >= 2 * BLK_PER_TILE


def _outproj_kernel(oc_ref, ol_ref, fc_ref, fl_ref, xp_ref, xs_ref, mod_ref, g_ref, wo_ref, wf_ref,
                    wrc_ref, br_ref, sut_ref, sl_ref,
                    x1_ref, hs_ref, rw_ref, nb_ref, lb_ref):
    i = pl.program_id(0)
    is_ctx = i < T_CTX // TM_OUT

    o = jnp.where(is_ctx, oc_ref[...], ol_ref[...]).astype(BF16)
    f = jnp.where(is_ctx, fc_ref[...], fl_ref[...]).astype(BF16)
    x = jnp.where(is_ctx, xp_ref[...], xs_ref[...])
    y = _dot(o, wo_ref[...]) + _dot(f, wf_ref[...])
    ga1 = mod_ref[:, 2 * D_MODEL:3 * D_MODEL]
    sh2 = mod_ref[:, 3 * D_MODEL:4 * D_MODEL]
    sc2 = mod_ref[:, 4 * D_MODEL:5 * D_MODEL]
    x1 = x + ga1 * y
    x1_ref[...] = x1
    h2 = _rms(x1, g_ref[...]) * (1.0 + sc2) + sh2

    h_hi, h_lo = _split2(h2)
    hw = _dot(h_hi, wrc_ref[...])
    lg_all = jnp.transpose(hw[:, 0:LANES] + _dot(h_lo, wrc_ref[:, 0:LANES]) + hw[:, LANES:2 * LANES]
                           + br_ref[...])

    row_i = lax.broadcasted_iota(I32, (LANES, TM_OUT), 0)
    row = row_i.astype(F32)
    neg = jnp.float32(-jnp.inf)
    big = jnp.float32(LANES)
    lg = jnp.where(row_i < N_GROUPS, lg_all, neg)
    gmax = jnp.max(lg, axis=0, keepdims=True)
    gsel = jnp.min(jnp.where(lg == gmax, row, big), axis=0, keepdims=True)
    den = jnp.sum(jnp.exp(lg - gmax), axis=0, keepdims=True)
    pg_sel = 1.0 / den

    e_idx = row_i - LANE_E0
    egrp = (e_idx >> 3).astype(F32)
    emask = (e_idx >= 0) & (e_idx < N_EXP) & (egrp == gsel)
    m1 = jnp.where(emask, lg_all, neg)
    v1 = jnp.max(m1, axis=0, keepdims=True)
    i1 = jnp.min(jnp.where(m1 == v1, row, big), axis=0, keepdims=True)
    m2 = jnp.where(row == i1, neg, m1)
    v2 = jnp.max(m2, axis=0, keepdims=True)
    i2 = jnp.min(jnp.where(m2 == v2, row, big), axis=0, keepdims=True)
    e2 = jnp.exp(v2 - v1)
    inv = 1.0 / (1.0 + e2)
    w1 = inv * pg_sel
    w2 = (e2 * inv) * pg_sel

    oh1 = row == i1
    oh2 = row == i2
    oh = jnp.where(oh1 | oh2, 1.0, 0.0).astype(BF16)
    cnt = _dot(oh, jnp.ones((TM_OUT, LANES), BF16))
    nblk = jnp.floor((cnt + (ROWS_PER_BLK - 1)) * (1.0 / ROWS_PER_BLK))
    lboff = _dot(sl_ref[...], nblk.astype(BF16))
    lrank = _dot(oh, sut_ref[...])
    posmat = jnp.concatenate([lboff, lboff], axis=1) * ROWS_PER_BLK + lrank
    p1 = jnp.sum(jnp.where(oh1, posmat, 0.0), axis=0, keepdims=True)
    p2 = jnp.sum(jnp.where(oh2, posmat, 0.0), axis=0, keepdims=True)
    nb_ref[...] = nblk[:, 0:8].astype(I32)
    lb_ref[...] = lboff[:, 0:8].astype(I32)

    prow = lax.broadcasted_iota(I32, (PACK_ROWS, TM_OUT), 0).astype(F32)
    place = jnp.where((prow == p1) | (prow == p2), 1.0, 0.0).astype(BF16)
    hs_ref[...] = _dot(place, h_hi).astype(BF16)

    rw_ref[...] = jnp.concatenate([w1, w2, p1, p2, jnp.zeros((4, TM_OUT), F32)], axis=0)


def _outproj(o_ctx, o_lat, f_ctx, f_lat, xp, xs, mod3, g_ffn, wo, wf, wr_cat, br, sut, sl):
    nt = T_ALL // TM_OUT
    nctx = T_CTX // TM_OUT
    ctx_map = lambda i: (jnp.minimum(i, nctx - 1), 0)
    lat_map = lambda i: (jnp.maximum(i - nctx, 0), 0)
    const = lambda i: (0, 0)
    return pl.pallas_call(
        _outproj_kernel,
        grid=(nt,),
        in_specs=[pl.BlockSpec((TM_OUT, V_W), ctx_map),
                  pl.BlockSpec((TM_OUT, V_W), lat_map),
                  pl.BlockSpec((TM_OUT, FN_G * FN_C), ctx_map),
                  pl.BlockSpec((TM_OUT, FN_G * FN_C), lat_map),
                  pl.BlockSpec((TM_OUT, D_MODEL), ctx_map),
                  pl.BlockSpec((TM_OUT, D_MODEL), lat_map),
                  pl.BlockSpec((None, 1, 6 * D_MODEL), lambda i: (_cond_row(i, TM_OUT), 0, 0)),
                  pl.BlockSpec((1, D_MODEL), const),
                  pl.BlockSpec((V_W, D_MODEL), const),
                  pl.BlockSpec((FN_G * FN_C, D_MODEL), const),
                  pl.BlockSpec((D_MODEL, 2 * LANES), const),
                  pl.BlockSpec((1, LANES), const),
                  pl.BlockSpec((TM_OUT, TM_OUT), const),
                  pl.BlockSpec((LANES, LANES), const)],
        out_specs=[pl.BlockSpec((TM_OUT, D_MODEL), lambda i: (i, 0)),
                   pl.BlockSpec((PACK_ROWS, D_MODEL), lambda i: (i, 0)),
                   pl.BlockSpec((8, TM_OUT), lambda i: (0, i)),
                   pl.BlockSpec((None, LANES, 8), lambda i: (i, 0, 0)),
                   pl.BlockSpec((None, LANES, 8), lambda i: (i, 0, 0))],
        out_shape=[jax.ShapeDtypeStruct((T_ALL, D_MODEL), F32),
                   jax.ShapeDtypeStruct((HS_ROWS, D_MODEL), BF16),
                   jax.ShapeDtypeStruct((8, T_ALL), F32),
                   jax.ShapeDtypeStruct((nt, LANES, 8), I32),
                   jax.ShapeDtypeStruct((nt, LANES, 8), I32)],
        compiler_params=_cparams(("arbitrary",)),
        name="outproj",
    )(o_ctx, o_lat, f_ctx, f_lat, xp, xs, mod3, g_ffn, wo, wf, wr_cat, br, sut, sl)


SRC_BITS = 16
SRC_MASK = (1 << SRC_BITS) - 1
X_SLOTS = 3
Y_SLOTS = 3
N_UP_CHUNKS = 2
N_DN_CHUNKS = 8
NT_MOE = (2 * T_ALL + N_TOK_TILES * N_EXP * (ROWS_PER_BLK - 1)) // TM_MOE + N_EXP


def _moe_kernel(texp_ref, nexp_ref, meta_ref, code_ref,
                h_hbm, wg_hbm, wu_hbm, wd_hbm,
                out_hbm,
                xbuf, ybuf, wg_f, wu_f, wd_f, wg_s, wu_s, wd_s, kcount, gsem, ssem, wsem):
    i = pl.program_id(0)
    nt = meta_ref[0]
    xs = i % X_SLOTS

    def blk_rows(b):
        if isinstance(b, int):
            return pl.ds(b * ROWS_PER_BLK, ROWS_PER_BLK)
        return pl.ds(pl.multiple_of(b * ROWS_PER_BLK, ROWS_PER_BLK), ROWS_PER_BLK)

    def gather_row(tile, sl, j):
        src = code_ref[(tile + 2) * BLK_PER_TILE + j] & SRC_MASK
        pltpu.make_async_copy(h_hbm.at[blk_rows(src)], xbuf.at[sl, blk_rows(j)], gsem.at[sl]).start()

    def scatter_row(tile, sl, j):
        dst = code_ref[(tile + 2) * BLK_PER_TILE + j] >> SRC_BITS
        pltpu.make_async_copy(ybuf.at[sl, blk_rows(j)], out_hbm.at[blk_rows(dst)], ssem.at[sl]).start(priority=1)

    def gather_wait(sl):
        pltpu.make_async_copy(h_hbm.at[pl.ds(0, TM_MOE)], xbuf.at[sl], gsem.at[sl]).wait()

    def scatter_wait(sl):
        pltpu.make_async_copy(ybuf.at[sl], out_hbm.at[pl.ds(0, TM_MOE)], ssem.at[sl]).wait()

    def y_slot(tile):
        return (tile + 1) % Y_SLOTS

    @pl.when(i == 0)
    def _():
        ybuf[y_slot(-2)] = jnp.zeros((TM_MOE, D_MODEL), BF16)
        ybuf[y_slot(-1)] = jnp.zeros((TM_MOE, D_MODEL), BF16)

        def body(j, c):
            gather_row(0, 0, j)
            gather_row(1, 1, j)
            scatter_row(-2, y_slot(-2), j)
            return c
        lax.fori_loop(0, BLK_PER_TILE, body, 0)

    @pl.when((i >= 1) & (i <= nt))
    def _():
        scatter_wait(y_slot(i - 3))

    @pl.when(i < nt)
    def _():
        prev = texp_ref[jnp.maximum(i - 1, 0)]

        def weight_copies(e, sl):
            return (pltpu.make_async_copy(wg_hbm.at[e], wg_f.at[sl], wsem.at[sl]),
                    pltpu.make_async_copy(wu_hbm.at[e], wu_f.at[sl], wsem.at[sl]),
                    pltpu.make_async_copy(wd_hbm.at[e], wd_f.at[sl], wsem.at[sl]))

        @pl.when(i == 0)
        def _():
            kcount[0] = 0
            for cp in weight_copies(texp_ref[0], 0):
                cp.start()

        @pl.when((i == 0) | (texp_ref[i] != prev))
        def _():
            k = kcount[0]
            for sl in range(2):
                @pl.when(k % 2 == sl)
                def _(sl=sl):
                    for cp in weight_copies(texp_ref[i], sl):
                        cp.wait()
                    def narrow(r, c):
                        up = pl.ds(pl.multiple_of(r * (D_MODEL // 16), D_MODEL // 16), D_MODEL // 16)
                        dn = pl.ds(pl.multiple_of(r * (D_EXP // 16), D_EXP // 16), D_EXP // 16)
                        wg_s[up, :] = wg_f[sl, up, :].astype(BF16)
                        wu_s[up, :] = wu_f[sl, up, :].astype(BF16)
                        wd_s[dn, :] = wd_f[sl, dn, :].astype(BF16)
                        return c
                    lax.fori_loop(0, 16, narrow, 0)

                    @pl.when(nexp_ref[i] >= 0)
                    def _():
                        for cp in weight_copies(nexp_ref[i], 1 - sl):
                            cp.start()
            kcount[0] = k + 1

        gather_wait(xs)
        x = xbuf[xs].astype(BF16)

        issues = []
        for j in range(BLK_PER_TILE):
            issues.append(functools.partial(gather_row, i + 2, (i + 2) % X_SLOTS, j))
            issues.append(functools.partial(scatter_row, i - 1, y_slot(i - 1), j))
        n_groups = N_UP_CHUNKS + N_DN_CHUNKS
        per_group = -(-len(issues) // n_groups)

        def issue_group(k):
            for fn in issues[k * per_group:(k + 1) * per_group]:
                fn()

        wu_c = D_EXP // N_UP_CHUNKS
        hid = []
        for n in range(N_UP_CHUNKS):
            issue_group(n)
            g = _dot(x, wg_s[:, n * wu_c:(n + 1) * wu_c])
            u = _dot(x, wu_s[:, n * wu_c:(n + 1) * wu_c])
            hid.append((_silu(g) * u).astype(BF16))
        hid = jnp.concatenate(hid, axis=1)
        wd_c = D_MODEL // N_DN_CHUNKS
        ys = y_slot(i)
        for n in range(N_DN_CHUNKS):
            issue_group(N_UP_CHUNKS + n)
            ybuf[ys, :, n * wd_c:(n + 1) * wd_c] = _dot(hid, wd_s[:, n * wd_c:(n + 1) * wd_c]).astype(BF16)

    @pl.when(i == nt)
    def _():
        gather_wait(xs)
        gather_wait((i + 1) % X_SLOTS)

        def body(j, c):
            scatter_row(nt - 1, y_slot(nt - 1), j)
            return c
        lax.fori_loop(0, BLK_PER_TILE, body, 0)
        scatter_wait(y_slot(nt - 2))
        scatter_wait(y_slot(nt - 1))


def _moe(texp, nexp, meta, code, hs, w_eg, w_eu, w_ed):
    hbm = pl.BlockSpec(memory_space=pl.ANY)
    grid_spec = pltpu.PrefetchScalarGridSpec(
        num_scalar_prefetch=4,
        grid=(NT_MOE + 1,),
        in_specs=[hbm, hbm, hbm, hbm],
        out_specs=hbm,
        scratch_shapes=[pltpu.VMEM((X_SLOTS, TM_MOE, D_MODEL), BF16),
                        pltpu.VMEM((Y_SLOTS, TM_MOE, D_MODEL), BF16),
                        pltpu.VMEM((2, D_MODEL, D_EXP), F32),
                        pltpu.VMEM((2, D_MODEL, D_EXP), F32),
                        pltpu.VMEM((2, D_EXP, D_MODEL), F32),
                        pltpu.VMEM((D_MODEL, D_EXP), BF16),
                        pltpu.VMEM((D_MODEL, D_EXP), BF16),
                        pltpu.VMEM((D_EXP, D_MODEL), BF16),
                        pltpu.SMEM((1,), I32),
                        pltpu.SemaphoreType.DMA((X_SLOTS,)),
                        pltpu.SemaphoreType.DMA((Y_SLOTS,)),
                        pltpu.SemaphoreType.DMA((2,))])
    return pl.pallas_call(
        _moe_kernel,
        grid_spec=grid_spec,
        out_shape=jax.ShapeDtypeStruct((HS_ROWS, D_MODEL), BF16),
        input_output_aliases={4: 0},
        compiler_params=_cparams(("arbitrary",)),
        name="moe",
    )(texp, nexp, meta, code, hs, w_eg, w_eu, w_ed)


TM_FIN = TM_OUT


def _final_kernel(x1_ref, ys_pack_ref, rw_ref, mod_ref, g_ref, yp_ref, ys_ref):
    i = pl.program_id(0)
    ga2 = mod_ref[:, 5 * D_MODEL:6 * D_MODEL]
    w0 = rw_ref[0:1, :]
    w1 = rw_ref[1:2, :]
    p0 = rw_ref[2:3, :]
    p1 = rw_ref[3:4, :]
    prow = lax.broadcasted_iota(I32, (PACK_ROWS, TM_FIN), 0).astype(F32)
    comb_t = jnp.where(prow == p0, w0, 0.0) + jnp.where(prow == p1, w1, 0.0)
    y_moe = lax.dot_general(comb_t.astype(BF16), ys_pack_ref[...].astype(BF16), (((0,), (0,)), ((), ())),
                            preferred_element_type=F32)
    y = x1_ref[...] + ga2 * y_moe
    out = _rms(y, g_ref[...])

    @pl.when(i < T_CTX // TM_FIN)
    def _():
        yp_ref[...] = out

    @pl.when(i >= T_CTX // TM_FIN)
    def _():
        ys_ref[...] = out


def _final(x1, y2, rw, mod3, g_fin):
    nt = T_ALL // TM_FIN
    nctx = T_CTX // TM_FIN
    return pl.pallas_call(
        _final_kernel,
        grid=(nt,),
        in_specs=[pl.BlockSpec((TM_FIN, D_MODEL), lambda i: (i, 0)),
                  pl.BlockSpec((PACK_ROWS, D_MODEL), lambda i: (i, 0)),
                  pl.BlockSpec((8, TM_FIN), lambda i: (0, i)),
                  pl.BlockSpec((None, 1, 6 * D_MODEL), lambda i: (_cond_row(i, TM_FIN), 0, 0)),
                  pl.BlockSpec((1, D_MODEL), lambda i: (0, 0))],
        out_specs=[pl.BlockSpec((TM_FIN, D_MODEL), lambda i: (jnp.minimum(i, nctx - 1), 0)),
                   pl.BlockSpec((TM_FIN, D_MODEL), lambda i: (jnp.maximum(i - nctx, 0), 0))],
        out_shape=[jax.ShapeDtypeStruct((T_CTX, D_MODEL), F32),
                   jax.ShapeDtypeStruct((T_LAT, D_MODEL), F32)],
        compiler_params=_cparams(("arbitrary",)),
        name="final",
    )(x1, y2, rw, mod3, g_fin)


def _np_bf16(a):
    return jnp.asarray(np.asarray(a, np.float32), dtype=BF16)


def _np_f32(a):
    return jnp.asarray(np.asarray(a, np.float32))


@functools.lru_cache(maxsize=None)
def _constants():
    c = {}
    k = np.arange(FN_C)
    ang = 2.0 * np.pi * np.outer(k, k) / FN_C
    c["cs"] = np.concatenate([np.cos(ang), np.sin(ang)], axis=1) / np.sqrt(FN_C)
    p = np.arange(L_CTX)
    ang = 2.0 * np.pi * np.outer(p, p) / L_CTX
    c["p2"] = np.concatenate([np.cos(ang), -np.sin(ang)], axis=1) / np.sqrt(L_CTX)
    pos = np.arange(L_LAT)
    rr, cc = pos // GRID_W, pos % GRID_W
    num = (np.outer(rr, rr) * (GRID_W // GRID_H) + np.outer(cc, cc)) % GRID_W
    ang = 2.0 * np.pi * num / GRID_W
    c["kr"] = np.concatenate([np.cos(ang), -np.sin(ang)], axis=1) / np.sqrt(L_LAT)
    i = np.arange(SB)
    same = (i[:, None] // CHUNK) == (i[None, :] // CHUNK)
    tl = same & (i[:, None] >= i[None, :])
    tu = same & (i[:, None] <= i[None, :])
    c["tlu"] = np.concatenate([tl, tu], axis=1).astype(np.float32)
    c["sut"] = (i[:, None] < i[None, :]).astype(np.float32)
    k = np.arange(LANES)
    c["sl"] = (k[:, None] > k[None, :]).astype(np.float32)
    return c


def kernel(x_prompt, x_sample, state_gla_fwd, state_gla_bwd, c, c_ctx, w_ada, b_ada, norm_attn, norm_ffn, w_in, w_gate_fwd, b_gate_fwd, w_gate_bwd, b_gate_bwd, norm_gla, w_out, w_router_group, b_router_group, w_router_expert, b_router_expert, w_expert_gate, w_expert_up, w_expert_down, norm_final):
    assert w_ada.shape[0] == 1, "single layer"
    cst = _constants()
    cs, p2, kr = _np_f32(cst["cs"]), _np_f32(cst["p2"]), _np_f32(cst["kr"])
    tlu, sut, sl = _np_bf16(cst["tlu"]), _np_bf16(cst["sut"]), _np_bf16(cst["sl"])

    xp = x_prompt.reshape(T_CTX, D_MODEL)
    xs = x_sample.reshape(T_LAT, D_MODEL)

    cond8 = jnp.concatenate([c_ctx[None, :], c, jnp.zeros((3, D_MODEL), F32)], axis=0)
    mod = _ada(cond8, w_ada[0], b_ada[0][None, :])
    mod3 = mod.reshape(8, 1, 6 * D_MODEL)

    wi = w_in[0]
    i_og = 2 * QK_W + 2 * V_W
    i_u = i_og + 2 * RANK
    w_a = wi[:, :i_og].astype(BF16)
    w_u = wi[:, i_u:].astype(BF16)
    w_r = jnp.pad(wi[:, i_og:i_u], ((0, 0), (0, LANES - 2 * RANK))).astype(BF16)

    wgf = w_gate_fwd[0].reshape(RANK, HEADS, DK)
    wgb = w_gate_bwd[0].reshape(RANK, HEADS, DK)
    zf = jnp.zeros_like(wgf)
    top = jnp.stack([wgf, zf], axis=2)
    bot = jnp.stack([zf, wgb], axis=2)
    wg = jnp.concatenate([top, bot], axis=0)
    wg = wg.reshape(2 * RANK, HEADS // 2, 4 * DK).transpose(1, 0, 2)
    wg = jnp.pad(wg, ((0, 0), (0, LANES - 2 * RANK), (0, 0)))
    wg_hi = wg.astype(BF16)
    wg_lo = (wg - wg_hi.astype(F32)).astype(BF16)
    bg = jnp.stack([b_gate_fwd[0].reshape(HEADS, DK), b_gate_bwd[0].reshape(HEADS, DK)], axis=1)
    bg = bg.reshape(HEADS // 2, 1, 4 * DK)
    ng = norm_gla[0].reshape(HEADS // 2, 1, 2 * DV)

    proj, r_all = _inproj(xp, xs, mod3, norm_attn, w_a, w_u, w_r)

    zero_state = jnp.zeros((N_CTX, HEADS, DK, DV), F32)
    o_ctx, sf_ctx, sb_ctx = _gla(proj, r_all, wg_hi, wg_lo, bg, ng, tlu, zero_state, zero_state,
                                 seq_len=L_CTX, n_seq=N_CTX, n_in_step=4, row0=0)
    o_lat, _, _ = _gla(proj, r_all, wg_hi, wg_lo, bg, ng, tlu,
                       state_gla_fwd[:, 0], state_gla_bwd[:, 0],
                       seq_len=L_LAT, n_seq=N_LAT, n_in_step=1, row0=T_CTX)

    f_ctx = _fnet_ctx(proj, cs, p2)
    f_lat = _fnet_lat(proj, cs, kr)

    wo = w_out[0][:V_W].astype(BF16)
    wf = w_out[0][V_W:].astype(BF16)
    wr = jnp.concatenate([w_router_group[0], w_router_expert[0]], axis=1)
    wr = jnp.pad(wr, ((0, 0), (0, LANES - N_GROUPS - N_EXP)))
    wr_hi = wr.astype(BF16)
    wr_lo = (wr - wr_hi.astype(F32)).astype(BF16)
    wr_cat = jnp.concatenate([wr_hi, wr_lo], axis=1)
    br = jnp.pad(jnp.concatenate([b_router_group[0], b_router_expert[0]]), (0, LANES - N_GROUPS - N_EXP))[None, :]

    x1, hs, rw, nb, lb = _outproj(o_ctx, o_lat, f_ctx, f_lat, xp, xs, mod3, norm_ffn, wo, wf,
                                  wr_cat, br, sut, sl)

    nb_e = nb[:, LANE_E0:LANE_E0 + N_EXP, 0].T
    lb_e = lb[:, LANE_E0:LANE_E0 + N_EXP, 0].T
    run_end = jnp.cumsum(nb_e, axis=1)
    blocks_e = run_end[:, -1]
    tiles_e = (blocks_e + BLK_PER_TILE - 1) // BLK_PER_TILE
    tile_end = jnp.cumsum(tiles_e)
    tile_start = tile_end - tiles_e
    n_tiles = tile_end[-1]
    n_code_tiles = NT_MOE + 4
    tile = jnp.arange(n_code_tiles, dtype=I32) - 2
    tile_c = jnp.clip(tile, 0, n_tiles - 1)
    t_exp = jnp.sum(tile_c[:, None] >= tile_end[None, :], axis=1)

    pick = t_exp[:, None] == jnp.arange(N_EXP, dtype=I32)[None, :]

    def per_tile(table):
        if table.ndim == 1:
            return jnp.sum(jnp.where(pick, table[None, :], 0), axis=1)
        return jnp.sum(jnp.where(pick[:, :, None], table[None, :, :], 0), axis=1)

    ends = per_tile(run_end)
    starts = ends - per_tile(nb_e)
    offs = per_tile(lb_e) + jnp.arange(N_TOK_TILES, dtype=I32)[None, :] * PACK_BLKS - starts
    j = jnp.arange(BLK_PER_TILE, dtype=I32)
    bi = ((tile_c - per_tile(tile_start)) * BLK_PER_TILE)[:, None] + j[None, :]
    in_run = (starts.T[:, :, None] <= bi[None, :, :]) & (bi[None, :, :] < ends.T[:, :, None])
    blk = bi + jnp.sum(jnp.where(in_run, offs.T[:, :, None], 0), axis=0)
    valid = (tile == tile_c)[:, None] & (bi < per_tile(blocks_e)[:, None])
    spare_ix = (jnp.arange(n_code_tiles, dtype=I32) % 2)[:, None] * BLK_PER_TILE + j[None, :]
    spare = (spare_ix % N_TOK_TILES) * PACK_BLKS + USED_BLKS + spare_ix // N_TOK_TILES
    code = jnp.where(valid, (blk << SRC_BITS) | blk, (spare << SRC_BITS) | blk[:, 0:1]).astype(I32).reshape(-1)
    texp = t_exp[2:NT_MOE + 3].astype(I32)
    e_ix = jnp.arange(N_EXP, dtype=I32)
    later = (e_ix[None, :] > e_ix[:, None]) & (tiles_e[None, :] > 0)
    next_e = jnp.min(jnp.where(later, e_ix[None, :], N_EXP), axis=1)
    next_e = jnp.where(next_e < N_EXP, next_e, -1)
    nexp = per_tile(next_e)[2:NT_MOE + 3].astype(I32)
    meta = n_tiles.reshape(1).astype(I32)

    y2 = _moe(texp, nexp, meta, code, hs, w_expert_gate[0], w_expert_up[0], w_expert_down[0])
    y_prompt, y_sample = _final(x1, y2, rw, mod3, norm_final[None, :])

    st_shape = (N_CTX, 1, HEADS, DK, DV)
    return (y_prompt.reshape(N_CTX, L_CTX, D_MODEL), y_sample.reshape(N_LAT, L_LAT, D_MODEL),
            sf_ctx.reshape(st_shape), sb_ctx.reshape(st_shape))
```

```python
import functools

import numpy as np
import jax
import jax.numpy as jnp
from jax import lax
from jax.experimental import pallas as pl
from jax.experimental.pallas import tpu as pltpu

F32 = jnp.float32
BF16 = jnp.bfloat16
I32 = jnp.int32

D_MODEL = 2048
N_CTX = 32
L_CTX = 256
N_LAT = 4
L_LAT = 2048
GRID_H = 32
GRID_W = 64
T_CTX = N_CTX * L_CTX
T_LAT = N_LAT * L_LAT
T_ALL = T_CTX + T_LAT
HEADS = 8
DK = 64
DV = 128
RANK = 16
TAU = 16.0
CHUNK = 64
FN_G = 8
FN_C = 128
QK_W = HEADS * DK
V_W = HEADS * DV
N_GROUPS = 4
EPG = 8
N_EXP = N_GROUPS * EPG
D_EXP = 512
EPS = 1e-6

LANES = 128
VMEM_LIMIT = 56 * 1024 * 1024

TM_MOE = 256


def _dot(a, b):
    return jnp.dot(a, b, preferred_element_type=F32)


def _dot_nt(a, b):
    return lax.dot_general(a, b, (((1,), (1,)), ((), ())), preferred_element_type=F32)


def _split2(x):
    hi = x.astype(BF16)
    lo = (x - hi.astype(F32)).astype(BF16)
    return hi, lo


def _silu(x):
    return x * (1.0 / (1.0 + jnp.exp(-x)))


def _rms(x, g):
    return x * lax.rsqrt(jnp.mean(x * x, axis=-1, keepdims=True) + EPS) * g


def _cparams(sem):
    return pltpu.CompilerParams(dimension_semantics=sem, vmem_limit_bytes=VMEM_LIMIT)


def _ada_kernel(c_ref, w_ref, b_ref, o_ref):
    s_hi, s_lo = _split2(_silu(c_ref[...]))
    w = w_ref[...]
    w_hi = w.astype(BF16)
    w_lo = (w - w_hi.astype(F32)).astype(BF16)
    o_ref[...] = _dot(s_hi, w_hi) + _dot(s_lo, w_hi) + _dot(s_hi, w_lo) + b_ref[...]


def _ada(cond8, w_ada, b_ada):
    tn = 1024
    n6 = 6 * D_MODEL
    return pl.pallas_call(
        _ada_kernel,
        grid=(n6 // tn,),
        in_specs=[pl.BlockSpec((8, D_MODEL), lambda j: (0, 0)),
                  pl.BlockSpec((D_MODEL, tn), lambda j: (0, j)),
                  pl.BlockSpec((1, tn), lambda j: (0, j))],
        out_specs=pl.BlockSpec((8, tn), lambda j: (0, j)),
        out_shape=jax.ShapeDtypeStruct((8, n6), F32),
        compiler_params=_cparams(("arbitrary",)),
        name="ada",
    )(cond8, w_ada, b_ada)


TM_IN = 512
TM_IN_HALF = 256
TN_IN = 1024
N_MAIN = 4096
N_QKVG = 3072


def _cond_row(tile, tm):
    ctx_tiles = T_CTX // tm
    per_seq = L_LAT // tm
    return jnp.where(tile < ctx_tiles, 0, 1 + (jnp.maximum(tile - ctx_tiles, 0)) // per_seq)


def _inproj_kernel(xp_ref, xs_ref, mod_ref, g_ref, wa_ref, wu_ref, wr_ref, proj_ref, r_ref):
    i = pl.program_id(0)
    is_ctx = i < T_CTX // TM_IN
    sh1 = mod_ref[:, 0:D_MODEL]
    sc1 = mod_ref[:, D_MODEL:2 * D_MODEL]
    for hf in range(TM_IN // TM_IN_HALF):
        rows = slice(hf * TM_IN_HALF, (hf + 1) * TM_IN_HALF)
        x = jnp.where(is_ctx, xp_ref[rows, :], xs_ref[rows, :])
        hb = (_rms(x, g_ref[...]) * (1.0 + sc1) + sh1).astype(BF16)
        r_ref[rows, :] = _dot(hb, wr_ref[...])
        for n in range(N_QKVG // TN_IN):
            cols = slice(n * TN_IN, (n + 1) * TN_IN)
            proj_ref[rows, cols] = _dot(hb, wa_ref[:, cols]).astype(BF16)
        for n in range((N_MAIN - N_QKVG) // TN_IN):
            cols = slice(N_QKVG + n * TN_IN, N_QKVG + (n + 1) * TN_IN)
            proj_ref[rows, cols] = _dot(hb, wu_ref[:, n * TN_IN:(n + 1) * TN_IN]).astype(BF16)


def _inproj(xp, xs, mod3, g_attn, w_a, w_u, w_r):
    nt = T_ALL // TM_IN
    nctx = T_CTX // TM_IN
    resident = pl.Buffered(1)
    return pl.pallas_call(
        _inproj_kernel,
        grid=(nt,),
        in_specs=[pl.BlockSpec((TM_IN, D_MODEL), lambda i: (jnp.minimum(i, nctx - 1), 0)),
                  pl.BlockSpec((TM_IN, D_MODEL), lambda i: (jnp.maximum(i - nctx, 0), 0)),
                  pl.BlockSpec((None, 1, 6 * D_MODEL), lambda i: (_cond_row(i, TM_IN), 0, 0)),
                  pl.BlockSpec((1, D_MODEL), lambda i: (0, 0)),
                  pl.BlockSpec((D_MODEL, N_QKVG), lambda i: (0, 0), pipeline_mode=resident),
                  pl.BlockSpec((D_MODEL, N_MAIN - N_QKVG), lambda i: (0, 0), pipeline_mode=resident),
                  pl.BlockSpec((D_MODEL, LANES), lambda i: (0, 0), pipeline_mode=resident)],
        out_specs=[pl.BlockSpec((TM_IN, N_MAIN), lambda i: (i, 0)),
                   pl.BlockSpec((TM_IN, LANES), lambda i: (i, 0))],
        out_shape=[jax.ShapeDtypeStruct((T_ALL, N_MAIN), BF16),
                   jax.ShapeDtypeStruct((T_ALL, LANES), F32)],
        compiler_params=_cparams(("arbitrary",)),
        name="inproj",
    )(xp, xs, mod3, g_attn, w_a, w_u, w_r)


SB = 256
CPB = SB // CHUNK


def _gla_kernel(q_ref, k_ref, v_ref, og_ref, r_ref, wgh_ref, wgl_ref, bg_ref, ng_ref,
                tlu_ref, s0f_ref, s0b_ref,
                o_ref, sf_ref, sb_ref,
                cum_s, last_s, qe_s, ke_s, kd_s, dec_s, kv_s, sbs_s, *, seq_len, n_in_step):
    sb_per_seq = seq_len // SB
    ch_per_seq = seq_len // CHUNK
    n_sb = n_in_step * sb_per_seq
    lane = lax.broadcasted_iota(I32, (1, LANES), 1)
    m_f = lane < DK

    r_hi, r_lo = _split2(r_ref[...])
    z = _dot(r_hi, wgh_ref[...]) + _dot(r_lo, wgh_ref[...]) + _dot(r_hi, wgl_ref[...]) + bg_ref[...]
    g_all = (jnp.minimum(z, 0.0) - jnp.log(1.0 + jnp.exp(-jnp.abs(z)))) * (1.0 / TAU)

    q_pair = q_ref[...].astype(F32)
    k_pair = k_ref[...].astype(F32)
    q_roll = pltpu.roll(q_pair, DK, axis=1)
    k_roll = pltpu.roll(k_pair, DK, axis=1)

    row_b = lax.broadcasted_iota(I32, (SB, SB), 0)
    col_b = lax.broadcasted_iota(I32, (SB, SB), 1)
    same_chunk = (row_b // CHUNK) == (col_b // CHUNK)
    tril = same_chunk & (row_b >= col_b)
    triu = same_chunk & (row_b <= col_b)
    row_chunk = lax.broadcasted_iota(I32, (SB, 1), 0) // CHUNK
    col_chunk = lax.broadcasted_iota(I32, (1, SB), 1) // CHUNK

    m_f2 = (lax.broadcasted_iota(I32, (1, 2 * LANES), 1) % LANES) < DK
    cum_s[...] = g_all

    def cum_body(s, carry):
        rows = pl.ds(pl.multiple_of(s * SB, SB), SB)
        g = cum_s[rows, :]
        f_hi, f_lo = _split2(jnp.where(m_f2, g, 0.0))
        b_hi, b_lo = _split2(jnp.where(m_f2, 0.0, g))
        cum = (_dot(tlu_ref[...], jnp.concatenate([f_hi, b_hi], axis=0))
               + _dot(tlu_ref[...], jnp.concatenate([f_lo, b_lo], axis=0)))
        cum_s[rows, :] = cum
        tots = []
        for c in range(CPB):
            tot = jnp.where(m_f2, cum[(c + 1) * CHUNK - 1:(c + 1) * CHUNK, :], cum[c * CHUNK:c * CHUNK + 1, :])
            dec_s[s * CPB + c] = jnp.exp(tot)
            tots.append(jnp.broadcast_to(tot, (CHUNK, 2 * LANES)))
        last_s[rows, :] = jnp.concatenate(tots, axis=0)
        return carry

    lax.fori_loop(0, n_sb, cum_body, 0)

    for j in range(2):
        cum = cum_s[:, j * LANES:(j + 1) * LANES]
        last = last_s[:, j * LANES:(j + 1) * LANES]
        if j == 0:
            q2 = jnp.where(m_f, q_pair, q_roll)
            k2 = jnp.where(m_f, k_pair, k_roll)
        else:
            q2 = jnp.where(m_f, q_roll, q_pair)
            k2 = jnp.where(m_f, k_roll, k_pair)
        qe_s[j] = ((q2 * (DK ** -0.5)) * jnp.exp(cum)).astype(BF16)
        ke_s[j] = (k2 * jnp.exp(-cum)).astype(BF16)
        kd_s[j] = (k2 * jnp.exp(last - cum)).astype(BF16)

    def kv_body(s, carry):
        rows = pl.ds(pl.multiple_of(s * SB, SB), SB)
        for j in range(2):
            v_t = jnp.transpose(v_ref[rows, j * DV:(j + 1) * DV].astype(F32)).astype(BF16)
            zero = jnp.zeros_like(v_t)
            v_st = jnp.concatenate([jnp.where(col_chunk == c, v_t, zero) for c in range(CPB)], axis=0)
            kv = _dot(v_st, kd_s[j, rows, :])
            for c in range(CPB):
                kv_s[j, s * CPB + c] = kv[c * LANES:(c + 1) * LANES, :]
        return carry

    lax.fori_loop(0, n_sb, kv_body, 0, unroll=min(2, n_sb))

    def dec_row(j, c):
        return dec_s[c][:, j * LANES:(j + 1) * LANES]

    def step_state(j, c, st):
        return st * dec_row(j, c) + kv_s[j, c]

    for q in range(n_in_step):
        st0 = tuple(jnp.transpose(jnp.concatenate([s0f_ref[q, j], s0b_ref[q, j]], axis=0)) for j in range(2))
        c0 = q * ch_per_seq
        s0 = q * sb_per_seq

        def bwd_body(t, sts, c0=c0):
            c = c0 + ch_per_seq - 1 - t
            for j in range(2):
                sbs_s[j, c] = sts[j]
            return tuple(step_state(j, c, sts[j]) for j in range(2))

        st_b_fin = lax.fori_loop(0, ch_per_seq, bwd_body, st0, unroll=CPB)

        def fwd_body(sl, sts, s0=s0):
            s = s0 + sl
            rows = pl.ds(pl.multiple_of(s * SB, SB), SB)
            new = []
            for j in range(2):
                lo = j * DV
                qe = qe_s[j, rows, :]
                ke = ke_s[j, rows, :]
                v_b = v_ref[rows, lo:lo + DV].astype(BF16)
                zero = jnp.zeros_like(qe)
                st = sts[j]
                q_parts, s_parts = [], []
                for c in range(CPB):
                    ci = s * CPB + c
                    s_parts.append(jnp.where(m_f, st, sbs_s[j, ci]).astype(BF16))
                    q_parts.append(jnp.where(row_chunk == c, qe, zero))
                    st = step_state(j, ci, st)
                o = _dot_nt(jnp.concatenate(q_parts, axis=1), jnp.concatenate(s_parts, axis=1))
                q_st = jnp.concatenate([jnp.where(m_f, qe, zero), jnp.where(m_f, zero, qe)], axis=0)
                sc = _dot_nt(q_st, ke)
                p = jnp.where(tril, sc[0:SB, :], 0.0) + jnp.where(triu, sc[SB:2 * SB, :], 0.0)
                o = o + _dot(p.astype(BF16), v_b)
                o = o * lax.rsqrt(jnp.mean(o * o, axis=-1, keepdims=True) + EPS) * ng_ref[:, lo:lo + DV]
                o_ref[rows, lo:lo + DV] = (o * _silu(og_ref[rows, lo:lo + DV].astype(F32))).astype(BF16)
                new.append(st)
            return tuple(new)

        st_f_fin = lax.fori_loop(0, sb_per_seq, fwd_body, st0, unroll=min(2, sb_per_seq))
        for j in range(2):
            sf_ref[q, j] = jnp.transpose(st_f_fin[j])[0:DK, :]
            sb_ref[q, j] = jnp.transpose(st_b_fin[j])[DK:2 * DK, :]


def _gla(proj, r_all, wg_hi, wg_lo, bg, ng, tlu, s0f, s0b, *, seq_len, n_seq, n_in_step, row0):
    kern = functools.partial(_gla_kernel, seq_len=seq_len, n_in_step=n_in_step)
    rows = n_in_step * seq_len
    n_chunks = rows // CHUNK
    rb = lambda b: b + row0 // rows
    st_spec = pl.BlockSpec((n_in_step, 2, DK, DV), lambda b, h: (b, h, 0, 0))
    return pl.pallas_call(
        kern,
        grid=(n_seq // n_in_step, HEADS // 2),
        in_specs=[pl.BlockSpec((rows, LANES), lambda b, h: (rb(b), h)),
                  pl.BlockSpec((rows, LANES), lambda b, h: (rb(b), QK_W // LANES + h)),
                  pl.BlockSpec((rows, 2 * DV), lambda b, h: (rb(b), 2 * QK_W // (2 * DV) + h)),
                  pl.BlockSpec((rows, 2 * DV), lambda b, h: (rb(b), (2 * QK_W + V_W) // (2 * DV) + h)),
                  pl.BlockSpec((rows, LANES), lambda b, h: (rb(b), 0)),
                  pl.BlockSpec((None, LANES, 2 * LANES), lambda b, h: (h, 0, 0)),
                  pl.BlockSpec((None, LANES, 2 * LANES), lambda b, h: (h, 0, 0)),
                  pl.BlockSpec((None, 1, 2 * LANES), lambda b, h: (h, 0, 0)),
                  pl.BlockSpec((None, 1, 2 * DV), lambda b, h: (h, 0, 0)),
                  pl.BlockSpec((SB, 2 * SB), lambda b, h: (0, 0)),
                  st_spec, st_spec],
        out_specs=[pl.BlockSpec((rows, 2 * DV), lambda b, h: (b, h)), st_spec, st_spec],
        out_shape=[jax.ShapeDtypeStruct((n_seq * seq_len, V_W), BF16),
                   jax.ShapeDtypeStruct((n_seq, HEADS, DK, DV), F32),
                   jax.ShapeDtypeStruct((n_seq, HEADS, DK, DV), F32)],
        scratch_shapes=[pltpu.VMEM((rows, 2 * LANES), F32),
                        pltpu.VMEM((rows, 2 * LANES), F32),
                        pltpu.VMEM((2, rows, LANES), BF16),
                        pltpu.VMEM((2, rows, LANES), BF16),
                        pltpu.VMEM((2, rows, LANES), BF16),
                        pltpu.VMEM((n_chunks, 1, 2 * LANES), F32),
                        pltpu.VMEM((2, n_chunks, DV, LANES), F32),
                        pltpu.VMEM((2, n_chunks, DV, LANES), F32)],
        compiler_params=_cparams(("arbitrary", "arbitrary")),
        name="gla_%d" % seq_len,
    )(proj, proj, proj, proj, r_all, wg_hi, wg_lo, bg, ng, tlu, s0f, s0b)


def _fnet_stage_a(u_bf, cs):
    cparts, sparts = [], []
    for g in range(FN_G):
        ab = _dot(u_bf[:, g * FN_C:(g + 1) * FN_C], cs)
        cparts.append(ab[:, 0:FN_C])
        sparts.append(ab[:, FN_C:2 * FN_C])
    return jnp.concatenate(cparts, axis=1), jnp.concatenate(sparts, axis=1)


def _fnet_ctx_kernel(u_ref, cs_ref, p2_ref, f_ref):
    uc, us = _fnet_stage_a(u_ref[...].astype(BF16), cs_ref[...].astype(BF16))
    ab = jnp.concatenate([uc, us], axis=0).astype(BF16)
    f_ref[...] = _dot(p2_ref[...].astype(BF16), ab).astype(BF16)


def _fnet_ctx(proj, cs, p2):
    return pl.pallas_call(
        _fnet_ctx_kernel,
        grid=(N_CTX,),
        in_specs=[pl.BlockSpec((L_CTX, FN_G * FN_C), lambda b: (b, 3)),
                  pl.BlockSpec((FN_C, 2 * FN_C), lambda b: (0, 0)),
                  pl.BlockSpec((L_CTX, 2 * L_CTX), lambda b: (0, 0))],
        out_specs=pl.BlockSpec((L_CTX, FN_G * FN_C), lambda b: (b, 0)),
        out_shape=jax.ShapeDtypeStruct((T_CTX, FN_G * FN_C), BF16),
        compiler_params=_cparams(("arbitrary",)),
        name="fnet_ctx",
    )(proj, cs, p2)


TM_FL = 256
RT_FL = 256


def _fnet_lat_kernel(u_ref, cs_ref, kr_ref, f_ref, ab_s):
    m = pl.program_id(1)

    @pl.when(m == 0)
    def _():
        def body(t, carry):
            rows = pl.ds(pl.multiple_of(t * RT_FL, RT_FL), RT_FL)
            uc, us = _fnet_stage_a(u_ref[rows, :].astype(BF16), cs_ref[...].astype(BF16))
            ab_s[rows, :] = uc.astype(BF16)
            ab_s[pl.ds(pl.multiple_of(L_LAT + t * RT_FL, RT_FL), RT_FL), :] = us.astype(BF16)
            return carry
        lax.fori_loop(0, L_LAT // RT_FL, body, 0)

    f_ref[...] = _dot(kr_ref[...].astype(BF16), ab_s[...]).astype(BF16)


def _fnet_lat(proj, cs, kr):
    nm = L_LAT // TM_FL
    return pl.pallas_call(
        _fnet_lat_kernel,
        grid=(N_LAT, nm),
        in_specs=[pl.BlockSpec((L_LAT, FN_G * FN_C), lambda b, m: (T_CTX // L_LAT + b, 3)),
                  pl.BlockSpec((FN_C, 2 * FN_C), lambda b, m: (0, 0)),
                  pl.BlockSpec((TM_FL, 2 * L_LAT), lambda b, m: (m, 0))],
        out_specs=pl.BlockSpec((TM_FL, FN_G * FN_C), lambda b, m: (b * nm + m, 0)),
        out_shape=jax.ShapeDtypeStruct((T_LAT, FN_G * FN_C), BF16),
        scratch_shapes=[pltpu.VMEM((2 * L_LAT, FN_G * FN_C), BF16)],
        compiler_params=_cparams(("arbitrary", "arbitrary")),
        name="fnet_lat",
    )(proj, cs, kr)


TM_OUT = 256
LANE_E0 = N_GROUPS
ROWS_PER_BLK = 8
PACK_ROWS = -(-(2 * TM_OUT + N_EXP * (ROWS_PER_BLK - 1)) // 256) * 256
PACK_BLKS = PACK_ROWS // ROWS_PER_BLK
N_TOK_TILES = T_ALL // TM_OUT
BLK_PER_TILE = TM_MOE // ROWS_PER_BLK
USED_BLKS = (2 * TM_OUT + N_EXP * (ROWS_PER_BLK - 1)) // ROWS_PER_BLK
HS_ROWS = N_TOK_TILES * PACK_ROWS
assert (PACK_BLKS - USED_BLKS) * N_TOK_TILES >= 2 * BLK_PER_TILE


def _outproj_kernel(oc_ref, ol_ref, fc_ref, fl_ref, xp_ref, xs_ref, mod_ref, g_ref, wo_ref, wf_ref,
                    wrc_ref, br_ref, sut_ref, sl_ref,
                    x1_ref, hs_ref, rw_ref, nb_ref, lb_ref):
    i = pl.program_id(0)
    is_ctx = i < T_CTX // TM_OUT

    o = jnp.where(is_ctx, oc_ref[...], ol_ref[...]).astype(BF16)
    f = jnp.where(is_ctx, fc_ref[...], fl_ref[...]).astype(BF16)
    x = jnp.where(is_ctx, xp_ref[...], xs_ref[...])
    y = _dot(o, wo_ref[...]) + _dot(f, wf_ref[...])
    ga1 = mod_ref[:, 2 * D_MODEL:3 * D_MODEL]
    sh2 = mod_ref[:, 3 * D_MODEL:4 * D_MODEL]
    sc2 = mod_ref[:, 4 * D_MODEL:5 * D_MODEL]
    x1 = x + ga1 * y
    x1_ref[...] = x1
    h2 = _rms(x1, g_ref[...]) * (1.0 + sc2) + sh2

    h_hi, h_lo = _split2(h2)
    hw = _dot(h_hi, wrc_ref[...])
    lg_all = jnp.transpose(hw[:, 0:LANES] + _dot(h_lo, wrc_ref[:, 0:LANES]) + hw[:, LANES:2 * LANES]
                           + br_ref[...])

    row_i = lax.broadcasted_iota(I32, (LANES, TM_OUT), 0)
    row = row_i.astype(F32)
    neg = jnp.float32(-jnp.inf)
    big = jnp.float32(LANES)
    lg = jnp.where(row_i < N_GROUPS, lg_all, neg)
    gmax = jnp.max(lg, axis=0, keepdims=True)
    gsel = jnp.min(jnp.where(lg == gmax, row, big), axis=0, keepdims=True)
    den = jnp.sum(jnp.exp(lg - gmax), axis=0, keepdims=True)
    pg_sel = 1.0 / den

    e_idx = row_i - LANE_E0
    egrp = (e_idx >> 3).astype(F32)
    emask = (e_idx >= 0) & (e_idx < N_EXP) & (egrp == gsel)
    m1 = jnp.where(emask, lg_all, neg)
    v1 = jnp.max(m1, axis=0, keepdims=True)
    i1 = jnp.min(jnp.where(m1 == v1, row, big), axis=0, keepdims=True)
    m2 = jnp.where(row == i1, neg, m1)
    v2 = jnp.max(m2, axis=0, keepdims=True)
    i2 = jnp.min(jnp.where(m2 == v2, row, big), axis=0, keepdims=True)
    e2 = jnp.exp(v2 - v1)
    inv = 1.0 / (1.0 + e2)
    w1 = inv * pg_sel
    w2 = (e2 * inv) * pg_sel

    oh1 = row == i1
    oh2 = row == i2
    oh = jnp.where(oh1 | oh2, 1.0, 0.0).astype(BF16)
    cnt = _dot(oh, jnp.ones((TM_OUT, LANES), BF16))
    nblk = jnp.floor((cnt + (ROWS_PER_BLK - 1)) * (1.0 / ROWS_PER_BLK))
    lboff = _dot(sl_ref[...], nblk.astype(BF16))
    lrank = _dot(oh, sut_ref[...])
    posmat = jnp.concatenate([lboff, lboff], axis=1) * ROWS_PER_BLK + lrank
    p1 = jnp.sum(jnp.where(oh1, posmat, 0.0), axis=0, keepdims=True)
    p2 = jnp.sum(jnp.where(oh2, posmat, 0.0), axis=0, keepdims=True)
    nb_ref[...] = nblk[:, 0:8].astype(I32)
    lb_ref[...] = lboff[:, 0:8].astype(I32)

    prow = lax.broadcasted_iota(I32, (PACK_ROWS, TM_OUT), 0).astype(F32)
    place = jnp.where((prow == p1) | (prow == p2), 1.0, 0.0).astype(BF16)
    hs_ref[...] = _dot(place, h_hi).astype(BF16)

    rw_ref[...] = jnp.concatenate([w1, w2, p1, p2, jnp.zeros((4, TM_OUT), F32)], axis=0)


def _outproj(o_ctx, o_lat, f_ctx, f_lat, xp, xs, mod3, g_ffn, wo, wf, wr_cat, br, sut, sl):
    nt = T_ALL // TM_OUT
    nctx = T_CTX // TM_OUT
    ctx_map = lambda i: (jnp.minimum(i, nctx - 1), 0)
    lat_map = lambda i: (jnp.maximum(i - nctx, 0), 0)
    const = lambda i: (0, 0)
    return pl.pallas_call(
        _outproj_kernel,
        grid=(nt,),
        in_specs=[pl.BlockSpec((TM_OUT, V_W), ctx_map),
                  pl.BlockSpec((TM_OUT, V_W), lat_map),
                  pl.BlockSpec((TM_OUT, FN_G * FN_C), ctx_map),
                  pl.BlockSpec((TM_OUT, FN_G * FN_C), lat_map),
                  pl.BlockSpec((TM_OUT, D_MODEL), ctx_map),
                  pl.BlockSpec((TM_OUT, D_MODEL), lat_map),
                  pl.BlockSpec((None, 1, 6 * D_MODEL), lambda i: (_cond_row(i, TM_OUT), 0, 0)),
                  pl.BlockSpec((1, D_MODEL), const),
                  pl.BlockSpec((V_W, D_MODEL), const),
                  pl.BlockSpec((FN_G * FN_C, D_MODEL), const),
                  pl.BlockSpec((D_MODEL, 2 * LANES), const),
                  pl.BlockSpec((1, LANES), const),
                  pl.BlockSpec((TM_OUT, TM_OUT), const),
                  pl.BlockSpec((LANES, LANES), const)],
        out_specs=[pl.BlockSpec((TM_OUT, D_MODEL), lambda i: (i, 0)),
                   pl.BlockSpec((PACK_ROWS, D_MODEL), lambda i: (i, 0)),
                   pl.BlockSpec((8, TM_OUT), lambda i: (0, i)),
                   pl.BlockSpec((None, LANES, 8), lambda i: (i, 0, 0)),
                   pl.BlockSpec((None, LANES, 8), lambda i: (i, 0, 0))],
        out_shape=[jax.ShapeDtypeStruct((T_ALL, D_MODEL), F32),
                   jax.ShapeDtypeStruct((HS_ROWS, D_MODEL), BF16),
                   jax.ShapeDtypeStruct((8, T_ALL), F32),
                   jax.ShapeDtypeStruct((nt, LANES, 8), I32),
                   jax.ShapeDtypeStruct((nt, LANES, 8), I32)],
        compiler_params=_cparams(("arbitrary",)),
        name="outproj",
    )(o_ctx, o_lat, f_ctx, f_lat, xp, xs, mod3, g_ffn, wo, wf, wr_cat, br, sut, sl)


SRC_BITS = 16
SRC_MASK = (1 << SRC_BITS) - 1
X_SLOTS = 3
Y_SLOTS = 3
N_UP_CHUNKS = 2
N_DN_CHUNKS = 8
NT_MOE = (2 * T_ALL + N_TOK_TILES * N_EXP * (ROWS_PER_BLK - 1)) // TM_MOE + N_EXP


def _moe_kernel(texp_ref, nexp_ref, meta_ref, code_ref,
                h_hbm, wg_hbm, wu_hbm, wd_hbm,
                out_hbm,
                xbuf, ybuf, wg_f, wu_f, wd_f, wg_s, wu_s, wd_s, kcount, gsem, ssem, wsem):
    i = pl.program_id(0)
    nt = meta_ref[0]
    xs = i % X_SLOTS

    def blk_rows(b):
        if isinstance(b, int):
            return pl.ds(b * ROWS_PER_BLK, ROWS_PER_BLK)
        return pl.ds(pl.multiple_of(b * ROWS_PER_BLK, ROWS_PER_BLK), ROWS_PER_BLK)

    def gather_row(tile, sl, j):
        src = code_ref[(tile + 2) * BLK_PER_TILE + j] & SRC_MASK
        pltpu.make_async_copy(h_hbm.at[blk_rows(src)], xbuf.at[sl, blk_rows(j)], gsem.at[sl]).start()

    def scatter_row(tile, sl, j):
        dst = code_ref[(tile + 2) * BLK_PER_TILE + j] >> SRC_BITS
        pltpu.make_async_copy(ybuf.at[sl, blk_rows(j)], out_hbm.at[blk_rows(dst)], ssem.at[sl]).start(priority=1)

    def gather_wait(sl):
        pltpu.make_async_copy(h_hbm.at[pl.ds(0, TM_MOE)], xbuf.at[sl], gsem.at[sl]).wait()

    def scatter_wait(sl):
        pltpu.make_async_copy(ybuf.at[sl], out_hbm.at[pl.ds(0, TM_MOE)], ssem.at[sl]).wait()

    def y_slot(tile):
        return (tile + 1) % Y_SLOTS

    @pl.when(i == 0)
    def _():
        ybuf[y_slot(-2)] = jnp.zeros((TM_MOE, D_MODEL), BF16)
        ybuf[y_slot(-1)] = jnp.zeros((TM_MOE, D_MODEL), BF16)

        def body(j, c):
            gather_row(0, 0, j)
            gather_row(1, 1, j)
            scatter_row(-2, y_slot(-2), j)
            return c
        lax.fori_loop(0, BLK_PER_TILE, body, 0)

    @pl.when((i >= 1) & (i <= nt))
    def _():
        scatter_wait(y_slot(i - 3))

    @pl.when(i < nt)
    def _():
        prev = texp_ref[jnp.maximum(i - 1, 0)]

        def weight_copies(e, sl):
            return (pltpu.make_async_copy(wg_hbm.at[e], wg_f.at[sl], wsem.at[sl]),
                    pltpu.make_async_copy(wu_hbm.at[e], wu_f.at[sl], wsem.at[sl]),
                    pltpu.make_async_copy(wd_hbm.at[e], wd_f.at[sl], wsem.at[sl]))

        @pl.when(i == 0)
        def _():
            kcount[0] = 0
            for cp in weight_copies(texp_ref[0], 0):
                cp.start()

        @pl.when((i == 0) | (texp_ref[i] != prev))
        def _():
            k = kcount[0]
            for sl in range(2):
                @pl.when(k % 2 == sl)
                def _(sl=sl):
                    for cp in weight_copies(texp_ref[i], sl):
                        cp.wait()
                    def narrow(r, c):
                        up = pl.ds(pl.multiple_of(r * (D_MODEL // 16), D_MODEL // 16), D_MODEL // 16)
                        dn = pl.ds(pl.multiple_of(r * (D_EXP // 16), D_EXP // 16), D_EXP // 16)
                        wg_s[up, :] = wg_f[sl, up, :].astype(BF16)
                        wu_s[up, :] = wu_f[sl, up, :].astype(BF16)
                        wd_s[dn, :] = wd_f[sl, dn, :].astype(BF16)
                        return c
                    lax.fori_loop(0, 16, narrow, 0)

                    @pl.when(nexp_ref[i] >= 0)
                    def _():
                        for cp in weight_copies(nexp_ref[i], 1 - sl):
                            cp.start(priority=1)
            kcount[0] = k + 1

        gather_wait(xs)
        x = xbuf[xs].astype(BF16)

        issues = []
        for j in range(BLK_PER_TILE):
            issues.append(functools.partial(gather_row, i + 2, (i + 2) % X_SLOTS, j))
            issues.append(functools.partial(scatter_row, i - 1, y_slot(i - 1), j))
        n_groups = N_UP_CHUNKS + N_DN_CHUNKS
        per_group = -(-len(issues) // n_groups)

        def issue_group(k):
            for fn in issues[k * per_group:(k + 1) * per_group]:
                fn()

        wu_c = D_EXP // N_UP_CHUNKS
        hid = []
        for n in range(N_UP_CHUNKS):
            issue_group(n)
            g = _dot(x, wg_s[:, n * wu_c:(n + 1) * wu_c])
            u = _dot(x, wu_s[:, n * wu_c:(n + 1) * wu_c])
            hid.append((_silu(g) * u).astype(BF16))
        hid = jnp.concatenate(hid, axis=1)
        wd_c = D_MODEL // N_DN_CHUNKS
        ys = y_slot(i)
        for n in range(N_DN_CHUNKS):
            issue_group(N_UP_CHUNKS + n)
            ybuf[ys, :, n * wd_c:(n + 1) * wd_c] = _dot(hid, wd_s[:, n * wd_c:(n + 1) * wd_c]).astype(BF16)

    @pl.when(i == nt)
    def _():
        gather_wait(xs)
        gather_wait((i + 1) % X_SLOTS)

        def body(j, c):
            scatter_row(nt - 1, y_slot(nt - 1), j)
            return c
        lax.fori_loop(0, BLK_PER_TILE, body, 0)
        scatter_wait(y_slot(nt - 2))
        scatter_wait(y_slot(nt - 1))


def _moe(texp, nexp, meta, code, hs, w_eg, w_eu, w_ed):
    hbm = pl.BlockSpec(memory_space=pl.ANY)
    grid_spec = pltpu.PrefetchScalarGridSpec(
        num_scalar_prefetch=4,
        grid=(NT_MOE + 1,),
        in_specs=[hbm, hbm, hbm, hbm],
        out_specs=hbm,
        scratch_shapes=[pltpu.VMEM((X_SLOTS, TM_MOE, D_MODEL), BF16),
                        pltpu.VMEM((Y_SLOTS, TM_MOE, D_MODEL), BF16),
                        pltpu.VMEM((2, D_MODEL, D_EXP), F32),
                        pltpu.VMEM((2, D_MODEL, D_EXP), F32),
                        pltpu.VMEM((2, D_EXP, D_MODEL), F32),
                        pltpu.VMEM((D_MODEL, D_EXP), BF16),
                        pltpu.VMEM((D_MODEL, D_EXP), BF16),
                        pltpu.VMEM((D_EXP, D_MODEL), BF16),
                        pltpu.SMEM((1,), I32),
                        pltpu.SemaphoreType.DMA((X_SLOTS,)),
                        pltpu.SemaphoreType.DMA((Y_SLOTS,)),
                        pltpu.SemaphoreType.DMA((2,))])
    return pl.pallas_call(
        _moe_kernel,
        grid_spec=grid_spec,
        out_shape=jax.ShapeDtypeStruct((HS_ROWS, D_MODEL), BF16),
        input_output_aliases={4: 0},
        compiler_params=_cparams(("arbitrary",)),
        name="moe",
    )(texp, nexp, meta, code, hs, w_eg, w_eu, w_ed)


TM_FIN = TM_OUT


def _final_kernel(x1_ref, ys_pack_ref, rw_ref, mod_ref, g_ref, yp_ref, ys_ref):
    i = pl.program_id(0)
    ga2 = mod_ref[:, 5 * D_MODEL:6 * D_MODEL]
    w0 = rw_ref[0:1, :]
    w1 = rw_ref[1:2, :]
    p0 = rw_ref[2:3, :]
    p1 = rw_ref[3:4, :]
    prow = lax.broadcasted_iota(I32, (PACK_ROWS, TM_FIN), 0).astype(F32)
    comb_t = jnp.where(prow == p0, w0, 0.0) + jnp.where(prow == p1, w1, 0.0)
    y_moe = lax.dot_general(comb_t.astype(BF16), ys_pack_ref[...].astype(BF16), (((0,), (0,)), ((), ())),
                            preferred_element_type=F32)
    y = x1_ref[...] + ga2 * y_moe
    out = _rms(y, g_ref[...])

    @pl.when(i < T_CTX // TM_FIN)
    def _():
        yp_ref[...] = out

    @pl.when(i >= T_CTX // TM_FIN)
    def _():
        ys_ref[...] = out


def _final(x1, y2, rw, mod3, g_fin):
    nt = T_ALL // TM_FIN
    nctx = T_CTX // TM_FIN
    return pl.pallas_call(
        _final_kernel,
        grid=(nt,),
        in_specs=[pl.BlockSpec((TM_FIN, D_MODEL), lambda i: (i, 0)),
                  pl.BlockSpec((PACK_ROWS, D_MODEL), lambda i: (i, 0)),
                  pl.BlockSpec((8, TM_FIN), lambda i: (0, i)),
                  pl.BlockSpec((None, 1, 6 * D_MODEL), lambda i: (_cond_row(i, TM_FIN), 0, 0)),
                  pl.BlockSpec((1, D_MODEL), lambda i: (0, 0))],
        out_specs=[pl.BlockSpec((TM_FIN, D_MODEL), lambda i: (jnp.minimum(i, nctx - 1), 0)),
                   pl.BlockSpec((TM_FIN, D_MODEL), lambda i: (jnp.maximum(i - nctx, 0), 0))],
        out_shape=[jax.ShapeDtypeStruct((T_CTX, D_MODEL), F32),
                   jax.ShapeDtypeStruct((T_LAT, D_MODEL), F32)],
        compiler_params=_cparams(("arbitrary",)),
        name="final",
    )(x1, y2, rw, mod3, g_fin)


def _np_bf16(a):
    return jnp.asarray(np.asarray(a, np.float32), dtype=BF16)


def _np_f32(a):
    return jnp.asarray(np.asarray(a, np.float32))


@functools.lru_cache(maxsize=None)
def _constants():
    c = {}
    k = np.arange(FN_C)
    ang = 2.0 * np.pi * np.outer(k, k) / FN_C
    c["cs"] = np.concatenate([np.cos(ang), np.sin(ang)], axis=1) / np.sqrt(FN_C)
    p = np.arange(L_CTX)
    ang = 2.0 * np.pi * np.outer(p, p) / L_CTX
    c["p2"] = np.concatenate([np.cos(ang), -np.sin(ang)], axis=1) / np.sqrt(L_CTX)
    pos = np.arange(L_LAT)
    rr, cc = pos // GRID_W, pos % GRID_W
    num = (np.outer(rr, rr) * (GRID_W // GRID_H) + np.outer(cc, cc)) % GRID_W
    ang = 2.0 * np.pi * num / GRID_W
    c["kr"] = np.concatenate([np.cos(ang), -np.sin(ang)], axis=1) / np.sqrt(L_LAT)
    i = np.arange(SB)
    same = (i[:, None] // CHUNK) == (i[None, :] // CHUNK)
    tl = same & (i[:, None] >= i[None, :])
    tu = same & (i[:, None] <= i[None, :])
    c["tlu"] = np.concatenate([tl, tu], axis=1).astype(np.float32)
    c["sut"] = (i[:, None] < i[None, :]).astype(np.float32)
    k = np.arange(LANES)
    c["sl"] = (k[:, None] > k[None, :]).astype(np.float32)
    return c


def kernel(x_prompt, x_sample, state_gla_fwd, state_gla_bwd, c, c_ctx, w_ada, b_ada, norm_attn, norm_ffn, w_in, w_gate_fwd, b_gate_fwd, w_gate_bwd, b_gate_bwd, norm_gla, w_out, w_router_group, b_router_group, w_router_expert, b_router_expert, w_expert_gate, w_expert_up, w_expert_down, norm_final):
    assert w_ada.shape[0] == 1, "single layer"
    cst = _constants()
    cs, p2, kr = _np_f32(cst["cs"]), _np_f32(cst["p2"]), _np_f32(cst["kr"])
    tlu, sut, sl = _np_bf16(cst["tlu"]), _np_bf16(cst["sut"]), _np_bf16(cst["sl"])

    xp = x_prompt.reshape(T_CTX, D_MODEL)
    xs = x_sample.reshape(T_LAT, D_MODEL)

    cond8 = jnp.concatenate([c_ctx[None, :], c, jnp.zeros((3, D_MODEL), F32)], axis=0)
    mod = _ada(cond8, w_ada[0], b_ada[0][None, :])
    mod3 = mod.reshape(8, 1, 6 * D_MODEL)

    wi = w_in[0]
    i_og = 2 * QK_W + 2 * V_W
    i_u = i_og + 2 * RANK
    w_a = wi[:, :i_og].astype(BF16)
    w_u = wi[:, i_u:].astype(BF16)
    w_r = jnp.pad(wi[:, i_og:i_u], ((0, 0), (0, LANES - 2 * RANK))).astype(BF16)

    wgf = w_gate_fwd[0].reshape(RANK, HEADS, DK)
    wgb = w_gate_bwd[0].reshape(RANK, HEADS, DK)
    zf = jnp.zeros_like(wgf)
    top = jnp.stack([wgf, zf], axis=2)
    bot = jnp.stack([zf, wgb], axis=2)
    wg = jnp.concatenate([top, bot], axis=0)
    wg = wg.reshape(2 * RANK, HEADS // 2, 4 * DK).transpose(1, 0, 2)
    wg = jnp.pad(wg, ((0, 0), (0, LANES - 2 * RANK), (0, 0)))
    wg_hi = wg.astype(BF16)
    wg_lo = (wg - wg_hi.astype(F32)).astype(BF16)
    bg = jnp.stack([b_gate_fwd[0].reshape(HEADS, DK), b_gate_bwd[0].reshape(HEADS, DK)], axis=1)
    bg = bg.reshape(HEADS // 2, 1, 4 * DK)
    ng = norm_gla[0].reshape(HEADS // 2, 1, 2 * DV)

    proj, r_all = _inproj(xp, xs, mod3, norm_attn, w_a, w_u, w_r)

    zero_state = jnp.zeros((N_CTX, HEADS, DK, DV), F32)
    o_ctx, sf_ctx, sb_ctx = _gla(proj, r_all, wg_hi, wg_lo, bg, ng, tlu, zero_state, zero_state,
                                 seq_len=L_CTX, n_seq=N_CTX, n_in_step=4, row0=0)
    o_lat, _, _ = _gla(proj, r_all, wg_hi, wg_lo, bg, ng, tlu,
                       state_gla_fwd[:, 0], state_gla_bwd[:, 0],
                       seq_len=L_LAT, n_seq=N_LAT, n_in_step=1, row0=T_CTX)

    f_ctx = _fnet_ctx(proj, cs, p2)
    f_lat = _fnet_lat(proj, cs, kr)

    wo = w_out[0][:V_W].astype(BF16)
    wf = w_out[0][V_W:].astype(BF16)
    wr = jnp.concatenate([w_router_group[0], w_router_expert[0]], axis=1)
    wr = jnp.pad(wr, ((0, 0), (0, LANES - N_GROUPS - N_EXP)))
    wr_hi = wr.astype(BF16)
    wr_lo = (wr - wr_hi.astype(F32)).astype(BF16)
    wr_cat = jnp.concatenate([wr_hi, wr_lo], axis=1)
    br = jnp.pad(jnp.concatenate([b_router_group[0], b_router_expert[0]]), (0, LANES - N_GROUPS - N_EXP))[None, :]

    x1, hs, rw, nb, lb = _outproj(o_ctx, o_lat, f_ctx, f_lat, xp, xs, mod3, norm_ffn, wo, wf,
                                  wr_cat, br, sut, sl)

    nb_e = nb[:, LANE_E0:LANE_E0 + N_EXP, 0].T
    lb_e = lb[:, LANE_E0:LANE_E0 + N_EXP, 0].T
    run_end = jnp.cumsum(nb_e, axis=1)
    blocks_e = run_end[:, -1]
    tiles_e = (blocks_e + BLK_PER_TILE - 1) // BLK_PER_TILE
    tile_end = jnp.cumsum(tiles_e)
    tile_start = tile_end - tiles_e
    n_tiles = tile_end[-1]
    n_code_tiles = NT_MOE + 4
    tile = jnp.arange(n_code_tiles, dtype=I32) - 2
    tile_c = jnp.clip(tile, 0, n_tiles - 1)
    t_exp = jnp.sum(tile_c[:, None] >= tile_end[None, :], axis=1)

    pick = t_exp[:, None] == jnp.arange(N_EXP, dtype=I32)[None, :]

    def per_tile(table):
        if table.ndim == 1:
            return jnp.sum(jnp.where(pick, table[None, :], 0), axis=1)
        return jnp.sum(jnp.where(pick[:, :, None], table[None, :, :], 0), axis=1)

    ends = per_tile(run_end)
    starts = ends - per_tile(nb_e)
    offs = per_tile(lb_e) + jnp.arange(N_TOK_TILES, dtype=I32)[None, :] * PACK_BLKS - starts
    j = jnp.arange(BLK_PER_TILE, dtype=I32)
    bi = ((tile_c - per_tile(tile_start)) * BLK_PER_TILE)[:, None] + j[None, :]
    in_run = (starts.T[:, :, None] <= bi[None, :, :]) & (bi[None, :, :] < ends.T[:, :, None])
    blk = bi + jnp.sum(jnp.where(in_run, offs.T[:, :, None], 0), axis=0)
    valid = (tile == tile_c)[:, None] & (bi < per_tile(blocks_e)[:, None])
    spare_ix = (jnp.arange(n_code_tiles, dtype=I32) % 2)[:, None] * BLK_PER_TILE + j[None, :]
    spare = (spare_ix % N_TOK_TILES) * PACK_BLKS + USED_BLKS + spare_ix // N_TOK_TILES
    code = jnp.where(valid, (blk << SRC_BITS) | blk, (spare << SRC_BITS) | blk[:, 0:1]).astype(I32).reshape(-1)
    texp = t_exp[2:NT_MOE + 3].astype(I32)
    e_ix = jnp.arange(N_EXP, dtype=I32)
    later = (e_ix[None, :] > e_ix[:, None]) & (tiles_e[None, :] > 0)
    next_e = jnp.min(jnp.where(later, e_ix[None, :], N_EXP), axis=1)
    next_e = jnp.where(next_e < N_EXP, next_e, -1)
    nexp = per_tile(next_e)[2:NT_MOE + 3].astype(I32)
    meta = n_tiles.reshape(1).astype(I32)

    y2 = _moe(texp, nexp, meta, code, hs, w_expert_gate[0], w_expert_up[0], w_expert_down[0])
    y_prompt, y_sample = _final(x1, y2, rw, mod3, norm_final[None, :])

    st_shape = (N_CTX, 1, HEADS, DK, DV)
    return (y_prompt.reshape(N_CTX, L_CTX, D_MODEL), y_sample.reshape(N_LAT, L_LAT, D_MODEL),
            sf_ctx.reshape(st_shape), sb_ctx.reshape(st_shape))
```

```python
import functools

import numpy as np
import jax
import jax.numpy as jnp
from jax import lax
from jax.experimental import pallas as pl
from jax.experimental.pallas import tpu as pltpu

F32 = jnp.float32
BF16 = jnp.bfloat16
I32 = jnp.int32

D_MODEL = 2048
N_CTX = 32
L_CTX = 256
N_LAT = 4
L_LAT = 2048
GRID_H = 32
GRID_W = 64
T_CTX = N_CTX * L_CTX
T_LAT = N_LAT * L_LAT
T_ALL = T_CTX + T_LAT
HEADS = 8
DK = 64
DV = 128
RANK = 16
TAU = 16.0
CHUNK = 64
FN_G = 8
FN_C = 128
QK_W = HEADS * DK
V_W = HEADS * DV
N_GROUPS = 4
EPG = 8
N_EXP = N_GROUPS * EPG
D_EXP = 512
EPS = 1e-6

LANES = 128
VMEM_LIMIT = 56 * 1024 * 1024

TM_MOE = 256


def _dot(a, b):
    return jnp.dot(a, b, preferred_element_type=F32)


def _dot_nt(a, b):
    return lax.dot_general(a, b, (((1,), (1,)), ((), ())), preferred_element_type=F32)


def _split2(x):
    hi = x.astype(BF16)
    lo = (x - hi.astype(F32)).astype(BF16)
    return hi, lo


def _silu(x):
    return x * (1.0 / (1.0 + jnp.exp(-x)))


def _rms(x, g):
    return x * lax.rsqrt(jnp.mean(x * x, axis=-1, keepdims=True) + EPS) * g


def _cparams(sem):
    return pltpu.CompilerParams(dimension_semantics=sem, vmem_limit_bytes=VMEM_LIMIT)


def _ada_kernel(c_ref, w_ref, b_ref, o_ref):
    s_hi, s_lo = _split2(_silu(c_ref[...]))
    w = w_ref[...]
    w_hi = w.astype(BF16)
    w_lo = (w - w_hi.astype(F32)).astype(BF16)
    o_ref[...] = _dot(s_hi, w_hi) + _dot(s_lo, w_hi) + _dot(s_hi, w_lo) + b_ref[...]


def _ada(cond8, w_ada, b_ada):
    tn = 1024
    n6 = 6 * D_MODEL
    return pl.pallas_call(
        _ada_kernel,
        grid=(n6 // tn,),
        in_specs=[pl.BlockSpec((8, D_MODEL), lambda j: (0, 0)),
                  pl.BlockSpec((D_MODEL, tn), lambda j: (0, j)),
                  pl.BlockSpec((1, tn), lambda j: (0, j))],
        out_specs=pl.BlockSpec((8, tn), lambda j: (0, j)),
        out_shape=jax.ShapeDtypeStruct((8, n6), F32),
        compiler_params=_cparams(("arbitrary",)),
        name="ada",
    )(cond8, w_ada, b_ada)


TM_IN = 512
TM_IN_HALF = 256
TN_IN = 1024
N_MAIN = 4096
N_QKVG = 3072


def _cond_row(tile, tm):
    ctx_tiles = T_CTX // tm
    per_seq = L_LAT // tm
    return jnp.where(tile < ctx_tiles, 0, 1 + (jnp.maximum(tile - ctx_tiles, 0)) // per_seq)


def _split_tail_kernel(t_ref, wu_ref, wr_ref):
    t = t_ref[...]
    lane = lax.broadcasted_iota(I32, (t.shape[0], LANES), 1)
    wr_ref[...] = jnp.where(lane < 2 * RANK, t[:, 0:LANES], 0.0).astype(BF16)
    wu_ref[...] = t[:, 2 * RANK:2 * RANK + FN_G * FN_C].astype(BF16)


def _split_tail(w_tail):
    rows_blk = 256
    n_cols = w_tail.shape[1]
    return pl.pallas_call(
        _split_tail_kernel,
        grid=(D_MODEL // rows_blk,),
        in_specs=[pl.BlockSpec((rows_blk, n_cols), lambda i: (i, 0))],
        out_specs=[pl.BlockSpec((rows_blk, FN_G * FN_C), lambda i: (i, 0)),
                   pl.BlockSpec((rows_blk, LANES), lambda i: (i, 0))],
        out_shape=[jax.ShapeDtypeStruct((D_MODEL, FN_G * FN_C), BF16),
                   jax.ShapeDtypeStruct((D_MODEL, LANES), BF16)],
        compiler_params=_cparams(("arbitrary",)),
        name="split_tail",
    )(w_tail)


def _inproj_kernel(xp_ref, xs_ref, mod_ref, g_ref, wa_ref, wu_ref, wr_ref, proj_ref, r_ref):
    i = pl.program_id(0)
    is_ctx = i < T_CTX // TM_IN
    sh1 = mod_ref[:, 0:D_MODEL]
    sc1 = mod_ref[:, D_MODEL:2 * D_MODEL]
    for hf in range(TM_IN // TM_IN_HALF):
        rows = slice(hf * TM_IN_HALF, (hf + 1) * TM_IN_HALF)
        x = jnp.where(is_ctx, xp_ref[rows, :], xs_ref[rows, :])
        hb = (_rms(x, g_ref[...]) * (1.0 + sc1) + sh1).astype(BF16)
        r_ref[rows, :] = _dot(hb, wr_ref[...])
        for n in range(N_QKVG // TN_IN):
            cols = slice(n * TN_IN, (n + 1) * TN_IN)
            proj_ref[rows, cols] = _dot(hb, wa_ref[:, cols]).astype(BF16)
        for n in range((N_MAIN - N_QKVG) // TN_IN):
            cols = slice(N_QKVG + n * TN_IN, N_QKVG + (n + 1) * TN_IN)
            proj_ref[rows, cols] = _dot(hb, wu_ref[:, n * TN_IN:(n + 1) * TN_IN]).astype(BF16)


def _inproj(xp, xs, mod3, g_attn, w_a, w_u, w_r):
    nt = T_ALL // TM_IN
    nctx = T_CTX // TM_IN
    resident = pl.Buffered(1)
    return pl.pallas_call(
        _inproj_kernel,
        grid=(nt,),
        in_specs=[pl.BlockSpec((TM_IN, D_MODEL), lambda i: (jnp.minimum(i, nctx - 1), 0)),
                  pl.BlockSpec((TM_IN, D_MODEL), lambda i: (jnp.maximum(i - nctx, 0), 0)),
                  pl.BlockSpec((None, 1, 6 * D_MODEL), lambda i: (_cond_row(i, TM_IN), 0, 0)),
                  pl.BlockSpec((1, D_MODEL), lambda i: (0, 0)),
                  pl.BlockSpec((D_MODEL, N_QKVG), lambda i: (0, 0), pipeline_mode=resident),
                  pl.BlockSpec((D_MODEL, N_MAIN - N_QKVG), lambda i: (0, 0), pipeline_mode=resident),
                  pl.BlockSpec((D_MODEL, LANES), lambda i: (0, 0), pipeline_mode=resident)],
        out_specs=[pl.BlockSpec((TM_IN, N_MAIN), lambda i: (i, 0)),
                   pl.BlockSpec((TM_IN, LANES), lambda i: (i, 0))],
        out_shape=[jax.ShapeDtypeStruct((T_ALL, N_MAIN), BF16),
                   jax.ShapeDtypeStruct((T_ALL, LANES), F32)],
        compiler_params=_cparams(("arbitrary",)),
        name="inproj",
    )(xp, xs, mod3, g_attn, w_a, w_u, w_r)


SB = 256
CPB = SB // CHUNK


def _gla_kernel(q_ref, k_ref, v_ref, og_ref, r_ref, wgh_ref, wgl_ref, bg_ref, ng_ref,
                tlu_ref, s0f_ref, s0b_ref,
                o_ref, sf_ref, sb_ref,
                cum_s, last_s, qe_s, ke_s, kd_s, dec_s, kv_s, sbs_s, *, seq_len, n_in_step):
    sb_per_seq = seq_len // SB
    ch_per_seq = seq_len // CHUNK
    n_sb = n_in_step * sb_per_seq
    lane = lax.broadcasted_iota(I32, (1, LANES), 1)
    m_f = lane < DK

    r_hi, r_lo = _split2(r_ref[...])
    z = _dot(r_hi, wgh_ref[...]) + _dot(r_lo, wgh_ref[...]) + _dot(r_hi, wgl_ref[...]) + bg_ref[...]
    g_all = (jnp.minimum(z, 0.0) - jnp.log(1.0 + jnp.exp(-jnp.abs(z)))) * (1.0 / TAU)

    q_pair = q_ref[...].astype(F32)
    k_pair = k_ref[...].astype(F32)
    q_roll = pltpu.roll(q_pair, DK, axis=1)
    k_roll = pltpu.roll(k_pair, DK, axis=1)

    row_b = lax.broadcasted_iota(I32, (SB, SB), 0)
    col_b = lax.broadcasted_iota(I32, (SB, SB), 1)
    same_chunk = (row_b // CHUNK) == (col_b // CHUNK)
    tril = same_chunk & (row_b >= col_b)
    triu = same_chunk & (row_b <= col_b)
    row_chunk = lax.broadcasted_iota(I32, (SB, 1), 0) // CHUNK
    col_chunk = lax.broadcasted_iota(I32, (1, SB), 1) // CHUNK

    m_f2 = (lax.broadcasted_iota(I32, (1, 2 * LANES), 1) % LANES) < DK
    cum_s[...] = g_all

    def cum_body(s, carry):
        rows = pl.ds(pl.multiple_of(s * SB, SB), SB)
        g = cum_s[rows, :]
        f_hi, f_lo = _split2(jnp.where(m_f2, g, 0.0))
        b_hi, b_lo = _split2(jnp.where(m_f2, 0.0, g))
        cum = (_dot(tlu_ref[...], jnp.concatenate([f_hi, b_hi], axis=0))
               + _dot(tlu_ref[...], jnp.concatenate([f_lo, b_lo], axis=0)))
        cum_s[rows, :] = cum
        tots = []
        for c in range(CPB):
            tot = jnp.where(m_f2, cum[(c + 1) * CHUNK - 1:(c + 1) * CHUNK, :], cum[c * CHUNK:c * CHUNK + 1, :])
            dec_s[s * CPB + c] = jnp.exp(tot)
            tots.append(jnp.broadcast_to(tot, (CHUNK, 2 * LANES)))
        last_s[rows, :] = jnp.concatenate(tots, axis=0)
        return carry

    lax.fori_loop(0, n_sb, cum_body, 0)

    for j in range(2):
        cum = cum_s[:, j * LANES:(j + 1) * LANES]
        last = last_s[:, j * LANES:(j + 1) * LANES]
        if j == 0:
            q2 = jnp.where(m_f, q_pair, q_roll)
            k2 = jnp.where(m_f, k_pair, k_roll)
        else:
            q2 = jnp.where(m_f, q_roll, q_pair)
            k2 = jnp.where(m_f, k_roll, k_pair)
        qe_s[j] = ((q2 * (DK ** -0.5)) * jnp.exp(cum)).astype(BF16)
        ke_s[j] = (k2 * jnp.exp(-cum)).astype(BF16)
        kd_s[j] = (k2 * jnp.exp(last - cum)).astype(BF16)

    def kv_body(s, carry):
        rows = pl.ds(pl.multiple_of(s * SB, SB), SB)
        for j in range(2):
            v_t = jnp.transpose(v_ref[rows, j * DV:(j + 1) * DV].astype(F32)).astype(BF16)
            zero = jnp.zeros_like(v_t)
            v_st = jnp.concatenate([jnp.where(col_chunk == c, v_t, zero) for c in range(CPB)], axis=0)
            kv = _dot(v_st, kd_s[j, rows, :])
            for c in range(CPB):
                kv_s[j, s * CPB + c] = kv[c * LANES:(c + 1) * LANES, :]
        return carry

    lax.fori_loop(0, n_sb, kv_body, 0, unroll=min(2, n_sb))

    def dec_row(j, c):
        return dec_s[c][:, j * LANES:(j + 1) * LANES]

    def step_state(j, c, st):
        return st * dec_row(j, c) + kv_s[j, c]

    for q in range(n_in_step):
        st0 = tuple(jnp.transpose(jnp.concatenate([s0f_ref[q, j], s0b_ref[q, j]], axis=0)) for j in range(2))
        c0 = q * ch_per_seq
        s0 = q * sb_per_seq

        def bwd_body(t, sts, c0=c0):
            c = c0 + ch_per_seq - 1 - t
            for j in range(2):
                sbs_s[j, c] = sts[j]
            return tuple(step_state(j, c, sts[j]) for j in range(2))

        st_b_fin = lax.fori_loop(0, ch_per_seq, bwd_body, st0, unroll=CPB)

        def fwd_body(sl, sts, s0=s0):
            s = s0 + sl
            rows = pl.ds(pl.multiple_of(s * SB, SB), SB)
            new = []
            for j in range(2):
                lo = j * DV
                qe = qe_s[j, rows, :]
                ke = ke_s[j, rows, :]
                v_b = v_ref[rows, lo:lo + DV].astype(BF16)
                zero = jnp.zeros_like(qe)
                st = sts[j]
                q_parts, s_parts = [], []
                for c in range(CPB):
                    ci = s * CPB + c
                    s_parts.append(jnp.where(m_f, st, sbs_s[j, ci]).astype(BF16))
                    q_parts.append(jnp.where(row_chunk == c, qe, zero))
                    st = step_state(j, ci, st)
                o = _dot_nt(jnp.concatenate(q_parts, axis=1), jnp.concatenate(s_parts, axis=1))
                q_st = jnp.concatenate([jnp.where(m_f, qe, zero), jnp.where(m_f, zero, qe)], axis=0)
                sc = _dot_nt(q_st, ke)
                p = jnp.where(tril, sc[0:SB, :], 0.0) + jnp.where(triu, sc[SB:2 * SB, :], 0.0)
                o = o + _dot(p.astype(BF16), v_b)
                o = o * lax.rsqrt(jnp.mean(o * o, axis=-1, keepdims=True) + EPS) * ng_ref[:, lo:lo + DV]
                o_ref[rows, lo:lo + DV] = (o * _silu(og_ref[rows, lo:lo + DV].astype(F32))).astype(BF16)
                new.append(st)
            return tuple(new)

        st_f_fin = lax.fori_loop(0, sb_per_seq, fwd_body, st0, unroll=min(2, sb_per_seq))
        for j in range(2):
            sf_ref[q, j] = jnp.transpose(st_f_fin[j])[0:DK, :]
            sb_ref[q, j] = jnp.transpose(st_b_fin[j])[DK:2 * DK, :]


def _gla(proj, r_all, wg_hi, wg_lo, bg, ng, tlu, s0f, s0b, *, seq_len, n_seq, n_in_step, row0):
    kern = functools.partial(_gla_kernel, seq_len=seq_len, n_in_step=n_in_step)
    rows = n_in_step * seq_len
    n_chunks = rows // CHUNK
    rb = lambda b: b + row0 // rows
    st_spec = pl.BlockSpec((n_in_step, 2, DK, DV), lambda b, h: (b, h, 0, 0))
    return pl.pallas_call(
        kern,
        grid=(n_seq // n_in_step, HEADS // 2),
        in_specs=[pl.BlockSpec((rows, LANES), lambda b, h: (rb(b), h)),
                  pl.BlockSpec((rows, LANES), lambda b, h: (rb(b), QK_W // LANES + h)),
                  pl.BlockSpec((rows, 2 * DV), lambda b, h: (rb(b), 2 * QK_W // (2 * DV) + h)),
                  pl.BlockSpec((rows, 2 * DV), lambda b, h: (rb(b), (2 * QK_W + V_W) // (2 * DV) + h)),
                  pl.BlockSpec((rows, LANES), lambda b, h: (rb(b), 0)),
                  pl.BlockSpec((None, LANES, 2 * LANES), lambda b, h: (h, 0, 0)),
                  pl.BlockSpec((None, LANES, 2 * LANES), lambda b, h: (h, 0, 0)),
                  pl.BlockSpec((None, 1, 2 * LANES), lambda b, h: (h, 0, 0)),
                  pl.BlockSpec((None, 1, 2 * DV), lambda b, h: (h, 0, 0)),
                  pl.BlockSpec((SB, 2 * SB), lambda b, h: (0, 0)),
                  st_spec, st_spec],
        out_specs=[pl.BlockSpec((rows, 2 * DV), lambda b, h: (b, h)), st_spec, st_spec],
        out_shape=[jax.ShapeDtypeStruct((n_seq * seq_len, V_W), BF16),
                   jax.ShapeDtypeStruct((n_seq, HEADS, DK, DV), F32),
                   jax.ShapeDtypeStruct((n_seq, HEADS, DK, DV), F32)],
        scratch_shapes=[pltpu.VMEM((rows, 2 * LANES), F32),
                        pltpu.VMEM((rows, 2 * LANES), F32),
                        pltpu.VMEM((2, rows, LANES), BF16),
                        pltpu.VMEM((2, rows, LANES), BF16),
                        pltpu.VMEM((2, rows, LANES), BF16),
                        pltpu.VMEM((n_chunks, 1, 2 * LANES), F32),
                        pltpu.VMEM((2, n_chunks, DV, LANES), F32),
                        pltpu.VMEM((2, n_chunks, DV, LANES), F32)],
        compiler_params=_cparams(("arbitrary", "arbitrary")),
        name="gla_%d" % seq_len,
    )(proj, proj, proj, proj, r_all, wg_hi, wg_lo, bg, ng, tlu, s0f, s0b)


def _fnet_stage_a(u_bf, cs):
    cparts, sparts = [], []
    for g in range(FN_G):
        ab = _dot(u_bf[:, g * FN_C:(g + 1) * FN_C], cs)
        cparts.append(ab[:, 0:FN_C])
        sparts.append(ab[:, FN_C:2 * FN_C])
    return jnp.concatenate(cparts, axis=1), jnp.concatenate(sparts, axis=1)


def _fnet_ctx_kernel(u_ref, cs_ref, p2_ref, f_ref):
    uc, us = _fnet_stage_a(u_ref[...].astype(BF16), cs_ref[...].astype(BF16))
    ab = jnp.concatenate([uc, us], axis=0).astype(BF16)
    f_ref[...] = _dot(p2_ref[...].astype(BF16), ab).astype(BF16)


def _fnet_ctx(proj, cs, p2):
    return pl.pallas_call(
        _fnet_ctx_kernel,
        grid=(N_CTX,),
        in_specs=[pl.BlockSpec((L_CTX, FN_G * FN_C), lambda b: (b, 3)),
                  pl.BlockSpec((FN_C, 2 * FN_C), lambda b: (0, 0)),
                  pl.BlockSpec((L_CTX, 2 * L_CTX), lambda b: (0, 0))],
        out_specs=pl.BlockSpec((L_CTX, FN_G * FN_C), lambda b: (b, 0)),
        out_shape=jax.ShapeDtypeStruct((T_CTX, FN_G * FN_C), BF16),
        compiler_params=_cparams(("arbitrary",)),
        name="fnet_ctx",
    )(proj, cs, p2)


TM_FL = 256
RT_FL = 256


def _fnet_lat_kernel(u_ref, cs_ref, kr_ref, f_ref, ab_s):
    m = pl.program_id(1)

    @pl.when(m == 0)
    def _():
        def body(t, carry):
            rows = pl.ds(pl.multiple_of(t * RT_FL, RT_FL), RT_FL)
            uc, us = _fnet_stage_a(u_ref[rows, :].astype(BF16), cs_ref[...].astype(BF16))
            ab_s[rows, :] = uc.astype(BF16)
            ab_s[pl.ds(pl.multiple_of(L_LAT + t * RT_FL, RT_FL), RT_FL), :] = us.astype(BF16)
            return carry
        lax.fori_loop(0, L_LAT // RT_FL, body, 0)

    f_ref[...] = _dot(kr_ref[...].astype(BF16), ab_s[...]).astype(BF16)


def _fnet_lat(proj, cs, kr):
    nm = L_LAT // TM_FL
    return pl.pallas_call(
        _fnet_lat_kernel,
        grid=(N_LAT, nm),
        in_specs=[pl.BlockSpec((L_LAT, FN_G * FN_C), lambda b, m: (T_CTX // L_LAT + b, 3)),
                  pl.BlockSpec((FN_C, 2 * FN_C), lambda b, m: (0, 0)),
                  pl.BlockSpec((TM_FL, 2 * L_LAT), lambda b, m: (m, 0))],
        out_specs=pl.BlockSpec((TM_FL, FN_G * FN_C), lambda b, m: (b * nm + m, 0)),
        out_shape=jax.ShapeDtypeStruct((T_LAT, FN_G * FN_C), BF16),
        scratch_shapes=[pltpu.VMEM((2 * L_LAT, FN_G * FN_C), BF16)],
        compiler_params=_cparams(("arbitrary", "arbitrary")),
        name="fnet_lat",
    )(proj, cs, kr)


TM_OUT = 256
LANE_E0 = N_GROUPS
ROWS_PER_BLK = 8
PACK_ROWS = -(-(2 * TM_OUT + N_EXP * (ROWS_PER_BLK - 1)) // 256) * 256
PACK_BLKS = PACK_ROWS // ROWS_PER_BLK
N_TOK_TILES = T_ALL // TM_OUT
BLK_PER_TILE = TM_MOE // ROWS_PER_BLK
USED_BLKS = (2 * TM_OUT + N_EXP * (ROWS_PER_BLK - 1)) // ROWS_PER_BLK
HS_ROWS = N_TOK_TILES * PACK_ROWS
assert (PACK_BLKS - USED_BLKS) * N_TOK_TILES >= 2 * BLK_PER_TILE


def _outproj_kernel(oc_ref, ol_ref, fc_ref, fl_ref, xp_ref, xs_ref, mod_ref, g_ref, wo_ref, wf_ref,
                    wrc_ref, br_ref, sut_ref, sl_ref,
                    x1_ref, hs_ref, rw_ref, nb_ref, lb_ref):
    i = pl.program_id(0)
    is_ctx = i < T_CTX // TM_OUT

    o = jnp.where(is_ctx, oc_ref[...], ol_ref[...]).astype(BF16)
    f = jnp.where(is_ctx, fc_ref[...], fl_ref[...]).astype(BF16)
    x = jnp.where(is_ctx, xp_ref[...], xs_ref[...])
    y = _dot(o, wo_ref[...]) + _dot(f, wf_ref[...])
    ga1 = mod_ref[:, 2 * D_MODEL:3 * D_MODEL]
    sh2 = mod_ref[:, 3 * D_MODEL:4 * D_MODEL]
    sc2 = mod_ref[:, 4 * D_MODEL:5 * D_MODEL]
    x1 = x + ga1 * y
    x1_ref[...] = x1
    h2 = _rms(x1, g_ref[...]) * (1.0 + sc2) + sh2

    h_hi, h_lo = _split2(h2)
    hw = _dot(h_hi, wrc_ref[...])
    lg_all = jnp.transpose(hw[:, 0:LANES] + _dot(h_lo, wrc_ref[:, 0:LANES]) + hw[:, LANES:2 * LANES]
                           + br_ref[...])

    row_i = lax.broadcasted_iota(I32, (LANES, TM_OUT), 0)
    row = row_i.astype(F32)
    neg = jnp.float32(-jnp.inf)
    big = jnp.float32(LANES)
    lg = jnp.where(row_i < N_GROUPS, lg_all, neg)
    gmax = jnp.max(lg, axis=0, keepdims=True)
    gsel = jnp.min(jnp.where(lg == gmax, row, big), axis=0, keepdims=True)
    den = jnp.sum(jnp.exp(lg - gmax), axis=0, keepdims=True)
    pg_sel = 1.0 / den

    e_idx = row_i - LANE_E0
    egrp = (e_idx >> 3).astype(F32)
    emask = (e_idx >= 0) & (e_idx < N_EXP) & (egrp == gsel)
    m1 = jnp.where(emask, lg_all, neg)
    v1 = jnp.max(m1, axis=0, keepdims=True)
    i1 = jnp.min(jnp.where(m1 == v1, row, big), axis=0, keepdims=True)
    m2 = jnp.where(row == i1, neg, m1)
    v2 = jnp.max(m2, axis=0, keepdims=True)
    i2 = jnp.min(jnp.where(m2 == v2, row, big), axis=0, keepdims=True)
    e2 = jnp.exp(v2 - v1)
    inv = 1.0 / (1.0 + e2)
    w1 = inv * pg_sel
    w2 = (e2 * inv) * pg_sel

    oh1 = row == i1
    oh2 = row == i2
    oh = jnp.where(oh1 | oh2, 1.0, 0.0).astype(BF16)
    cnt = _dot(oh, jnp.ones((TM_OUT, LANES), BF16))
    nblk = jnp.floor((cnt + (ROWS_PER_BLK - 1)) * (1.0 / ROWS_PER_BLK))
    lboff = _dot(sl_ref[...], nblk.astype(BF16))
    lrank = _dot(oh, sut_ref[...])
    posmat = jnp.concatenate([lboff, lboff], axis=1) * ROWS_PER_BLK + lrank
    p1 = jnp.sum(jnp.where(oh1, posmat, 0.0), axis=0, keepdims=True)
    p2 = jnp.sum(jnp.where(oh2, posmat, 0.0), axis=0, keepdims=True)
    nb_ref[...] = nblk[:, 0:8].astype(I32)
    lb_ref[...] = lboff[:, 0:8].astype(I32)

    prow = lax.broadcasted_iota(I32, (PACK_ROWS, TM_OUT), 0).astype(F32)
    place = jnp.where((prow == p1) | (prow == p2), 1.0, 0.0).astype(BF16)
    hs_ref[...] = _dot(place, h_hi).astype(BF16)

    rw_ref[...] = jnp.concatenate([w1, w2, p1, p2, jnp.zeros((4, TM_OUT), F32)], axis=0)


def _outproj(o_ctx, o_lat, f_ctx, f_lat, xp, xs, mod3, g_ffn, wo, wf, wr_cat, br, sut, sl):
    nt = T_ALL // TM_OUT
    nctx = T_CTX // TM_OUT
    ctx_map = lambda i: (jnp.minimum(i, nctx - 1), 0)
    lat_map = lambda i: (jnp.maximum(i - nctx, 0), 0)
    const = lambda i: (0, 0)
    return pl.pallas_call(
        _outproj_kernel,
        grid=(nt,),
        in_specs=[pl.BlockSpec((TM_OUT, V_W), ctx_map),
                  pl.BlockSpec((TM_OUT, V_W), lat_map),
                  pl.BlockSpec((TM_OUT, FN_G * FN_C), ctx_map),
                  pl.BlockSpec((TM_OUT, FN_G * FN_C), lat_map),
                  pl.BlockSpec((TM_OUT, D_MODEL), ctx_map),
                  pl.BlockSpec((TM_OUT, D_MODEL), lat_map),
                  pl.BlockSpec((None, 1, 6 * D_MODEL), lambda i: (_cond_row(i, TM_OUT), 0, 0)),
                  pl.BlockSpec((1, D_MODEL), const),
                  pl.BlockSpec((V_W, D_MODEL), const),
                  pl.BlockSpec((FN_G * FN_C, D_MODEL), const),
                  pl.BlockSpec((D_MODEL, 2 * LANES), const),
                  pl.BlockSpec((1, LANES), const),
                  pl.BlockSpec((TM_OUT, TM_OUT), const),
                  pl.BlockSpec((LANES, LANES), const)],
        out_specs=[pl.BlockSpec((TM_OUT, D_MODEL), lambda i: (i, 0)),
                   pl.BlockSpec((PACK_ROWS, D_MODEL), lambda i: (i, 0)),
                   pl.BlockSpec((8, TM_OUT), lambda i: (0, i)),
                   pl.BlockSpec((None, LANES, 8), lambda i: (i, 0, 0)),
                   pl.BlockSpec((None, LANES, 8), lambda i: (i, 0, 0))],
        out_shape=[jax.ShapeDtypeStruct((T_ALL, D_MODEL), F32),
                   jax.ShapeDtypeStruct((HS_ROWS, D_MODEL), BF16),
                   jax.ShapeDtypeStruct((8, T_ALL), F32),
                   jax.ShapeDtypeStruct((nt, LANES, 8), I32),
                   jax.ShapeDtypeStruct((nt, LANES, 8), I32)],
        compiler_params=_cparams(("arbitrary",)),
        name="outproj",
    )(o_ctx, o_lat, f_ctx, f_lat, xp, xs, mod3, g_ffn, wo, wf, wr_cat, br, sut, sl)


SRC_BITS = 16
SRC_MASK = (1 << SRC_BITS) - 1
X_SLOTS = 3
Y_SLOTS = 3
N_UP_CHUNKS = 2
N_DN_CHUNKS = 8
NT_MOE = (2 * T_ALL + N_TOK_TILES * N_EXP * (ROWS_PER_BLK - 1)) // TM_MOE + N_EXP


def _moe_kernel(texp_ref, nexp_ref, meta_ref, code_ref,
                h_hbm, wg_hbm, wu_hbm, wd_hbm,
                out_hbm,
                xbuf, ybuf, wg_f, wu_f, wd_f, wg_s, wu_s, wd_s, kcount, gsem, ssem, wsem):
    i = pl.program_id(0)
    nt = meta_ref[0]
    xs = i % X_SLOTS

    def blk_rows(b):
        if isinstance(b, int):
            return pl.ds(b * ROWS_PER_BLK, ROWS_PER_BLK)
        return pl.ds(pl.multiple_of(b * ROWS_PER_BLK, ROWS_PER_BLK), ROWS_PER_BLK)

    def gather_row(tile, sl, j):
        src = code_ref[(tile + 2) * BLK_PER_TILE + j] & SRC_MASK
        pltpu.make_async_copy(h_hbm.at[blk_rows(src)], xbuf.at[sl, blk_rows(j)], gsem.at[sl]).start()

    def scatter_row(tile, sl, j):
        dst = code_ref[(tile + 2) * BLK_PER_TILE + j] >> SRC_BITS
        pltpu.make_async_copy(ybuf.at[sl, blk_rows(j)], out_hbm.at[blk_rows(dst)], ssem.at[sl]).start(priority=1)

    def gather_wait(sl):
        pltpu.make_async_copy(h_hbm.at[pl.ds(0, TM_MOE)], xbuf.at[sl], gsem.at[sl]).wait()

    def scatter_wait(sl):
        pltpu.make_async_copy(ybuf.at[sl], out_hbm.at[pl.ds(0, TM_MOE)], ssem.at[sl]).wait()

    def y_slot(tile):
        return (tile + 1) % Y_SLOTS

    @pl.when(i == 0)
    def _():
        ybuf[y_slot(-2)] = jnp.zeros((TM_MOE, D_MODEL), BF16)
        ybuf[y_slot(-1)] = jnp.zeros((TM_MOE, D_MODEL), BF16)

        def body(j, c):
            gather_row(0, 0, j)
            gather_row(1, 1, j)
            scatter_row(-2, y_slot(-2), j)
            return c
        lax.fori_loop(0, BLK_PER_TILE, body, 0)

    @pl.when((i >= 1) & (i <= nt))
    def _():
        scatter_wait(y_slot(i - 3))

    @pl.when(i < nt)
    def _():
        prev = texp_ref[jnp.maximum(i - 1, 0)]

        def weight_copies(e, sl):
            return (pltpu.make_async_copy(wg_hbm.at[e], wg_f.at[sl], wsem.at[sl]),
                    pltpu.make_async_copy(wu_hbm.at[e], wu_f.at[sl], wsem.at[sl]),
                    pltpu.make_async_copy(wd_hbm.at[e], wd_f.at[sl], wsem.at[sl]))

        @pl.when(i == 0)
        def _():
            kcount[0] = 0
            for cp in weight_copies(texp_ref[0], 0):
                cp.start()

        @pl.when((i == 0) | (texp_ref[i] != prev))
        def _():
            k = kcount[0]
            for sl in range(2):
                @pl.when(k % 2 == sl)
                def _(sl=sl):
                    for cp in weight_copies(texp_ref[i], sl):
                        cp.wait()
                    def narrow(r, c):
                        up = pl.ds(pl.multiple_of(r * (D_MODEL // 16), D_MODEL // 16), D_MODEL // 16)
                        dn = pl.ds(pl.multiple_of(r * (D_EXP // 16), D_EXP // 16), D_EXP // 16)
                        wg_s[up, :] = wg_f[sl, up, :].astype(BF16)
                        wu_s[up, :] = wu_f[sl, up, :].astype(BF16)
                        wd_s[dn, :] = wd_f[sl, dn, :].astype(BF16)
                        return c
                    lax.fori_loop(0, 16, narrow, 0)

                    @pl.when(nexp_ref[i] >= 0)
                    def _():
                        for cp in weight_copies(nexp_ref[i], 1 - sl):
                            cp.start(priority=1)
            kcount[0] = k + 1

        gather_wait(xs)
        x = xbuf[xs].astype(BF16)

        issues = []
        for j in range(BLK_PER_TILE):
            issues.append(functools.partial(gather_row, i + 2, (i + 2) % X_SLOTS, j))
            issues.append(functools.partial(scatter_row, i - 1, y_slot(i - 1), j))
        n_groups = N_UP_CHUNKS + N_DN_CHUNKS
        per_group = -(-len(issues) // n_groups)

        def issue_group(k):
            for fn in issues[k * per_group:(k + 1) * per_group]:
                fn()

        wu_c = D_EXP // N_UP_CHUNKS
        hid = []
        for n in range(N_UP_CHUNKS):
            issue_group(n)
            g = _dot(x, wg_s[:, n * wu_c:(n + 1) * wu_c])
            u = _dot(x, wu_s[:, n * wu_c:(n + 1) * wu_c])
            hid.append((_silu(g) * u).astype(BF16))
        hid = jnp.concatenate(hid, axis=1)
        wd_c = D_MODEL // N_DN_CHUNKS
        ys = y_slot(i)
        for n in range(N_DN_CHUNKS):
            issue_group(N_UP_CHUNKS + n)
            ybuf[ys, :, n * wd_c:(n + 1) * wd_c] = _dot(hid, wd_s[:, n * wd_c:(n + 1) * wd_c]).astype(BF16)

    @pl.when(i == nt)
    def _():
        gather_wait(xs)
        gather_wait((i + 1) % X_SLOTS)

        def body(j, c):
            scatter_row(nt - 1, y_slot(nt - 1), j)
            return c
        lax.fori_loop(0, BLK_PER_TILE, body, 0)
        scatter_wait(y_slot(nt - 2))
        scatter_wait(y_slot(nt - 1))


def _moe(texp, nexp, meta, code, hs, w_eg, w_eu, w_ed):
    hbm = pl.BlockSpec(memory_space=pl.ANY)
    grid_spec = pltpu.PrefetchScalarGridSpec(
        num_scalar_prefetch=4,
        grid=(NT_MOE + 1,),
        in_specs=[hbm, hbm, hbm, hbm],
        out_specs=hbm,
        scratch_shapes=[pltpu.VMEM((X_SLOTS, TM_MOE, D_MODEL), BF16),
                        pltpu.VMEM((Y_SLOTS, TM_MOE, D_MODEL), BF16),
                        pltpu.VMEM((2, D_MODEL, D_EXP), F32),
                        pltpu.VMEM((2, D_MODEL, D_EXP), F32),
                        pltpu.VMEM((2, D_EXP, D_MODEL), F32),
                        pltpu.VMEM((D_MODEL, D_EXP), BF16),
                        pltpu.VMEM((D_MODEL, D_EXP), BF16),
                        pltpu.VMEM((D_EXP, D_MODEL), BF16),
                        pltpu.SMEM((1,), I32),
                        pltpu.SemaphoreType.DMA((X_SLOTS,)),
                        pltpu.SemaphoreType.DMA((Y_SLOTS,)),
                        pltpu.SemaphoreType.DMA((2,))])
    return pl.pallas_call(
        _moe_kernel,
        grid_spec=grid_spec,
        out_shape=jax.ShapeDtypeStruct((HS_ROWS, D_MODEL), BF16),
        input_output_aliases={4: 0},
        compiler_params=_cparams(("arbitrary",)),
        name="moe",
    )(texp, nexp, meta, code, hs, w_eg, w_eu, w_ed)


TM_FIN = TM_OUT


def _final_kernel(x1_ref, ys_pack_ref, rw_ref, mod_ref, g_ref, yp_ref, ys_ref):
    i = pl.program_id(0)
    ga2 = mod_ref[:, 5 * D_MODEL:6 * D_MODEL]
    w0 = rw_ref[0:1, :]
    w1 = rw_ref[1:2, :]
    p0 = rw_ref[2:3, :]
    p1 = rw_ref[3:4, :]
    prow = lax.broadcasted_iota(I32, (PACK_ROWS, TM_FIN), 0).astype(F32)
    comb_t = jnp.where(prow == p0, w0, 0.0) + jnp.where(prow == p1, w1, 0.0)
    y_moe = lax.dot_general(comb_t.astype(BF16), ys_pack_ref[...].astype(BF16), (((0,), (0,)), ((), ())),
                            preferred_element_type=F32)
    y = x1_ref[...] + ga2 * y_moe
    out = _rms(y, g_ref[...])

    @pl.when(i < T_CTX // TM_FIN)
    def _():
        yp_ref[...] = out

    @pl.when(i >= T_CTX // TM_FIN)
    def _():
        ys_ref[...] = out


def _final(x1, y2, rw, mod3, g_fin):
    nt = T_ALL // TM_FIN
    nctx = T_CTX // TM_FIN
    return pl.pallas_call(
        _final_kernel,
        grid=(nt,),
        in_specs=[pl.BlockSpec((TM_FIN, D_MODEL), lambda i: (i, 0)),
                  pl.BlockSpec((PACK_ROWS, D_MODEL), lambda i: (i, 0)),
                  pl.BlockSpec((8, TM_FIN), lambda i: (0, i)),
                  pl.BlockSpec((None, 1, 6 * D_MODEL), lambda i: (_cond_row(i, TM_FIN), 0, 0)),
                  pl.BlockSpec((1, D_MODEL), lambda i: (0, 0))],
        out_specs=[pl.BlockSpec((TM_FIN, D_MODEL), lambda i: (jnp.minimum(i, nctx - 1), 0)),
                   pl.BlockSpec((TM_FIN, D_MODEL), lambda i: (jnp.maximum(i - nctx, 0), 0))],
        out_shape=[jax.ShapeDtypeStruct((T_CTX, D_MODEL), F32),
                   jax.ShapeDtypeStruct((T_LAT, D_MODEL), F32)],
        compiler_params=_cparams(("arbitrary",)),
        name="final",
    )(x1, y2, rw, mod3, g_fin)


def _np_bf16(a):
    return jnp.asarray(np.asarray(a, np.float32), dtype=BF16)


def _np_f32(a):
    return jnp.asarray(np.asarray(a, np.float32))


@functools.lru_cache(maxsize=None)
def _constants():
    c = {}
    k = np.arange(FN_C)
    ang = 2.0 * np.pi * np.outer(k, k) / FN_C
    c["cs"] = np.concatenate([np.cos(ang), np.sin(ang)], axis=1) / np.sqrt(FN_C)
    p = np.arange(L_CTX)
    ang = 2.0 * np.pi * np.outer(p, p) / L_CTX
    c["p2"] = np.concatenate([np.cos(ang), -np.sin(ang)], axis=1) / np.sqrt(L_CTX)
    pos = np.arange(L_LAT)
    rr, cc = pos // GRID_W, pos % GRID_W
    num = (np.outer(rr, rr) * (GRID_W // GRID_H) + np.outer(cc, cc)) % GRID_W
    ang = 2.0 * np.pi * num / GRID_W
    c["kr"] = np.concatenate([np.cos(ang), -np.sin(ang)], axis=1) / np.sqrt(L_LAT)
    i = np.arange(SB)
    same = (i[:, None] // CHUNK) == (i[None, :] // CHUNK)
    tl = same & (i[:, None] >= i[None, :])
    tu = same & (i[:, None] <= i[None, :])
    c["tlu"] = np.concatenate([tl, tu], axis=1).astype(np.float32)
    c["sut"] = (i[:, None] < i[None, :]).astype(np.float32)
    k = np.arange(LANES)
    c["sl"] = (k[:, None] > k[None, :]).astype(np.float32)
    return c


def kernel(x_prompt, x_sample, state_gla_fwd, state_gla_bwd, c, c_ctx, w_ada, b_ada, norm_attn, norm_ffn, w_in, w_gate_fwd, b_gate_fwd, w_gate_bwd, b_gate_bwd, norm_gla, w_out, w_router_group, b_router_group, w_router_expert, b_router_expert, w_expert_gate, w_expert_up, w_expert_down, norm_final):
    assert w_ada.shape[0] == 1, "single layer"
    cst = _constants()
    cs, p2, kr = _np_f32(cst["cs"]), _np_f32(cst["p2"]), _np_f32(cst["kr"])
    tlu, sut, sl = _np_bf16(cst["tlu"]), _np_bf16(cst["sut"]), _np_bf16(cst["sl"])

    xp = x_prompt.reshape(T_CTX, D_MODEL)
    xs = x_sample.reshape(T_LAT, D_MODEL)

    cond8 = jnp.concatenate([c_ctx[None, :], c, jnp.zeros((3, D_MODEL), F32)], axis=0)
    mod = _ada(cond8, w_ada[0], b_ada[0][None, :])
    mod3 = mod.reshape(8, 1, 6 * D_MODEL)

    wi = w_in[0]
    i_og = 2 * QK_W + 2 * V_W
    i_u = i_og + 2 * RANK
    w_a = wi[:, :i_og].astype(BF16)
    w_u, w_r = _split_tail(wi[:, i_og:])

    wgf = w_gate_fwd[0].reshape(RANK, HEADS, DK)
    wgb = w_gate_bwd[0].reshape(RANK, HEADS, DK)
    zf = jnp.zeros_like(wgf)
    top = jnp.stack([wgf, zf], axis=2)
    bot = jnp.stack([zf, wgb], axis=2)
    wg = jnp.concatenate([top, bot], axis=0)
    wg = wg.reshape(2 * RANK, HEADS // 2, 4 * DK).transpose(1, 0, 2)
    wg = jnp.pad(wg, ((0, 0), (0, LANES - 2 * RANK), (0, 0)))
    wg_hi = wg.astype(BF16)
    wg_lo = (wg - wg_hi.astype(F32)).astype(BF16)
    bg = jnp.stack([b_gate_fwd[0].reshape(HEADS, DK), b_gate_bwd[0].reshape(HEADS, DK)], axis=1)
    bg = bg.reshape(HEADS // 2, 1, 4 * DK)
    ng = norm_gla[0].reshape(HEADS // 2, 1, 2 * DV)

    proj, r_all = _inproj(xp, xs, mod3, norm_attn, w_a, w_u, w_r)

    zero_state = jnp.zeros((N_CTX, HEADS, DK, DV), F32)
    o_ctx, sf_ctx, sb_ctx = _gla(proj, r_all, wg_hi, wg_lo, bg, ng, tlu, zero_state, zero_state,
                                 seq_len=L_CTX, n_seq=N_CTX, n_in_step=4, row0=0)
    o_lat, _, _ = _gla(proj, r_all, wg_hi, wg_lo, bg, ng, tlu,
                       state_gla_fwd[:, 0], state_gla_bwd[:, 0],
                       seq_len=L_LAT, n_seq=N_LAT, n_in_step=1, row0=T_CTX)

    f_ctx = _fnet_ctx(proj, cs, p2)
    f_lat = _fnet_lat(proj, cs, kr)

    wo = w_out[0][:V_W].astype(BF16)
    wf = w_out[0][V_W:].astype(BF16)
    wr = jnp.concatenate([w_router_group[0], w_router_expert[0]], axis=1)
    wr = jnp.pad(wr, ((0, 0), (0, LANES - N_GROUPS - N_EXP)))
    wr_hi = wr.astype(BF16)
    wr_lo = (wr - wr_hi.astype(F32)).astype(BF16)
    wr_cat = jnp.concatenate([wr_hi, wr_lo], axis=1)
    br = jnp.pad(jnp.concatenate([b_router_group[0], b_router_expert[0]]), (0, LANES - N_GROUPS - N_EXP))[None, :]

    x1, hs, rw, nb, lb = _outproj(o_ctx, o_lat, f_ctx, f_lat, xp, xs, mod3, norm_ffn, wo, wf,
                                  wr_cat, br, sut, sl)

    nb_e = nb[:, LANE_E0:LANE_E0 + N_EXP, 0].T
    lb_e = lb[:, LANE_E0:LANE_E0 + N_EXP, 0].T
    run_end = jnp.cumsum(nb_e, axis=1)
    blocks_e = run_end[:, -1]
    tiles_e = (blocks_e + BLK_PER_TILE - 1) // BLK_PER_TILE
    tile_end = jnp.cumsum(tiles_e)
    tile_start = tile_end - tiles_e
    n_tiles = tile_end[-1]
    n_code_tiles = NT_MOE + 4
    tile = jnp.arange(n_code_tiles, dtype=I32) - 2
    tile_c = jnp.clip(tile, 0, n_tiles - 1)
    t_exp = jnp.sum(tile_c[:, None] >= tile_end[None, :], axis=1)

    pick = t_exp[:, None] == jnp.arange(N_EXP, dtype=I32)[None, :]

    def per_tile(table):
        if table.ndim == 1:
            return jnp.sum(jnp.where(pick, table[None, :], 0), axis=1)
        return jnp.sum(jnp.where(pick[:, :, None], table[None, :, :], 0), axis=1)

    ends = per_tile(run_end)
    starts = ends - per_tile(nb_e)
    offs = per_tile(lb_e) + jnp.arange(N_TOK_TILES, dtype=I32)[None, :] * PACK_BLKS - starts
    j = jnp.arange(BLK_PER_TILE, dtype=I32)
    bi = ((tile_c - per_tile(tile_start)) * BLK_PER_TILE)[:, None] + j[None, :]
    in_run = (starts.T[:, :, None] <= bi[None, :, :]) & (bi[None, :, :] < ends.T[:, :, None])
    blk = bi + jnp.sum(jnp.where(in_run, offs.T[:, :, None], 0), axis=0)
    valid = (tile == tile_c)[:, None] & (bi < per_tile(blocks_e)[:, None])
    spare_ix = (jnp.arange(n_code_tiles, dtype=I32) % 2)[:, None] * BLK_PER_TILE + j[None, :]
    spare = (spare_ix % N_TOK_TILES) * PACK_BLKS + USED_BLKS + spare_ix // N_TOK_TILES
    code = jnp.where(valid, (blk << SRC_BITS) | blk, (spare << SRC_BITS) | blk[:, 0:1]).astype(I32).reshape(-1)
    texp = t_exp[2:NT_MOE + 3].astype(I32)
    e_ix = jnp.arange(N_EXP, dtype=I32)
    later = (e_ix[None, :] > e_ix[:, None]) & (tiles_e[None, :] > 0)
    next_e = jnp.min(jnp.where(later, e_ix[None, :], N_EXP), axis=1)
    next_e = jnp.where(next_e < N_EXP, next_e, -1)
    nexp = per_tile(next_e)[2:NT_MOE + 3].astype(I32)
    meta = n_tiles.reshape(1).astype(I32)

    y2 = _moe(texp, nexp, meta, code, hs, w_expert_gate[0], w_expert_up[0], w_expert_down[0])
    y_prompt, y_sample = _final(x1, y2, rw, mod3, norm_final[None, :])

    st_shape = (N_CTX, 1, HEADS, DK, DV)
    return (y_prompt.reshape(N_CTX, L_CTX, D_MODEL), y_sample.reshape(N_LAT, L_LAT, D_MODEL),
            sf_ctx.reshape(st_shape), sb_ctx.reshape(st_shape))
```

```python
import functools

import numpy as np
import jax
import jax.numpy as jnp
from jax import lax
from jax.experimental import pallas as pl
from jax.experimental.pallas import tpu as pltpu

F32 = jnp.float32
BF16 = jnp.bfloat16
I32 = jnp.int32

D_MODEL = 2048
N_CTX = 32
L_CTX = 256
N_LAT = 4
L_LAT = 2048
GRID_H = 32
GRID_W = 64
T_CTX = N_CTX * L_CTX
T_LAT = N_LAT * L_LAT
T_ALL = T_CTX + T_LAT
HEADS = 8
DK = 64
DV = 128
RANK = 16
TAU = 16.0
CHUNK = 64
FN_G = 8
FN_C = 128
QK_W = HEADS * DK
V_W = HEADS * DV
N_GROUPS = 4
EPG = 8
N_EXP = N_GROUPS * EPG
D_EXP = 512
EPS = 1e-6

LANES = 128
VMEM_LIMIT = 56 * 1024 * 1024

TM_MOE = 256


def _dot(a, b):
    return jnp.dot(a, b, preferred_element_type=F32)


def _dot_nt(a, b):
    return lax.dot_general(a, b, (((1,), (1,)), ((), ())), preferred_element_type=F32)


def _split2(x):
    hi = x.astype(BF16)
    lo = (x - hi.astype(F32)).astype(BF16)
    return hi, lo


def _silu(x):
    return x * (1.0 / (1.0 + jnp.exp(-x)))


def _rms(x, g):
    return x * lax.rsqrt(jnp.mean(x * x, axis=-1, keepdims=True) + EPS) * g


def _cparams(sem):
    return pltpu.CompilerParams(dimension_semantics=sem, vmem_limit_bytes=VMEM_LIMIT)


def _ada_kernel(c_ref, w_ref, b_ref, o_ref):
    s_hi, s_lo = _split2(_silu(c_ref[...]))
    w = w_ref[...]
    w_hi = w.astype(BF16)
    w_lo = (w - w_hi.astype(F32)).astype(BF16)
    o_ref[...] = _dot(s_hi, w_hi) + _dot(s_lo, w_hi) + _dot(s_hi, w_lo) + b_ref[...]


def _ada(cond8, w_ada, b_ada):
    tn = 1024
    n6 = 6 * D_MODEL
    return pl.pallas_call(
        _ada_kernel,
        grid=(n6 // tn,),
        in_specs=[pl.BlockSpec((8, D_MODEL), lambda j: (0, 0)),
                  pl.BlockSpec((D_MODEL, tn), lambda j: (0, j)),
                  pl.BlockSpec((1, tn), lambda j: (0, j))],
        out_specs=pl.BlockSpec((8, tn), lambda j: (0, j)),
        out_shape=jax.ShapeDtypeStruct((8, n6), F32),
        compiler_params=_cparams(("arbitrary",)),
        name="ada",
    )(cond8, w_ada, b_ada)


TM_IN = 512
TM_IN_HALF = 256
TN_IN = 1024
N_MAIN = 4096
N_QKVG = 3072


def _cond_row(tile, tm):
    ctx_tiles = T_CTX // tm
    per_seq = L_LAT // tm
    return jnp.where(tile < ctx_tiles, 0, 1 + (jnp.maximum(tile - ctx_tiles, 0)) // per_seq)


def _inproj_kernel(xp_ref, xs_ref, mod_ref, g_ref, wa_ref, wu_ref, wr_ref, proj_ref, r_ref):
    i = pl.program_id(0)
    is_ctx = i < T_CTX // TM_IN
    sh1 = mod_ref[:, 0:D_MODEL]
    sc1 = mod_ref[:, D_MODEL:2 * D_MODEL]
    for hf in range(TM_IN // TM_IN_HALF):
        rows = slice(hf * TM_IN_HALF, (hf + 1) * TM_IN_HALF)
        x = jnp.where(is_ctx, xp_ref[rows, :], xs_ref[rows, :])
        hb = (_rms(x, g_ref[...]) * (1.0 + sc1) + sh1).astype(BF16)
        r_ref[rows, :] = _dot(hb, wr_ref[...])
        for n in range(N_QKVG // TN_IN):
            cols = slice(n * TN_IN, (n + 1) * TN_IN)
            proj_ref[rows, cols] = _dot(hb, wa_ref[:, cols]).astype(BF16)
        for n in range((N_MAIN - N_QKVG) // TN_IN):
            cols = slice(N_QKVG + n * TN_IN, N_QKVG + (n + 1) * TN_IN)
            proj_ref[rows, cols] = _dot(hb, wu_ref[:, n * TN_IN:(n + 1) * TN_IN]).astype(BF16)


def _inproj(xp, xs, mod3, g_attn, w_a, w_u, w_r):
    nt = T_ALL // TM_IN
    nctx = T_CTX // TM_IN
    resident = pl.Buffered(1)
    return pl.pallas_call(
        _inproj_kernel,
        grid=(nt,),
        in_specs=[pl.BlockSpec((TM_IN, D_MODEL), lambda i: (jnp.minimum(i, nctx - 1), 0)),
                  pl.BlockSpec((TM_IN, D_MODEL), lambda i: (jnp.maximum(i - nctx, 0), 0)),
                  pl.BlockSpec((None, 1, 6 * D_MODEL), lambda i: (_cond_row(i, TM_IN), 0, 0)),
                  pl.BlockSpec((1, D_MODEL), lambda i: (0, 0)),
                  pl.BlockSpec((D_MODEL, N_QKVG), lambda i: (0, 0), pipeline_mode=resident),
                  pl.BlockSpec((D_MODEL, N_MAIN - N_QKVG), lambda i: (0, 0), pipeline_mode=resident),
                  pl.BlockSpec((D_MODEL, LANES), lambda i: (0, 0), pipeline_mode=resident)],
        out_specs=[pl.BlockSpec((TM_IN, N_MAIN), lambda i: (i, 0)),
                   pl.BlockSpec((TM_IN, LANES), lambda i: (i, 0))],
        out_shape=[jax.ShapeDtypeStruct((T_ALL, N_MAIN), BF16),
                   jax.ShapeDtypeStruct((T_ALL, LANES), F32)],
        compiler_params=_cparams(("arbitrary",)),
        name="inproj",
    )(xp, xs, mod3, g_attn, w_a, w_u, w_r)


SB = 256
CPB = SB // CHUNK


def _gla_kernel(q_ref, k_ref, v_ref, og_ref, r_ref, wgh_ref, wgl_ref, bg_ref, ng_ref,
                tlu_ref, s0f_ref, s0b_ref,
                o_ref, sf_ref, sb_ref,
                cum_s, last_s, qe_s, ke_s, kd_s, dec_s, kv_s, sbs_s, *, seq_len, n_in_step):
    sb_per_seq = seq_len // SB
    ch_per_seq = seq_len // CHUNK
    n_sb = n_in_step * sb_per_seq
    lane = lax.broadcasted_iota(I32, (1, LANES), 1)
    m_f = lane < DK

    r_hi, r_lo = _split2(r_ref[...])
    z = _dot(r_hi, wgh_ref[...]) + _dot(r_lo, wgh_ref[...]) + _dot(r_hi, wgl_ref[...]) + bg_ref[...]
    g_all = (jnp.minimum(z, 0.0) - jnp.log(1.0 + jnp.exp(-jnp.abs(z)))) * (1.0 / TAU)

    q_pair = q_ref[...].astype(F32)
    k_pair = k_ref[...].astype(F32)
    q_roll = pltpu.roll(q_pair, DK, axis=1)
    k_roll = pltpu.roll(k_pair, DK, axis=1)

    row_b = lax.broadcasted_iota(I32, (SB, SB), 0)
    col_b = lax.broadcasted_iota(I32, (SB, SB), 1)
    same_chunk = (row_b // CHUNK) == (col_b // CHUNK)
    tril = same_chunk & (row_b >= col_b)
    triu = same_chunk & (row_b <= col_b)
    row_chunk = lax.broadcasted_iota(I32, (SB, 1), 0) // CHUNK
    col_chunk = lax.broadcasted_iota(I32, (1, SB), 1) // CHUNK

    m_f2 = (lax.broadcasted_iota(I32, (1, 2 * LANES), 1) % LANES) < DK
    cum_s[...] = g_all

    def cum_body(s, carry):
        rows = pl.ds(pl.multiple_of(s * SB, SB), SB)
        g = cum_s[rows, :]
        f_hi, f_lo = _split2(jnp.where(m_f2, g, 0.0))
        b_hi, b_lo = _split2(jnp.where(m_f2, 0.0, g))
        cum = (_dot(tlu_ref[...], jnp.concatenate([f_hi, b_hi], axis=0))
               + _dot(tlu_ref[...], jnp.concatenate([f_lo, b_lo], axis=0)))
        cum_s[rows, :] = cum
        tots = []
        for c in range(CPB):
            tot = jnp.where(m_f2, cum[(c + 1) * CHUNK - 1:(c + 1) * CHUNK, :], cum[c * CHUNK:c * CHUNK + 1, :])
            dec_s[s * CPB + c] = jnp.exp(tot)
            tots.append(jnp.broadcast_to(tot, (CHUNK, 2 * LANES)))
        last_s[rows, :] = jnp.concatenate(tots, axis=0)
        return carry

    lax.fori_loop(0, n_sb, cum_body, 0)

    for j in range(2):
        cum = cum_s[:, j * LANES:(j + 1) * LANES]
        last = last_s[:, j * LANES:(j + 1) * LANES]
        if j == 0:
            q2 = jnp.where(m_f, q_pair, q_roll)
            k2 = jnp.where(m_f, k_pair, k_roll)
        else:
            q2 = jnp.where(m_f, q_roll, q_pair)
            k2 = jnp.where(m_f, k_roll, k_pair)
        qe_s[j] = ((q2 * (DK ** -0.5)) * jnp.exp(cum)).astype(BF16)
        ke_s[j] = (k2 * jnp.exp(-cum)).astype(BF16)
        kd_s[j] = (k2 * jnp.exp(last - cum)).astype(BF16)

    def kv_body(s, carry):
        rows = pl.ds(pl.multiple_of(s * SB, SB), SB)
        for j in range(2):
            v_t = jnp.transpose(v_ref[rows, j * DV:(j + 1) * DV].astype(F32)).astype(BF16)
            zero = jnp.zeros_like(v_t)
            v_st = jnp.concatenate([jnp.where(col_chunk == c, v_t, zero) for c in range(CPB)], axis=0)
            kv = _dot(v_st, kd_s[j, rows, :])
            for c in range(CPB):
                kv_s[j, s * CPB + c] = kv[c * LANES:(c + 1) * LANES, :]
        return carry

    lax.fori_loop(0, n_sb, kv_body, 0, unroll=min(2, n_sb))

    def dec_row(j, c):
        return dec_s[c][:, j * LANES:(j + 1) * LANES]

    def step_state(j, c, st):
        return st * dec_row(j, c) + kv_s[j, c]

    for q in range(n_in_step):
        st0 = tuple(jnp.transpose(jnp.concatenate([s0f_ref[q, j], s0b_ref[q, j]], axis=0)) for j in range(2))
        c0 = q * ch_per_seq
        s0 = q * sb_per_seq

        def bwd_body(t, sts, c0=c0):
            c = c0 + ch_per_seq - 1 - t
            for j in range(2):
                sbs_s[j, c] = sts[j]
            return tuple(step_state(j, c, sts[j]) for j in range(2))

        st_b_fin = lax.fori_loop(0, ch_per_seq, bwd_body, st0, unroll=CPB)

        def fwd_body(sl, sts, s0=s0):
            s = s0 + sl
            rows = pl.ds(pl.multiple_of(s * SB, SB), SB)
            new = []
            for j in range(2):
                lo = j * DV
                qe = qe_s[j, rows, :]
                ke = ke_s[j, rows, :]
                v_b = v_ref[rows, lo:lo + DV].astype(BF16)
                zero = jnp.zeros_like(qe)
                st = sts[j]
                q_parts, s_parts = [], []
                for c in range(CPB):
                    ci = s * CPB + c
                    s_parts.append(jnp.where(m_f, st, sbs_s[j, ci]).astype(BF16))
                    q_parts.append(jnp.where(row_chunk == c, qe, zero))
                    st = step_state(j, ci, st)
                o = _dot_nt(jnp.concatenate(q_parts, axis=1), jnp.concatenate(s_parts, axis=1))
                q_st = jnp.concatenate([jnp.where(m_f, qe, zero), jnp.where(m_f, zero, qe)], axis=0)
                sc = _dot_nt(q_st, ke)
                p = jnp.where(tril, sc[0:SB, :], 0.0) + jnp.where(triu, sc[SB:2 * SB, :], 0.0)
                o = o + _dot(p.astype(BF16), v_b)
                o = o * lax.rsqrt(jnp.mean(o * o, axis=-1, keepdims=True) + EPS) * ng_ref[:, lo:lo + DV]
                o_ref[rows, lo:lo + DV] = (o * _silu(og_ref[rows, lo:lo + DV].astype(F32))).astype(BF16)
                new.append(st)
            return tuple(new)

        st_f_fin = lax.fori_loop(0, sb_per_seq, fwd_body, st0, unroll=min(2, sb_per_seq))
        for j in range(2):
            sf_ref[q, j] = jnp.transpose(st_f_fin[j])[0:DK, :]
            sb_ref[q, j] = jnp.transpose(st_b_fin[j])[DK:2 * DK, :]


def _gla(proj, r_all, wg_hi, wg_lo, bg, ng, tlu, s0f, s0b, *, seq_len, n_seq, n_in_step, row0):
    kern = functools.partial(_gla_kernel, seq_len=seq_len, n_in_step=n_in_step)
    rows = n_in_step * seq_len
    n_chunks = rows // CHUNK
    rb = lambda b: b + row0 // rows
    st_spec = pl.BlockSpec((n_in_step, 2, DK, DV), lambda b, h: (b, h, 0, 0))
    return pl.pallas_call(
        kern,
        grid=(n_seq // n_in_step, HEADS // 2),
        in_specs=[pl.BlockSpec((rows, LANES), lambda b, h: (rb(b), h)),
                  pl.BlockSpec((rows, LANES), lambda b, h: (rb(b), QK_W // LANES + h)),
                  pl.BlockSpec((rows, 2 * DV), lambda b, h: (rb(b), 2 * QK_W // (2 * DV) + h)),
                  pl.BlockSpec((rows, 2 * DV), lambda b, h: (rb(b), (2 * QK_W + V_W) // (2 * DV) + h)),
                  pl.BlockSpec((rows, LANES), lambda b, h: (rb(b), 0)),
                  pl.BlockSpec((None, LANES, 2 * LANES), lambda b, h: (h, 0, 0)),
                  pl.BlockSpec((None, LANES, 2 * LANES), lambda b, h: (h, 0, 0)),
                  pl.BlockSpec((None, 1, 2 * LANES), lambda b, h: (h, 0, 0)),
                  pl.BlockSpec((None, 1, 2 * DV), lambda b, h: (h, 0, 0)),
                  pl.BlockSpec((SB, 2 * SB), lambda b, h: (0, 0)),
                  st_spec, st_spec],
        out_specs=[pl.BlockSpec((rows, 2 * DV), lambda b, h: (b, h)), st_spec, st_spec],
        out_shape=[jax.ShapeDtypeStruct((n_seq * seq_len, V_W), BF16),
                   jax.ShapeDtypeStruct((n_seq, HEADS, DK, DV), F32),
                   jax.ShapeDtypeStruct((n_seq, HEADS, DK, DV), F32)],
        scratch_shapes=[pltpu.VMEM((rows, 2 * LANES), F32),
                        pltpu.VMEM((rows, 2 * LANES), F32),
                        pltpu.VMEM((2, rows, LANES), BF16),
                        pltpu.VMEM((2, rows, LANES), BF16),
                        pltpu.VMEM((2, rows, LANES), BF16),
                        pltpu.VMEM((n_chunks, 1, 2 * LANES), F32),
                        pltpu.VMEM((2, n_chunks, DV, LANES), F32),
                        pltpu.VMEM((2, n_chunks, DV, LANES), F32)],
        compiler_params=_cparams(("arbitrary", "arbitrary")),
        name="gla_%d" % seq_len,
    )(proj, proj, proj, proj, r_all, wg_hi, wg_lo, bg, ng, tlu, s0f, s0b)


def _fnet_stage_a(u_bf, cs):
    cparts, sparts = [], []
    for g in range(FN_G):
        ab = _dot(u_bf[:, g * FN_C:(g + 1) * FN_C], cs)
        cparts.append(ab[:, 0:FN_C])
        sparts.append(ab[:, FN_C:2 * FN_C])
    return jnp.concatenate(cparts, axis=1), jnp.concatenate(sparts, axis=1)


def _fnet_ctx_kernel(u_ref, cs_ref, p2_ref, f_ref):
    uc, us = _fnet_stage_a(u_ref[...].astype(BF16), cs_ref[...].astype(BF16))
    ab = jnp.concatenate([uc, us], axis=0).astype(BF16)
    f_ref[...] = _dot(p2_ref[...].astype(BF16), ab).astype(BF16)


def _fnet_ctx(proj, cs, p2):
    return pl.pallas_call(
        _fnet_ctx_kernel,
        grid=(N_CTX,),
        in_specs=[pl.BlockSpec((L_CTX, FN_G * FN_C), lambda b: (b, 3)),
                  pl.BlockSpec((FN_C, 2 * FN_C), lambda b: (0, 0)),
                  pl.BlockSpec((L_CTX, 2 * L_CTX), lambda b: (0, 0))],
        out_specs=pl.BlockSpec((L_CTX, FN_G * FN_C), lambda b: (b, 0)),
        out_shape=jax.ShapeDtypeStruct((T_CTX, FN_G * FN_C), BF16),
        compiler_params=_cparams(("arbitrary",)),
        name="fnet_ctx",
    )(proj, cs, p2)


TM_FL = 256
RT_FL = 256


def _fnet_lat_kernel(u_ref, cs_ref, kr_ref, f_ref, ab_s):
    m = pl.program_id(1)

    @pl.when(m == 0)
    def _():
        def body(t, carry):
            rows = pl.ds(pl.multiple_of(t * RT_FL, RT_FL), RT_FL)
            uc, us = _fnet_stage_a(u_ref[rows, :].astype(BF16), cs_ref[...].astype(BF16))
            ab_s[rows, :] = uc.astype(BF16)
            ab_s[pl.ds(pl.multiple_of(L_LAT + t * RT_FL, RT_FL), RT_FL), :] = us.astype(BF16)
            return carry
        lax.fori_loop(0, L_LAT // RT_FL, body, 0)

    f_ref[...] = _dot(kr_ref[...].astype(BF16), ab_s[...]).astype(BF16)


def _fnet_lat(proj, cs, kr):
    nm = L_LAT // TM_FL
    return pl.pallas_call(
        _fnet_lat_kernel,
        grid=(N_LAT, nm),
        in_specs=[pl.BlockSpec((L_LAT, FN_G * FN_C), lambda b, m: (T_CTX // L_LAT + b, 3)),
                  pl.BlockSpec((FN_C, 2 * FN_C), lambda b, m: (0, 0)),
                  pl.BlockSpec((TM_FL, 2 * L_LAT), lambda b, m: (m, 0))],
        out_specs=pl.BlockSpec((TM_FL, FN_G * FN_C), lambda b, m: (b * nm + m, 0)),
        out_shape=jax.ShapeDtypeStruct((T_LAT, FN_G * FN_C), BF16),
        scratch_shapes=[pltpu.VMEM((2 * L_LAT, FN_G * FN_C), BF16)],
        compiler_params=_cparams(("arbitrary", "arbitrary")),
        name="fnet_lat",
    )(proj, cs, kr)


TM_OUT = 256
LANE_E0 = N_GROUPS
ROWS_PER_BLK = 8
PACK_ROWS = -(-(2 * TM_OUT + N_EXP * (ROWS_PER_BLK - 1)) // 256) * 256
PACK_BLKS = PACK_ROWS // ROWS_PER_BLK
N_TOK_TILES = T_ALL // TM_OUT
BLK_PER_TILE = TM_MOE // ROWS_PER_BLK
USED_BLKS = (2 * TM_OUT + N_EXP * (ROWS_PER_BLK - 1)) // ROWS_PER_BLK
HS_ROWS = N_TOK_TILES * PACK_ROWS
assert (PACK_BLKS - USED_BLKS) * N_TOK_TILES >= 2 * BLK_PER_TILE


def _outproj_kernel(oc_ref, ol_ref, fc_ref, fl_ref, xp_ref, xs_ref, mod_ref, g_ref, wo_ref, wf_ref,
                    wrc_ref, br_ref, sut_ref, sl_ref,
                    x1_ref, hs_ref, rw_ref, nb_ref, lb_ref):
    i = pl.program_id(0)
    is_ctx = i < T_CTX // TM_OUT

    o = jnp.where(is_ctx, oc_ref[...], ol_ref[...]).astype(BF16)
    f = jnp.where(is_ctx, fc_ref[...], fl_ref[...]).astype(BF16)
    x = jnp.where(is_ctx, xp_ref[...], xs_ref[...])
    y = _dot(o, wo_ref[...]) + _dot(f, wf_ref[...])
    ga1 = mod_ref[:, 2 * D_MODEL:3 * D_MODEL]
    sh2 = mod_ref[:, 3 * D_MODEL:4 * D_MODEL]
    sc2 = mod_ref[:, 4 * D_MODEL:5 * D_MODEL]
    x1 = x + ga1 * y
    x1_ref[...] = x1
    h2 = _rms(x1, g_ref[...]) * (1.0 + sc2) + sh2

    h_hi, h_lo = _split2(h2)
    hw = _dot(h_hi, wrc_ref[...])
    lg_all = jnp.transpose(hw[:, 0:LANES] + _dot(h_lo, wrc_ref[:, 0:LANES]) + hw[:, LANES:2 * LANES]
                           + br_ref[...])

    row_i = lax.broadcasted_iota(I32, (LANES, TM_OUT), 0)
    row = row_i.astype(F32)
    neg = jnp.float32(-jnp.inf)
    big = jnp.float32(LANES)
    lg = jnp.where(row_i < N_GROUPS, lg_all, neg)
    gmax = jnp.max(lg, axis=0, keepdims=True)
    gsel = jnp.min(jnp.where(lg == gmax, row, big), axis=0, keepdims=True)
    den = jnp.sum(jnp.exp(lg - gmax), axis=0, keepdims=True)
    pg_sel = 1.0 / den

    e_idx = row_i - LANE_E0
    egrp = (e_idx >> 3).astype(F32)
    emask = (e_idx >= 0) & (e_idx < N_EXP) & (egrp == gsel)
    m1 = jnp.where(emask, lg_all, neg)
    v1 = jnp.max(m1, axis=0, keepdims=True)
    i1 = jnp.min(jnp.where(m1 == v1, row, big), axis=0, keepdims=True)
    m2 = jnp.where(row == i1, neg, m1)
    v2 = jnp.max(m2, axis=0, keepdims=True)
    i2 = jnp.min(jnp.where(m2 == v2, row, big), axis=0, keepdims=True)
    e2 = jnp.exp(v2 - v1)
    inv = 1.0 / (1.0 + e2)
    w1 = inv * pg_sel
    w2 = (e2 * inv) * pg_sel

    oh1 = row == i1
    oh2 = row == i2
    oh = jnp.where(oh1 | oh2, 1.0, 0.0).astype(BF16)
    cnt = _dot(oh, jnp.ones((TM_OUT, LANES), BF16))
    nblk = jnp.floor((cnt + (ROWS_PER_BLK - 1)) * (1.0 / ROWS_PER_BLK))
    lboff = _dot(sl_ref[...], nblk.astype(BF16))
    lrank = _dot(oh, sut_ref[...])
    posmat = jnp.concatenate([lboff, lboff], axis=1) * ROWS_PER_BLK + lrank
    p1 = jnp.sum(jnp.where(oh1, posmat, 0.0), axis=0, keepdims=True)
    p2 = jnp.sum(jnp.where(oh2, posmat, 0.0), axis=0, keepdims=True)
    nb_ref[...] = nblk[:, 0:8].astype(I32)
    lb_ref[...] = lboff[:, 0:8].astype(I32)

    prow = lax.broadcasted_iota(I32, (PACK_ROWS, TM_OUT), 0).astype(F32)
    place = jnp.where((prow == p1) | (prow == p2), 1.0, 0.0).astype(BF16)
    hs_ref[...] = _dot(place, h_hi).astype(BF16)

    rw_ref[...] = jnp.concatenate([w1, w2, p1, p2, jnp.zeros((4, TM_OUT), F32)], axis=0)


def _outproj(o_ctx, o_lat, f_ctx, f_lat, xp, xs, mod3, g_ffn, wo, wf, wr_cat, br, sut, sl):
    nt = T_ALL // TM_OUT
    nctx = T_CTX // TM_OUT
    ctx_map = lambda i: (jnp.minimum(i, nctx - 1), 0)
    lat_map = lambda i: (jnp.maximum(i - nctx, 0), 0)
    const = lambda i: (0, 0)
    return pl.pallas_call(
        _outproj_kernel,
        grid=(nt,),
        in_specs=[pl.BlockSpec((TM_OUT, V_W), ctx_map),
                  pl.BlockSpec((TM_OUT, V_W), lat_map),
                  pl.BlockSpec((TM_OUT, FN_G * FN_C), ctx_map),
                  pl.BlockSpec((TM_OUT, FN_G * FN_C), lat_map),
                  pl.BlockSpec((TM_OUT, D_MODEL), ctx_map),
                  pl.BlockSpec((TM_OUT, D_MODEL), lat_map),
                  pl.BlockSpec((None, 1, 6 * D_MODEL), lambda i: (_cond_row(i, TM_OUT), 0, 0)),
                  pl.BlockSpec((1, D_MODEL), const),
                  pl.BlockSpec((V_W, D_MODEL), const),
                  pl.BlockSpec((FN_G * FN_C, D_MODEL), const),
                  pl.BlockSpec((D_MODEL, 2 * LANES), const),
                  pl.BlockSpec((1, LANES), const),
                  pl.BlockSpec((TM_OUT, TM_OUT), const),
                  pl.BlockSpec((LANES, LANES), const)],
        out_specs=[pl.BlockSpec((TM_OUT, D_MODEL), lambda i: (i, 0)),
                   pl.BlockSpec((PACK_ROWS, D_MODEL), lambda i: (i, 0)),
                   pl.BlockSpec((8, TM_OUT), lambda i: (0, i)),
                   pl.BlockSpec((None, LANES, 8), lambda i: (i, 0, 0)),
                   pl.BlockSpec((None, LANES, 8), lambda i: (i, 0, 0))],
        out_shape=[jax.ShapeDtypeStruct((T_ALL, D_MODEL), F32),
                   jax.ShapeDtypeStruct((HS_ROWS, D_MODEL), BF16),
                   jax.ShapeDtypeStruct((8, T_ALL), F32),
                   jax.ShapeDtypeStruct((nt, LANES, 8), I32),
                   jax.ShapeDtypeStruct((nt, LANES, 8), I32)],
        compiler_params=_cparams(("arbitrary",)),
        name="outproj",
    )(o_ctx, o_lat, f_ctx, f_lat, xp, xs, mod3, g_ffn, wo, wf, wr_cat, br, sut, sl)


SRC_BITS = 16
SRC_MASK = (1 << SRC_BITS) - 1
X_SLOTS = 3
Y_SLOTS = 3
N_UP_CHUNKS = 2
N_DN_CHUNKS = 8
NT_MOE = (2 * T_ALL + N_TOK_TILES * N_EXP * (ROWS_PER_BLK - 1)) // TM_MOE + N_EXP


def _moe_kernel(texp_ref, nexp_ref, meta_ref, code_ref,
                h_hbm, wg_hbm, wu_hbm, wd_hbm,
                out_hbm,
                xbuf, ybuf, wg_f, wu_f, wd_f, wg_s, wu_s, wd_s, kcount, gsem, ssem, wsem):
    i = pl.program_id(0)
    nt = meta_ref[0]
    xs = i % X_SLOTS

    def blk_rows(b):
        if isinstance(b, int):
            return pl.ds(b * ROWS_PER_BLK, ROWS_PER_BLK)
        return pl.ds(pl.multiple_of(b * ROWS_PER_BLK, ROWS_PER_BLK), ROWS_PER_BLK)

    def gather_row(tile, sl, j):
        src = code_ref[(tile + 2) * BLK_PER_TILE + j] & SRC_MASK
        pltpu.make_async_copy(h_hbm.at[blk_rows(src)], xbuf.at[sl, blk_rows(j)], gsem.at[sl]).start()

    def scatter_row(tile, sl, j):
        dst = code_ref[(tile + 2) * BLK_PER_TILE + j] >> SRC_BITS
        pltpu.make_async_copy(ybuf.at[sl, blk_rows(j)], out_hbm.at[blk_rows(dst)], ssem.at[sl]).start()

    def gather_wait(sl):
        pltpu.make_async_copy(h_hbm.at[pl.ds(0, TM_MOE)], xbuf.at[sl], gsem.at[sl]).wait()

    def scatter_wait(sl):
        pltpu.make_async_copy(ybuf.at[sl], out_hbm.at[pl.ds(0, TM_MOE)], ssem.at[sl]).wait()

    def y_slot(tile):
        return (tile + 1) % Y_SLOTS

    @pl.when(i == 0)
    def _():
        ybuf[y_slot(-2)] = jnp.zeros((TM_MOE, D_MODEL), BF16)
        ybuf[y_slot(-1)] = jnp.zeros((TM_MOE, D_MODEL), BF16)

        def body(j, c):
            gather_row(0, 0, j)
            gather_row(1, 1, j)
            scatter_row(-2, y_slot(-2), j)
            return c
        lax.fori_loop(0, BLK_PER_TILE, body, 0)

    @pl.when((i >= 1) & (i <= nt))
    def _():
        scatter_wait(y_slot(i - 3))

    @pl.when(i < nt)
    def _():
        prev = texp_ref[jnp.maximum(i - 1, 0)]

        def weight_copies(e, sl):
            return (pltpu.make_async_copy(wg_hbm.at[e], wg_f.at[sl], wsem.at[sl]),
                    pltpu.make_async_copy(wu_hbm.at[e], wu_f.at[sl], wsem.at[sl]),
                    pltpu.make_async_copy(wd_hbm.at[e], wd_f.at[sl], wsem.at[sl]))

        @pl.when(i == 0)
        def _():
            kcount[0] = 0
            for cp in weight_copies(texp_ref[0], 0):
                cp.start()

        @pl.when((i == 0) | (texp_ref[i] != prev))
        def _():
            k = kcount[0]
            for sl in range(2):
                @pl.when(k % 2 == sl)
                def _(sl=sl):
                    for cp in weight_copies(texp_ref[i], sl):
                        cp.wait()
                    def narrow(r, c):
                        up = pl.ds(pl.multiple_of(r * (D_MODEL // 16), D_MODEL // 16), D_MODEL // 16)
                        dn = pl.ds(pl.multiple_of(r * (D_EXP // 16), D_EXP // 16), D_EXP // 16)
                        wg_s[up, :] = wg_f[sl, up, :].astype(BF16)
                        wu_s[up, :] = wu_f[sl, up, :].astype(BF16)
                        wd_s[dn, :] = wd_f[sl, dn, :].astype(BF16)
                        return c
                    lax.fori_loop(0, 16, narrow, 0)

                    @pl.when(nexp_ref[i] >= 0)
                    def _():
                        for cp in weight_copies(nexp_ref[i], 1 - sl):
                            cp.start(priority=1)
            kcount[0] = k + 1

        gather_wait(xs)
        x = xbuf[xs].astype(BF16)

        issues = []
        for j in range(BLK_PER_TILE):
            issues.append(functools.partial(gather_row, i + 2, (i + 2) % X_SLOTS, j))
            issues.append(functools.partial(scatter_row, i - 1, y_slot(i - 1), j))
        n_groups = N_UP_CHUNKS + N_DN_CHUNKS
        per_group = -(-len(issues) // n_groups)

        def issue_group(k):
            for fn in issues[k * per_group:(k + 1) * per_group]:
                fn()

        wu_c = D_EXP // N_UP_CHUNKS
        hid = []
        for n in range(N_UP_CHUNKS):
            issue_group(n)
            g = _dot(x, wg_s[:, n * wu_c:(n + 1) * wu_c])
            u = _dot(x, wu_s[:, n * wu_c:(n + 1) * wu_c])
            hid.append((_silu(g) * u).astype(BF16))
        hid = jnp.concatenate(hid, axis=1)
        wd_c = D_MODEL // N_DN_CHUNKS
        ys = y_slot(i)
        for n in range(N_DN_CHUNKS):
            issue_group(N_UP_CHUNKS + n)
            ybuf[ys, :, n * wd_c:(n + 1) * wd_c] = _dot(hid, wd_s[:, n * wd_c:(n + 1) * wd_c]).astype(BF16)

    @pl.when(i == nt)
    def _():
        gather_wait(xs)
        gather_wait((i + 1) % X_SLOTS)

        def body(j, c):
            scatter_row(nt - 1, y_slot(nt - 1), j)
            return c
        lax.fori_loop(0, BLK_PER_TILE, body, 0)
        scatter_wait(y_slot(nt - 2))
        scatter_wait(y_slot(nt - 1))


def _moe(texp, nexp, meta, code, hs, w_eg, w_eu, w_ed):
    hbm = pl.BlockSpec(memory_space=pl.ANY)
    grid_spec = pltpu.PrefetchScalarGridSpec(
        num_scalar_prefetch=4,
        grid=(NT_MOE + 1,),
        in_specs=[hbm, hbm, hbm, hbm],
        out_specs=hbm,
        scratch_shapes=[pltpu.VMEM((X_SLOTS, TM_MOE, D_MODEL), BF16),
                        pltpu.VMEM((Y_SLOTS, TM_MOE, D_MODEL), BF16),
                        pltpu.VMEM((2, D_MODEL, D_EXP), F32),
                        pltpu.VMEM((2, D_MODEL, D_EXP), F32),
                        pltpu.VMEM((2, D_EXP, D_MODEL), F32),
                        pltpu.VMEM((D_MODEL, D_EXP), BF16),
                        pltpu.VMEM((D_MODEL, D_EXP), BF16),
                        pltpu.VMEM((D_EXP, D_MODEL), BF16),
                        pltpu.SMEM((1,), I32),
                        pltpu.SemaphoreType.DMA((X_SLOTS,)),
                        pltpu.SemaphoreType.DMA((Y_SLOTS,)),
                        pltpu.SemaphoreType.DMA((2,))])
    return pl.pallas_call(
        _moe_kernel,
        grid_spec=grid_spec,
        out_shape=jax.ShapeDtypeStruct((HS_ROWS, D_MODEL), BF16),
        input_output_aliases={4: 0},
        compiler_params=_cparams(("arbitrary",)),
        name="moe",
    )(texp, nexp, meta, code, hs, w_eg, w_eu, w_ed)


TM_FIN = TM_OUT


def _final_kernel(x1_ref, ys_pack_ref, rw_ref, mod_ref, g_ref, yp_ref, ys_ref):
    i = pl.program_id(0)
    ga2 = mod_ref[:, 5 * D_MODEL:6 * D_MODEL]
    w0 = rw_ref[0:1, :]
    w1 = rw_ref[1:2, :]
    p0 = rw_ref[2:3, :]
    p1 = rw_ref[3:4, :]
    prow = lax.broadcasted_iota(I32, (PACK_ROWS, TM_FIN), 0).astype(F32)
    comb_t = jnp.where(prow == p0, w0, 0.0) + jnp.where(prow == p1, w1, 0.0)
    y_moe = lax.dot_general(comb_t.astype(BF16), ys_pack_ref[...].astype(BF16), (((0,), (0,)), ((), ())),
                            preferred_element_type=F32)
    y = x1_ref[...] + ga2 * y_moe
    out = _rms(y, g_ref[...])

    @pl.when(i < T_CTX // TM_FIN)
    def _():
        yp_ref[...] = out

    @pl.when(i >= T_CTX // TM_FIN)
    def _():
        ys_ref[...] = out


def _final(x1, y2, rw, mod3, g_fin):
    nt = T_ALL // TM_FIN
    nctx = T_CTX // TM_FIN
    return pl.pallas_call(
        _final_kernel,
        grid=(nt,),
        in_specs=[pl.BlockSpec((TM_FIN, D_MODEL), lambda i: (i, 0)),
                  pl.BlockSpec((PACK_ROWS, D_MODEL), lambda i: (i, 0)),
                  pl.BlockSpec((8, TM_FIN), lambda i: (0, i)),
                  pl.BlockSpec((None, 1, 6 * D_MODEL), lambda i: (_cond_row(i, TM_FIN), 0, 0)),
                  pl.BlockSpec((1, D_MODEL), lambda i: (0, 0))],
        out_specs=[pl.BlockSpec((TM_FIN, D_MODEL), lambda i: (jnp.minimum(i, nctx - 1), 0)),
                   pl.BlockSpec((TM_FIN, D_MODEL), lambda i: (jnp.maximum(i - nctx, 0), 0))],
        out_shape=[jax.ShapeDtypeStruct((T_CTX, D_MODEL), F32),
                   jax.ShapeDtypeStruct((T_LAT, D_MODEL), F32)],
        compiler_params=_cparams(("arbitrary",)),
        name="final",
    )(x1, y2, rw, mod3, g_fin)


def _np_bf16(a):
    return jnp.asarray(np.asarray(a, np.float32), dtype=BF16)


def _np_f32(a):
    return jnp.asarray(np.asarray(a, np.float32))


@functools.lru_cache(maxsize=None)
def _constants():
    c = {}
    k = np.arange(FN_C)
    ang = 2.0 * np.pi * np.outer(k, k) / FN_C
    c["cs"] = np.concatenate([np.cos(ang), np.sin(ang)], axis=1) / np.sqrt(FN_C)
    p = np.arange(L_CTX)
    ang = 2.0 * np.pi * np.outer(p, p) / L_CTX
    c["p2"] = np.concatenate([np.cos(ang), -np.sin(ang)], axis=1) / np.sqrt(L_CTX)
    pos = np.arange(L_LAT)
    rr, cc = pos // GRID_W, pos % GRID_W
    num = (np.outer(rr, rr) * (GRID_W // GRID_H) + np.outer(cc, cc)) % GRID_W
    ang = 2.0 * np.pi * num / GRID_W
    c["kr"] = np.concatenate([np.cos(ang), -np.sin(ang)], axis=1) / np.sqrt(L_LAT)
    i = np.arange(SB)
    same = (i[:, None] // CHUNK) == (i[None, :] // CHUNK)
    tl = same & (i[:, None] >= i[None, :])
    tu = same & (i[:, None] <= i[None, :])
    c["tlu"] = np.concatenate([tl, tu], axis=1).astype(np.float32)
    c["sut"] = (i[:, None] < i[None, :]).astype(np.float32)
    k = np.arange(LANES)
    c["sl"] = (k[:, None] > k[None, :]).astype(np.float32)
    return c


def kernel(x_prompt, x_sample, state_gla_fwd, state_gla_bwd, c, c_ctx, w_ada, b_ada, norm_attn, norm_ffn, w_in, w_gate_fwd, b_gate_fwd, w_gate_bwd, b_gate_bwd, norm_gla, w_out, w_router_group, b_router_group, w_router_expert, b_router_expert, w_expert_gate, w_expert_up, w_expert_down, norm_final):
    assert w_ada.shape[0] == 1, "single layer"
    cst = _constants()
    cs, p2, kr = _np_f32(cst["cs"]), _np_f32(cst["p2"]), _np_f32(cst["kr"])
    tlu, sut, sl = _np_bf16(cst["tlu"]), _np_bf16(cst["sut"]), _np_bf16(cst["sl"])

    xp = x_prompt.reshape(T_CTX, D_MODEL)
    xs = x_sample.reshape(T_LAT, D_MODEL)

    cond8 = jnp.concatenate([c_ctx[None, :], c, jnp.zeros((3, D_MODEL), F32)], axis=0)
    mod = _ada(cond8, w_ada[0], b_ada[0][None, :])
    mod3 = mod.reshape(8, 1, 6 * D_MODEL)

    wi = w_in[0]
    i_og = 2 * QK_W + 2 * V_W
    i_u = i_og + 2 * RANK
    w_a = wi[:, :i_og].astype(BF16)
    w_u = wi[:, i_u:].astype(BF16)
    w_r = jnp.pad(wi[:, i_og:i_u], ((0, 0), (0, LANES - 2 * RANK))).astype(BF16)

    wgf = w_gate_fwd[0].reshape(RANK, HEADS, DK)
    wgb = w_gate_bwd[0].reshape(RANK, HEADS, DK)
    zf = jnp.zeros_like(wgf)
    top = jnp.stack([wgf, zf], axis=2)
    bot = jnp.stack([zf, wgb], axis=2)
    wg = jnp.concatenate([top, bot], axis=0)
    wg = wg.reshape(2 * RANK, HEADS // 2, 4 * DK).transpose(1, 0, 2)
    wg = jnp.pad(wg, ((0, 0), (0, LANES - 2 * RANK), (0, 0)))
    wg_hi = wg.astype(BF16)
    wg_lo = (wg - wg_hi.astype(F32)).astype(BF16)
    bg = jnp.stack([b_gate_fwd[0].reshape(HEADS, DK), b_gate_bwd[0].reshape(HEADS, DK)], axis=1)
    bg = bg.reshape(HEADS // 2, 1, 4 * DK)
    ng = norm_gla[0].reshape(HEADS // 2, 1, 2 * DV)

    proj, r_all = _inproj(xp, xs, mod3, norm_attn, w_a, w_u, w_r)

    zero_state = jnp.zeros((N_CTX, HEADS, DK, DV), F32)
    o_ctx, sf_ctx, sb_ctx = _gla(proj, r_all, wg_hi, wg_lo, bg, ng, tlu, zero_state, zero_state,
                                 seq_len=L_CTX, n_seq=N_CTX, n_in_step=4, row0=0)
    o_lat, _, _ = _gla(proj, r_all, wg_hi, wg_lo, bg, ng, tlu,
                       state_gla_fwd[:, 0], state_gla_bwd[:, 0],
                       seq_len=L_LAT, n_seq=N_LAT, n_in_step=1, row0=T_CTX)

    f_ctx = _fnet_ctx(proj, cs, p2)
    f_lat = _fnet_lat(proj, cs, kr)

    wo = w_out[0][:V_W].astype(BF16)
    wf = w_out[0][V_W:].astype(BF16)
    wr = jnp.concatenate([w_router_group[0], w_router_expert[0]], axis=1)
    wr = jnp.pad(wr, ((0, 0), (0, LANES - N_GROUPS - N_EXP)))
    wr_hi = wr.astype(BF16)
    wr_lo = (wr - wr_hi.astype(F32)).astype(BF16)
    wr_cat = jnp.concatenate([wr_hi, wr_lo], axis=1)
    br = jnp.pad(jnp.concatenate([b_router_group[0], b_router_expert[0]]), (0, LANES - N_GROUPS - N_EXP))[None, :]

    x1, hs, rw, nb, lb = _outproj(o_ctx, o_lat, f_ctx, f_lat, xp, xs, mod3, norm_ffn, wo, wf,
                                  wr_cat, br, sut, sl)

    nb_e = nb[:, LANE_E0:LANE_E0 + N_EXP, 0].T
    lb_e = lb[:, LANE_E0:LANE_E0 + N_EXP, 0].T
    run_end = jnp.cumsum(nb_e, axis=1)
    blocks_e = run_end[:, -1]
    tiles_e = (blocks_e + BLK_PER_TILE - 1) // BLK_PER_TILE
    tile_end = jnp.cumsum(tiles_e)
    tile_start = tile_end - tiles_e
    n_tiles = tile_end[-1]
    n_code_tiles = NT_MOE + 4
    tile = jnp.arange(n_code_tiles, dtype=I32) - 2
    tile_c = jnp.clip(tile, 0, n_tiles - 1)
    t_exp = jnp.sum(tile_c[:, None] >= tile_end[None, :], axis=1)

    pick = t_exp[:, None] == jnp.arange(N_EXP, dtype=I32)[None, :]

    def per_tile(table):
        if table.ndim == 1:
            return jnp.sum(jnp.where(pick, table[None, :], 0), axis=1)
        return jnp.sum(jnp.where(pick[:, :, None], table[None, :, :], 0), axis=1)

    ends = per_tile(run_end)
    starts = ends - per_tile(nb_e)
    offs = per_tile(lb_e) + jnp.arange(N_TOK_TILES, dtype=I32)[None, :] * PACK_BLKS - starts
    j = jnp.arange(BLK_PER_TILE, dtype=I32)
    bi = ((tile_c - per_tile(tile_start)) * BLK_PER_TILE)[:, None] + j[None, :]
    in_run = (starts.T[:, :, None] <= bi[None, :, :]) & (bi[None, :, :] < ends.T[:, :, None])
    blk = bi + jnp.sum(jnp.where(in_run, offs.T[:, :, None], 0), axis=0)
    valid = (tile == tile_c)[:, None] & (bi < per_tile(blocks_e)[:, None])
    spare_ix = (jnp.arange(n_code_tiles, dtype=I32) % 2)[:, None] * BLK_PER_TILE + j[None, :]
    spare = (spare_ix % N_TOK_TILES) * PACK_BLKS + USED_BLKS + spare_ix // N_TOK_TILES
    code = jnp.where(valid, (blk << SRC_BITS) | blk, (spare << SRC_BITS) | blk[:, 0:1]).astype(I32).reshape(-1)
    texp = t_exp[2:NT_MOE + 3].astype(I32)
    e_ix = jnp.arange(N_EXP, dtype=I32)
    later = (e_ix[None, :] > e_ix[:, None]) & (tiles_e[None, :] > 0)
    next_e = jnp.min(jnp.where(later, e_ix[None, :], N_EXP), axis=1)
    next_e = jnp.where(next_e < N_EXP, next_e, -1)
    nexp = per_tile(next_e)[2:NT_MOE + 3].astype(I32)
    meta = n_tiles.reshape(1).astype(I32)

    y2 = _moe(texp, nexp, meta, code, hs, w_expert_gate[0], w_expert_up[0], w_expert_down[0])
    y_prompt, y_sample = _final(x1, y2, rw, mod3, norm_final[None, :])

    st_shape = (N_CTX, 1, HEADS, DK, DV)
    return (y_prompt.reshape(N_CTX, L_CTX, D_MODEL), y_sample.reshape(N_LAT, L_LAT, D_MODEL),
            sf_ctx.reshape(st_shape), sb_ctx.reshape(st_shape))
```

```python
import functools

import numpy as np
import jax
import jax.numpy as jnp
from jax import lax
from jax.experimental import pallas as pl
from jax.experimental.pallas import tpu as pltpu

F32 = jnp.float32
BF16 = jnp.bfloat16
I32 = jnp.int32

D_MODEL = 2048
N_CTX = 32
L_CTX = 256
N_LAT = 4
L_LAT = 2048
GRID_H = 32
GRID_W = 64
T_CTX = N_CTX * L_CTX
T_LAT = N_LAT * L_LAT
T_ALL = T_CTX + T_LAT
HEADS = 8
DK = 64
DV = 128
RANK = 16
TAU = 16.0
CHUNK = 64
FN_G = 8
FN_C = 128
QK_W = HEADS * DK
V_W = HEADS * DV
N_GROUPS = 4
EPG = 8
N_EXP = N_GROUPS * EPG
D_EXP = 512
EPS = 1e-6

LANES = 128
VMEM_LIMIT = 56 * 1024 * 1024

TM_MOE = 256


def _dot(a, b):
    return jnp.dot(a, b, preferred_element_type=F32)


def _dot_nt(a, b):
    return lax.dot_general(a, b, (((1,), (1,)), ((), ())), preferred_element_type=F32)


def _split2(x):
    hi = x.astype(BF16)
    lo = (x - hi.astype(F32)).astype(BF16)
    return hi, lo


def _silu(x):
    return x * (1.0 / (1.0 + jnp.exp(-x)))


def _rms(x, g):
    return x * lax.rsqrt(jnp.mean(x * x, axis=-1, keepdims=True) + EPS) * g


def _cparams(sem):
    return pltpu.CompilerParams(dimension_semantics=sem, vmem_limit_bytes=VMEM_LIMIT)


def _ada_kernel(c_ref, w_ref, b_ref, o_ref):
    s_hi, s_lo = _split2(_silu(c_ref[...]))
    w = w_ref[...]
    w_hi = w.astype(BF16)
    w_lo = (w - w_hi.astype(F32)).astype(BF16)
    o_ref[...] = _dot(s_hi, w_hi) + _dot(s_lo, w_hi) + _dot(s_hi, w_lo) + b_ref[...]


def _ada(cond8, w_ada, b_ada):
    tn = 1024
    n6 = 6 * D_MODEL
    return pl.pallas_call(
        _ada_kernel,
        grid=(n6 // tn,),
        in_specs=[pl.BlockSpec((8, D_MODEL), lambda j: (0, 0)),
                  pl.BlockSpec((D_MODEL, tn), lambda j: (0, j)),
                  pl.BlockSpec((1, tn), lambda j: (0, j))],
        out_specs=pl.BlockSpec((8, tn), lambda j: (0, j)),
        out_shape=jax.ShapeDtypeStruct((8, n6), F32),
        compiler_params=_cparams(("arbitrary",)),
        name="ada",
    )(cond8, w_ada, b_ada)


TM_IN = 512
TM_IN_HALF = 256
TN_IN = 1024
N_MAIN = 4096
N_QKVG = 3072


def _cond_row(tile, tm):
    ctx_tiles = T_CTX // tm
    per_seq = L_LAT // tm
    return jnp.where(tile < ctx_tiles, 0, 1 + (jnp.maximum(tile - ctx_tiles, 0)) // per_seq)


def _inproj_kernel(xp_ref, xs_ref, mod_ref, g_ref, wa_ref, wu_ref, wr_ref, proj_ref, r_ref):
    i = pl.program_id(0)
    is_ctx = i < T_CTX // TM_IN
    sh1 = mod_ref[:, 0:D_MODEL]
    sc1 = mod_ref[:, D_MODEL:2 * D_MODEL]
    for hf in range(TM_IN // TM_IN_HALF):
        rows = slice(hf * TM_IN_HALF, (hf + 1) * TM_IN_HALF)
        x = jnp.where(is_ctx, xp_ref[rows, :], xs_ref[rows, :])
        hb = (_rms(x, g_ref[...]) * (1.0 + sc1) + sh1).astype(BF16)
        r_ref[rows, :] = _dot(hb, wr_ref[...])
        for n in range(N_QKVG // TN_IN):
            cols = slice(n * TN_IN, (n + 1) * TN_IN)
            proj_ref[rows, cols] = _dot(hb, wa_ref[:, cols]).astype(BF16)
        for n in range((N_MAIN - N_QKVG) // TN_IN):
            cols = slice(N_QKVG + n * TN_IN, N_QKVG + (n + 1) * TN_IN)
            proj_ref[rows, cols] = _dot(hb, wu_ref[:, n * TN_IN:(n + 1) * TN_IN]).astype(BF16)


def _inproj(xp, xs, mod3, g_attn, w_a, w_u, w_r):
    nt = T_ALL // TM_IN
    nctx = T_CTX // TM_IN
    resident = pl.Buffered(1)
    return pl.pallas_call(
        _inproj_kernel,
        grid=(nt,),
        in_specs=[pl.BlockSpec((TM_IN, D_MODEL), lambda i: (jnp.minimum(i, nctx - 1), 0)),
                  pl.BlockSpec((TM_IN, D_MODEL), lambda i: (jnp.maximum(i - nctx, 0), 0)),
                  pl.BlockSpec((None, 1, 6 * D_MODEL), lambda i: (_cond_row(i, TM_IN), 0, 0)),
                  pl.BlockSpec((1, D_MODEL), lambda i: (0, 0)),
                  pl.BlockSpec((D_MODEL, N_QKVG), lambda i: (0, 0), pipeline_mode=resident),
                  pl.BlockSpec((D_MODEL, N_MAIN - N_QKVG), lambda i: (0, 0), pipeline_mode=resident),
                  pl.BlockSpec((D_MODEL, LANES), lambda i: (0, 0), pipeline_mode=resident)],
        out_specs=[pl.BlockSpec((TM_IN, N_MAIN), lambda i: (i, 0)),
                   pl.BlockSpec((TM_IN, LANES), lambda i: (i, 0))],
        out_shape=[jax.ShapeDtypeStruct((T_ALL, N_MAIN), BF16),
                   jax.ShapeDtypeStruct((T_ALL, LANES), F32)],
        compiler_params=_cparams(("arbitrary",)),
        name="inproj",
    )(xp, xs, mod3, g_attn, w_a, w_u, w_r)


SB = 256
CPB = SB // CHUNK


def _gla_kernel(q_ref, k_ref, v_ref, og_ref, r_ref, wgh_ref, wgl_ref, bg_ref, ng_ref,
                tl_ref, s0f_ref, s0b_ref,
                o_ref, sf_ref, sb_ref,
                cum_s, last_s, qe_s, ke_s, kd_s, dec_s, kv_s, sbs_s, *, seq_len, n_in_step):
    sb_per_seq = seq_len // SB
    ch_per_seq = seq_len // CHUNK
    n_sb = n_in_step * sb_per_seq
    lane = lax.broadcasted_iota(I32, (1, LANES), 1)
    m_f = lane < DK

    r_hi, r_lo = _split2(r_ref[...])
    z = _dot(r_hi, wgh_ref[...]) + _dot(r_lo, wgh_ref[...]) + _dot(r_hi, wgl_ref[...]) + bg_ref[...]
    g_all = (jnp.minimum(z, 0.0) - jnp.log(1.0 + jnp.exp(-jnp.abs(z)))) * (1.0 / TAU)

    q_pair = q_ref[...].astype(F32)
    k_pair = k_ref[...].astype(F32)
    q_roll = pltpu.roll(q_pair, DK, axis=1)
    k_roll = pltpu.roll(k_pair, DK, axis=1)

    row_b = lax.broadcasted_iota(I32, (SB, SB), 0)
    col_b = lax.broadcasted_iota(I32, (SB, SB), 1)
    same_chunk = (row_b // CHUNK) == (col_b // CHUNK)
    tril = same_chunk & (row_b >= col_b)
    triu = same_chunk & (row_b <= col_b)
    row_chunk = lax.broadcasted_iota(I32, (SB, 1), 0) // CHUNK
    col_chunk = lax.broadcasted_iota(I32, (1, SB), 1) // CHUNK

    m_f2 = (lax.broadcasted_iota(I32, (1, 2 * LANES), 1) % LANES) < DK
    cum_s[...] = g_all

    def cum_body(s, carry):
        rows = pl.ds(pl.multiple_of(s * SB, SB), SB)
        g = cum_s[rows, :]
        g_hi, g_lo = _split2(g)
        pre = _dot(tl_ref[...], g_hi) + _dot(tl_ref[...], g_lo)
        tots = []
        for c in range(CPB):
            tot = pre[(c + 1) * CHUNK - 1:(c + 1) * CHUNK, :]
            dec_s[s * CPB + c] = jnp.exp(tot)
            tots.append(jnp.broadcast_to(tot, (CHUNK, 2 * LANES)))
        last = jnp.concatenate(tots, axis=0)
        last_s[rows, :] = last
        cum_s[rows, :] = jnp.where(m_f2, pre, (last - pre) + g)
        return carry

    lax.fori_loop(0, n_sb, cum_body, 0)

    for j in range(2):
        cum = cum_s[:, j * LANES:(j + 1) * LANES]
        last = last_s[:, j * LANES:(j + 1) * LANES]
        if j == 0:
            q2 = jnp.where(m_f, q_pair, q_roll)
            k2 = jnp.where(m_f, k_pair, k_roll)
        else:
            q2 = jnp.where(m_f, q_roll, q_pair)
            k2 = jnp.where(m_f, k_roll, k_pair)
        qe_s[j] = ((q2 * (DK ** -0.5)) * jnp.exp(cum)).astype(BF16)
        ke_s[j] = (k2 * jnp.exp(-cum)).astype(BF16)
        kd_s[j] = (k2 * jnp.exp(last - cum)).astype(BF16)

    def kv_body(s, carry):
        rows = pl.ds(pl.multiple_of(s * SB, SB), SB)
        for j in range(2):
            v_t = jnp.transpose(v_ref[rows, j * DV:(j + 1) * DV].astype(F32)).astype(BF16)
            zero = jnp.zeros_like(v_t)
            v_st = jnp.concatenate([jnp.where(col_chunk == c, v_t, zero) for c in range(CPB)], axis=0)
            kv = _dot(v_st, kd_s[j, rows, :])
            for c in range(CPB):
                kv_s[j, s * CPB + c] = kv[c * LANES:(c + 1) * LANES, :]
        return carry

    lax.fori_loop(0, n_sb, kv_body, 0, unroll=min(2, n_sb))

    def dec_row(j, c):
        return dec_s[c][:, j * LANES:(j + 1) * LANES]

    def step_state(j, c, st):
        return st * dec_row(j, c) + kv_s[j, c]

    for q in range(n_in_step):
        st0 = tuple(jnp.transpose(jnp.concatenate([s0f_ref[q, j], s0b_ref[q, j]], axis=0)) for j in range(2))
        c0 = q * ch_per_seq
        s0 = q * sb_per_seq

        def bwd_body(t, sts, c0=c0):
            c = c0 + ch_per_seq - 1 - t
            for j in range(2):
                sbs_s[j, c] = sts[j]
            return tuple(step_state(j, c, sts[j]) for j in range(2))

        st_b_fin = lax.fori_loop(0, ch_per_seq, bwd_body, st0, unroll=CPB)

        def fwd_body(sl, sts, s0=s0):
            s = s0 + sl
            rows = pl.ds(pl.multiple_of(s * SB, SB), SB)
            new = []
            for j in range(2):
                lo = j * DV
                qe = qe_s[j, rows, :]
                ke = ke_s[j, rows, :]
                v_b = v_ref[rows, lo:lo + DV].astype(BF16)
                zero = jnp.zeros_like(qe)
                st = sts[j]
                q_parts, s_parts = [], []
                for c in range(CPB):
                    ci = s * CPB + c
                    s_parts.append(jnp.where(m_f, st, sbs_s[j, ci]).astype(BF16))
                    q_parts.append(jnp.where(row_chunk == c, qe, zero))
                    st = step_state(j, ci, st)
                o = _dot_nt(jnp.concatenate(q_parts, axis=1), jnp.concatenate(s_parts, axis=1))
                q_st = jnp.concatenate([jnp.where(m_f, qe, zero), jnp.where(m_f, zero, qe)], axis=0)
                sc = _dot_nt(q_st, ke)
                p = jnp.where(tril, sc[0:SB, :], 0.0) + jnp.where(triu, sc[SB:2 * SB, :], 0.0)
                o = o + _dot(p.astype(BF16), v_b)
                o = o * lax.rsqrt(jnp.mean(o * o, axis=-1, keepdims=True) + EPS) * ng_ref[:, lo:lo + DV]
                o_ref[rows, lo:lo + DV] = (o * _silu(og_ref[rows, lo:lo + DV].astype(F32))).astype(BF16)
                new.append(st)
            return tuple(new)

        st_f_fin = lax.fori_loop(0, sb_per_seq, fwd_body, st0, unroll=min(2, sb_per_seq))
        for j in range(2):
            sf_ref[q, j] = jnp.transpose(st_f_fin[j])[0:DK, :]
            sb_ref[q, j] = jnp.transpose(st_b_fin[j])[DK:2 * DK, :]


def _gla(proj, r_all, wg_hi, wg_lo, bg, ng, tl_m, s0f, s0b, *, seq_len, n_seq, n_in_step, row0):
    kern = functools.partial(_gla_kernel, seq_len=seq_len, n_in_step=n_in_step)
    rows = n_in_step * seq_len
    n_chunks = rows // CHUNK
    rb = lambda b: b + row0 // rows
    st_spec = pl.BlockSpec((n_in_step, 2, DK, DV), lambda b, h: (b, h, 0, 0))
    return pl.pallas_call(
        kern,
        grid=(n_seq // n_in_step, HEADS // 2),
        in_specs=[pl.BlockSpec((rows, LANES), lambda b, h: (rb(b), h)),
                  pl.BlockSpec((rows, LANES), lambda b, h: (rb(b), QK_W // LANES + h)),
                  pl.BlockSpec((rows, 2 * DV), lambda b, h: (rb(b), 2 * QK_W // (2 * DV) + h)),
                  pl.BlockSpec((rows, 2 * DV), lambda b, h: (rb(b), (2 * QK_W + V_W) // (2 * DV) + h)),
                  pl.BlockSpec((rows, LANES), lambda b, h: (rb(b), 0)),
                  pl.BlockSpec((None, LANES, 2 * LANES), lambda b, h: (h, 0, 0)),
                  pl.BlockSpec((None, LANES, 2 * LANES), lambda b, h: (h, 0, 0)),
                  pl.BlockSpec((None, 1, 2 * LANES), lambda b, h: (h, 0, 0)),
                  pl.BlockSpec((None, 1, 2 * DV), lambda b, h: (h, 0, 0)),
                  pl.BlockSpec((SB, SB), lambda b, h: (0, 0)),
                  st_spec, st_spec],
        out_specs=[pl.BlockSpec((rows, 2 * DV), lambda b, h: (b, h)), st_spec, st_spec],
        out_shape=[jax.ShapeDtypeStruct((n_seq * seq_len, V_W), BF16),
                   jax.ShapeDtypeStruct((n_seq, HEADS, DK, DV), F32),
                   jax.ShapeDtypeStruct((n_seq, HEADS, DK, DV), F32)],
        scratch_shapes=[pltpu.VMEM((rows, 2 * LANES), F32),
                        pltpu.VMEM((rows, 2 * LANES), F32),
                        pltpu.VMEM((2, rows, LANES), BF16),
                        pltpu.VMEM((2, rows, LANES), BF16),
                        pltpu.VMEM((2, rows, LANES), BF16),
                        pltpu.VMEM((n_chunks, 1, 2 * LANES), F32),
                        pltpu.VMEM((2, n_chunks, DV, LANES), F32),
                        pltpu.VMEM((2, n_chunks, DV, LANES), F32)],
        compiler_params=_cparams(("arbitrary", "arbitrary")),
        name="gla_%d" % seq_len,
    )(proj, proj, proj, proj, r_all, wg_hi, wg_lo, bg, ng, tl_m, s0f, s0b)


def _fnet_stage_a(u_bf, cs):
    cparts, sparts = [], []
    for g in range(FN_G):
        ab = _dot(u_bf[:, g * FN_C:(g + 1) * FN_C], cs)
        cparts.append(ab[:, 0:FN_C])
        sparts.append(ab[:, FN_C:2 * FN_C])
    return jnp.concatenate(cparts, axis=1), jnp.concatenate(sparts, axis=1)


def _fnet_ctx_kernel(u_ref, cs_ref, p2_ref, f_ref):
    uc, us = _fnet_stage_a(u_ref[...].astype(BF16), cs_ref[...].astype(BF16))
    ab = jnp.concatenate([uc, us], axis=0).astype(BF16)
    f_ref[...] = _dot(p2_ref[...].astype(BF16), ab).astype(BF16)


def _fnet_ctx(proj, cs, p2):
    return pl.pallas_call(
        _fnet_ctx_kernel,
        grid=(N_CTX,),
        in_specs=[pl.BlockSpec((L_CTX, FN_G * FN_C), lambda b: (b, 3)),
                  pl.BlockSpec((FN_C, 2 * FN_C), lambda b: (0, 0)),
                  pl.BlockSpec((L_CTX, 2 * L_CTX), lambda b: (0, 0))],
        out_specs=pl.BlockSpec((L_CTX, FN_G * FN_C), lambda b: (b, 0)),
        out_shape=jax.ShapeDtypeStruct((T_CTX, FN_G * FN_C), BF16),
        compiler_params=_cparams(("arbitrary",)),
        name="fnet_ctx",
    )(proj, cs, p2)


TM_FL = 256
RT_FL = 256


def _fnet_lat_kernel(u_ref, cs_ref, kr_ref, f_ref, ab_s):
    m = pl.program_id(1)

    @pl.when(m == 0)
    def _():
        def body(t, carry):
            rows = pl.ds(pl.multiple_of(t * RT_FL, RT_FL), RT_FL)
            uc, us = _fnet_stage_a(u_ref[rows, :].astype(BF16), cs_ref[...].astype(BF16))
            ab_s[rows, :] = uc.astype(BF16)
            ab_s[pl.ds(pl.multiple_of(L_LAT + t * RT_FL, RT_FL), RT_FL), :] = us.astype(BF16)
            return carry
        lax.fori_loop(0, L_LAT // RT_FL, body, 0)

    f_ref[...] = _dot(kr_ref[...].astype(BF16), ab_s[...]).astype(BF16)


def _fnet_lat(proj, cs, kr):
    nm = L_LAT // TM_FL
    return pl.pallas_call(
        _fnet_lat_kernel,
        grid=(N_LAT, nm),
        in_specs=[pl.BlockSpec((L_LAT, FN_G * FN_C), lambda b, m: (T_CTX // L_LAT + b, 3)),
                  pl.BlockSpec((FN_C, 2 * FN_C), lambda b, m: (0, 0)),
                  pl.BlockSpec((TM_FL, 2 * L_LAT), lambda b, m: (m, 0))],
        out_specs=pl.BlockSpec((TM_FL, FN_G * FN_C), lambda b, m: (b * nm + m, 0)),
        out_shape=jax.ShapeDtypeStruct((T_LAT, FN_G * FN_C), BF16),
        scratch_shapes=[pltpu.VMEM((2 * L_LAT, FN_G * FN_C), BF16)],
        compiler_params=_cparams(("arbitrary", "arbitrary")),
        name="fnet_lat",
    )(proj, cs, kr)


TM_OUT = 256
LANE_E0 = N_GROUPS
ROWS_PER_BLK = 8
PACK_ROWS = -(-(2 * TM_OUT + N_EXP * (ROWS_PER_BLK - 1)) // 256) * 256
PACK_BLKS = PACK_ROWS // ROWS_PER_BLK
N_TOK_TILES = T_ALL // TM_OUT
BLK_PER_TILE = TM_MOE // ROWS_PER_BLK
USED_BLKS = (2 * TM_OUT + N_EXP * (ROWS_PER_BLK - 1)) // ROWS_PER_BLK
HS_ROWS = N_TOK_TILES * PACK_ROWS
assert (PACK_BLKS - USED_BLKS) * N_TOK_TILES >= 2 * BLK_PER_TILE


def _outproj_kernel(oc_ref, ol_ref, fc_ref, fl_ref, xp_ref, xs_ref, mod_ref, g_ref, wo_ref, wf_ref,
                    wrc_ref, br_ref, sut_ref, sl_ref,
                    x1_ref, hs_ref, rw_ref, nb_ref, lb_ref):
    i = pl.program_id(0)
    is_ctx = i < T_CTX // TM_OUT

    o = jnp.where(is_ctx, oc_ref[...], ol_ref[...]).astype(BF16)
    f = jnp.where(is_ctx, fc_ref[...], fl_ref[...]).astype(BF16)
    x = jnp.where(is_ctx, xp_ref[...], xs_ref[...])
    y = _dot(o, wo_ref[...]) + _dot(f, wf_ref[...])
    ga1 = mod_ref[:, 2 * D_MODEL:3 * D_MODEL]
    sh2 = mod_ref[:, 3 * D_MODEL:4 * D_MODEL]
    sc2 = mod_ref[:, 4 * D_MODEL:5 * D_MODEL]
    x1 = x + ga1 * y
    x1_ref[...] = x1
    h2 = _rms(x1, g_ref[...]) * (1.0 + sc2) + sh2

    h_hi, h_lo = _split2(h2)
    hw = _dot(h_hi, wrc_ref[...])
    lg_all = jnp.transpose(hw[:, 0:LANES] + _dot(h_lo, wrc_ref[:, 0:LANES]) + hw[:, LANES:2 * LANES]
                           + br_ref[...])

    row_i = lax.broadcasted_iota(I32, (LANES, TM_OUT), 0)
    row = row_i.astype(F32)
    neg = jnp.float32(-jnp.inf)
    big = jnp.float32(LANES)
    lg = jnp.where(row_i < N_GROUPS, lg_all, neg)
    gmax = jnp.max(lg, axis=0, keepdims=True)
    gsel = jnp.min(jnp.where(lg == gmax, row, big), axis=0, keepdims=True)
    den = jnp.sum(jnp.exp(lg - gmax), axis=0, keepdims=True)
    pg_sel = 1.0 / den

    e_idx = row_i - LANE_E0
    egrp = (e_idx >> 3).astype(F32)
    emask = (e_idx >= 0) & (e_idx < N_EXP) & (egrp == gsel)
    m1 = jnp.where(emask, lg_all, neg)
    v1 = jnp.max(m1, axis=0, keepdims=True)
    i1 = jnp.min(jnp.where(m1 == v1, row, big), axis=0, keepdims=True)
    m2 = jnp.where(row == i1, neg, m1)
    v2 = jnp.max(m2, axis=0, keepdims=True)
    i2 = jnp.min(jnp.where(m2 == v2, row, big), axis=0, keepdims=True)
    e2 = jnp.exp(v2 - v1)
    inv = 1.0 / (1.0 + e2)
    w1 = inv * pg_sel
    w2 = (e2 * inv) * pg_sel

    oh1 = row == i1
    oh2 = row == i2
    oh = jnp.where(oh1 | oh2, 1.0, 0.0).astype(BF16)
    cnt = _dot(oh, jnp.ones((TM_OUT, LANES), BF16))
    nblk = jnp.floor((cnt + (ROWS_PER_BLK - 1)) * (1.0 / ROWS_PER_BLK))
    lboff = _dot(sl_ref[...], nblk.astype(BF16))
    lrank = _dot(oh, sut_ref[...])
    posmat = jnp.concatenate([lboff, lboff], axis=1) * ROWS_PER_BLK + lrank
    p1 = jnp.sum(jnp.where(oh1, posmat, 0.0), axis=0, keepdims=True)
    p2 = jnp.sum(jnp.where(oh2, posmat, 0.0), axis=0, keepdims=True)
    nb_ref[...] = nblk[:, 0:8].astype(I32)
    lb_ref[...] = lboff[:, 0:8].astype(I32)

    prow = lax.broadcasted_iota(I32, (PACK_ROWS, TM_OUT), 0).astype(F32)
    place = jnp.where((prow == p1) | (prow == p2), 1.0, 0.0).astype(BF16)
    hs_ref[...] = _dot(place, h_hi).astype(BF16)

    rw_ref[...] = jnp.concatenate([w1, w2, p1, p2, jnp.zeros((4, TM_OUT), F32)], axis=0)


def _outproj(o_ctx, o_lat, f_ctx, f_lat, xp, xs, mod3, g_ffn, wo, wf, wr_cat, br, sut, sl):
    nt = T_ALL // TM_OUT
    nctx = T_CTX // TM_OUT
    ctx_map = lambda i: (jnp.minimum(i, nctx - 1), 0)
    lat_map = lambda i: (jnp.maximum(i - nctx, 0), 0)
    const = lambda i: (0, 0)
    return pl.pallas_call(
        _outproj_kernel,
        grid=(nt,),
        in_specs=[pl.BlockSpec((TM_OUT, V_W), ctx_map),
                  pl.BlockSpec((TM_OUT, V_W), lat_map),
                  pl.BlockSpec((TM_OUT, FN_G * FN_C), ctx_map),
                  pl.BlockSpec((TM_OUT, FN_G * FN_C), lat_map),
                  pl.BlockSpec((TM_OUT, D_MODEL), ctx_map),
                  pl.BlockSpec((TM_OUT, D_MODEL), lat_map),
                  pl.BlockSpec((None, 1, 6 * D_MODEL), lambda i: (_cond_row(i, TM_OUT), 0, 0)),
                  pl.BlockSpec((1, D_MODEL), const),
                  pl.BlockSpec((V_W, D_MODEL), const),
                  pl.BlockSpec((FN_G * FN_C, D_MODEL), const),
                  pl.BlockSpec((D_MODEL, 2 * LANES), const),
                  pl.BlockSpec((1, LANES), const),
                  pl.BlockSpec((TM_OUT, TM_OUT), const),
                  pl.BlockSpec((LANES, LANES), const)],
        out_specs=[pl.BlockSpec((TM_OUT, D_MODEL), lambda i: (i, 0)),
                   pl.BlockSpec((PACK_ROWS, D_MODEL), lambda i: (i, 0)),
                   pl.BlockSpec((8, TM_OUT), lambda i: (0, i)),
                   pl.BlockSpec((None, LANES, 8), lambda i: (i, 0, 0)),
                   pl.BlockSpec((None, LANES, 8), lambda i: (i, 0, 0))],
        out_shape=[jax.ShapeDtypeStruct((T_ALL, D_MODEL), F32),
                   jax.ShapeDtypeStruct((HS_ROWS, D_MODEL), BF16),
                   jax.ShapeDtypeStruct((8, T_ALL), F32),
                   jax.ShapeDtypeStruct((nt, LANES, 8), I32),
                   jax.ShapeDtypeStruct((nt, LANES, 8), I32)],
        compiler_params=_cparams(("arbitrary",)),
        name="outproj",
    )(o_ctx, o_lat, f_ctx, f_lat, xp, xs, mod3, g_ffn, wo, wf, wr_cat, br, sut, sl)


SRC_BITS = 16
SRC_MASK = (1 << SRC_BITS) - 1
X_SLOTS = 3
Y_SLOTS = 3
N_UP_CHUNKS = 2
N_DN_CHUNKS = 8
NT_MOE = (2 * T_ALL + N_TOK_TILES * N_EXP * (ROWS_PER_BLK - 1)) // TM_MOE + N_EXP


def _moe_kernel(texp_ref, nexp_ref, meta_ref, code_ref,
                h_hbm, wg_hbm, wu_hbm, wd_hbm,
                out_hbm,
                xbuf, ybuf, wg_f, wu_f, wd_f, wg_s, wu_s, wd_s, kcount, gsem, ssem, wsem):
    i = pl.program_id(0)
    nt = meta_ref[0]
    xs = i % X_SLOTS

    def blk_rows(b):
        if isinstance(b, int):
            return pl.ds(b * ROWS_PER_BLK, ROWS_PER_BLK)
        return pl.ds(pl.multiple_of(b * ROWS_PER_BLK, ROWS_PER_BLK), ROWS_PER_BLK)

    def gather_row(tile, sl, j):
        src = code_ref[(tile + 2) * BLK_PER_TILE + j] & SRC_MASK
        pltpu.make_async_copy(h_hbm.at[blk_rows(src)], xbuf.at[sl, blk_rows(j)], gsem.at[sl]).start()

    def scatter_row(tile, sl, j):
        dst = code_ref[(tile + 2) * BLK_PER_TILE + j] >> SRC_BITS
        pltpu.make_async_copy(ybuf.at[sl, blk_rows(j)], out_hbm.at[blk_rows(dst)], ssem.at[sl]).start()

    def gather_wait(sl):
        pltpu.make_async_copy(h_hbm.at[pl.ds(0, TM_MOE)], xbuf.at[sl], gsem.at[sl]).wait()

    def scatter_wait(sl):
        pltpu.make_async_copy(ybuf.at[sl], out_hbm.at[pl.ds(0, TM_MOE)], ssem.at[sl]).wait()

    def y_slot(tile):
        return (tile + 1) % Y_SLOTS

    @pl.when(i == 0)
    def _():
        ybuf[y_slot(-2)] = jnp.zeros((TM_MOE, D_MODEL), BF16)
        ybuf[y_slot(-1)] = jnp.zeros((TM_MOE, D_MODEL), BF16)

        def body(j, c):
            gather_row(0, 0, j)
            gather_row(1, 1, j)
            scatter_row(-2, y_slot(-2), j)
            return c
        lax.fori_loop(0, BLK_PER_TILE, body, 0)

    @pl.when((i >= 1) & (i <= nt))
    def _():
        scatter_wait(y_slot(i - 3))

    @pl.when(i < nt)
    def _():
        prev = texp_ref[jnp.maximum(i - 1, 0)]

        def weight_copies(e, sl):
            return (pltpu.make_async_copy(wg_hbm.at[e], wg_f.at[sl], wsem.at[sl]),
                    pltpu.make_async_copy(wu_hbm.at[e], wu_f.at[sl], wsem.at[sl]),
                    pltpu.make_async_copy(wd_hbm.at[e], wd_f.at[sl], wsem.at[sl]))

        @pl.when(i == 0)
        def _():
            kcount[0] = 0
            for cp in weight_copies(texp_ref[0], 0):
                cp.start()

        @pl.when((i == 0) | (texp_ref[i] != prev))
        def _():
            k = kcount[0]
            for sl in range(2):
                @pl.when(k % 2 == sl)
                def _(sl=sl):
                    for cp in weight_copies(texp_ref[i], sl):
                        cp.wait()
                    def narrow(r, c):
                        up = pl.ds(pl.multiple_of(r * (D_MODEL // 16), D_MODEL // 16), D_MODEL // 16)
                        dn = pl.ds(pl.multiple_of(r * (D_EXP // 16), D_EXP // 16), D_EXP // 16)
                        wg_s[up, :] = wg_f[sl, up, :].astype(BF16)
                        wu_s[up, :] = wu_f[sl, up, :].astype(BF16)
                        wd_s[dn, :] = wd_f[sl, dn, :].astype(BF16)
                        return c
                    lax.fori_loop(0, 16, narrow, 0)

                    @pl.when(nexp_ref[i] >= 0)
                    def _():
                        for cp in weight_copies(nexp_ref[i], 1 - sl):
                            cp.start(priority=1)
            kcount[0] = k + 1

        gather_wait(xs)
        x = xbuf[xs].astype(BF16)

        issues = []
        for j in range(BLK_PER_TILE):
            issues.append(functools.partial(gather_row, i + 2, (i + 2) % X_SLOTS, j))
            issues.append(functools.partial(scatter_row, i - 1, y_slot(i - 1), j))
        n_groups = N_UP_CHUNKS + N_DN_CHUNKS
        per_group = -(-len(issues) // n_groups)

        def issue_group(k):
            for fn in issues[k * per_group:(k + 1) * per_group]:
                fn()

        wu_c = D_EXP // N_UP_CHUNKS
        hid = []
        for n in range(N_UP_CHUNKS):
            issue_group(n)
            g = _dot(x, wg_s[:, n * wu_c:(n + 1) * wu_c])
            u = _dot(x, wu_s[:, n * wu_c:(n + 1) * wu_c])
            hid.append((_silu(g) * u).astype(BF16))
        hid = jnp.concatenate(hid, axis=1)
        wd_c = D_MODEL // N_DN_CHUNKS
        ys = y_slot(i)
        for n in range(N_DN_CHUNKS):
            issue_group(N_UP_CHUNKS + n)
            ybuf[ys, :, n * wd_c:(n + 1) * wd_c] = _dot(hid, wd_s[:, n * wd_c:(n + 1) * wd_c]).astype(BF16)

    @pl.when(i == nt)
    def _():
        gather_wait(xs)
        gather_wait((i + 1) % X_SLOTS)

        def body(j, c):
            scatter_row(nt - 1, y_slot(nt - 1), j)
            return c
        lax.fori_loop(0, BLK_PER_TILE, body, 0)
        scatter_wait(y_slot(nt - 2))
        scatter_wait(y_slot(nt - 1))


def _moe(texp, nexp, meta, code, hs, w_eg, w_eu, w_ed):
    hbm = pl.BlockSpec(memory_space=pl.ANY)
    grid_spec = pltpu.PrefetchScalarGridSpec(
        num_scalar_prefetch=4,
        grid=(NT_MOE + 1,),
        in_specs=[hbm, hbm, hbm, hbm],
        out_specs=hbm,
        scratch_shapes=[pltpu.VMEM((X_SLOTS, TM_MOE, D_MODEL), BF16),
                        pltpu.VMEM((Y_SLOTS, TM_MOE, D_MODEL), BF16),
                        pltpu.VMEM((2, D_MODEL, D_EXP), F32),
                        pltpu.VMEM((2, D_MODEL, D_EXP), F32),
                        pltpu.VMEM((2, D_EXP, D_MODEL), F32),
                        pltpu.VMEM((D_MODEL, D_EXP), BF16),
                        pltpu.VMEM((D_MODEL, D_EXP), BF16),
                        pltpu.VMEM((D_EXP, D_MODEL), BF16),
                        pltpu.SMEM((1,), I32),
                        pltpu.SemaphoreType.DMA((X_SLOTS,)),
                        pltpu.SemaphoreType.DMA((Y_SLOTS,)),
                        pltpu.SemaphoreType.DMA((2,))])
    return pl.pallas_call(
        _moe_kernel,
        grid_spec=grid_spec,
        out_shape=jax.ShapeDtypeStruct((HS_ROWS, D_MODEL), BF16),
        input_output_aliases={4: 0},
        compiler_params=_cparams(("arbitrary",)),
        name="moe",
    )(texp, nexp, meta, code, hs, w_eg, w_eu, w_ed)


TM_FIN = TM_OUT


def _final_kernel(x1_ref, ys_pack_ref, rw_ref, mod_ref, g_ref, yp_ref, ys_ref):
    i = pl.program_id(0)
    ga2 = mod_ref[:, 5 * D_MODEL:6 * D_MODEL]
    w0 = rw_ref[0:1, :]
    w1 = rw_ref[1:2, :]
    p0 = rw_ref[2:3, :]
    p1 = rw_ref[3:4, :]
    prow = lax.broadcasted_iota(I32, (PACK_ROWS, TM_FIN), 0).astype(F32)
    comb_t = jnp.where(prow == p0, w0, 0.0) + jnp.where(prow == p1, w1, 0.0)
    y_moe = lax.dot_general(comb_t.astype(BF16), ys_pack_ref[...].astype(BF16), (((0,), (0,)), ((), ())),
                            preferred_element_type=F32)
    y = x1_ref[...] + ga2 * y_moe
    out = _rms(y, g_ref[...])

    @pl.when(i < T_CTX // TM_FIN)
    def _():
        yp_ref[...] = out

    @pl.when(i >= T_CTX // TM_FIN)
    def _():
        ys_ref[...] = out


def _final(x1, y2, rw, mod3, g_fin):
    nt = T_ALL // TM_FIN
    nctx = T_CTX // TM_FIN
    return pl.pallas_call(
        _final_kernel,
        grid=(nt,),
        in_specs=[pl.BlockSpec((TM_FIN, D_MODEL), lambda i: (i, 0)),
                  pl.BlockSpec((PACK_ROWS, D_MODEL), lambda i: (i, 0)),
                  pl.BlockSpec((8, TM_FIN), lambda i: (0, i)),
                  pl.BlockSpec((None, 1, 6 * D_MODEL), lambda i: (_cond_row(i, TM_FIN), 0, 0)),
                  pl.BlockSpec((1, D_MODEL), lambda i: (0, 0))],
        out_specs=[pl.BlockSpec((TM_FIN, D_MODEL), lambda i: (jnp.minimum(i, nctx - 1), 0)),
                   pl.BlockSpec((TM_FIN, D_MODEL), lambda i: (jnp.maximum(i - nctx, 0), 0))],
        out_shape=[jax.ShapeDtypeStruct((T_CTX, D_MODEL), F32),
                   jax.ShapeDtypeStruct((T_LAT, D_MODEL), F32)],
        compiler_params=_cparams(("arbitrary",)),
        name="final",
    )(x1, y2, rw, mod3, g_fin)


def _np_bf16(a):
    return jnp.asarray(np.asarray(a, np.float32), dtype=BF16)


def _np_f32(a):
    return jnp.asarray(np.asarray(a, np.float32))


@functools.lru_cache(maxsize=None)
def _constants():
    c = {}
    k = np.arange(FN_C)
    ang = 2.0 * np.pi * np.outer(k, k) / FN_C
    c["cs"] = np.concatenate([np.cos(ang), np.sin(ang)], axis=1) / np.sqrt(FN_C)
    p = np.arange(L_CTX)
    ang = 2.0 * np.pi * np.outer(p, p) / L_CTX
    c["p2"] = np.concatenate([np.cos(ang), -np.sin(ang)], axis=1) / np.sqrt(L_CTX)
    pos = np.arange(L_LAT)
    rr, cc = pos // GRID_W, pos % GRID_W
    num = (np.outer(rr, rr) * (GRID_W // GRID_H) + np.outer(cc, cc)) % GRID_W
    ang = 2.0 * np.pi * num / GRID_W
    c["kr"] = np.concatenate([np.cos(ang), -np.sin(ang)], axis=1) / np.sqrt(L_LAT)
    i = np.arange(SB)
    same = (i[:, None] // CHUNK) == (i[None, :] // CHUNK)
    c["tl"] = (same & (i[:, None] >= i[None, :])).astype(np.float32)
    c["sut"] = (i[:, None] < i[None, :]).astype(np.float32)
    k = np.arange(LANES)
    c["sl"] = (k[:, None] > k[None, :]).astype(np.float32)
    return c


def kernel(x_prompt, x_sample, state_gla_fwd, state_gla_bwd, c, c_ctx, w_ada, b_ada, norm_attn, norm_ffn, w_in, w_gate_fwd, b_gate_fwd, w_gate_bwd, b_gate_bwd, norm_gla, w_out, w_router_group, b_router_group, w_router_expert, b_router_expert, w_expert_gate, w_expert_up, w_expert_down, norm_final):
    assert w_ada.shape[0] == 1, "single layer"
    cst = _constants()
    cs, p2, kr = _np_f32(cst["cs"]), _np_f32(cst["p2"]), _np_f32(cst["kr"])
    tl_m, sut, sl = _np_bf16(cst["tl"]), _np_bf16(cst["sut"]), _np_bf16(cst["sl"])

    xp = x_prompt.reshape(T_CTX, D_MODEL)
    xs = x_sample.reshape(T_LAT, D_MODEL)

    cond8 = jnp.concatenate([c_ctx[None, :], c, jnp.zeros((3, D_MODEL), F32)], axis=0)
    mod = _ada(cond8, w_ada[0], b_ada[0][None, :])
    mod3 = mod.reshape(8, 1, 6 * D_MODEL)

    wi = w_in[0]
    i_og = 2 * QK_W + 2 * V_W
    i_u = i_og + 2 * RANK
    w_a = wi[:, :i_og].astype(BF16)
    w_u = wi[:, i_u:].astype(BF16)
    w_r = jnp.pad(wi[:, i_og:i_u], ((0, 0), (0, LANES - 2 * RANK))).astype(BF16)

    wgf = w_gate_fwd[0].reshape(RANK, HEADS, DK)
    wgb = w_gate_bwd[0].reshape(RANK, HEADS, DK)
    zf = jnp.zeros_like(wgf)
    top = jnp.stack([wgf, zf], axis=2)
    bot = jnp.stack([zf, wgb], axis=2)
    wg = jnp.concatenate([top, bot], axis=0)
    wg = wg.reshape(2 * RANK, HEADS // 2, 4 * DK).transpose(1, 0, 2)
    wg = jnp.pad(wg, ((0, 0), (0, LANES - 2 * RANK), (0, 0)))
    wg_hi = wg.astype(BF16)
    wg_lo = (wg - wg_hi.astype(F32)).astype(BF16)
    bg = jnp.stack([b_gate_fwd[0].reshape(HEADS, DK), b_gate_bwd[0].reshape(HEADS, DK)], axis=1)
    bg = bg.reshape(HEADS // 2, 1, 4 * DK)
    ng = norm_gla[0].reshape(HEADS // 2, 1, 2 * DV)

    proj, r_all = _inproj(xp, xs, mod3, norm_attn, w_a, w_u, w_r)

    zero_state = jnp.zeros((N_CTX, HEADS, DK, DV), F32)
    o_ctx, sf_ctx, sb_ctx = _gla(proj, r_all, wg_hi, wg_lo, bg, ng, tl_m, zero_state, zero_state,
                                 seq_len=L_CTX, n_seq=N_CTX, n_in_step=4, row0=0)
    o_lat, _, _ = _gla(proj, r_all, wg_hi, wg_lo, bg, ng, tl_m,
                       state_gla_fwd[:, 0], state_gla_bwd[:, 0],
                       seq_len=L_LAT, n_seq=N_LAT, n_in_step=1, row0=T_CTX)

    f_ctx = _fnet_ctx(proj, cs, p2)
    f_lat = _fnet_lat(proj, cs, kr)

    wo = w_out[0][:V_W].astype(BF16)
    wf = w_out[0][V_W:].astype(BF16)
    wr = jnp.concatenate([w_router_group[0], w_router_expert[0]], axis=1)
    wr = jnp.pad(wr, ((0, 0), (0, LANES - N_GROUPS - N_EXP)))
    wr_hi = wr.astype(BF16)
    wr_lo = (wr - wr_hi.astype(F32)).astype(BF16)
    wr_cat = jnp.concatenate([wr_hi, wr_lo], axis=1)
    br = jnp.pad(jnp.concatenate([b_router_group[0], b_router_expert[0]]), (0, LANES - N_GROUPS - N_EXP))[None, :]

    x1, hs, rw, nb, lb = _outproj(o_ctx, o_lat, f_ctx, f_lat, xp, xs, mod3, norm_ffn, wo, wf,
                                  wr_cat, br, sut, sl)

    nb_e = nb[:, LANE_E0:LANE_E0 + N_EXP, 0].T
    lb_e = lb[:, LANE_E0:LANE_E0 + N_EXP, 0].T
    run_end = jnp.cumsum(nb_e, axis=1)
    blocks_e = run_end[:, -1]
    tiles_e = (blocks_e + BLK_PER_TILE - 1) // BLK_PER_TILE
    tile_end = jnp.cumsum(tiles_e)
    tile_start = tile_end - tiles_e
    n_tiles = tile_end[-1]
    n_code_tiles = NT_MOE + 4
    tile = jnp.arange(n_code_tiles, dtype=I32) - 2
    tile_c = jnp.clip(tile, 0, n_tiles - 1)
    t_exp = jnp.sum(tile_c[:, None] >= tile_end[None, :], axis=1)

    pick = t_exp[:, None] == jnp.arange(N_EXP, dtype=I32)[None, :]

    def per_tile(table):
        if table.ndim == 1:
            return jnp.sum(jnp.where(pick, table[None, :], 0), axis=1)
        return jnp.sum(jnp.where(pick[:, :, None], table[None, :, :], 0), axis=1)

    ends = per_tile(run_end)
    starts = ends - per_tile(nb_e)
    offs = per_tile(lb_e) + jnp.arange(N_TOK_TILES, dtype=I32)[None, :] * PACK_BLKS - starts
    j = jnp.arange(BLK_PER_TILE, dtype=I32)
    bi = ((tile_c - per_tile(tile_start)) * BLK_PER_TILE)[:, None] + j[None, :]
    in_run = (starts.T[:, :, None] <= bi[None, :, :]) & (bi[None, :, :] < ends.T[:, :, None])
    blk = bi + jnp.sum(jnp.where(in_run, offs.T[:, :, None], 0), axis=0)
    valid = (tile == tile_c)[:, None] & (bi < per_tile(blocks_e)[:, None])
    spare_ix = (jnp.arange(n_code_tiles, dtype=I32) % 2)[:, None] * BLK_PER_TILE + j[None, :]
    spare = (spare_ix % N_TOK_TILES) * PACK_BLKS + USED_BLKS + spare_ix // N_TOK_TILES
    code = jnp.where(valid, (blk << SRC_BITS) | blk, (spare << SRC_BITS) | blk[:, 0:1]).astype(I32).reshape(-1)
    texp = t_exp[2:NT_MOE + 3].astype(I32)
    e_ix = jnp.arange(N_EXP, dtype=I32)
    later = (e_ix[None, :] > e_ix[:, None]) & (tiles_e[None, :] > 0)
    next_e = jnp.min(jnp.where(later, e_ix[None, :], N_EXP), axis=1)
    next_e = jnp.where(next_e < N_EXP, next_e, -1)
    nexp = per_tile(next_e)[2:NT_MOE + 3].astype(I32)
    meta = n_tiles.reshape(1).astype(I32)

    y2 = _moe(texp, nexp, meta, code, hs, w_expert_gate[0], w_expert_up[0], w_expert_down[0])
    y_prompt, y_sample = _final(x1, y2, rw, mod3, norm_final[None, :])

    st_shape = (N_CTX, 1, HEADS, DK, DV)
    return (y_prompt.reshape(N_CTX, L_CTX, D_MODEL), y_sample.reshape(N_LAT, L_LAT, D_MODEL),
            sf_ctx.reshape(st_shape), sb_ctx.reshape(st_shape))
```

```python
import functools

import numpy as np
import jax
import jax.numpy as jnp
from jax import lax
from jax.experimental import pallas as pl
from jax.experimental.pallas import tpu as pltpu

F32 = jnp.float32
BF16 = jnp.bfloat16
I32 = jnp.int32

D_MODEL = 2048
N_CTX = 32
L_CTX = 256
N_LAT = 4
L_LAT = 2048
GRID_H = 32
GRID_W = 64
T_CTX = N_CTX * L_CTX
T_LAT = N_LAT * L_LAT
T_ALL = T_CTX + T_LAT
HEADS = 8
DK = 64
DV = 128
RANK = 16
TAU = 16.0
CHUNK = 64
FN_G = 8
FN_C = 128
QK_W = HEADS * DK
V_W = HEADS * DV
N_GROUPS = 4
EPG = 8
N_EXP = N_GROUPS * EPG
D_EXP = 512
EPS = 1e-6

LANES = 128
VMEM_LIMIT = 56 * 1024 * 1024

TM_MOE = 256


def _dot(a, b):
    return jnp.dot(a, b, preferred_element_type=F32)


def _dot_nt(a, b):
    return lax.dot_general(a, b, (((1,), (1,)), ((), ())), preferred_element_type=F32)


def _split2(x):
    hi = x.astype(BF16)
    lo = (x - hi.astype(F32)).astype(BF16)
    return hi, lo


def _silu(x):
    return x * (1.0 / (1.0 + jnp.exp(-x)))


def _rms(x, g):
    return x * lax.rsqrt(jnp.mean(x * x, axis=-1, keepdims=True) + EPS) * g


def _cparams(sem):
    return pltpu.CompilerParams(dimension_semantics=sem, vmem_limit_bytes=VMEM_LIMIT)


def _ada_kernel(c_ref, w_ref, b_ref, o_ref):
    s_hi, s_lo = _split2(_silu(c_ref[...]))
    w = w_ref[...]
    w_hi = w.astype(BF16)
    w_lo = (w - w_hi.astype(F32)).astype(BF16)
    o_ref[...] = _dot(s_hi, w_hi) + _dot(s_lo, w_hi) + _dot(s_hi, w_lo) + b_ref[...]


def _ada(cond8, w_ada, b_ada):
    tn = 1024
    n6 = 6 * D_MODEL
    return pl.pallas_call(
        _ada_kernel,
        grid=(n6 // tn,),
        in_specs=[pl.BlockSpec((8, D_MODEL), lambda j: (0, 0)),
                  pl.BlockSpec((D_MODEL, tn), lambda j: (0, j)),
                  pl.BlockSpec((1, tn), lambda j: (0, j))],
        out_specs=pl.BlockSpec((8, tn), lambda j: (0, j)),
        out_shape=jax.ShapeDtypeStruct((8, n6), F32),
        compiler_params=_cparams(("arbitrary",)),
        name="ada",
    )(cond8, w_ada, b_ada)


TM_IN = 512
TM_IN_HALF = 256
TN_IN = 1024
N_MAIN = 4096
N_QKVG = 3072


def _cond_row(tile, tm):
    ctx_tiles = T_CTX // tm
    per_seq = L_LAT // tm
    return jnp.where(tile < ctx_tiles, 0, 1 + (jnp.maximum(tile - ctx_tiles, 0)) // per_seq)


def _inproj_kernel(xp_ref, xs_ref, mod_ref, g_ref, wa_ref, wu_ref, wr_ref, proj_ref, r_ref):
    i = pl.program_id(0)
    is_ctx = i < T_CTX // TM_IN
    sh1 = mod_ref[:, 0:D_MODEL]
    sc1 = mod_ref[:, D_MODEL:2 * D_MODEL]
    for hf in range(TM_IN // TM_IN_HALF):
        rows = slice(hf * TM_IN_HALF, (hf + 1) * TM_IN_HALF)
        x = jnp.where(is_ctx, xp_ref[rows, :], xs_ref[rows, :])
        hb = (_rms(x, g_ref[...]) * (1.0 + sc1) + sh1).astype(BF16)
        r_ref[rows, :] = _dot(hb, wr_ref[...])
        for n in range(N_QKVG // TN_IN):
            cols = slice(n * TN_IN, (n + 1) * TN_IN)
            proj_ref[rows, cols] = _dot(hb, wa_ref[:, cols]).astype(BF16)
        for n in range((N_MAIN - N_QKVG) // TN_IN):
            cols = slice(N_QKVG + n * TN_IN, N_QKVG + (n + 1) * TN_IN)
            proj_ref[rows, cols] = _dot(hb, wu_ref[:, n * TN_IN:(n + 1) * TN_IN]).astype(BF16)


def _inproj(xp, xs, mod3, g_attn, w_a, w_u, w_r):
    nt = T_ALL // TM_IN
    nctx = T_CTX // TM_IN
    resident = pl.Buffered(1)
    return pl.pallas_call(
        _inproj_kernel,
        grid=(nt,),
        in_specs=[pl.BlockSpec((TM_IN, D_MODEL), lambda i: (jnp.minimum(i, nctx - 1), 0)),
                  pl.BlockSpec((TM_IN, D_MODEL), lambda i: (jnp.maximum(i - nctx, 0), 0)),
                  pl.BlockSpec((None, 1, 6 * D_MODEL), lambda i: (_cond_row(i, TM_IN), 0, 0)),
                  pl.BlockSpec((1, D_MODEL), lambda i: (0, 0)),
                  pl.BlockSpec((D_MODEL, N_QKVG), lambda i: (0, 0), pipeline_mode=resident),
                  pl.BlockSpec((D_MODEL, N_MAIN - N_QKVG), lambda i: (0, 0), pipeline_mode=resident),
                  pl.BlockSpec((D_MODEL, LANES), lambda i: (0, 0), pipeline_mode=resident)],
        out_specs=[pl.BlockSpec((TM_IN, N_MAIN), lambda i: (i, 0)),
                   pl.BlockSpec((TM_IN, LANES), lambda i: (i, 0))],
        out_shape=[jax.ShapeDtypeStruct((T_ALL, N_MAIN), BF16),
                   jax.ShapeDtypeStruct((T_ALL, LANES), F32)],
        compiler_params=_cparams(("arbitrary",)),
        name="inproj",
    )(xp, xs, mod3, g_attn, w_a, w_u, w_r)


SB = 256
CPB = SB // CHUNK


def _gla_kernel(q_ref, k_ref, v_ref, og_ref, r_ref, wgh_ref, wgl_ref, bg_ref, ng_ref,
                tl_ref, s0f_ref, s0b_ref,
                o_ref, sf_ref, sb_ref,
                cum_s, last_s, qe_s, ke_s, kd_s, dec_s, kv_s, sbs_s, *, seq_len, n_in_step):
    sb_per_seq = seq_len // SB
    ch_per_seq = seq_len // CHUNK
    n_sb = n_in_step * sb_per_seq
    lane = lax.broadcasted_iota(I32, (1, LANES), 1)
    m_f = lane < DK

    r_hi, r_lo = _split2(r_ref[...])
    z = _dot(r_hi, wgh_ref[...]) + _dot(r_lo, wgh_ref[...]) + _dot(r_hi, wgl_ref[...]) + bg_ref[...]
    g_all = (jnp.minimum(z, 0.0) - jnp.log(1.0 + jnp.exp(-jnp.abs(z)))) * (1.0 / TAU)

    q_pair = q_ref[...].astype(F32)
    k_pair = k_ref[...].astype(F32)
    q_roll = pltpu.roll(q_pair, DK, axis=1)
    k_roll = pltpu.roll(k_pair, DK, axis=1)

    row_b = lax.broadcasted_iota(I32, (SB, SB), 0)
    col_b = lax.broadcasted_iota(I32, (SB, SB), 1)
    same_chunk = (row_b // CHUNK) == (col_b // CHUNK)
    tril = same_chunk & (row_b >= col_b)
    triu = same_chunk & (row_b <= col_b)
    row_chunk = lax.broadcasted_iota(I32, (SB, 1), 0) // CHUNK
    col_chunk = lax.broadcasted_iota(I32, (1, SB), 1) // CHUNK

    m_f2 = (lax.broadcasted_iota(I32, (1, 2 * LANES), 1) % LANES) < DK
    cum_s[...] = g_all

    def cum_body(s, carry):
        rows = pl.ds(pl.multiple_of(s * SB, SB), SB)
        g = cum_s[rows, :]
        g_hi, g_lo = _split2(g)
        pre = _dot(tl_ref[...], g_hi) + _dot(tl_ref[...], g_lo)
        tots = []
        for c in range(CPB):
            tot = pre[(c + 1) * CHUNK - 1:(c + 1) * CHUNK, :]
            dec_s[s * CPB + c] = jnp.exp(tot)
            tots.append(jnp.broadcast_to(tot, (CHUNK, 2 * LANES)))
        last = jnp.concatenate(tots, axis=0)
        last_s[rows, :] = last
        cum_s[rows, :] = jnp.where(m_f2, pre, (last - pre) + g)
        return carry

    lax.fori_loop(0, n_sb, cum_body, 0)

    for j in range(2):
        cum = cum_s[:, j * LANES:(j + 1) * LANES]
        last = last_s[:, j * LANES:(j + 1) * LANES]
        if j == 0:
            q2 = jnp.where(m_f, q_pair, q_roll)
            k2 = jnp.where(m_f, k_pair, k_roll)
        else:
            q2 = jnp.where(m_f, q_roll, q_pair)
            k2 = jnp.where(m_f, k_roll, k_pair)
        qe_s[j] = ((q2 * (DK ** -0.5)) * jnp.exp(cum)).astype(BF16)
        ke_s[j] = (k2 * jnp.exp(-cum)).astype(BF16)
        kd_s[j] = (k2 * jnp.exp(last - cum)).astype(BF16)

    def kv_body(s, carry):
        rows = pl.ds(pl.multiple_of(s * SB, SB), SB)
        for j in range(2):
            v_t = jnp.transpose(v_ref[rows, j * DV:(j + 1) * DV].astype(F32)).astype(BF16)
            zero = jnp.zeros_like(v_t)
            v_st = jnp.concatenate([jnp.where(col_chunk == c, v_t, zero) for c in range(CPB)], axis=0)
            kv = _dot(v_st, kd_s[j, rows, :])
            for c in range(CPB):
                kv_s[j, s * CPB + c] = kv[c * LANES:(c + 1) * LANES, :]
        return carry

    lax.fori_loop(0, n_sb, kv_body, 0, unroll=min(2, n_sb))

    def dec_row(j, c):
        return dec_s[c][:, j * LANES:(j + 1) * LANES]

    def step_state(j, c, st):
        return st * dec_row(j, c) + kv_s[j, c]

    for q in range(n_in_step):
        st0 = tuple(jnp.transpose(jnp.concatenate([s0f_ref[q, j], s0b_ref[q, j]], axis=0)) for j in range(2))
        c0 = q * ch_per_seq
        s0 = q * sb_per_seq

        def bwd_body(t, sts, c0=c0):
            c = c0 + ch_per_seq - 1 - t
            for j in range(2):
                sbs_s[j, c] = sts[j]
            return tuple(step_state(j, c, sts[j]) for j in range(2))

        st_b_fin = lax.fori_loop(0, ch_per_seq, bwd_body, st0, unroll=CPB)

        def fwd_body(sl, sts, s0=s0):
            s = s0 + sl
            rows = pl.ds(pl.multiple_of(s * SB, SB), SB)
            new = []
            for j in range(2):
                lo = j * DV
                qe = qe_s[j, rows, :]
                ke = ke_s[j, rows, :]
                v_b = v_ref[rows, lo:lo + DV].astype(BF16)
                zero = jnp.zeros_like(qe)
                st = sts[j]
                q_parts, s_parts = [], []
                for c in range(CPB):
                    ci = s * CPB + c
                    s_parts.append(jnp.where(m_f, st, sbs_s[j, ci]).astype(BF16))
                    q_parts.append(jnp.where(row_chunk == c, qe, zero))
                    st = step_state(j, ci, st)
                o = _dot_nt(jnp.concatenate(q_parts, axis=1), jnp.concatenate(s_parts, axis=1))
                q_st = jnp.concatenate([jnp.where(m_f, qe, zero), jnp.where(m_f, zero, qe)], axis=0)
                sc = _dot_nt(q_st, ke)
                p = jnp.where(tril, sc[0:SB, :], 0.0) + jnp.where(triu, sc[SB:2 * SB, :], 0.0)
                o = o + _dot(p.astype(BF16), v_b)
                o = o * lax.rsqrt(jnp.mean(o * o, axis=-1, keepdims=True) + EPS) * ng_ref[:, lo:lo + DV]
                o_ref[rows, lo:lo + DV] = (o * _silu(og_ref[rows, lo:lo + DV].astype(F32))).astype(BF16)
                new.append(st)
            return tuple(new)

        st_f_fin = lax.fori_loop(0, sb_per_seq, fwd_body, st0, unroll=min(2, sb_per_seq))
        for j in range(2):
            sf_ref[q, j] = jnp.transpose(st_f_fin[j])[0:DK, :]
            sb_ref[q, j] = jnp.transpose(st_b_fin[j])[DK:2 * DK, :]


def _gla(proj, r_all, wg_hi, wg_lo, bg, ng, tl_m, s0f, s0b, *, seq_len, n_seq, n_in_step, row0):
    kern = functools.partial(_gla_kernel, seq_len=seq_len, n_in_step=n_in_step)
    rows = n_in_step * seq_len
    n_chunks = rows // CHUNK
    rb = lambda b: b + row0 // rows
    st_spec = pl.BlockSpec((n_in_step, 2, DK, DV), lambda b, h: (b, h, 0, 0))
    return pl.pallas_call(
        kern,
        grid=(n_seq // n_in_step, HEADS // 2),
        in_specs=[pl.BlockSpec((rows, LANES), lambda b, h: (rb(b), h)),
                  pl.BlockSpec((rows, LANES), lambda b, h: (rb(b), QK_W // LANES + h)),
                  pl.BlockSpec((rows, 2 * DV), lambda b, h: (rb(b), 2 * QK_W // (2 * DV) + h)),
                  pl.BlockSpec((rows, 2 * DV), lambda b, h: (rb(b), (2 * QK_W + V_W) // (2 * DV) + h)),
                  pl.BlockSpec((rows, LANES), lambda b, h: (rb(b), 0)),
                  pl.BlockSpec((None, LANES, 2 * LANES), lambda b, h: (h, 0, 0)),
                  pl.BlockSpec((None, LANES, 2 * LANES), lambda b, h: (h, 0, 0)),
                  pl.BlockSpec((None, 1, 2 * LANES), lambda b, h: (h, 0, 0)),
                  pl.BlockSpec((None, 1, 2 * DV), lambda b, h: (h, 0, 0)),
                  pl.BlockSpec((SB, SB), lambda b, h: (0, 0)),
                  st_spec, st_spec],
        out_specs=[pl.BlockSpec((rows, 2 * DV), lambda b, h: (b, h)), st_spec, st_spec],
        out_shape=[jax.ShapeDtypeStruct((n_seq * seq_len, V_W), BF16),
                   jax.ShapeDtypeStruct((n_seq, HEADS, DK, DV), F32),
                   jax.ShapeDtypeStruct((n_seq, HEADS, DK, DV), F32)],
        scratch_shapes=[pltpu.VMEM((rows, 2 * LANES), F32),
                        pltpu.VMEM((rows, 2 * LANES), F32),
                        pltpu.VMEM((2, rows, LANES), BF16),
                        pltpu.VMEM((2, rows, LANES), BF16),
                        pltpu.VMEM((2, rows, LANES), BF16),
                        pltpu.VMEM((n_chunks, 1, 2 * LANES), F32),
                        pltpu.VMEM((2, n_chunks, DV, LANES), F32),
                        pltpu.VMEM((2, n_chunks, DV, LANES), F32)],
        compiler_params=_cparams(("arbitrary", "arbitrary")),
        name="gla_%d" % seq_len,
    )(proj, proj, proj, proj, r_all, wg_hi, wg_lo, bg, ng, tl_m, s0f, s0b)


def _fnet_stage_a(u_bf, cs):
    cparts, sparts = [], []
    for g in range(FN_G):
        ab = _dot(u_bf[:, g * FN_C:(g + 1) * FN_C], cs)
        cparts.append(ab[:, 0:FN_C])
        sparts.append(ab[:, FN_C:2 * FN_C])
    return jnp.concatenate(cparts, axis=1), jnp.concatenate(sparts, axis=1)


SEQ_PER_FC = 4


def _fnet_ctx_kernel(u_ref, cs_ref, p2_ref, f_ref):
    uc, us = _fnet_stage_a(u_ref[...].astype(BF16), cs_ref[...].astype(BF16))
    ab = jnp.concatenate(
        [jnp.concatenate([uc[q * L_CTX:(q + 1) * L_CTX], us[q * L_CTX:(q + 1) * L_CTX]], axis=0)
         for q in range(SEQ_PER_FC)], axis=1).astype(BF16)
    f = _dot(p2_ref[...].astype(BF16), ab)
    w = FN_G * FN_C
    for q in range(SEQ_PER_FC):
        f_ref[q * L_CTX:(q + 1) * L_CTX, :] = f[:, q * w:(q + 1) * w].astype(BF16)


def _fnet_ctx(proj, cs, p2):
    rows = SEQ_PER_FC * L_CTX
    return pl.pallas_call(
        _fnet_ctx_kernel,
        grid=(N_CTX // SEQ_PER_FC,),
        in_specs=[pl.BlockSpec((rows, FN_G * FN_C), lambda b: (b, 3)),
                  pl.BlockSpec((FN_C, 2 * FN_C), lambda b: (0, 0)),
                  pl.BlockSpec((L_CTX, 2 * L_CTX), lambda b: (0, 0))],
        out_specs=pl.BlockSpec((rows, FN_G * FN_C), lambda b: (b, 0)),
        out_shape=jax.ShapeDtypeStruct((T_CTX, FN_G * FN_C), BF16),
        compiler_params=_cparams(("arbitrary",)),
        name="fnet_ctx",
    )(proj, cs, p2)


TM_FL = 256
RT_FL = 256


def _fnet_lat_kernel(u_ref, cs_ref, kr_ref, f_ref, ab_s):
    m = pl.program_id(1)

    @pl.when(m == 0)
    def _():
        def body(t, carry):
            rows = pl.ds(pl.multiple_of(t * RT_FL, RT_FL), RT_FL)
            uc, us = _fnet_stage_a(u_ref[rows, :].astype(BF16), cs_ref[...].astype(BF16))
            ab_s[rows, :] = uc.astype(BF16)
            ab_s[pl.ds(pl.multiple_of(L_LAT + t * RT_FL, RT_FL), RT_FL), :] = us.astype(BF16)
            return carry
        lax.fori_loop(0, L_LAT // RT_FL, body, 0)

    f_ref[...] = _dot(kr_ref[...].astype(BF16), ab_s[...]).astype(BF16)


def _fnet_lat(proj, cs, kr):
    nm = L_LAT // TM_FL
    return pl.pallas_call(
        _fnet_lat_kernel,
        grid=(N_LAT, nm),
        in_specs=[pl.BlockSpec((L_LAT, FN_G * FN_C), lambda b, m: (T_CTX // L_LAT + b, 3)),
                  pl.BlockSpec((FN_C, 2 * FN_C), lambda b, m: (0, 0)),
                  pl.BlockSpec((TM_FL, 2 * L_LAT), lambda b, m: (m, 0))],
        out_specs=pl.BlockSpec((TM_FL, FN_G * FN_C), lambda b, m: (b * nm + m, 0)),
        out_shape=jax.ShapeDtypeStruct((T_LAT, FN_G * FN_C), BF16),
        scratch_shapes=[pltpu.VMEM((2 * L_LAT, FN_G * FN_C), BF16)],
        compiler_params=_cparams(("arbitrary", "arbitrary")),
        name="fnet_lat",
    )(proj, cs, kr)


TM_OUT = 256
LANE_E0 = N_GROUPS
ROWS_PER_BLK = 8
PACK_ROWS = -(-(2 * TM_OUT + N_EXP * (ROWS_PER_BLK - 1)) // 256) * 256
PACK_BLKS = PACK_ROWS // ROWS_PER_BLK
N_TOK_TILES = T_ALL // TM_OUT
BLK_PER_TILE = TM_MOE // ROWS_PER_BLK
USED_BLKS = (2 * TM_OUT + N_EXP * (ROWS_PER_BLK - 1)) // ROWS_PER_BLK
HS_ROWS = N_TOK_TILES * PACK_ROWS
assert (PACK_BLKS - USED_BLKS) * N_TOK_TILES >= 2 * BLK_PER_TILE


def _outproj_kernel(oc_ref, ol_ref, fc_ref, fl_ref, xp_ref, xs_ref, mod_ref, g_ref, wo_ref, wf_ref,
                    wrc_ref, br_ref, sut_ref, sl_ref,
                    x1_ref, hs_ref, rw_ref, nb_ref, lb_ref):
    i = pl.program_id(0)
    is_ctx = i < T_CTX // TM_OUT

    o = jnp.where(is_ctx, oc_ref[...], ol_ref[...]).astype(BF16)
    f = jnp.where(is_ctx, fc_ref[...], fl_ref[...]).astype(BF16)
    x = jnp.where(is_ctx, xp_ref[...], xs_ref[...])
    y = _dot(o, wo_ref[...]) + _dot(f, wf_ref[...])
    ga1 = mod_ref[:, 2 * D_MODEL:3 * D_MODEL]
    sh2 = mod_ref[:, 3 * D_MODEL:4 * D_MODEL]
    sc2 = mod_ref[:, 4 * D_MODEL:5 * D_MODEL]
    x1 = x + ga1 * y
    x1_ref[...] = x1
    h2 = _rms(x1, g_ref[...]) * (1.0 + sc2) + sh2

    h_hi, h_lo = _split2(h2)
    hw = _dot(h_hi, wrc_ref[...])
    lg_all = jnp.transpose(hw[:, 0:LANES] + _dot(h_lo, wrc_ref[:, 0:LANES]) + hw[:, LANES:2 * LANES]
                           + br_ref[...])

    row_i = lax.broadcasted_iota(I32, (LANES, TM_OUT), 0)
    row = row_i.astype(F32)
    neg = jnp.float32(-jnp.inf)
    big = jnp.float32(LANES)
    lg = jnp.where(row_i < N_GROUPS, lg_all, neg)
    gmax = jnp.max(lg, axis=0, keepdims=True)
    gsel = jnp.min(jnp.where(lg == gmax, row, big), axis=0, keepdims=True)
    den = jnp.sum(jnp.exp(lg - gmax), axis=0, keepdims=True)
    pg_sel = 1.0 / den

    e_idx = row_i - LANE_E0
    egrp = (e_idx >> 3).astype(F32)
    emask = (e_idx >= 0) & (e_idx < N_EXP) & (egrp == gsel)
    m1 = jnp.where(emask, lg_all, neg)
    v1 = jnp.max(m1, axis=0, keepdims=True)
    i1 = jnp.min(jnp.where(m1 == v1, row, big), axis=0, keepdims=True)
    m2 = jnp.where(row == i1, neg, m1)
    v2 = jnp.max(m2, axis=0, keepdims=True)
    i2 = jnp.min(jnp.where(m2 == v2, row, big), axis=0, keepdims=True)
    e2 = jnp.exp(v2 - v1)
    inv = 1.0 / (1.0 + e2)
    w1 = inv * pg_sel
    w2 = (e2 * inv) * pg_sel

    oh1 = row == i1
    oh2 = row == i2
    oh = jnp.where(oh1 | oh2, 1.0, 0.0).astype(BF16)
    cnt = _dot(oh, jnp.ones((TM_OUT, LANES), BF16))
    nblk = jnp.floor((cnt + (ROWS_PER_BLK - 1)) * (1.0 / ROWS_PER_BLK))
    lboff = _dot(sl_ref[...], nblk.astype(BF16))
    lrank = _dot(oh, sut_ref[...])
    posmat = jnp.concatenate([lboff, lboff], axis=1) * ROWS_PER_BLK + lrank
    p1 = jnp.sum(jnp.where(oh1, posmat, 0.0), axis=0, keepdims=True)
    p2 = jnp.sum(jnp.where(oh2, posmat, 0.0), axis=0, keepdims=True)
    nb_ref[...] = nblk[:, 0:8].astype(I32)
    lb_ref[...] = lboff[:, 0:8].astype(I32)

    prow = lax.broadcasted_iota(I32, (PACK_ROWS, TM_OUT), 0).astype(F32)
    place = jnp.where((prow == p1) | (prow == p2), 1.0, 0.0).astype(BF16)
    hs_ref[...] = _dot(place, h_hi).astype(BF16)

    rw_ref[...] = jnp.concatenate([w1, w2, p1, p2, jnp.zeros((4, TM_OUT), F32)], axis=0)


def _outproj(o_ctx, o_lat, f_ctx, f_lat, xp, xs, mod3, g_ffn, wo, wf, wr_cat, br, sut, sl):
    nt = T_ALL // TM_OUT
    nctx = T_CTX // TM_OUT
    ctx_map = lambda i: (jnp.minimum(i, nctx - 1), 0)
    lat_map = lambda i: (jnp.maximum(i - nctx, 0), 0)
    const = lambda i: (0, 0)
    return pl.pallas_call(
        _outproj_kernel,
        grid=(nt,),
        in_specs=[pl.BlockSpec((TM_OUT, V_W), ctx_map),
                  pl.BlockSpec((TM_OUT, V_W), lat_map),
                  pl.BlockSpec((TM_OUT, FN_G * FN_C), ctx_map),
                  pl.BlockSpec((TM_OUT, FN_G * FN_C), lat_map),
                  pl.BlockSpec((TM_OUT, D_MODEL), ctx_map),
                  pl.BlockSpec((TM_OUT, D_MODEL), lat_map),
                  pl.BlockSpec((None, 1, 6 * D_MODEL), lambda i: (_cond_row(i, TM_OUT), 0, 0)),
                  pl.BlockSpec((1, D_MODEL), const),
                  pl.BlockSpec((V_W, D_MODEL), const),
                  pl.BlockSpec((FN_G * FN_C, D_MODEL), const),
                  pl.BlockSpec((D_MODEL, 2 * LANES), const),
                  pl.BlockSpec((1, LANES), const),
                  pl.BlockSpec((TM_OUT, TM_OUT), const),
                  pl.BlockSpec((LANES, LANES), const)],
        out_specs=[pl.BlockSpec((TM_OUT, D_MODEL), lambda i: (i, 0)),
                   pl.BlockSpec((PACK_ROWS, D_MODEL), lambda i: (i, 0)),
                   pl.BlockSpec((8, TM_OUT), lambda i: (0, i)),
                   pl.BlockSpec((None, LANES, 8), lambda i: (i, 0, 0)),
                   pl.BlockSpec((None, LANES, 8), lambda i: (i, 0, 0))],
        out_shape=[jax.ShapeDtypeStruct((T_ALL, D_MODEL), F32),
                   jax.ShapeDtypeStruct((HS_ROWS, D_MODEL), BF16),
                   jax.ShapeDtypeStruct((8, T_ALL), F32),
                   jax.ShapeDtypeStruct((nt, LANES, 8), I32),
                   jax.ShapeDtypeStruct((nt, LANES, 8), I32)],
        compiler_params=_cparams(("arbitrary",)),
        name="outproj",
    )(o_ctx, o_lat, f_ctx, f_lat, xp, xs, mod3, g_ffn, wo, wf, wr_cat, br, sut, sl)


SRC_BITS = 16
SRC_MASK = (1 << SRC_BITS) - 1
X_SLOTS = 3
Y_SLOTS = 3
N_UP_CHUNKS = 2
N_DN_CHUNKS = 8
NT_MOE = (2 * T_ALL + N_TOK_TILES * N_EXP * (ROWS_PER_BLK - 1)) // TM_MOE + N_EXP


def _moe_kernel(texp_ref, nexp_ref, meta_ref, code_ref,
                h_hbm, wg_hbm, wu_hbm, wd_hbm,
                out_hbm,
                xbuf, ybuf, wg_f, wu_f, wd_f, wg_s, wu_s, wd_s, kcount, gsem, ssem, wsem):
    i = pl.program_id(0)
    nt = meta_ref[0]
    xs = i % X_SLOTS

    def blk_rows(b):
        if isinstance(b, int):
            return pl.ds(b * ROWS_PER_BLK, ROWS_PER_BLK)
        return pl.ds(pl.multiple_of(b * ROWS_PER_BLK, ROWS_PER_BLK), ROWS_PER_BLK)

    def gather_row(tile, sl, j):
        src = code_ref[(tile + 2) * BLK_PER_TILE + j] & SRC_MASK
        pltpu.make_async_copy(h_hbm.at[blk_rows(src)], xbuf.at[sl, blk_rows(j)], gsem.at[sl]).start()

    def scatter_row(tile, sl, j):
        dst = code_ref[(tile + 2) * BLK_PER_TILE + j] >> SRC_BITS
        pltpu.make_async_copy(ybuf.at[sl, blk_rows(j)], out_hbm.at[blk_rows(dst)], ssem.at[sl]).start()

    def gather_wait(sl):
        pltpu.make_async_copy(h_hbm.at[pl.ds(0, TM_MOE)], xbuf.at[sl], gsem.at[sl]).wait()

    def scatter_wait(sl):
        pltpu.make_async_copy(ybuf.at[sl], out_hbm.at[pl.ds(0, TM_MOE)], ssem.at[sl]).wait()

    def y_slot(tile):
        return (tile + 1) % Y_SLOTS

    @pl.when(i == 0)
    def _():
        ybuf[y_slot(-2)] = jnp.zeros((TM_MOE, D_MODEL), BF16)
        ybuf[y_slot(-1)] = jnp.zeros((TM_MOE, D_MODEL), BF16)

        def body(j, c):
            gather_row(0, 0, j)
            gather_row(1, 1, j)
            scatter_row(-2, y_slot(-2), j)
            return c
        lax.fori_loop(0, BLK_PER_TILE, body, 0)

    @pl.when((i >= 1) & (i <= nt))
    def _():
        scatter_wait(y_slot(i - 3))

    @pl.when(i < nt)
    def _():
        prev = texp_ref[jnp.maximum(i - 1, 0)]

        def weight_copies(e, sl):
            return (pltpu.make_async_copy(wg_hbm.at[e], wg_f.at[sl], wsem.at[sl]),
                    pltpu.make_async_copy(wu_hbm.at[e], wu_f.at[sl], wsem.at[sl]),
                    pltpu.make_async_copy(wd_hbm.at[e], wd_f.at[sl], wsem.at[sl]))

        @pl.when(i == 0)
        def _():
            kcount[0] = 0
            for cp in weight_copies(texp_ref[0], 0):
                cp.start()

        @pl.when((i == 0) | (texp_ref[i] != prev))
        def _():
            k = kcount[0]
            for sl in range(2):
                @pl.when(k % 2 == sl)
                def _(sl=sl):
                    for cp in weight_copies(texp_ref[i], sl):
                        cp.wait()
                    def narrow(r, c):
                        up = pl.ds(pl.multiple_of(r * (D_MODEL // 16), D_MODEL // 16), D_MODEL // 16)
                        dn = pl.ds(pl.multiple_of(r * (D_EXP // 16), D_EXP // 16), D_EXP // 16)
                        wg_s[up, :] = wg_f[sl, up, :].astype(BF16)
                        wu_s[up, :] = wu_f[sl, up, :].astype(BF16)
                        wd_s[dn, :] = wd_f[sl, dn, :].astype(BF16)
                        return c
                    lax.fori_loop(0, 16, narrow, 0)

                    @pl.when(nexp_ref[i] >= 0)
                    def _():
                        for cp in weight_copies(nexp_ref[i], 1 - sl):
                            cp.start(priority=1)
            kcount[0] = k + 1

        gather_wait(xs)
        x = xbuf[xs].astype(BF16)

        issues = []
        for j in range(BLK_PER_TILE):
            issues.append(functools.partial(gather_row, i + 2, (i + 2) % X_SLOTS, j))
            issues.append(functools.partial(scatter_row, i - 1, y_slot(i - 1), j))
        n_groups = N_UP_CHUNKS + N_DN_CHUNKS
        per_group = -(-len(issues) // n_groups)

        def issue_group(k):
            for fn in issues[k * per_group:(k + 1) * per_group]:
                fn()

        wu_c = D_EXP // N_UP_CHUNKS
        hid = []
        for n in range(N_UP_CHUNKS):
            issue_group(n)
            g = _dot(x, wg_s[:, n * wu_c:(n + 1) * wu_c])
            u = _dot(x, wu_s[:, n * wu_c:(n + 1) * wu_c])
            hid.append((_silu(g) * u).astype(BF16))
        hid = jnp.concatenate(hid, axis=1)
        wd_c = D_MODEL // N_DN_CHUNKS
        ys = y_slot(i)
        for n in range(N_DN_CHUNKS):
            issue_group(N_UP_CHUNKS + n)
            ybuf[ys, :, n * wd_c:(n + 1) * wd_c] = _dot(hid, wd_s[:, n * wd_c:(n + 1) * wd_c]).astype(BF16)

    @pl.when(i == nt)
    def _():
        gather_wait(xs)
        gather_wait((i + 1) % X_SLOTS)

        def body(j, c):
            scatter_row(nt - 1, y_slot(nt - 1), j)
            return c
        lax.fori_loop(0, BLK_PER_TILE, body, 0)
        scatter_wait(y_slot(nt - 2))
        scatter_wait(y_slot(nt - 1))


def _moe(texp, nexp, meta, code, hs, w_eg, w_eu, w_ed):
    hbm = pl.BlockSpec(memory_space=pl.ANY)
    grid_spec = pltpu.PrefetchScalarGridSpec(
        num_scalar_prefetch=4,
        grid=(NT_MOE + 1,),
        in_specs=[hbm, hbm, hbm, hbm],
        out_specs=hbm,
        scratch_shapes=[pltpu.VMEM((X_SLOTS, TM_MOE, D_MODEL), BF16),
                        pltpu.VMEM((Y_SLOTS, TM_MOE, D_MODEL), BF16),
                        pltpu.VMEM((2, D_MODEL, D_EXP), F32),
                        pltpu.VMEM((2, D_MODEL, D_EXP), F32),
                        pltpu.VMEM((2, D_EXP, D_MODEL), F32),
                        pltpu.VMEM((D_MODEL, D_EXP), BF16),
                        pltpu.VMEM((D_MODEL, D_EXP), BF16),
                        pltpu.VMEM((D_EXP, D_MODEL), BF16),
                        pltpu.SMEM((1,), I32),
                        pltpu.SemaphoreType.DMA((X_SLOTS,)),
                        pltpu.SemaphoreType.DMA((Y_SLOTS,)),
                        pltpu.SemaphoreType.DMA((2,))])
    return pl.pallas_call(
        _moe_kernel,
        grid_spec=grid_spec,
        out_shape=jax.ShapeDtypeStruct((HS_ROWS, D_MODEL), BF16),
        input_output_aliases={4: 0},
        compiler_params=_cparams(("arbitrary",)),
        name="moe",
    )(texp, nexp, meta, code, hs, w_eg, w_eu, w_ed)


TM_FIN = TM_OUT


def _final_kernel(x1_ref, ys_pack_ref, rw_ref, mod_ref, g_ref, yp_ref, ys_ref):
    i = pl.program_id(0)
    ga2 = mod_ref[:, 5 * D_MODEL:6 * D_MODEL]
    w0 = rw_ref[0:1, :]
    w1 = rw_ref[1:2, :]
    p0 = rw_ref[2:3, :]
    p1 = rw_ref[3:4, :]
    prow = lax.broadcasted_iota(I32, (PACK_ROWS, TM_FIN), 0).astype(F32)
    comb_t = jnp.where(prow == p0, w0, 0.0) + jnp.where(prow == p1, w1, 0.0)
    y_moe = lax.dot_general(comb_t.astype(BF16), ys_pack_ref[...].astype(BF16), (((0,), (0,)), ((), ())),
                            preferred_element_type=F32)
    y = x1_ref[...] + ga2 * y_moe
    out = _rms(y, g_ref[...])

    @pl.when(i < T_CTX // TM_FIN)
    def _():
        yp_ref[...] = out

    @pl.when(i >= T_CTX // TM_FIN)
    def _():
        ys_ref[...] = out


def _final(x1, y2, rw, mod3, g_fin):
    nt = T_ALL // TM_FIN
    nctx = T_CTX // TM_FIN
    return pl.pallas_call(
        _final_kernel,
        grid=(nt,),
        in_specs=[pl.BlockSpec((TM_FIN, D_MODEL), lambda i: (i, 0)),
                  pl.BlockSpec((PACK_ROWS, D_MODEL), lambda i: (i, 0)),
                  pl.BlockSpec((8, TM_FIN), lambda i: (0, i)),
                  pl.BlockSpec((None, 1, 6 * D_MODEL), lambda i: (_cond_row(i, TM_FIN), 0, 0)),
                  pl.BlockSpec((1, D_MODEL), lambda i: (0, 0))],
        out_specs=[pl.BlockSpec((TM_FIN, D_MODEL), lambda i: (jnp.minimum(i, nctx - 1), 0)),
                   pl.BlockSpec((TM_FIN, D_MODEL), lambda i: (jnp.maximum(i - nctx, 0), 0))],
        out_shape=[jax.ShapeDtypeStruct((T_CTX, D_MODEL), F32),
                   jax.ShapeDtypeStruct((T_LAT, D_MODEL), F32)],
        compiler_params=_cparams(("arbitrary",)),
        name="final",
    )(x1, y2, rw, mod3, g_fin)


def _np_bf16(a):
    return jnp.asarray(np.asarray(a, np.float32), dtype=BF16)


def _np_f32(a):
    return jnp.asarray(np.asarray(a, np.float32))


@functools.lru_cache(maxsize=None)
def _constants():
    c = {}
    k = np.arange(FN_C)
    ang = 2.0 * np.pi * np.outer(k, k) / FN_C
    c["cs"] = np.concatenate([np.cos(ang), np.sin(ang)], axis=1) / np.sqrt(FN_C)
    p = np.arange(L_CTX)
    ang = 2.0 * np.pi * np.outer(p, p) / L_CTX
    c["p2"] = np.concatenate([np.cos(ang), -np.sin(ang)], axis=1) / np.sqrt(L_CTX)
    pos = np.arange(L_LAT)
    rr, cc = pos // GRID_W, pos % GRID_W
    num = (np.outer(rr, rr) * (GRID_W // GRID_H) + np.outer(cc, cc)) % GRID_W
    ang = 2.0 * np.pi * num / GRID_W
    c["kr"] = np.concatenate([np.cos(ang), -np.sin(ang)], axis=1) / np.sqrt(L_LAT)
    i = np.arange(SB)
    same = (i[:, None] // CHUNK) == (i[None, :] // CHUNK)
    c["tl"] = (same & (i[:, None] >= i[None, :])).astype(np.float32)
    c["sut"] = (i[:, None] < i[None, :]).astype(np.float32)
    k = np.arange(LANES)
    c["sl"] = (k[:, None] > k[None, :]).astype(np.float32)
    return c


def kernel(x_prompt, x_sample, state_gla_fwd, state_gla_bwd, c, c_ctx, w_ada, b_ada, norm_attn, norm_ffn, w_in, w_gate_fwd, b_gate_fwd, w_gate_bwd, b_gate_bwd, norm_gla, w_out, w_router_group, b_router_group, w_router_expert, b_router_expert, w_expert_gate, w_expert_up, w_expert_down, norm_final):
    assert w_ada.shape[0] == 1, "single layer"
    cst = _constants()
    cs, p2, kr = _np_f32(cst["cs"]), _np_f32(cst["p2"]), _np_f32(cst["kr"])
    tl_m, sut, sl = _np_bf16(cst["tl"]), _np_bf16(cst["sut"]), _np_bf16(cst["sl"])

    xp = x_prompt.reshape(T_CTX, D_MODEL)
    xs = x_sample.reshape(T_LAT, D_MODEL)

    cond8 = jnp.concatenate([c_ctx[None, :], c, jnp.zeros((3, D_MODEL), F32)], axis=0)
    mod = _ada(cond8, w_ada[0], b_ada[0][None, :])
    mod3 = mod.reshape(8, 1, 6 * D_MODEL)

    wi = w_in[0]
    i_og = 2 * QK_W + 2 * V_W
    i_u = i_og + 2 * RANK
    w_a = wi[:, :i_og].astype(BF16)
    w_u = wi[:, i_u:].astype(BF16)
    w_r = jnp.pad(wi[:, i_og:i_u], ((0, 0), (0, LANES - 2 * RANK))).astype(BF16)

    wgf = w_gate_fwd[0].reshape(RANK, HEADS, DK)
    wgb = w_gate_bwd[0].reshape(RANK, HEADS, DK)
    zf = jnp.zeros_like(wgf)
    top = jnp.stack([wgf, zf], axis=2)
    bot = jnp.stack([zf, wgb], axis=2)
    wg = jnp.concatenate([top, bot], axis=0)
    wg = wg.reshape(2 * RANK, HEADS // 2, 4 * DK).transpose(1, 0, 2)
    wg = jnp.pad(wg, ((0, 0), (0, LANES - 2 * RANK), (0, 0)))
    wg_hi = wg.astype(BF16)
    wg_lo = (wg - wg_hi.astype(F32)).astype(BF16)
    bg = jnp.stack([b_gate_fwd[0].reshape(HEADS, DK), b_gate_bwd[0].reshape(HEADS, DK)], axis=1)
    bg = bg.reshape(HEADS // 2, 1, 4 * DK)
    ng = norm_gla[0].reshape(HEADS // 2, 1, 2 * DV)

    proj, r_all = _inproj(xp, xs, mod3, norm_attn, w_a, w_u, w_r)

    zero_state = jnp.zeros((N_CTX, HEADS, DK, DV), F32)
    o_ctx, sf_ctx, sb_ctx = _gla(proj, r_all, wg_hi, wg_lo, bg, ng, tl_m, zero_state, zero_state,
                                 seq_len=L_CTX, n_seq=N_CTX, n_in_step=4, row0=0)
    o_lat, _, _ = _gla(proj, r_all, wg_hi, wg_lo, bg, ng, tl_m,
                       state_gla_fwd[:, 0], state_gla_bwd[:, 0],
                       seq_len=L_LAT, n_seq=N_LAT, n_in_step=1, row0=T_CTX)

    f_ctx = _fnet_ctx(proj, cs, p2)
    f_lat = _fnet_lat(proj, cs, kr)

    wo = w_out[0][:V_W].astype(BF16)
    wf = w_out[0][V_W:].astype(BF16)
    wr = jnp.concatenate([w_router_group[0], w_router_expert[0]], axis=1)
    wr = jnp.pad(wr, ((0, 0), (0, LANES - N_GROUPS - N_EXP)))
    wr_hi = wr.astype(BF16)
    wr_lo = (wr - wr_hi.astype(F32)).astype(BF16)
    wr_cat = jnp.concatenate([wr_hi, wr_lo], axis=1)
    br = jnp.pad(jnp.concatenate([b_router_group[0], b_router_expert[0]]), (0, LANES - N_GROUPS - N_EXP))[None, :]

    x1, hs, rw, nb, lb = _outproj(o_ctx, o_lat, f_ctx, f_lat, xp, xs, mod3, norm_ffn, wo, wf,
                                  wr_cat, br, sut, sl)

    nb_e = nb[:, LANE_E0:LANE_E0 + N_EXP, 0].T
    lb_e = lb[:, LANE_E0:LANE_E0 + N_EXP, 0].T
    run_end = jnp.cumsum(nb_e, axis=1)
    blocks_e = run_end[:, -1]
    tiles_e = (blocks_e + BLK_PER_TILE - 1) // BLK_PER_TILE
    tile_end = jnp.cumsum(tiles_e)
    tile_start = tile_end - tiles_e
    n_tiles = tile_end[-1]
    n_code_tiles = NT_MOE + 4
    tile = jnp.arange(n_code_tiles, dtype=I32) - 2
    tile_c = jnp.clip(tile, 0, n_tiles - 1)
    t_exp = jnp.sum(tile_c[:, None] >= tile_end[None, :], axis=1)

    pick = t_exp[:, None] == jnp.arange(N_EXP, dtype=I32)[None, :]

    def per_tile(table):
        if table.ndim == 1:
            return jnp.sum(jnp.where(pick, table[None, :], 0), axis=1)
        return jnp.sum(jnp.where(pick[:, :, None], table[None, :, :], 0), axis=1)

    ends = per_tile(run_end)
    starts = ends - per_tile(nb_e)
    offs = per_tile(lb_e) + jnp.arange(N_TOK_TILES, dtype=I32)[None, :] * PACK_BLKS - starts
    j = jnp.arange(BLK_PER_TILE, dtype=I32)
    bi = ((tile_c - per_tile(tile_start)) * BLK_PER_TILE)[:, None] + j[None, :]
    in_run = (starts.T[:, :, None] <= bi[None, :, :]) & (bi[None, :, :] < ends.T[:, :, None])
    blk = bi + jnp.sum(jnp.where(in_run, offs.T[:, :, None], 0), axis=0)
    valid = (tile == tile_c)[:, None] & (bi < per_tile(blocks_e)[:, None])
    spare_ix = (jnp.arange(n_code_tiles, dtype=I32) % 2)[:, None] * BLK_PER_TILE + j[None, :]
    spare = (spare_ix % N_TOK_TILES) * PACK_BLKS + USED_BLKS + spare_ix // N_TOK_TILES
    code = jnp.where(valid, (blk << SRC_BITS) | blk, (spare << SRC_BITS) | blk[:, 0:1]).astype(I32).reshape(-1)
    texp = t_exp[2:NT_MOE + 3].astype(I32)
    e_ix = jnp.arange(N_EXP, dtype=I32)
    later = (e_ix[None, :] > e_ix[:, None]) & (tiles_e[None, :] > 0)
    next_e = jnp.min(jnp.where(later, e_ix[None, :], N_EXP), axis=1)
    next_e = jnp.where(next_e < N_EXP, next_e, -1)
    nexp = per_tile(next_e)[2:NT_MOE + 3].astype(I32)
    meta = n_tiles.reshape(1).astype(I32)

    y2 = _moe(texp, nexp, meta, code, hs, w_expert_gate[0], w_expert_up[0], w_expert_down[0])
    y_prompt, y_sample = _final(x1, y2, rw, mod3, norm_final[None, :])

    st_shape = (N_CTX, 1, HEADS, DK, DV)
    return (y_prompt.reshape(N_CTX, L_CTX, D_MODEL), y_sample.reshape(N_LAT, L_LAT, D_MODEL),
            sf_ctx.reshape(st_shape), sb_ctx.reshape(st_shape))
```
